```python
import jax, jax.numpy as jnp
from jax import lax
import numpy as np

D_MODEL = 1024
BATCH = 8
SEQ = 4096
DEPTH = 2

GRID_W = 64
CTX_LEN = 256
D_MIX = D_MODEL
N_MIXERS = 4
W_GROUP = D_MIX // N_MIXERS
N_SUB = 4
D_SUB = W_GROUP // N_SUB
OFF_A_X = 0
OFF_A_G = W_GROUP
OFF_B = 2 * W_GROUP
OFF_C = 3 * W_GROUP
OFF_D = 4 * W_GROUP
D_IN = 6 * W_GROUP
RG_CONV = 4
RG_C = 8.0
POOL_WINDOWS = (2, 4, 8, 16)
CONF_KERNEL = 31
N_EXPERTS = 32
TOP_K = 4
D_FF = D_MODEL
SWIGLU_LIMIT = 7.0
SWIGLU_ALPHA = 1.702
MOE_BLOCK = 256
EPS = 1e-6

kernel_name = "hybrid_rglru_pool_fourier_conformer_moe_dit"


def rmsnorm(x, g):
    xf = x.astype(jnp.float32)
    y = xf * lax.rsqrt(jnp.mean(xf * xf, axis=-1, keepdims=True) + EPS)
    return (y * g.astype(jnp.float32)).astype(x.dtype)


def modulate(h, shift, scale):
    return h * (1 + scale) + shift


def pos_embed_2d(n_tokens, dtype):
    rows_n = n_tokens // GRID_W
    row = jnp.repeat(jnp.arange(rows_n), GRID_W).astype(jnp.float32)
    col = jnp.tile(jnp.arange(GRID_W), rows_n).astype(jnp.float32)
    q = D_MODEL // 4
    omega = 1.0 / (10000.0 ** (jnp.arange(q, dtype=jnp.float32) / q))

    def emb(p):
        ang = p[:, None] * omega[None, :]
        return jnp.concatenate([jnp.sin(ang), jnp.cos(ang)], axis=-1)

    return jnp.concatenate([emb(row), emb(col)], axis=-1).astype(dtype)


def depthwise_conv(x, w, b, pad):
    y = lax.conv_general_dilated(x, w[:, None, :].astype(x.dtype), window_strides=(1,), padding=[pad],
                                 dimension_numbers=('NWC', 'WIO', 'NWC'), feature_group_count=x.shape[-1])
    return y + b


def linear_scan(a, b, h0):
    b = b.at[:, 0].add(a[:, 0] * h0)

    def comb(l, r):
        return (l[0] * r[0], r[0] * l[1] + r[1])

    _, h = lax.associative_scan(comb, (a, b), axis=1)
    return h


def rglru_direction(xa, conv_w, conv_b, w_r, b_r, w_i, b_i, lam, h0):
    bn, L, _ = xa.shape
    xc = depthwise_conv(xa, conv_w, conv_b, (RG_CONV - 1, 0))
    xh = xc.reshape(bn, L, N_SUB, D_SUB)
    r = jax.nn.sigmoid(jnp.einsum('blhc,hcd->blhd', xh, w_r) + b_r).reshape(bn, L, W_GROUP)
    i = jax.nn.sigmoid(jnp.einsum('blhc,hcd->blhd', xh, w_i) + b_i).reshape(bn, L, W_GROUP)
    log_a = -RG_C * r.astype(jnp.float32) * jax.nn.softplus(-lam.astype(jnp.float32))
    a = jnp.exp(log_a)
    bterm = jnp.sqrt(-jnp.expm1(2.0 * log_a)) * (i * xc).astype(jnp.float32)
    h = linear_scan(a, bterm, h0)
    return h.astype(xa.dtype), h[:, -1]


def pool_mixer(xb, w_pool, b_pool, pool_scale):
    bn, L, _ = xb.shape
    xg = xb.reshape(bn, L, N_SUB, D_SUB)
    cs = jnp.concatenate([jnp.zeros((bn, 1, N_SUB, D_SUB), jnp.float32),
                          jnp.cumsum(xg.astype(jnp.float32), axis=1)], axis=1)
    t = jnp.arange(L)
    outs = []
    for g, w in enumerate(POOL_WINDOWS):
        lo = jnp.clip(t - w // 2, 0, L)
        hi = jnp.clip(t + w - w // 2, 0, L)
        s = cs[:, hi, g] - cs[:, lo, g]
        outs.append(s / (hi - lo).astype(jnp.float32)[None, :, None])
    pooled = jnp.stack(outs, axis=2).astype(xb.dtype) - xg
    y = jnp.einsum('blgc,gcd->blgd', pooled, w_pool) + b_pool
    return y.reshape(bn, L, W_GROUP) * pool_scale


def fourier_mixer(xc, w_four, b_four):
    bn, L, _ = xc.shape
    xg = xc.reshape(bn, L, N_SUB, D_SUB).astype(jnp.float32)
    f = jnp.fft.fft2(xg, axes=(1, 3), norm='ortho').real.astype(xc.dtype)
    y = jnp.einsum('blgc,gcd->blgd', f, w_four) + b_four
    return y.reshape(bn, L, W_GROUP)


def conformer_conv(xd, conv_w, conv_b, ln_g, ln_b, w_pw, b_pw):
    bn, L, _ = xd.shape
    v = xd[..., :W_GROUP] * jax.nn.sigmoid(xd[..., W_GROUP:])
    v = depthwise_conv(v, conv_w, conv_b, (CONF_KERNEL // 2, CONF_KERNEL // 2))
    vh = v.reshape(bn, L, N_SUB, D_SUB).astype(jnp.float32)
    mu = jnp.mean(vh, axis=-1, keepdims=True)
    var = jnp.mean(jnp.square(vh - mu), axis=-1, keepdims=True)
    vh = (vh - mu) * lax.rsqrt(var + EPS)
    v = (vh.reshape(bn, L, W_GROUP) * ln_g.astype(jnp.float32) + ln_b.astype(jnp.float32)).astype(xd.dtype)
    return jax.nn.silu(v) @ w_pw + b_pw


def token_mixers(u, h0_f, h0_b, w_in, b_in, conv_a_w, conv_a_b, w_rg_r, b_rg_r, w_rg_i, b_rg_i, rg_lambda,
                 w_pool, b_pool, pool_scale, w_four, b_four, conv_d_w, conv_d_b, ln_d_g, ln_d_b, w_pw, b_pw,
                 w_out, b_out):
    proj = u @ w_in + b_in
    xa = proj[..., OFF_A_X:OFF_A_G]
    ga = proj[..., OFF_A_G:OFF_B]
    xb = proj[..., OFF_B:OFF_C]
    xc = proj[..., OFF_C:OFF_D]
    xd = proj[..., OFF_D:]
    y_f, hf = rglru_direction(xa, conv_a_w[0], conv_a_b[0], w_rg_r[0], b_rg_r[0], w_rg_i[0], b_rg_i[0],
                              rg_lambda[0], h0_f)
    y_b, hb = rglru_direction(jnp.flip(xa, axis=1), conv_a_w[1], conv_a_b[1], w_rg_r[1], b_rg_r[1],
                              w_rg_i[1], b_rg_i[1], rg_lambda[1], h0_b)
    ya = (y_f + jnp.flip(y_b, axis=1)) * jax.nn.gelu(ga)
    yb = pool_mixer(xb, w_pool, b_pool, pool_scale)
    yc = fourier_mixer(xc, w_four, b_four)
    yd = conformer_conv(xd, conv_d_w, conv_d_b, ln_d_g, ln_d_b, w_pw, b_pw)
    y = jnp.concatenate([ya, yb, yc, yd], axis=-1) @ w_out + b_out
    return (y, hf, hb)


def moe(h, w_router, b_router, w_gu, b_gu, w_down, b_down):
    T, D = h.shape
    logits = (h @ w_router + b_router).astype(jnp.float32)
    top_vals, top_idx = lax.top_k(logits, TOP_K)
    gates = jax.nn.softmax(top_vals, axis=-1)
    A = T * TOP_K
    e_flat = top_idx.reshape(A)
    g_flat = gates.reshape(A)
    tok_flat = jnp.arange(A, dtype=jnp.int32) // TOP_K
    order = jnp.argsort(e_flat)
    e_s, tok_s, g_s = e_flat[order], tok_flat[order], g_flat[order]
    counts = jnp.bincount(e_flat, length=N_EXPERTS)
    padded = ((counts + MOE_BLOCK - 1) // MOE_BLOCK) * MOE_BLOCK
    start = jnp.cumsum(counts) - counts
    pend = jnp.cumsum(padded)
    pstart = pend - padded
    dest = pstart[e_s] + (jnp.arange(A, dtype=jnp.int32) - start[e_s])
    n_blocks = -(-A // MOE_BLOCK) + N_EXPERTS
    P = n_blocks * MOE_BLOCK
    xs = jnp.zeros((P, D), h.dtype).at[dest].set(h[tok_s])
    block_e = jnp.minimum(jnp.searchsorted(pend, jnp.arange(n_blocks) * MOE_BLOCK, side='right'),
                          N_EXPERTS - 1)

    def expert_block(args):
        xb, e = args
        gu = xb @ w_gu[e] + b_gu[e]
        gt = jnp.minimum(gu[:, :D_FF], SWIGLU_LIMIT)
        up = jnp.clip(gu[:, D_FF:], -SWIGLU_LIMIT, SWIGLU_LIMIT)
        act = (up + 1) * (gt * jax.nn.sigmoid(SWIGLU_ALPHA * gt))
        return act @ w_down[e] + b_down[e]

    ys = lax.map(expert_block, (xs.reshape(n_blocks, MOE_BLOCK, D), block_e)).reshape(P, D)
    return jnp.zeros((T, D), h.dtype).at[tok_s].add(ys[dest] * g_s[:, None].astype(h.dtype))


def setup_inputs(seed: int = 0) -> dict:
    key = jax.random.key(seed)
    keys = jax.random.split(key, 40)
    f32 = jnp.float32
    D = D_MODEL

    def nrm(i, shape, scale):
        return scale * jax.random.normal(keys[i], shape, f32)

    def gain(i, shape):
        return 1.0 + nrm(i, shape, 0.02)

    u = jax.random.uniform(keys[39], (DEPTH, 2, W_GROUP), f32, 0.9, 0.999)
    s = u ** (1.0 / RG_C)
    rg_lambda = jnp.log(s) - jnp.log1p(-s)
    return {
        "x": nrm(0, (BATCH, SEQ, D), 1.0),
        "c": nrm(1, (BATCH, D), 1.0),
        "ctx": nrm(2, (BATCH, CTX_LEN, D), 1.0),
        "c_ctx": nrm(3, (D,), 1.0),
        "w_mod": nrm(4, (DEPTH, D, 6 * D), 0.5 * D ** -0.5),
        "b_mod": nrm(5, (DEPTH, 6 * D), 0.01),
        "norm1_g": gain(6, (DEPTH, D)),
        "norm2_g": gain(7, (DEPTH, D)),
        "w_in": nrm(8, (DEPTH, D, D_IN), D ** -0.5),
        "b_in": nrm(9, (DEPTH, D_IN), 0.01),
        "conv_a_w": nrm(10, (DEPTH, 2, RG_CONV, W_GROUP), RG_CONV ** -0.5),
        "conv_a_b": nrm(11, (DEPTH, 2, W_GROUP), 0.01),
        "w_rg_r": nrm(12, (DEPTH, 2, N_SUB, D_SUB, D_SUB), D_SUB ** -0.5),
        "b_rg_r": nrm(13, (DEPTH, 2, N_SUB, D_SUB), 0.01),
        "w_rg_i": nrm(14, (DEPTH, 2, N_SUB, D_SUB, D_SUB), D_SUB ** -0.5),
        "b_rg_i": nrm(15, (DEPTH, 2, N_SUB, D_SUB), 0.01),
        "rg_lambda": rg_lambda,
        "w_pool": nrm(16, (DEPTH, N_SUB, D_SUB, D_SUB), D_SUB ** -0.5),
        "b_pool": nrm(17, (DEPTH, N_SUB, D_SUB), 0.01),
        "pool_scale": gain(18, (DEPTH, W_GROUP)),
        "w_four": nrm(19, (DEPTH, N_SUB, D_SUB, D_SUB), D_SUB ** -0.5),
        "b_four": nrm(20, (DEPTH, N_SUB, D_SUB), 0.01),
        "conv_d_w": nrm(21, (DEPTH, CONF_KERNEL, W_GROUP), CONF_KERNEL ** -0.5),
        "conv_d_b": nrm(22, (DEPTH, W_GROUP), 0.01),
        "ln_d_g": gain(23, (DEPTH, W_GROUP)),
        "ln_d_b": nrm(24, (DEPTH, W_GROUP), 0.01),
        "w_pw": nrm(25, (DEPTH, W_GROUP, W_GROUP), W_GROUP ** -0.5),
        "b_pw": nrm(26, (DEPTH, W_GROUP), 0.01),
        "w_out": nrm(27, (DEPTH, D_MIX, D), D_MIX ** -0.5),
        "b_out": nrm(28, (DEPTH, D), 0.01),
        "w_router": nrm(29, (DEPTH, D, N_EXPERTS), D ** -0.5),
        "b_router": nrm(30, (DEPTH, N_EXPERTS), 0.01),
        "w_gu": nrm(31, (DEPTH, N_EXPERTS, D, 2 * D_FF), D ** -0.5),
        "b_gu": nrm(32, (DEPTH, N_EXPERTS, 2 * D_FF), 0.01),
        "w_down": nrm(33, (DEPTH, N_EXPERTS, D_FF, D), D_FF ** -0.5),
        "b_down": nrm(34, (DEPTH, N_EXPERTS, D), 0.01),
        "final_norm_g": gain(35, (D,)),
    }


def reference(x, c, ctx, c_ctx, w_mod, b_mod, norm1_g, norm2_g, w_in, b_in, conv_a_w, conv_a_b, w_rg_r, b_rg_r,
              w_rg_i, b_rg_i, rg_lambda, w_pool, b_pool, pool_scale, w_four, b_four, conv_d_w, conv_d_b, ln_d_g,
              ln_d_b, w_pw, b_pw, w_out, b_out, w_router, b_router, w_gu, b_gu, w_down, b_down, final_norm_g):
    bn, L, D = x.shape
    Lc = ctx.shape[1]
    x = x + pos_embed_2d(L, x.dtype)[None]
    sc = jax.nn.silu(c)
    scc = jax.nn.silu(c_ctx)
    for l in range(DEPTH):
        last = l == DEPTH - 1
        mod = sc @ w_mod[l] + b_mod[l]
        mod_c = scc @ w_mod[l] + b_mod[l]
        sh1, sc1, g1, sh2, sc2, g2 = jnp.split(mod[:, None, :], 6, axis=-1)
        csh1, csc1, cg1, csh2, csc2, cg2 = jnp.split(mod_c, 6, axis=-1)
        mix_p = (w_in[l], b_in[l], conv_a_w[l], conv_a_b[l], w_rg_r[l], b_rg_r[l], w_rg_i[l], b_rg_i[l],
                 rg_lambda[l], w_pool[l], b_pool[l], pool_scale[l], w_four[l], b_four[l], conv_d_w[l],
                 conv_d_b[l], ln_d_g[l], ln_d_b[l], w_pw[l], b_pw[l], w_out[l], b_out[l])
        moe_p = (w_router[l], b_router[l], w_gu[l], b_gu[l], w_down[l], b_down[l])
        h0 = jnp.zeros((bn, W_GROUP), jnp.float32)
        u_ctx = modulate(rmsnorm(ctx, norm1_g[l]), csh1, csc1)
        y_ctx, hf, hb = token_mixers(u_ctx, h0, h0, *mix_p)
        u_x = modulate(rmsnorm(x, norm1_g[l]), sh1, sc1)
        y_x, _, _ = token_mixers(u_x, hf, hb, *mix_p)
        x = x + g1 * y_x
        h_x = modulate(rmsnorm(x, norm2_g[l]), sh2, sc2)
        if not last:
            ctx = ctx + cg1 * y_ctx
            h_c = modulate(rmsnorm(ctx, norm2_g[l]), csh2, csc2)
            y_all = moe(jnp.concatenate([h_c.reshape(-1, D), h_x.reshape(-1, D)], axis=0), *moe_p)
            ctx = ctx + cg2 * y_all[:bn * Lc].reshape(bn, Lc, D)
            x = x + g2 * y_all[bn * Lc:].reshape(bn, L, D)
        else:
            x = x + g2 * moe(h_x.reshape(-1, D), *moe_p).reshape(bn, L, D)
    return rmsnorm(x, final_norm_g)
```

```python
import functools
import math

import jax
import jax.numpy as jnp
from jax import lax
from jax.experimental import pallas as pl
from jax.experimental.pallas import tpu as pltpu

F32 = jnp.float32
BF16 = jnp.bfloat16

D_MODEL = 1024
DEPTH = 2
GRID_W = 64
W_GROUP = 256
N_SUB = 4
D_SUB = 64
D_IN = 6 * W_GROUP
RG_CONV = 4
RG_C = 8.0
CONF_KERNEL = 31
N_EXPERTS = 32
TOP_K = 4
D_FF = D_MODEL
SWIGLU_LIMIT = 7.0
SWIGLU_ALPHA = 1.702
EPS = 1e-6

SUBLANES = 8
VMEM_LIMIT_BYTES = 56 * 1024 * 1024
MOD_ROWS = 16
CTX_ROW = 8
RG_HALO = 8 * SUBLANES
POOL_HALO = 8 * SUBLANES
CONF_HALO = 16 * SUBLANES
MOE_TILE = 512
TOK_TILE = 256
ROUTE_TILE = 512


def _cparams(sem):
    return pltpu.CompilerParams(dimension_semantics=sem, vmem_limit_bytes=VMEM_LIMIT_BYTES)


def _full(shape):
    nd = len(shape)
    return pl.BlockSpec(shape, lambda *_: (0,) * nd)


def _dot(a, b):
    return jnp.dot(a, b, preferred_element_type=F32)


def _split_bf16(v):
    hi = v.astype(BF16)
    lo = (v - hi.astype(F32)).astype(BF16)
    return hi, lo


def _mod_body(c_ref, w_ref, b_ref, o_ref):
    c = c_ref[...]
    s = c * jax.nn.sigmoid(c)
    o_ref[...] = jnp.dot(s, w_ref[...], precision=lax.Precision.HIGHEST,
                         preferred_element_type=F32) + b_ref[...]


def _modulation(c_rows, w_mod, b_mod):
    tn = 1536
    n6 = 6 * D_MODEL
    return pl.pallas_call(
        _mod_body,
        grid=(DEPTH, n6 // tn),
        in_specs=[_full((MOD_ROWS, D_MODEL)),
                  pl.BlockSpec((None, D_MODEL, tn), lambda l, j: (l, 0, j)),
                  pl.BlockSpec((None, 1, tn), lambda l, j: (l, 0, j))],
        out_specs=pl.BlockSpec((None, MOD_ROWS, tn), lambda l, j: (l, 0, j)),
        out_shape=jax.ShapeDtypeStruct((DEPTH, MOD_ROWS, n6), F32),
        compiler_params=_cparams(("parallel", "parallel")),
        name="modulation",
    )(c_rows, w_mod, b_mod.reshape(DEPTH, 1, n6))


def _mod_spec(chunk, ctx):
    if ctx:
        return pl.BlockSpec((None, 1, D_MODEL), lambda b, i: (CTX_ROW * 6 + chunk, 0, 0))
    return pl.BlockSpec((None, 1, D_MODEL), lambda b, i: (b * 6 + chunk, 0, 0))


def _rms_mod(x, g, shift, scale):
    y = x * lax.rsqrt(jnp.mean(x * x, axis=-1, keepdims=True) + EPS) * g
    return y * (1.0 + scale) + shift


def _inproj_body(*refs, add_pos):
    if add_pos:
        x_ref, pos_ref, sh_ref, sc_ref, g_ref, w_ref, b_ref, xa_ref, ga_ref, xb_ref, xc_ref, xd_ref = refs
        x = x_ref[...] + pos_ref[...]
    else:
        x_ref, sh_ref, sc_ref, g_ref, w_ref, b_ref, xa_ref, ga_ref, xb_ref, xc_ref, xd_ref = refs
        x = x_ref[...]
    u = _rms_mod(x, g_ref[...], sh_ref[...], sc_ref[...])
    p = _dot(u.astype(BF16), w_ref[...]) + b_ref[...]
    xa_ref[...] = p[:, 0:256]
    ga_ref[...] = p[:, 256:512]
    xb_ref[...] = p[:, 512:768]
    xc_ref[...] = p[:, 768:1024].astype(BF16)
    xd_ref[...] = p[:, 1024:1536]


def _in_projection(x, pos, mod3, norm_g, w_in_bf, b_in, ctx):
    bn, L, D = x.shape
    tl = min(L, 512)
    add_pos = pos is not None
    in_specs = [pl.BlockSpec((None, tl, D), lambda b, i: (b, i, 0))]
    args = [x]
    if add_pos:
        in_specs.append(pl.BlockSpec((tl, D), lambda b, i: (i, 0)))
        args.append(pos)
    in_specs += [_mod_spec(0, ctx), _mod_spec(1, ctx), _full((1, D)), _full((D, D_IN)), _full((1, D_IN))]
    args += [mod3, mod3, norm_g.reshape(1, D), w_in_bf, b_in.reshape(1, D_IN)]
    tm_spec = lambda w: pl.BlockSpec((tl, w), lambda b, i: (i, b))
    out_shape = [jax.ShapeDtypeStruct((L, bn * 256), F32)] * 3 + [
        jax.ShapeDtypeStruct((L, bn * 256), BF16), jax.ShapeDtypeStruct((L, bn * 512), F32)]
    return pl.pallas_call(
        functools.partial(_inproj_body, add_pos=add_pos),
        grid=(bn, L // tl),
        in_specs=in_specs,
        out_specs=[tm_spec(256), tm_spec(256), tm_spec(256), tm_spec(256), tm_spec(512)],
        out_shape=out_shape,
        compiler_params=_cparams(("parallel", "parallel")),
        name="in_projection",
    )(*args)


def _rg_gates(xc, wg, bg, lam):
    g = _dot(xc.astype(BF16), wg) + bg
    r = jax.nn.sigmoid(g[:, :W_GROUP])
    gi = jax.nn.sigmoid(g[:, W_GROUP:])
    z = -lam
    softplus = jnp.maximum(z, 0.0) + jnp.log1p(jnp.exp(-jnp.abs(z)))
    log_a = (-RG_C) * r * softplus
    a = jnp.exp(log_a)
    b = jnp.sqrt(-jnp.tanh(log_a) * (a * a + 1.0)) * (gi * xc)
    return a, b


def _rg_body(xf_ref, xfh_ref, xr_ref, xrh_ref, cw_ref, cb_ref, wg_ref, bg_ref, lam_ref, h0_ref,
             yf_ref, yb_ref, hfin_ref, af_ref, ab_ref, hc_ref, *, n, tt):
    i = pl.program_id(0)
    tr = tt * SUBLANES
    keep = RG_HALO - (RG_CONV - 1) * SUBLANES

    @pl.when(i == 0)
    def _():
        hc_ref[...] = h0_ref[...]

    halo = jnp.where(i > 0, xfh_ref[...], 0.0)
    ext = jnp.concatenate([halo[keep:], xf_ref[...]], axis=0)
    xc = cb_ref[0]
    for k in range(RG_CONV):
        xc = xc + cw_ref[0, k:k + 1, :] * ext[k * SUBLANES:k * SUBLANES + tr]
    a, b = _rg_gates(xc, wg_ref[0], bg_ref[0], lam_ref[0])
    af_ref[...] = a
    yf_ref[...] = b

    halo = jnp.where(i > 0, xrh_ref[...], 0.0)
    ext = jnp.concatenate([xr_ref[...], halo[:(RG_CONV - 1) * SUBLANES]], axis=0)
    xc = cb_ref[1]
    for k in range(RG_CONV):
        o = (RG_CONV - 1 - k) * SUBLANES
        xc = xc + cw_ref[1, k:k + 1, :] * ext[o:o + tr]
    a, b = _rg_gates(xc, wg_ref[1], bg_ref[1], lam_ref[1])
    ab_ref[...] = a
    yb_ref[...] = b

    def step(t, carry):
        hf, hb = carry
        rf = pl.multiple_of(t * SUBLANES, SUBLANES)
        hf = af_ref[pl.ds(rf, SUBLANES), :] * hf + yf_ref[pl.ds(rf, SUBLANES), :]
        yf_ref[pl.ds(rf, SUBLANES), :] = hf
        rb = pl.multiple_of((tt - 1 - t) * SUBLANES, SUBLANES)
        hb = ab_ref[pl.ds(rb, SUBLANES), :] * hb + yb_ref[pl.ds(rb, SUBLANES), :]
        yb_ref[pl.ds(rb, SUBLANES), :] = hb
        return hf, hb

    hf, hb = lax.fori_loop(0, tt, step, (hc_ref[0], hc_ref[1]), unroll=8)
    hc_ref[0] = hf
    hc_ref[1] = hb
    hfin_ref[0] = hf
    hfin_ref[1] = hb


def _rglru(xa2, conv_w, conv_b, wg_bf, bg, lam, h0, L):
    tt = min(L, 256)
    n = L // tt
    tr = tt * SUBLANES
    per = tr // RG_HALO
    last_halo = L * SUBLANES // RG_HALO - 1
    row = lambda i: (i, 0)
    rev = lambda i: (n - 1 - i, 0)
    in_specs = [
        pl.BlockSpec((tr, W_GROUP), row),
        pl.BlockSpec((RG_HALO, W_GROUP), lambda i: (jnp.maximum(i * per - 1, 0), 0)),
        pl.BlockSpec((tr, W_GROUP), rev),
        pl.BlockSpec((RG_HALO, W_GROUP), lambda i: (jnp.minimum((n - i) * per, last_halo), 0)),
        _full((2, RG_CONV, W_GROUP)), _full((2, 1, W_GROUP)), _full((2, W_GROUP, 2 * W_GROUP)),
        _full((2, 1, 2 * W_GROUP)), _full((2, 1, W_GROUP)), _full((2, SUBLANES, W_GROUP)),
    ]
    return pl.pallas_call(
        functools.partial(_rg_body, n=n, tt=tt),
        grid=(n,),
        in_specs=in_specs,
        out_specs=[pl.BlockSpec((tr, W_GROUP), row), pl.BlockSpec((tr, W_GROUP), rev),
                   _full((2, SUBLANES, W_GROUP))],
        out_shape=[jax.ShapeDtypeStruct((L * SUBLANES, W_GROUP), F32)] * 2
        + [jax.ShapeDtypeStruct((2, SUBLANES, W_GROUP), F32)],
        scratch_shapes=[pltpu.VMEM((tr, W_GROUP), F32), pltpu.VMEM((tr, W_GROUP), F32),
                        pltpu.VMEM((2, SUBLANES, W_GROUP), F32)],
        compiler_params=_cparams(("arbitrary",)),
        name="rglru",
    )(xa2, xa2, xa2, xa2, conv_w, conv_b.reshape(2, 1, W_GROUP), wg_bf, bg, lam.reshape(2, 1, W_GROUP), h0)


def _pool_body(xm_ref, xp_ref, xn_ref, w_ref, b_ref, s_ref, o_ref, *, n, tt, L):
    i = pl.program_id(0)
    tr = tt * SUBLANES
    S = SUBLANES
    xm = xm_ref[...]
    prev = jnp.where(i > 0, xp_ref[...], 0.0)
    nxt = jnp.where(i < n - 1, xn_ref[...], 0.0)
    xe = jnp.concatenate([prev, xm, nxt], axis=0)
    e = xe.shape[0]
    p2 = xe[S:e] + xe[0:e - S]
    n4 = (tt + 13) * S
    p4 = p2[0:n4] + p2[2 * S:2 * S + n4]
    n8 = (tt + 9) * S
    p8 = p4[0:n8] + p4[4 * S:4 * S + n8]
    s16 = p8[0:tr] + p8[8 * S:8 * S + tr]
    s2 = p2[7 * S:7 * S + tr]
    s4 = p4[6 * S:6 * S + tr]
    s8 = p8[4 * S:4 * S + tr]
    grp = lax.broadcasted_iota(jnp.int32, (1, W_GROUP), 1) // D_SUB
    half = jnp.left_shift(1, grp)
    t = i * tt + lax.broadcasted_iota(jnp.int32, (tr, 1), 0) // S
    cnt = (jnp.minimum(t + half, L) - jnp.maximum(t - half, 0)).astype(F32)
    s = jnp.where(grp == 0, s2, jnp.where(grp == 1, s4, jnp.where(grp == 2, s8, s16)))
    pooled = s / cnt - xm
    y = _dot(pooled.astype(BF16), w_ref[...]) + b_ref[...]
    o_ref[...] = y * s_ref[...]


def _pool_mixer(xb2, w_bd_bf, b, scale, L):
    tt = min(L, 256)
    n = L // tt
    tr = tt * SUBLANES
    per = tr // POOL_HALO
    last_halo = L * SUBLANES // POOL_HALO - 1
    return pl.pallas_call(
        functools.partial(_pool_body, n=n, tt=tt, L=L),
        grid=(n,),
        in_specs=[pl.BlockSpec((tr, W_GROUP), lambda i: (i, 0)),
                  pl.BlockSpec((POOL_HALO, W_GROUP), lambda i: (jnp.maximum(i * per - 1, 0), 0)),
                  pl.BlockSpec((POOL_HALO, W_GROUP), lambda i: (jnp.minimum((i + 1) * per, last_halo), 0)),
                  _full((W_GROUP, W_GROUP)), _full((1, W_GROUP)), _full((1, W_GROUP))],
        out_specs=pl.BlockSpec((tr, W_GROUP), lambda i: (i, 0)),
        out_shape=jax.ShapeDtypeStruct((L * SUBLANES, W_GROUP), F32),
        compiler_params=_cparams(("parallel",)),
        name="pool_mixer",
    )(xb2, xb2, xb2, w_bd_bf, b.reshape(1, W_GROUP), scale.reshape(1, W_GROUP))


def _fourier_body(c_ref, s_ref, x_ref, cc_ref, sc_ref, w_ref, b_ref, o_ref):
    x = x_ref[...]
    z1 = _dot(c_ref[...], x)
    z2 = _dot(s_ref[...], x)
    for j in range(x.shape[1] // W_GROUP):
        sl = slice(j * W_GROUP, (j + 1) * W_GROUP)
        a_hi, a_lo = _split_bf16(z1[:, sl])
        b_hi, b_lo = _split_bf16(z2[:, sl])
        f = (_dot(a_hi, cc_ref[...]) + _dot(a_lo, cc_ref[...])) - (_dot(b_hi, sc_ref[...]) + _dot(b_lo, sc_ref[...]))
        o_ref[:, sl] = _dot(f.astype(BF16), w_ref[...]) + b_ref[...]


def _fourier_mixer(xc_tm, cl, sl, cc, sc, w_bd_bf, b, L):
    ncol = xc_tm.shape[1]
    nb = min(ncol, 1024)
    tk = min(L, 256)
    return pl.pallas_call(
        _fourier_body,
        grid=(ncol // nb, L // tk),
        in_specs=[pl.BlockSpec((tk, L), lambda j, k: (k, 0)),
                  pl.BlockSpec((tk, L), lambda j, k: (k, 0)),
                  pl.BlockSpec((L, nb), lambda j, k: (0, j)),
                  _full((W_GROUP, W_GROUP)), _full((W_GROUP, W_GROUP)), _full((W_GROUP, W_GROUP)),
                  _full((1, W_GROUP))],
        out_specs=pl.BlockSpec((tk, nb), lambda j, k: (k, j)),
        out_shape=jax.ShapeDtypeStruct((L, ncol), F32),
        compiler_params=_cparams(("parallel", "parallel")),
        name="fourier_mixer",
    )(cl, sl, xc_tm, cc, sc, w_bd_bf, b.reshape(1, W_GROUP))


def _dft_matrices(L):
    k = jnp.arange(L, dtype=jnp.int32)
    ang = ((k[:, None] * k[None, :]) % L).astype(F32) * (2.0 * math.pi / L)
    scale = 1.0 / math.sqrt(L)
    return jnp.cos(ang) * scale, jnp.sin(ang) * scale


def _block_diag(w):
    g, a, b = w.shape
    eye = jnp.eye(g, dtype=w.dtype)
    return (eye[:, None, :, None] * w[:, :, None, :]).reshape(g * a, g * b)


CONF_CHUNK = 64


def _conformer_body(xm_ref, xp_ref, xn_ref, cw_ref, cb_ref, lg_ref, lb_ref, avg_ref, w_ref, b_ref,
                    o_ref, v_ref, c_ref, *, n, tt):
    i = pl.program_id(0)
    tr = tt * SUBLANES
    H = CONF_HALO

    def glu(v):
        return v[:, :W_GROUP] * jax.nn.sigmoid(v[:, W_GROUP:])

    v_ref[0:H] = jnp.where(i > 0, glu(xp_ref[...]), 0.0)
    v_ref[H:H + tr] = glu(xm_ref[...])
    v_ref[H + tr:H + tr + H] = jnp.where(i < n - 1, glu(xn_ref[...]), 0.0)

    def chunk(c, carry):
        r0 = pl.multiple_of(c * CONF_CHUNK, CONF_CHUNK)
        acc = jnp.broadcast_to(cb_ref[...], (CONF_CHUNK, W_GROUP))
        for k in range(CONF_KERNEL):
            acc = acc + cw_ref[k:k + 1, :] * v_ref[pl.ds(r0 + (k + 1) * SUBLANES, CONF_CHUNK), :]
        c_ref[pl.ds(r0, CONF_CHUNK), :] = acc
        return carry

    lax.fori_loop(0, tr // CONF_CHUNK, chunk, 0)

    v = c_ref[...]
    avg = avg_ref[...]
    v_hi, v_lo = _split_bf16(v)
    mu = _dot(v_hi, avg) + _dot(v_lo, avg)
    d = v - mu
    q_hi, q_lo = _split_bf16(d * d)
    var = _dot(q_hi, avg) + _dot(q_lo, avg)
    vn = d * lax.rsqrt(var + EPS) * lg_ref[...] + lb_ref[...]
    act = vn * jax.nn.sigmoid(vn)
    o_ref[...] = _dot(act.astype(BF16), w_ref[...]) + b_ref[...]


def _conformer(xd2, conv_w, conv_b, ln_g, ln_b, avg_bf, w_pw_bf, b_pw, L):
    tt = min(L, 256)
    n = L // tt
    tr = tt * SUBLANES
    per = tr // CONF_HALO
    last_halo = L * SUBLANES // CONF_HALO - 1
    vec = lambda a: a.reshape(1, W_GROUP)
    return pl.pallas_call(
        functools.partial(_conformer_body, n=n, tt=tt),
        grid=(n,),
        in_specs=[pl.BlockSpec((tr, 2 * W_GROUP), lambda i: (i, 0)),
                  pl.BlockSpec((CONF_HALO, 2 * W_GROUP), lambda i: (jnp.maximum(i * per - 1, 0), 0)),
                  pl.BlockSpec((CONF_HALO, 2 * W_GROUP), lambda i: (jnp.minimum((i + 1) * per, last_halo), 0)),
                  _full((CONF_KERNEL, W_GROUP)), _full((1, W_GROUP)), _full((1, W_GROUP)), _full((1, W_GROUP)),
                  _full((W_GROUP, W_GROUP)), _full((W_GROUP, W_GROUP)), _full((1, W_GROUP))],
        out_specs=pl.BlockSpec((tr, W_GROUP), lambda i: (i, 0)),
        out_shape=jax.ShapeDtypeStruct((L * SUBLANES, W_GROUP), F32),
        scratch_shapes=[pltpu.VMEM((tr + 2 * CONF_HALO, W_GROUP), F32), pltpu.VMEM((tr, W_GROUP), F32)],
        compiler_params=_cparams(("parallel",)),
        name="conformer",
    )(xd2, xd2, xd2, conv_w, vec(conv_b), vec(ln_g), vec(ln_b), avg_bf, w_pw_bf, vec(b_pw))


def _gelu_tanh(x):
    return 0.5 * x * (1.0 + jnp.tanh(math.sqrt(2.0 / math.pi) * (x + 0.044715 * (x * x * x))))


def _pack_bf16_pairs(h_bf):
    u = pltpu.bitcast(h_bf.astype(F32), jnp.uint32)
    half = h_bf.shape[1] // 2
    return (u[:, :half] & jnp.uint32(0xFFFF0000)) | (u[:, half:] >> 16)


def _unpack_bf16_pairs(p):
    hi = pltpu.bitcast(p & jnp.uint32(0xFFFF0000), F32).astype(BF16)
    lo = pltpu.bitcast(p << 16, F32).astype(BF16)
    return hi, lo


def _outproj_body(*refs, add_pos):
    if add_pos:
        (x_ref, pos_ref, yf_ref, yb_ref, ga_ref, yp_ref, yc_ref, yd_ref, g1_ref, sh_ref, sc_ref, ng_ref,
         wo_ref, bo_ref, wr_ref, br_ref, xo_ref, hp_ref, lg_ref) = refs
        x = x_ref[...] + pos_ref[...]
    else:
        (x_ref, yf_ref, yb_ref, ga_ref, yp_ref, yc_ref, yd_ref, g1_ref, sh_ref, sc_ref, ng_ref,
         wo_ref, bo_ref, wr_ref, br_ref, xo_ref, hp_ref, lg_ref) = refs
        x = x_ref[...]
    ya = (yf_ref[...] + yb_ref[...]) * _gelu_tanh(ga_ref[...])
    ycat = jnp.concatenate([ya, yp_ref[...], yc_ref[...], yd_ref[...]], axis=1).astype(BF16)
    y = _dot(ycat, wo_ref[...]) + bo_ref[...]
    xn = x + g1_ref[...] * y
    xo_ref[...] = xn
    h = _rms_mod(xn, ng_ref[...], sh_ref[...], sc_ref[...])
    h_hi, h_lo = _split_bf16(h)
    nt = (((1,), (1,)), ((), ()))
    wr = wr_ref[...]
    acc = lax.dot_general(wr, h_hi, nt, preferred_element_type=F32)
    acc = acc + lax.dot_general(wr, h_lo, nt, preferred_element_type=F32)
    lg_ref[...] = acc[:N_EXPERTS] + acc[N_EXPERTS:] + br_ref[...]
    hp_ref[...] = _pack_bf16_pairs(h_hi)


def _out_projection(x, pos, mix, mod3, norm_g, w_out_bf, b_out, wr2_bf, b_router, ctx):
    bn, L, D = x.shape
    tl = min(L, 256)
    nl = L // tl
    add_pos = pos is not None
    in_specs = [pl.BlockSpec((None, tl, D), lambda b, i: (b, i, 0))]
    args = [x]
    if add_pos:
        in_specs.append(pl.BlockSpec((tl, D), lambda b, i: (i, 0)))
        args.append(pos)
    in_specs += [pl.BlockSpec((tl, W_GROUP), lambda b, i: (i, b))] * 6
    args += list(mix)
    in_specs += [_mod_spec(2, ctx), _mod_spec(3, ctx), _mod_spec(4, ctx), _full((1, D)), _full((D, D)),
                 _full((1, D)), _full((2 * N_EXPERTS, D)), _full((N_EXPERTS, 1))]
    args += [mod3, mod3, mod3, norm_g.reshape(1, D), w_out_bf, b_out.reshape(1, D), wr2_bf,
             b_router.reshape(N_EXPERTS, 1)]
    T = bn * L
    return pl.pallas_call(
        functools.partial(_outproj_body, add_pos=add_pos),
        grid=(bn, nl),
        in_specs=in_specs,
        out_specs=[pl.BlockSpec((None, tl, D), lambda b, i: (b, i, 0)),
                   pl.BlockSpec((tl, D // 2), lambda b, i: (b * nl + i, 0)),
                   pl.BlockSpec((N_EXPERTS, tl), lambda b, i: (0, b * nl + i))],
        out_shape=[jax.ShapeDtypeStruct((bn, L, D), F32),
                   jax.ShapeDtypeStruct((T, D // 2), jnp.uint32),
                   jax.ShapeDtypeStruct((N_EXPERTS, T), F32)],
        compiler_params=_cparams(("parallel", "parallel")),
        name="out_projection",
    )(*args)


def _top4(v):
    eid = lax.broadcasted_iota(jnp.int32, v.shape, 0)
    out = []
    work = v
    for _ in range(TOP_K):
        m = jnp.max(work, axis=0, keepdims=True)
        idx = jnp.min(jnp.where(work == m, eid, N_EXPERTS), axis=0, keepdims=True)
        oh = eid == idx
        out.append((m, oh))
        work = jnp.where(oh, -jnp.inf, work)
    return out


def _count_body(lg_ref, cnt_ref):
    @pl.when(pl.program_id(0) == 0)
    def _():
        cnt_ref[...] = jnp.zeros_like(cnt_ref)

    sel = jnp.zeros(lg_ref.shape, F32)
    for _, oh in _top4(lg_ref[...]):
        sel = sel + oh.astype(F32)
    cnt_ref[...] += jnp.sum(sel, axis=1, keepdims=True)


def _route_body(lg_ref, ps_ref, tri_ref, dest_ref, gate_ref, carry_ref):
    @pl.when(pl.program_id(0) == 0)
    def _():
        carry_ref[...] = jnp.zeros_like(carry_ref)

    top = _top4(lg_ref[...])
    sel = jnp.zeros(lg_ref.shape, F32)
    for _, oh in top:
        sel = sel + oh.astype(F32)
    before = _dot(sel.astype(BF16), tri_ref[...]) + carry_ref[...] + ps_ref[...]
    m0 = top[0][0]
    es = [jnp.exp(m - m0) for m, _ in top]
    den = es[0] + es[1] + es[2] + es[3]
    for k, (_, oh) in enumerate(top):
        dest_ref[k:k + 1, :] = jnp.sum(jnp.where(oh, before, 0.0), axis=0, keepdims=True).astype(jnp.int32)
        gate_ref[k:k + 1, :] = es[k] / den
    carry_ref[...] += jnp.sum(sel, axis=1, keepdims=True)


def _routing(logits_t):
    E, T = logits_t.shape
    tt = ROUTE_TILE
    nt = T // tt
    counts = pl.pallas_call(
        _count_body,
        grid=(nt,),
        in_specs=[pl.BlockSpec((E, tt), lambda i: (0, i))],
        out_specs=_full((E, 1)),
        out_shape=jax.ShapeDtypeStruct((E, 1), F32),
        compiler_params=_cparams(("arbitrary",)),
        name="route_count",
    )(logits_t)
    cnt = counts[:, 0].astype(jnp.int32)
    padded = ((cnt + MOE_TILE - 1) // MOE_TILE) * MOE_TILE
    pend = jnp.cumsum(padded)
    pstart = pend - padded
    n_tiles = -(-(T * TOP_K) // MOE_TILE) + N_EXPERTS
    tile_e = jnp.minimum(jnp.searchsorted(pend, jnp.arange(n_tiles, dtype=jnp.int32) * MOE_TILE, side='right'),
                         N_EXPERTS - 1).astype(jnp.int32)
    n_used = (pend[-1] // MOE_TILE).astype(jnp.int32).reshape(1)
    tri = (jnp.arange(tt)[:, None] < jnp.arange(tt)[None, :]).astype(BF16)
    dest, gates = pl.pallas_call(
        _route_body,
        grid=(nt,),
        in_specs=[pl.BlockSpec((E, tt), lambda i: (0, i)), _full((E, 1)), _full((tt, tt))],
        out_specs=[pl.BlockSpec((TOP_K, tt), lambda i: (0, i)), pl.BlockSpec((TOP_K, tt), lambda i: (0, i))],
        out_shape=[jax.ShapeDtypeStruct((TOP_K, T), jnp.int32), jax.ShapeDtypeStruct((TOP_K, T), F32)],
        scratch_shapes=[pltpu.VMEM((E, 1), F32)],
        compiler_params=_cparams(("arbitrary",)),
        name="route_assign",
    )(logits_t, pstart.astype(F32).reshape(E, 1), tri)
    return dest, gates, tile_e, n_used, n_tiles


def _row_copy(src_ref, s, dst_ref, d, sem):
    return pltpu.make_async_copy(src_ref.at[pl.ds(s, 1)], dst_ref.at[pl.ds(d, 1)], sem)


def _dispatch_body(dest_ref, hp_ref, xs_in_ref, xs_ref, sem):
    del xs_in_ref
    tl = hp_ref.shape[0]

    def issue(r, c):
        for k in range(TOP_K):
            _row_copy(hp_ref, r, xs_ref, dest_ref[k * tl + r], sem).start()
        return c

    lax.fori_loop(0, tl, issue, 0)

    def drain(r, c):
        for k in range(TOP_K):
            _row_copy(hp_ref, r, xs_ref, dest_ref[k * tl + r], sem).wait()
        return c

    lax.fori_loop(0, tl, drain, 0)


def _dispatch(hp, dest_tiles, n_rows):
    T, W = hp.shape
    tl = TOK_TILE
    xs0 = jnp.zeros((n_rows, W), jnp.uint32)
    return pl.pallas_call(
        _dispatch_body,
        grid=(T // tl,),
        in_specs=[pl.BlockSpec((TOP_K * tl,), lambda i: (i,), memory_space=pltpu.SMEM),
                  pl.BlockSpec((tl, W), lambda i: (i, 0)),
                  pl.BlockSpec(memory_space=pl.ANY)],
        out_specs=pl.BlockSpec(memory_space=pl.ANY),
        out_shape=jax.ShapeDtypeStruct((n_rows, W), jnp.uint32),
        scratch_shapes=[pltpu.SemaphoreType.DMA],
        input_output_aliases={2: 0},
        compiler_params=_cparams(("arbitrary",)),
        name="moe_dispatch",
    )(dest_tiles, hp, xs0)


def _expert_body(te_ref, nu_ref, xs_ref, wgu_ref, bgu_ref, wd_ref, bd_ref, ys_ref):
    del te_ref
    i = pl.program_id(0)

    @pl.when(i < nu_ref[0])
    def _():
        x_hi, x_lo = _unpack_bf16_pairs(xs_ref[...])
        half = D_MODEL // 2
        gu = _dot(x_hi, wgu_ref[:half, :]) + _dot(x_lo, wgu_ref[half:, :]) + bgu_ref[...]
        gt = jnp.minimum(gu[:, :D_FF], SWIGLU_LIMIT)
        up = jnp.clip(gu[:, D_FF:], -SWIGLU_LIMIT, SWIGLU_LIMIT)
        act = (up + 1.0) * (gt * jax.nn.sigmoid(SWIGLU_ALPHA * gt))
        ys_ref[...] = _dot(act.astype(BF16), wd_ref[...]) + bd_ref[...]

    @pl.when(i >= nu_ref[0])
    def _():
        ys_ref[...] = jnp.zeros_like(ys_ref)


def _experts(xs, tile_e, n_used, w_gu_bf, b_gu, w_down_bf, b_down):
    n_rows, W = xs.shape
    n_tiles = n_rows // MOE_TILE
    grid_spec = pltpu.PrefetchScalarGridSpec(
        num_scalar_prefetch=2,
        grid=(n_tiles,),
        in_specs=[pl.BlockSpec((MOE_TILE, W), lambda i, te, nu: (i, 0)),
                  pl.BlockSpec((None, D_MODEL, 2 * D_FF), lambda i, te, nu: (te[i], 0, 0)),
                  pl.BlockSpec((None, 1, 2 * D_FF), lambda i, te, nu: (te[i], 0, 0)),
                  pl.BlockSpec((None, D_FF, D_MODEL), lambda i, te, nu: (te[i], 0, 0)),
                  pl.BlockSpec((None, 1, D_MODEL), lambda i, te, nu: (te[i], 0, 0))],
        out_specs=pl.BlockSpec((MOE_TILE, D_MODEL), lambda i, te, nu: (i, 0)),
    )
    return pl.pallas_call(
        _expert_body,
        grid_spec=grid_spec,
        out_shape=jax.ShapeDtypeStruct((n_rows, D_MODEL), F32),
        compiler_params=_cparams(("arbitrary",)),
        name="moe_experts",
    )(tile_e, n_used, xs, w_gu_bf, b_gu.reshape(N_EXPERTS, 1, 2 * D_FF), w_down_bf,
      b_down.reshape(N_EXPERTS, 1, D_MODEL))


def _combine_body(dest_ref, x_ref, gate_ref, g2_ref, ng_ref, ys_ref, o_ref, buf_ref, sem, *, final_norm):
    tl = x_ref.shape[0]

    def issue(r, c):
        for k in range(TOP_K):
            _row_copy(ys_ref, dest_ref[k * tl + r], buf_ref.at[k], r, sem).start()
        return c

    lax.fori_loop(0, tl, issue, 0)

    def drain(r, c):
        for k in range(TOP_K):
            _row_copy(ys_ref, dest_ref[k * tl + r], buf_ref.at[k], r, sem).wait()
        return c

    lax.fori_loop(0, tl, drain, 0)

    g = gate_ref[...]
    acc = g[:, 0:1] * buf_ref[0]
    for k in range(1, TOP_K):
        acc = acc + g[:, k:k + 1] * buf_ref[k]
    xn = x_ref[...] + g2_ref[...] * acc
    if final_norm:
        xn = xn * lax.rsqrt(jnp.mean(xn * xn, axis=-1, keepdims=True) + EPS) * ng_ref[...]
    o_ref[...] = xn


def _combine(x, ys, dest_tiles, gates_t, tok_off, mod3, final_g, ctx):
    bn, L, D = x.shape
    tl = TOK_TILE
    nl = L // tl
    t0 = tok_off // tl
    final_norm = final_g is not None
    ng = final_g.reshape(1, D) if final_norm else jnp.ones((1, D), F32)
    return pl.pallas_call(
        functools.partial(_combine_body, final_norm=final_norm),
        grid=(bn, nl),
        in_specs=[pl.BlockSpec((TOP_K * tl,), lambda b, i: (t0 + b * nl + i,), memory_space=pltpu.SMEM),
                  pl.BlockSpec((None, tl, D), lambda b, i: (b, i, 0)),
                  pl.BlockSpec((tl, TOP_K), lambda b, i: (t0 + b * nl + i, 0)),
                  _mod_spec(5, ctx), _full((1, D)),
                  pl.BlockSpec(memory_space=pl.ANY)],
        out_specs=pl.BlockSpec((None, tl, D), lambda b, i: (b, i, 0)),
        out_shape=jax.ShapeDtypeStruct((bn, L, D), F32),
        scratch_shapes=[pltpu.VMEM((TOP_K, tl, D), F32), pltpu.SemaphoreType.DMA],
        compiler_params=_cparams(("arbitrary", "arbitrary")),
        name="moe_combine",
    )(dest_tiles, x, gates_t, mod3, ng, ys)


def _moe(hp, logits_t, w_gu_bf, b_gu, w_down_bf, b_down):
    T = hp.shape[0]
    dest, gates, tile_e, n_used, n_tiles = _routing(logits_t)
    nt = T // TOK_TILE
    dest_tiles = dest.reshape(TOP_K, nt, TOK_TILE).transpose(1, 0, 2).reshape(-1)
    xs = _dispatch(hp, dest_tiles, n_tiles * MOE_TILE)
    ys = _experts(xs, tile_e, n_used, w_gu_bf, b_gu, w_down_bf, b_down)
    return ys, dest_tiles, gates.T


def _token_mixers(x, pos, mod3, h0, p, consts, ctx, need_out):
    bn, L, D = x.shape
    R = L * bn
    xa, ga, xb, xc, xd = _in_projection(x, pos, mod3, p["norm1_g"], p["w_in"], p["b_in"], ctx)
    yf, yb, hfin = _rglru(xa.reshape(R, W_GROUP), p["conv_a_w"], p["conv_a_b"], p["wg"], p["bg"],
                          p["rg_lambda"], h0, L)
    if not need_out:
        return None, hfin
    yp = _pool_mixer(xb.reshape(R, W_GROUP), p["w_pool"], p["b_pool"], p["pool_scale"], L)
    cl, sl = consts["dft"][L]
    yc = _fourier_mixer(xc, cl, sl, consts["cc"], consts["sc"], p["w_four"], p["b_four"], L)
    yd = _conformer(xd.reshape(R, 2 * W_GROUP), p["conv_d_w"], p["conv_d_b"], p["ln_d_g"], p["ln_d_b"],
                    consts["avg"], p["w_pw"], p["b_pw"], L)
    tm = lambda a: a.reshape(L, bn * W_GROUP)
    return (tm(yf), tm(yb), ga, tm(yp), yc, tm(yd)), hfin


def _pos_embed(n_tokens):
    rows_n = n_tokens // GRID_W
    row = jnp.repeat(jnp.arange(rows_n), GRID_W).astype(F32)
    col = jnp.tile(jnp.arange(GRID_W), rows_n).astype(F32)
    q = D_MODEL // 4
    omega = 1.0 / (10000.0 ** (jnp.arange(q, dtype=F32) / q))

    def emb(v):
        ang = v[:, None] * omega[None, :]
        return jnp.concatenate([jnp.sin(ang), jnp.cos(ang)], axis=-1)

    return jnp.concatenate([emb(row), emb(col)], axis=-1)


def _layer_params(l, w_in, b_in, conv_a_w, conv_a_b, w_rg_r, b_rg_r, w_rg_i, b_rg_i, rg_lambda, w_pool, b_pool,
                  pool_scale, w_four, b_four, conv_d_w, conv_d_b, ln_d_g, ln_d_b, w_pw, b_pw, norm1_g):
    wg = jnp.stack([jnp.concatenate([_block_diag(w_rg_r[l, d]), _block_diag(w_rg_i[l, d])], axis=1)
                    for d in range(2)]).astype(BF16)
    bg = jnp.concatenate([b_rg_r[l].reshape(2, 1, W_GROUP), b_rg_i[l].reshape(2, 1, W_GROUP)], axis=-1)
    return dict(
        norm1_g=norm1_g[l], w_in=w_in[l].astype(BF16), b_in=b_in[l],
        conv_a_w=conv_a_w[l], conv_a_b=conv_a_b[l], wg=wg, bg=bg, rg_lambda=rg_lambda[l],
        w_pool=_block_diag(w_pool[l]).astype(BF16), b_pool=b_pool[l], pool_scale=pool_scale[l],
        w_four=_block_diag(w_four[l]).astype(BF16), b_four=b_four[l],
        conv_d_w=conv_d_w[l], conv_d_b=conv_d_b[l], ln_d_g=ln_d_g[l], ln_d_b=ln_d_b[l],
        w_pw=w_pw[l].astype(BF16), b_pw=b_pw[l])


def kernel(x, c, ctx, c_ctx, w_mod, b_mod, norm1_g, norm2_g, w_in, b_in, conv_a_w, conv_a_b, w_rg_r, b_rg_r,
           w_rg_i, b_rg_i, rg_lambda, w_pool, b_pool, pool_scale, w_four, b_four, conv_d_w, conv_d_b, ln_d_g,
           ln_d_b, w_pw, b_pw, w_out, b_out, w_router, b_router, w_gu, b_gu, w_down, b_down, final_norm_g):
    bn, L, D = x.shape
    Lc = ctx.shape[1]
    assert bn == SUBLANES and D == D_MODEL

    pos = _pos_embed(L)
    c_rows = jnp.zeros((MOD_ROWS, D), F32).at[:bn].set(c).at[CTX_ROW].set(c_ctx)
    mod = _modulation(c_rows, w_mod, b_mod)

    cc1, sc1 = _dft_matrices(D_SUB)
    eye = jnp.eye(N_SUB, dtype=F32)
    consts = dict(
        dft={n: tuple(m.astype(BF16) for m in _dft_matrices(n)) for n in sorted({L, Lc})},
        cc=jnp.kron(eye, cc1).astype(BF16), sc=jnp.kron(eye, sc1).astype(BF16),
        avg=jnp.kron(eye, jnp.full((D_SUB, D_SUB), 1.0 / D_SUB, F32)).astype(BF16))
    h_zero = jnp.zeros((2, SUBLANES, W_GROUP), F32)

    for l in range(DEPTH):
        last = l == DEPTH - 1
        p = _layer_params(l, w_in, b_in, conv_a_w, conv_a_b, w_rg_r, b_rg_r, w_rg_i, b_rg_i, rg_lambda, w_pool,
                          b_pool, pool_scale, w_four, b_four, conv_d_w, conv_d_b, ln_d_g, ln_d_b, w_pw, b_pw,
                          norm1_g)
        mod3 = mod[l].reshape(MOD_ROWS * 6, 1, D)
        w_out_bf = w_out[l].astype(BF16)
        wr_t = w_router[l].T
        wr_hi = wr_t.astype(BF16)
        wr2 = jnp.concatenate([wr_hi, (wr_t - wr_hi.astype(F32)).astype(BF16)], axis=0)
        w_gu_bf = w_gu[l].astype(BF16)
        w_down_bf = w_down[l].astype(BF16)
        x_pos = pos if l == 0 else None

        mix_c, h_ctx = _token_mixers(ctx, None, mod3, h_zero, p, consts, True, not last)
        mix_x, _ = _token_mixers(x, x_pos, mod3, h_ctx, p, consts, False, True)
        x, hp_x, lg_x = _out_projection(x, x_pos, mix_x, mod3, norm2_g[l], w_out_bf, b_out[l], wr2,
                                        b_router[l], False)
        if not last:
            ctx, hp_c, lg_c = _out_projection(ctx, None, mix_c, mod3, norm2_g[l], w_out_bf, b_out[l], wr2,
                                              b_router[l], True)
            hp = jnp.concatenate([hp_c, hp_x], axis=0)
            lg = jnp.concatenate([lg_c, lg_x], axis=1)
            ys, dest_tiles, gates_t = _moe(hp, lg, w_gu_bf, b_gu[l], w_down_bf, b_down[l])
            ctx = _combine(ctx, ys, dest_tiles, gates_t, 0, mod3, None, True)
            x = _combine(x, ys, dest_tiles, gates_t, bn * Lc, mod3, None, False)
        else:
            ys, dest_tiles, gates_t = _moe(hp_x, lg_x, w_gu_bf, b_gu[l], w_down_bf, b_down[l])
            x = _combine(x, ys, dest_tiles, gates_t, 0, mod3, final_norm_g, False)
    return x
```

```python
import functools
import math

import jax
import jax.numpy as jnp
from jax import lax
from jax.experimental import pallas as pl
from jax.experimental.pallas import tpu as pltpu

F32 = jnp.float32
BF16 = jnp.bfloat16

D_MODEL = 1024
DEPTH = 2
GRID_W = 64
W_GROUP = 256
N_SUB = 4
D_SUB = 64
D_IN = 6 * W_GROUP
RG_CONV = 4
RG_C = 8.0
CONF_KERNEL = 31
N_EXPERTS = 32
TOP_K = 4
D_FF = D_MODEL
SWIGLU_LIMIT = 7.0
SWIGLU_ALPHA = 1.702
EPS = 1e-6

SUBLANES = 8
VMEM_LIMIT_BYTES = 56 * 1024 * 1024
MOD_ROWS = 16
RG_HALO = 8 * SUBLANES
POOL_HALO = 8 * SUBLANES
CONF_HALO = 16 * SUBLANES
MOE_TILE = 512
TOK_TILE = 256
ROUTE_TILE = 512


def _cparams(sem):
    return pltpu.CompilerParams(dimension_semantics=sem, vmem_limit_bytes=VMEM_LIMIT_BYTES)


def _full(shape):
    nd = len(shape)
    return pl.BlockSpec(shape, lambda *_: (0,) * nd)


def _dot(a, b):
    return jnp.dot(a, b, preferred_element_type=F32)


def _split_bf16(v):
    hi = v.astype(BF16)
    lo = (v - hi.astype(F32)).astype(BF16)
    return hi, lo


def _mod_body(c_ref, w_ref, b_ref, o_ref):
    c = c_ref[...]
    s = c * jax.nn.sigmoid(c)
    o_ref[...] = jnp.dot(s, w_ref[...], precision=lax.Precision.HIGHEST,
                         preferred_element_type=F32) + b_ref[...]


def _modulation(c_rows, w_mod, b_mod):
    tn = 1536
    n6 = 6 * D_MODEL
    return pl.pallas_call(
        _mod_body,
        grid=(DEPTH, n6 // tn),
        in_specs=[_full((MOD_ROWS, D_MODEL)),
                  pl.BlockSpec((None, D_MODEL, tn), lambda l, j: (l, 0, j)),
                  pl.BlockSpec((None, 1, tn), lambda l, j: (l, 0, j))],
        out_specs=pl.BlockSpec((None, MOD_ROWS, tn), lambda l, j: (l, 0, j)),
        out_shape=jax.ShapeDtypeStruct((DEPTH, MOD_ROWS, n6), F32),
        compiler_params=_cparams(("parallel", "parallel")),
        name="modulation",
    )(c_rows, w_mod, b_mod.reshape(DEPTH, 1, n6))


def _mod_spec(chunk, ctx):
    return pl.BlockSpec((SUBLANES, D_MODEL), lambda i: (1 if ctx else 0, chunk))


def _scale_rows(v, m):
    r, d = v.shape
    return (v.reshape(r // SUBLANES, SUBLANES, d) * m[None]).reshape(r, d)


def _rms_mod(x, g, shift, scale):
    r, d = x.shape
    y = x * lax.rsqrt(jnp.mean(x * x, axis=-1, keepdims=True) + EPS) * g
    y3 = y.reshape(r // SUBLANES, SUBLANES, d)
    return (y3 * (1.0 + scale)[None] + shift[None]).reshape(r, d)


def _load_plus_pos(x_ref, pos_ref):
    s = SUBLANES
    return jnp.concatenate([x_ref[t * s:(t + 1) * s, :] + pos_ref[t:t + 1, :] for t in range(pos_ref.shape[0])],
                           axis=0)


def _inproj_body(*refs, add_pos):
    if add_pos:
        x_ref, pos_ref, sh_ref, sc_ref, g_ref, w_ref, b_ref, xa_ref, ga_ref, xb_ref, xc_ref, xd_ref = refs
        x = _load_plus_pos(x_ref, pos_ref)
    else:
        x_ref, sh_ref, sc_ref, g_ref, w_ref, b_ref, xa_ref, ga_ref, xb_ref, xc_ref, xd_ref = refs
        x = x_ref[...]
    u = _rms_mod(x, g_ref[...], sh_ref[...], sc_ref[...])
    p = _dot(u.astype(BF16), w_ref[...]) + b_ref[...]
    xa_ref[...] = p[:, 0:256]
    ga_ref[...] = p[:, 256:512]
    xb_ref[...] = p[:, 512:768]
    xc_ref[...] = p[:, 768:1024].astype(BF16)
    xd_ref[...] = p[:, 1024:1536]


def _in_projection(x, pos, mod, norm_g, w_in_bf, b_in, ctx):
    R, D = x.shape
    tr = min(R, 512)
    add_pos = pos is not None
    row = lambda w: pl.BlockSpec((tr, w), lambda i: (i, 0))
    in_specs = [row(D)]
    args = [x]
    if add_pos:
        in_specs.append(pl.BlockSpec((tr // SUBLANES, D), lambda i: (i, 0)))
        args.append(pos)
    in_specs += [_mod_spec(0, ctx), _mod_spec(1, ctx), _full((1, D)), _full((D, D_IN)), _full((1, D_IN))]
    args += [mod, mod, norm_g.reshape(1, D), w_in_bf, b_in.reshape(1, D_IN)]
    out_shape = [jax.ShapeDtypeStruct((R, 256), F32)] * 3 + [
        jax.ShapeDtypeStruct((R, 256), BF16), jax.ShapeDtypeStruct((R, 512), F32)]
    return pl.pallas_call(
        functools.partial(_inproj_body, add_pos=add_pos),
        grid=(R // tr,),
        in_specs=in_specs,
        out_specs=[row(256), row(256), row(256), row(256), row(512)],
        out_shape=out_shape,
        compiler_params=_cparams(("parallel",)),
        name="in_projection",
    )(*args)


def _rg_gates(xc, wg, bg, lam):
    g = _dot(xc.astype(BF16), wg) + bg
    r = jax.nn.sigmoid(g[:, :W_GROUP])
    gi = jax.nn.sigmoid(g[:, W_GROUP:])
    z = -lam
    softplus = jnp.maximum(z, 0.0) + jnp.log1p(jnp.exp(-jnp.abs(z)))
    log_a = (-RG_C) * r * softplus
    a = jnp.exp(log_a)
    b = jnp.sqrt(-jnp.tanh(log_a) * (a * a + 1.0)) * (gi * xc)
    return a, b


def _rg_body(xf_ref, xfh_ref, xr_ref, xrh_ref, cw_ref, cb_ref, wg_ref, bg_ref, lam_ref, h0_ref,
             yf_ref, yb_ref, hfin_ref, af_ref, ab_ref, hc_ref, *, n, tt):
    i = pl.program_id(0)
    tr = tt * SUBLANES
    keep = RG_HALO - (RG_CONV - 1) * SUBLANES

    @pl.when(i == 0)
    def _():
        hc_ref[...] = h0_ref[...]

    halo = jnp.where(i > 0, xfh_ref[...], 0.0)
    ext = jnp.concatenate([halo[keep:], xf_ref[...]], axis=0)
    xc = cb_ref[0]
    for k in range(RG_CONV):
        xc = xc + cw_ref[0, k:k + 1, :] * ext[k * SUBLANES:k * SUBLANES + tr]
    a, b = _rg_gates(xc, wg_ref[0], bg_ref[0], lam_ref[0])
    af_ref[...] = a
    yf_ref[...] = b

    halo = jnp.where(i > 0, xrh_ref[...], 0.0)
    ext = jnp.concatenate([xr_ref[...], halo[:(RG_CONV - 1) * SUBLANES]], axis=0)
    xc = cb_ref[1]
    for k in range(RG_CONV):
        o = (RG_CONV - 1 - k) * SUBLANES
        xc = xc + cw_ref[1, k:k + 1, :] * ext[o:o + tr]
    a, b = _rg_gates(xc, wg_ref[1], bg_ref[1], lam_ref[1])
    ab_ref[...] = a
    yb_ref[...] = b

    def step(t, carry):
        hf, hb = carry
        rf = pl.multiple_of(t * SUBLANES, SUBLANES)
        hf = af_ref[pl.ds(rf, SUBLANES), :] * hf + yf_ref[pl.ds(rf, SUBLANES), :]
        yf_ref[pl.ds(rf, SUBLANES), :] = hf
        rb = pl.multiple_of((tt - 1 - t) * SUBLANES, SUBLANES)
        hb = ab_ref[pl.ds(rb, SUBLANES), :] * hb + yb_ref[pl.ds(rb, SUBLANES), :]
        yb_ref[pl.ds(rb, SUBLANES), :] = hb
        return hf, hb

    hf, hb = lax.fori_loop(0, tt, step, (hc_ref[0], hc_ref[1]), unroll=8)
    hc_ref[0] = hf
    hc_ref[1] = hb
    hfin_ref[0] = hf
    hfin_ref[1] = hb


def _rglru(xa2, conv_w, conv_b, wg_bf, bg, lam, h0, L):
    tt = min(L, 256)
    n = L // tt
    tr = tt * SUBLANES
    per = tr // RG_HALO
    last_halo = L * SUBLANES // RG_HALO - 1
    row = lambda i: (i, 0)
    rev = lambda i: (n - 1 - i, 0)
    in_specs = [
        pl.BlockSpec((tr, W_GROUP), row),
        pl.BlockSpec((RG_HALO, W_GROUP), lambda i: (jnp.maximum(i * per - 1, 0), 0)),
        pl.BlockSpec((tr, W_GROUP), rev),
        pl.BlockSpec((RG_HALO, W_GROUP), lambda i: (jnp.minimum((n - i) * per, last_halo), 0)),
        _full((2, RG_CONV, W_GROUP)), _full((2, 1, W_GROUP)), _full((2, W_GROUP, 2 * W_GROUP)),
        _full((2, 1, 2 * W_GROUP)), _full((2, 1, W_GROUP)), _full((2, SUBLANES, W_GROUP)),
    ]
    return pl.pallas_call(
        functools.partial(_rg_body, n=n, tt=tt),
        grid=(n,),
        in_specs=in_specs,
        out_specs=[pl.BlockSpec((tr, W_GROUP), row), pl.BlockSpec((tr, W_GROUP), rev),
                   _full((2, SUBLANES, W_GROUP))],
        out_shape=[jax.ShapeDtypeStruct((L * SUBLANES, W_GROUP), F32)] * 2
        + [jax.ShapeDtypeStruct((2, SUBLANES, W_GROUP), F32)],
        scratch_shapes=[pltpu.VMEM((tr, W_GROUP), F32), pltpu.VMEM((tr, W_GROUP), F32),
                        pltpu.VMEM((2, SUBLANES, W_GROUP), F32)],
        compiler_params=_cparams(("arbitrary",)),
        name="rglru",
    )(xa2, xa2, xa2, xa2, conv_w, conv_b.reshape(2, 1, W_GROUP), wg_bf, bg, lam.reshape(2, 1, W_GROUP), h0)


def _pool_body(xm_ref, xp_ref, xn_ref, w_ref, b_ref, s_ref, o_ref, *, n, tt, L):
    i = pl.program_id(0)
    tr = tt * SUBLANES
    S = SUBLANES
    xm = xm_ref[...]
    prev = jnp.where(i > 0, xp_ref[...], 0.0)
    nxt = jnp.where(i < n - 1, xn_ref[...], 0.0)
    xe = jnp.concatenate([prev, xm, nxt], axis=0)
    e = xe.shape[0]
    p2 = xe[S:e] + xe[0:e - S]
    n4 = (tt + 13) * S
    p4 = p2[0:n4] + p2[2 * S:2 * S + n4]
    n8 = (tt + 9) * S
    p8 = p4[0:n8] + p4[4 * S:4 * S + n8]
    s16 = p8[0:tr] + p8[8 * S:8 * S + tr]
    s2 = p2[7 * S:7 * S + tr]
    s4 = p4[6 * S:6 * S + tr]
    s8 = p8[4 * S:4 * S + tr]
    grp = lax.broadcasted_iota(jnp.int32, (1, W_GROUP), 1) // D_SUB
    half = jnp.left_shift(1, grp)
    t = i * tt + lax.broadcasted_iota(jnp.int32, (tr, 1), 0) // S
    cnt = (jnp.minimum(t + half, L) - jnp.maximum(t - half, 0)).astype(F32)
    s = jnp.where(grp == 0, s2, jnp.where(grp == 1, s4, jnp.where(grp == 2, s8, s16)))
    pooled = s / cnt - xm
    y = _dot(pooled.astype(BF16), w_ref[...]) + b_ref[...]
    o_ref[...] = y * s_ref[...]


def _pool_mixer(xb2, w_bd_bf, b, scale, L):
    tt = min(L, 256)
    n = L // tt
    tr = tt * SUBLANES
    per = tr // POOL_HALO
    last_halo = L * SUBLANES // POOL_HALO - 1
    return pl.pallas_call(
        functools.partial(_pool_body, n=n, tt=tt, L=L),
        grid=(n,),
        in_specs=[pl.BlockSpec((tr, W_GROUP), lambda i: (i, 0)),
                  pl.BlockSpec((POOL_HALO, W_GROUP), lambda i: (jnp.maximum(i * per - 1, 0), 0)),
                  pl.BlockSpec((POOL_HALO, W_GROUP), lambda i: (jnp.minimum((i + 1) * per, last_halo), 0)),
                  _full((W_GROUP, W_GROUP)), _full((1, W_GROUP)), _full((1, W_GROUP))],
        out_specs=pl.BlockSpec((tr, W_GROUP), lambda i: (i, 0)),
        out_shape=jax.ShapeDtypeStruct((L * SUBLANES, W_GROUP), F32),
        compiler_params=_cparams(("parallel",)),
        name="pool_mixer",
    )(xb2, xb2, xb2, w_bd_bf, b.reshape(1, W_GROUP), scale.reshape(1, W_GROUP))


def _fourier_body(c_ref, s_ref, x_ref, cc_ref, sc_ref, w_ref, b_ref, o_ref):
    x = x_ref[...]
    z1 = _dot(c_ref[...], x)
    z2 = _dot(s_ref[...], x)
    for j in range(x.shape[1] // W_GROUP):
        sl = slice(j * W_GROUP, (j + 1) * W_GROUP)
        a_hi, a_lo = _split_bf16(z1[:, sl])
        b_hi, b_lo = _split_bf16(z2[:, sl])
        f = (_dot(a_hi, cc_ref[...]) + _dot(a_lo, cc_ref[...])) - (_dot(b_hi, sc_ref[...]) + _dot(b_lo, sc_ref[...]))
        o_ref[:, sl] = _dot(f.astype(BF16), w_ref[...]) + b_ref[...]


def _fourier_mixer(xc_tm, cl, sl, cc, sc, w_bd_bf, b, L):
    ncol = xc_tm.shape[1]
    nb = min(ncol, 1024)
    tk = min(L, 256)
    return pl.pallas_call(
        _fourier_body,
        grid=(ncol // nb, L // tk),
        in_specs=[pl.BlockSpec((tk, L), lambda j, k: (k, 0)),
                  pl.BlockSpec((tk, L), lambda j, k: (k, 0)),
                  pl.BlockSpec((L, nb), lambda j, k: (0, j)),
                  _full((W_GROUP, W_GROUP)), _full((W_GROUP, W_GROUP)), _full((W_GROUP, W_GROUP)),
                  _full((1, W_GROUP))],
        out_specs=pl.BlockSpec((tk, nb), lambda j, k: (k, j)),
        out_shape=jax.ShapeDtypeStruct((L, ncol), F32),
        compiler_params=_cparams(("parallel", "parallel")),
        name="fourier_mixer",
    )(cl, sl, xc_tm, cc, sc, w_bd_bf, b.reshape(1, W_GROUP))


def _dft_matrices(L):
    k = jnp.arange(L, dtype=jnp.int32)
    ang = ((k[:, None] * k[None, :]) % L).astype(F32) * (2.0 * math.pi / L)
    scale = 1.0 / math.sqrt(L)
    return jnp.cos(ang) * scale, jnp.sin(ang) * scale


def _block_diag(w):
    g, a, b = w.shape
    eye = jnp.eye(g, dtype=w.dtype)
    return (eye[:, None, :, None] * w[:, :, None, :]).reshape(g * a, g * b)


CONF_CHUNK = 64


def _conformer_body(xm_ref, xp_ref, xn_ref, cw_ref, cb_ref, lg_ref, lb_ref, avg_ref, w_ref, b_ref,
                    o_ref, v_ref, c_ref, *, n, tt):
    i = pl.program_id(0)
    tr = tt * SUBLANES
    H = CONF_HALO

    def glu(v):
        return v[:, :W_GROUP] * jax.nn.sigmoid(v[:, W_GROUP:])

    v_ref[0:H] = jnp.where(i > 0, glu(xp_ref[...]), 0.0)
    v_ref[H:H + tr] = glu(xm_ref[...])
    v_ref[H + tr:H + tr + H] = jnp.where(i < n - 1, glu(xn_ref[...]), 0.0)

    def chunk(c, carry):
        r0 = pl.multiple_of(c * CONF_CHUNK, CONF_CHUNK)
        acc = jnp.broadcast_to(cb_ref[...], (CONF_CHUNK, W_GROUP))
        for k in range(CONF_KERNEL):
            acc = acc + cw_ref[k:k + 1, :] * v_ref[pl.ds(r0 + (k + 1) * SUBLANES, CONF_CHUNK), :]
        c_ref[pl.ds(r0, CONF_CHUNK), :] = acc
        return carry

    lax.fori_loop(0, tr // CONF_CHUNK, chunk, 0)

    v = c_ref[...]
    avg = avg_ref[...]
    v_hi, v_lo = _split_bf16(v)
    mu = _dot(v_hi, avg) + _dot(v_lo, avg)
    d = v - mu
    q_hi, q_lo = _split_bf16(d * d)
    var = _dot(q_hi, avg) + _dot(q_lo, avg)
    vn = d * lax.rsqrt(var + EPS) * lg_ref[...] + lb_ref[...]
    act = vn * jax.nn.sigmoid(vn)
    o_ref[...] = _dot(act.astype(BF16), w_ref[...]) + b_ref[...]


def _conformer(xd2, conv_w, conv_b, ln_g, ln_b, avg_bf, w_pw_bf, b_pw, L):
    tt = min(L, 256)
    n = L // tt
    tr = tt * SUBLANES
    per = tr // CONF_HALO
    last_halo = L * SUBLANES // CONF_HALO - 1
    vec = lambda a: a.reshape(1, W_GROUP)
    return pl.pallas_call(
        functools.partial(_conformer_body, n=n, tt=tt),
        grid=(n,),
        in_specs=[pl.BlockSpec((tr, 2 * W_GROUP), lambda i: (i, 0)),
                  pl.BlockSpec((CONF_HALO, 2 * W_GROUP), lambda i: (jnp.maximum(i * per - 1, 0), 0)),
                  pl.BlockSpec((CONF_HALO, 2 * W_GROUP), lambda i: (jnp.minimum((i + 1) * per, last_halo), 0)),
                  _full((CONF_KERNEL, W_GROUP)), _full((1, W_GROUP)), _full((1, W_GROUP)), _full((1, W_GROUP)),
                  _full((W_GROUP, W_GROUP)), _full((W_GROUP, W_GROUP)), _full((1, W_GROUP))],
        out_specs=pl.BlockSpec((tr, W_GROUP), lambda i: (i, 0)),
        out_shape=jax.ShapeDtypeStruct((L * SUBLANES, W_GROUP), F32),
        scratch_shapes=[pltpu.VMEM((tr + 2 * CONF_HALO, W_GROUP), F32), pltpu.VMEM((tr, W_GROUP), F32)],
        compiler_params=_cparams(("parallel",)),
        name="conformer",
    )(xd2, xd2, xd2, conv_w, vec(conv_b), vec(ln_g), vec(ln_b), avg_bf, w_pw_bf, vec(b_pw))


def _gelu_tanh(x):
    return 0.5 * x * (1.0 + jnp.tanh(math.sqrt(2.0 / math.pi) * (x + 0.044715 * (x * x * x))))


def _pack_bf16_pairs(h_bf):
    u = pltpu.bitcast(h_bf.astype(F32), jnp.uint32)
    half = h_bf.shape[1] // 2
    return (u[:, :half] & jnp.uint32(0xFFFF0000)) | (u[:, half:] >> 16)


def _unpack_bf16_pairs(p):
    hi = pltpu.bitcast(p & jnp.uint32(0xFFFF0000), F32).astype(BF16)
    lo = pltpu.bitcast(p << 16, F32).astype(BF16)
    return hi, lo


def _outproj_body(*refs, add_pos):
    if add_pos:
        (x_ref, pos_ref, yf_ref, yb_ref, ga_ref, yp_ref, yc_ref, yd_ref, g1_ref, sh_ref, sc_ref, ng_ref,
         wo_ref, bo_ref, wr_ref, br_ref, xo_ref, hp_ref, lg_ref) = refs
        x = _load_plus_pos(x_ref, pos_ref)
    else:
        (x_ref, yf_ref, yb_ref, ga_ref, yp_ref, yc_ref, yd_ref, g1_ref, sh_ref, sc_ref, ng_ref,
         wo_ref, bo_ref, wr_ref, br_ref, xo_ref, hp_ref, lg_ref) = refs
        x = x_ref[...]
    ya = (yf_ref[...] + yb_ref[...]) * _gelu_tanh(ga_ref[...])
    ycat = jnp.concatenate([ya, yp_ref[...], yc_ref[...], yd_ref[...]], axis=1).astype(BF16)
    y = _dot(ycat, wo_ref[...]) + bo_ref[...]
    xn = x + _scale_rows(y, g1_ref[...])
    xo_ref[...] = xn
    h = _rms_mod(xn, ng_ref[...], sh_ref[...], sc_ref[...])
    h_hi, h_lo = _split_bf16(h)
    nt = (((1,), (1,)), ((), ()))
    wr = wr_ref[...]
    acc = lax.dot_general(wr, h_hi, nt, preferred_element_type=F32)
    acc = acc + lax.dot_general(wr, h_lo, nt, preferred_element_type=F32)
    lg_ref[...] = acc[:N_EXPERTS] + acc[N_EXPERTS:] + br_ref[...]
    hp_ref[...] = _pack_bf16_pairs(h_hi)


def _out_projection(x, pos, mix, mod, norm_g, w_out_bf, b_out, wr2_bf, b_router, ctx):
    R, D = x.shape
    tr = min(R, 512)
    add_pos = pos is not None
    row = lambda w: pl.BlockSpec((tr, w), lambda i: (i, 0))
    in_specs = [row(D)]
    args = [x]
    if add_pos:
        in_specs.append(pl.BlockSpec((tr // SUBLANES, D), lambda i: (i, 0)))
        args.append(pos)
    in_specs += [row(W_GROUP)] * 6
    args += list(mix)
    in_specs += [_mod_spec(2, ctx), _mod_spec(3, ctx), _mod_spec(4, ctx), _full((1, D)), _full((D, D)),
                 _full((1, D)), _full((2 * N_EXPERTS, D)), _full((N_EXPERTS, 1))]
    args += [mod, mod, mod, norm_g.reshape(1, D), w_out_bf, b_out.reshape(1, D), wr2_bf,
             b_router.reshape(N_EXPERTS, 1)]
    return pl.pallas_call(
        functools.partial(_outproj_body, add_pos=add_pos),
        grid=(R // tr,),
        in_specs=in_specs,
        out_specs=[row(D), row(D // 2), pl.BlockSpec((N_EXPERTS, tr), lambda i: (0, i))],
        out_shape=[jax.ShapeDtypeStruct((R, D), F32),
                   jax.ShapeDtypeStruct((R, D // 2), jnp.uint32),
                   jax.ShapeDtypeStruct((N_EXPERTS, R), F32)],
        compiler_params=_cparams(("parallel",)),
        name="out_projection",
    )(*args)


def _top4(v):
    eid = lax.broadcasted_iota(jnp.int32, v.shape, 0)
    out = []
    work = v
    for _ in range(TOP_K):
        m = jnp.max(work, axis=0, keepdims=True)
        idx = jnp.min(jnp.where(work == m, eid, N_EXPERTS), axis=0, keepdims=True)
        oh = eid == idx
        out.append((m, oh))
        work = jnp.where(oh, -jnp.inf, work)
    return out


def _count_body(lg_ref, cnt_ref):
    @pl.when(pl.program_id(0) == 0)
    def _():
        cnt_ref[...] = jnp.zeros_like(cnt_ref)

    sel = jnp.zeros(lg_ref.shape, F32)
    for _, oh in _top4(lg_ref[...]):
        sel = sel + oh.astype(F32)
    cnt_ref[...] += jnp.sum(sel, axis=1, keepdims=True)


def _route_body(lg_ref, ps_ref, tri_ref, dest_ref, gate_ref, carry_ref):
    @pl.when(pl.program_id(0) == 0)
    def _():
        carry_ref[...] = jnp.zeros_like(carry_ref)

    top = _top4(lg_ref[...])
    sel = jnp.zeros(lg_ref.shape, F32)
    for _, oh in top:
        sel = sel + oh.astype(F32)
    before = _dot(sel.astype(BF16), tri_ref[...]) + carry_ref[...] + ps_ref[...]
    m0 = top[0][0]
    es = [jnp.exp(m - m0) for m, _ in top]
    den = es[0] + es[1] + es[2] + es[3]
    for k, (_, oh) in enumerate(top):
        dest_ref[k:k + 1, :] = jnp.sum(jnp.where(oh, before, 0.0), axis=0, keepdims=True).astype(jnp.int32)
        gate_ref[k:k + 1, :] = es[k] / den
    carry_ref[...] += jnp.sum(sel, axis=1, keepdims=True)


def _routing(logits_t):
    E, T = logits_t.shape
    tt = ROUTE_TILE
    nt = T // tt
    counts = pl.pallas_call(
        _count_body,
        grid=(nt,),
        in_specs=[pl.BlockSpec((E, tt), lambda i: (0, i))],
        out_specs=_full((E, 1)),
        out_shape=jax.ShapeDtypeStruct((E, 1), F32),
        compiler_params=_cparams(("arbitrary",)),
        name="route_count",
    )(logits_t)
    cnt = counts[:, 0].astype(jnp.int32)
    padded = ((cnt + MOE_TILE - 1) // MOE_TILE) * MOE_TILE
    pend = jnp.cumsum(padded)
    pstart = pend - padded
    n_tiles = -(-(T * TOP_K) // MOE_TILE) + N_EXPERTS
    tile_start = jnp.arange(n_tiles, dtype=jnp.int32) * MOE_TILE
    tile_e = jnp.minimum(jnp.sum((pend[None, :] <= tile_start[:, None]).astype(jnp.int32), axis=1), N_EXPERTS - 1)
    n_used = (pend[-1] // MOE_TILE).astype(jnp.int32).reshape(1)
    tri = (jnp.arange(tt)[:, None] < jnp.arange(tt)[None, :]).astype(BF16)
    dest, gates = pl.pallas_call(
        _route_body,
        grid=(nt,),
        in_specs=[pl.BlockSpec((E, tt), lambda i: (0, i)), _full((E, 1)), _full((tt, tt))],
        out_specs=[pl.BlockSpec((TOP_K, tt), lambda i: (0, i)), pl.BlockSpec((TOP_K, tt), lambda i: (0, i))],
        out_shape=[jax.ShapeDtypeStruct((TOP_K, T), jnp.int32), jax.ShapeDtypeStruct((TOP_K, T), F32)],
        scratch_shapes=[pltpu.VMEM((E, 1), F32)],
        compiler_params=_cparams(("arbitrary",)),
        name="route_assign",
    )(logits_t, pstart.astype(F32).reshape(E, 1), tri)
    return dest, gates, tile_e, n_used, n_tiles


def _row_copy(src_ref, s, dst_ref, d, sem):
    return pltpu.make_async_copy(src_ref.at[pl.ds(s, 1)], dst_ref.at[pl.ds(d, 1)], sem)


def _dispatch_body(dest_ref, hp_ref, xs_in_ref, xs_ref, sem):
    del xs_in_ref
    tl = hp_ref.shape[0]

    def issue(r, c):
        for k in range(TOP_K):
            _row_copy(hp_ref, r, xs_ref, dest_ref[k * tl + r], sem).start()
        return c

    lax.fori_loop(0, tl, issue, 0)

    def drain(r, c):
        for k in range(TOP_K):
            _row_copy(hp_ref, r, xs_ref, dest_ref[k * tl + r], sem).wait()
        return c

    lax.fori_loop(0, tl, drain, 0)


def _dispatch(hp, dest_tiles, n_rows):
    T, W = hp.shape
    tl = TOK_TILE
    xs0 = jnp.zeros((n_rows, W), jnp.uint32)
    return pl.pallas_call(
        _dispatch_body,
        grid=(T // tl,),
        in_specs=[pl.BlockSpec((TOP_K * tl,), lambda i: (i,), memory_space=pltpu.SMEM),
                  pl.BlockSpec((tl, W), lambda i: (i, 0)),
                  pl.BlockSpec(memory_space=pl.ANY)],
        out_specs=pl.BlockSpec(memory_space=pl.ANY),
        out_shape=jax.ShapeDtypeStruct((n_rows, W), jnp.uint32),
        scratch_shapes=[pltpu.SemaphoreType.DMA],
        input_output_aliases={2: 0},
        compiler_params=_cparams(("arbitrary",)),
        name="moe_dispatch",
    )(dest_tiles, hp, xs0)


def _expert_body(te_ref, nu_ref, xs_ref, wgu_ref, bgu_ref, wd_ref, bd_ref, ys_ref, wgu_bf_ref, wd_bf_ref):
    i = pl.program_id(0)
    used = i < nu_ref[0]
    new_expert = jnp.logical_or(i == 0, te_ref[i] != te_ref[jnp.maximum(i - 1, 0)])

    @pl.when(jnp.logical_and(used, new_expert))
    def _():
        wgu_bf_ref[...] = wgu_ref[...].astype(BF16)
        wd_bf_ref[...] = wd_ref[...].astype(BF16)

    @pl.when(used)
    def _():
        x_hi, x_lo = _unpack_bf16_pairs(xs_ref[...])
        half = D_MODEL // 2
        gu = _dot(x_hi, wgu_bf_ref[:half, :]) + _dot(x_lo, wgu_bf_ref[half:, :]) + bgu_ref[...]
        gt = jnp.minimum(gu[:, :D_FF], SWIGLU_LIMIT)
        up = jnp.clip(gu[:, D_FF:], -SWIGLU_LIMIT, SWIGLU_LIMIT)
        act = (up + 1.0) * (gt * jax.nn.sigmoid(SWIGLU_ALPHA * gt))
        ys_ref[...] = _dot(act.astype(BF16), wd_bf_ref[...]) + bd_ref[...]

    @pl.when(jnp.logical_not(used))
    def _():
        ys_ref[...] = jnp.zeros_like(ys_ref)


def _experts(xs, tile_e, n_used, layer, w_gu, b_gu, w_down, b_down):
    n_rows, W = xs.shape
    n_tiles = n_rows // MOE_TILE
    grid_spec = pltpu.PrefetchScalarGridSpec(
        num_scalar_prefetch=2,
        grid=(n_tiles,),
        in_specs=[pl.BlockSpec((MOE_TILE, W), lambda i, te, nu: (i, 0)),
                  pl.BlockSpec((None, None, D_MODEL, 2 * D_FF), lambda i, te, nu: (layer, te[i], 0, 0)),
                  pl.BlockSpec((None, None, 1, 2 * D_FF), lambda i, te, nu: (layer, te[i], 0, 0)),
                  pl.BlockSpec((None, None, D_FF, D_MODEL), lambda i, te, nu: (layer, te[i], 0, 0)),
                  pl.BlockSpec((None, None, 1, D_MODEL), lambda i, te, nu: (layer, te[i], 0, 0))],
        out_specs=pl.BlockSpec((MOE_TILE, D_MODEL), lambda i, te, nu: (i, 0)),
        scratch_shapes=[pltpu.VMEM((D_MODEL, 2 * D_FF), BF16), pltpu.VMEM((D_FF, D_MODEL), BF16)],
    )
    return pl.pallas_call(
        _expert_body,
        grid_spec=grid_spec,
        out_shape=jax.ShapeDtypeStruct((n_rows, D_MODEL), F32),
        compiler_params=_cparams(("arbitrary",)),
        name="moe_experts",
    )(tile_e, n_used, xs, w_gu, b_gu.reshape(DEPTH, N_EXPERTS, 1, 2 * D_FF), w_down,
      b_down.reshape(DEPTH, N_EXPERTS, 1, D_MODEL))


def _combine_body(dest_ref, x_ref, gate_ref, g2_ref, ng_ref, ys_ref, o_ref, buf_ref, sem, *, final_norm):
    tl = x_ref.shape[0]

    def issue(r, c):
        for k in range(TOP_K):
            _row_copy(ys_ref, dest_ref[k * tl + r], buf_ref.at[k], r, sem).start()
        return c

    lax.fori_loop(0, tl, issue, 0)

    def drain(r, c):
        for k in range(TOP_K):
            _row_copy(ys_ref, dest_ref[k * tl + r], buf_ref.at[k], r, sem).wait()
        return c

    lax.fori_loop(0, tl, drain, 0)

    g = gate_ref[...]
    acc = g[:, 0:1] * buf_ref[0]
    for k in range(1, TOP_K):
        acc = acc + g[:, k:k + 1] * buf_ref[k]
    xn = x_ref[...] + _scale_rows(acc, g2_ref[...])
    if final_norm:
        xn = xn * lax.rsqrt(jnp.mean(xn * xn, axis=-1, keepdims=True) + EPS) * ng_ref[...]
    o_ref[...] = xn


def _combine(x, ys, dest_tiles, gates_t, tok_off, mod, final_g, ctx):
    R, D = x.shape
    tl = TOK_TILE
    t0 = tok_off // tl
    final_norm = final_g is not None
    ng = final_g.reshape(1, D) if final_norm else jnp.ones((1, D), F32)
    return pl.pallas_call(
        functools.partial(_combine_body, final_norm=final_norm),
        grid=(R // tl,),
        in_specs=[pl.BlockSpec((TOP_K * tl,), lambda i: (t0 + i,), memory_space=pltpu.SMEM),
                  pl.BlockSpec((tl, D), lambda i: (i, 0)),
                  pl.BlockSpec((tl, TOP_K), lambda i: (t0 + i, 0)),
                  _mod_spec(5, ctx), _full((1, D)),
                  pl.BlockSpec(memory_space=pl.ANY)],
        out_specs=pl.BlockSpec((tl, D), lambda i: (i, 0)),
        out_shape=jax.ShapeDtypeStruct((R, D), F32),
        scratch_shapes=[pltpu.VMEM((TOP_K, tl, D), F32), pltpu.SemaphoreType.DMA],
        compiler_params=_cparams(("arbitrary",)),
        name="moe_combine",
    )(dest_tiles, x, gates_t, mod, ng, ys)


def _moe(hp, logits_t, layer, w_gu, b_gu, w_down, b_down):
    T = hp.shape[0]
    dest, gates, tile_e, n_used, n_tiles = _routing(logits_t)
    nt = T // TOK_TILE
    dest_tiles = dest.reshape(TOP_K, nt, TOK_TILE).transpose(1, 0, 2).reshape(-1)
    xs = _dispatch(hp, dest_tiles, n_tiles * MOE_TILE)
    ys = _experts(xs, tile_e, n_used, layer, w_gu, b_gu, w_down, b_down)
    return ys, dest_tiles, gates.T


def _token_mixers(x, pos, mod, h0, p, consts, ctx, need_out):
    R, D = x.shape
    L = R // SUBLANES
    xa, ga, xb, xc, xd = _in_projection(x, pos, mod, p["norm1_g"], p["w_in"], p["b_in"], ctx)
    yf, yb, hfin = _rglru(xa, p["conv_a_w"], p["conv_a_b"], p["wg"], p["bg"], p["rg_lambda"], h0, L)
    if not need_out:
        return None, hfin
    yp = _pool_mixer(xb, p["w_pool"], p["b_pool"], p["pool_scale"], L)
    cl, sl = consts["dft"][L]
    yc = _fourier_mixer(xc.reshape(L, SUBLANES * W_GROUP), cl, sl, consts["cc"], consts["sc"], p["w_four"],
                        p["b_four"], L).reshape(R, W_GROUP)
    yd = _conformer(xd, p["conv_d_w"], p["conv_d_b"], p["ln_d_g"], p["ln_d_b"], consts["avg"], p["w_pw"],
                    p["b_pw"], L)
    return (yf, yb, ga, yp, yc, yd), hfin


def _pos_embed(n_tokens):
    rows_n = n_tokens // GRID_W
    row = jnp.repeat(jnp.arange(rows_n), GRID_W).astype(F32)
    col = jnp.tile(jnp.arange(GRID_W), rows_n).astype(F32)
    q = D_MODEL // 4
    omega = 1.0 / (10000.0 ** (jnp.arange(q, dtype=F32) / q))

    def emb(v):
        ang = v[:, None] * omega[None, :]
        return jnp.concatenate([jnp.sin(ang), jnp.cos(ang)], axis=-1)

    return jnp.concatenate([emb(row), emb(col)], axis=-1)


def _layer_params(l, w_in, b_in, conv_a_w, conv_a_b, w_rg_r, b_rg_r, w_rg_i, b_rg_i, rg_lambda, w_pool, b_pool,
                  pool_scale, w_four, b_four, conv_d_w, conv_d_b, ln_d_g, ln_d_b, w_pw, b_pw, norm1_g):
    wg = jnp.stack([jnp.concatenate([_block_diag(w_rg_r[l, d]), _block_diag(w_rg_i[l, d])], axis=1)
                    for d in range(2)]).astype(BF16)
    bg = jnp.concatenate([b_rg_r[l].reshape(2, 1, W_GROUP), b_rg_i[l].reshape(2, 1, W_GROUP)], axis=-1)
    return dict(
        norm1_g=norm1_g[l], w_in=w_in[l].astype(BF16), b_in=b_in[l],
        conv_a_w=conv_a_w[l], conv_a_b=conv_a_b[l], wg=wg, bg=bg, rg_lambda=rg_lambda[l],
        w_pool=_block_diag(w_pool[l]).astype(BF16), b_pool=b_pool[l], pool_scale=pool_scale[l],
        w_four=_block_diag(w_four[l]).astype(BF16), b_four=b_four[l],
        conv_d_w=conv_d_w[l], conv_d_b=conv_d_b[l], ln_d_g=ln_d_g[l], ln_d_b=ln_d_b[l],
        w_pw=w_pw[l].astype(BF16), b_pw=b_pw[l])


def kernel(x, c, ctx, c_ctx, w_mod, b_mod, norm1_g, norm2_g, w_in, b_in, conv_a_w, conv_a_b, w_rg_r, b_rg_r,
           w_rg_i, b_rg_i, rg_lambda, w_pool, b_pool, pool_scale, w_four, b_four, conv_d_w, conv_d_b, ln_d_g,
           ln_d_b, w_pw, b_pw, w_out, b_out, w_router, b_router, w_gu, b_gu, w_down, b_down, final_norm_g):
    bn, L, D = x.shape
    Lc = ctx.shape[1]
    assert bn == SUBLANES and D == D_MODEL

    pos = _pos_embed(L)
    c_rows = jnp.concatenate([c, jnp.broadcast_to(c_ctx[None], (MOD_ROWS - bn, D))], axis=0)
    mod = _modulation(c_rows, w_mod, b_mod)
    x = jnp.transpose(x, (1, 0, 2)).reshape(L * bn, D)
    ctx = jnp.transpose(ctx, (1, 0, 2)).reshape(Lc * bn, D)

    cc1, sc1 = _dft_matrices(D_SUB)
    eye = jnp.eye(N_SUB, dtype=F32)
    consts = dict(
        dft={n: tuple(m.astype(BF16) for m in _dft_matrices(n)) for n in sorted({L, Lc})},
        cc=jnp.kron(eye, cc1).astype(BF16), sc=jnp.kron(eye, sc1).astype(BF16),
        avg=jnp.kron(eye, jnp.full((D_SUB, D_SUB), 1.0 / D_SUB, F32)).astype(BF16))
    h_zero = jnp.zeros((2, SUBLANES, W_GROUP), F32)

    for l in range(DEPTH):
        last = l == DEPTH - 1
        p = _layer_params(l, w_in, b_in, conv_a_w, conv_a_b, w_rg_r, b_rg_r, w_rg_i, b_rg_i, rg_lambda, w_pool,
                          b_pool, pool_scale, w_four, b_four, conv_d_w, conv_d_b, ln_d_g, ln_d_b, w_pw, b_pw,
                          norm1_g)
        mod3 = mod[l]
        w_out_bf = w_out[l].astype(BF16)
        wr_t = w_router[l].T
        wr_hi = wr_t.astype(BF16)
        wr2 = jnp.concatenate([wr_hi, (wr_t - wr_hi.astype(F32)).astype(BF16)], axis=0)
        x_pos = pos if l == 0 else None

        mix_c, h_ctx = _token_mixers(ctx, None, mod3, h_zero, p, consts, True, not last)
        mix_x, _ = _token_mixers(x, x_pos, mod3, h_ctx, p, consts, False, True)
        x, hp_x, lg_x = _out_projection(x, x_pos, mix_x, mod3, norm2_g[l], w_out_bf, b_out[l], wr2,
                                        b_router[l], False)
        if not last:
            ctx, hp_c, lg_c = _out_projection(ctx, None, mix_c, mod3, norm2_g[l], w_out_bf, b_out[l], wr2,
                                              b_router[l], True)
            hp = jnp.concatenate([hp_c, hp_x], axis=0)
            lg = jnp.concatenate([lg_c, lg_x], axis=1)
            ys, dest_tiles, gates_t = _moe(hp, lg, l, w_gu, b_gu, w_down, b_down)
            ctx = _combine(ctx, ys, dest_tiles, gates_t, 0, mod3, None, True)
            x = _combine(x, ys, dest_tiles, gates_t, bn * Lc, mod3, None, False)
        else:
            ys, dest_tiles, gates_t = _moe(hp_x, lg_x, l, w_gu, b_gu, w_down, b_down)
            x = _combine(x, ys, dest_tiles, gates_t, 0, mod3, final_norm_g, False)
    return jnp.transpose(x.reshape(L, bn, D), (1, 0, 2))
```

```python
import functools
import math

import jax
import jax.numpy as jnp
from jax import lax
from jax.experimental import pallas as pl
from jax.experimental.pallas import tpu as pltpu

F32 = jnp.float32
BF16 = jnp.bfloat16

D_MODEL = 1024
DEPTH = 2
GRID_W = 64
W_GROUP = 256
N_SUB = 4
D_SUB = 64
D_IN = 6 * W_GROUP
RG_CONV = 4
RG_C = 8.0
CONF_KERNEL = 31
N_EXPERTS = 32
TOP_K = 4
D_FF = D_MODEL
SWIGLU_LIMIT = 7.0
SWIGLU_ALPHA = 1.702
EPS = 1e-6

SUBLANES = 8
VMEM_LIMIT_BYTES = 56 * 1024 * 1024
MOD_ROWS = 16
RG_HALO = 8 * SUBLANES
POOL_HALO = 8 * SUBLANES
CONF_HALO = 16 * SUBLANES
MOE_TILE = 512
TOK_TILE = 256
ROUTE_TILE = 512


def _cparams(sem):
    return pltpu.CompilerParams(dimension_semantics=sem, vmem_limit_bytes=VMEM_LIMIT_BYTES)


def _full(shape):
    nd = len(shape)
    return pl.BlockSpec(shape, lambda *_: (0,) * nd)


def _dot(a, b):
    return jnp.dot(a, b, preferred_element_type=F32)


def _split_bf16(v):
    hi = v.astype(BF16)
    lo = (v - hi.astype(F32)).astype(BF16)
    return hi, lo


def _mod_body(c_ref, w_ref, b_ref, o_ref):
    c = c_ref[...]
    s = c * jax.nn.sigmoid(c)
    o_ref[...] = jnp.dot(s, w_ref[...], precision=lax.Precision.HIGHEST,
                         preferred_element_type=F32) + b_ref[...]


def _modulation(c_rows, w_mod, b_mod):
    tn = 1536
    n6 = 6 * D_MODEL
    return pl.pallas_call(
        _mod_body,
        grid=(DEPTH, n6 // tn),
        in_specs=[_full((MOD_ROWS, D_MODEL)),
                  pl.BlockSpec((None, D_MODEL, tn), lambda l, j: (l, 0, j)),
                  pl.BlockSpec((None, 1, tn), lambda l, j: (l, 0, j))],
        out_specs=pl.BlockSpec((None, MOD_ROWS, tn), lambda l, j: (l, 0, j)),
        out_shape=jax.ShapeDtypeStruct((DEPTH, MOD_ROWS, n6), F32),
        compiler_params=_cparams(("parallel", "parallel")),
        name="modulation",
    )(c_rows, w_mod, b_mod.reshape(DEPTH, 1, n6))


def _mod_spec(chunk, ctx):
    return pl.BlockSpec((SUBLANES, D_MODEL), lambda i: (1 if ctx else 0, chunk))


def _scale_rows(v, m):
    r, d = v.shape
    return (v.reshape(r // SUBLANES, SUBLANES, d) * m[None]).reshape(r, d)


def _rms_mod(x, g, shift, scale):
    r, d = x.shape
    y = x * lax.rsqrt(jnp.mean(x * x, axis=-1, keepdims=True) + EPS) * g
    y3 = y.reshape(r // SUBLANES, SUBLANES, d)
    return (y3 * (1.0 + scale)[None] + shift[None]).reshape(r, d)


def _load_plus_pos(x_ref, pos_ref):
    s = SUBLANES
    return jnp.concatenate([x_ref[t * s:(t + 1) * s, :] + pos_ref[t:t + 1, :] for t in range(pos_ref.shape[0])],
                           axis=0)


def _inproj_body(*refs, add_pos):
    if add_pos:
        x_ref, pos_ref, sh_ref, sc_ref, g_ref, w_ref, b_ref, xa_ref, ga_ref, xb_ref, xc_ref, xd_ref = refs
        x = _load_plus_pos(x_ref, pos_ref)
    else:
        x_ref, sh_ref, sc_ref, g_ref, w_ref, b_ref, xa_ref, ga_ref, xb_ref, xc_ref, xd_ref = refs
        x = x_ref[...]
    u = _rms_mod(x, g_ref[...], sh_ref[...], sc_ref[...])
    p = _dot(u.astype(BF16), w_ref[...]) + b_ref[...]
    xa_ref[...] = p[:, 0:256]
    ga_ref[...] = p[:, 256:512]
    xb_ref[...] = p[:, 512:768]
    xc_ref[...] = p[:, 768:1024].astype(BF16)
    xd_ref[...] = p[:, 1024:1536]


def _in_projection(x, pos, mod, norm_g, w_in_bf, b_in, ctx):
    R, D = x.shape
    tr = min(R, 512)
    add_pos = pos is not None
    row = lambda w: pl.BlockSpec((tr, w), lambda i: (i, 0))
    in_specs = [row(D)]
    args = [x]
    if add_pos:
        in_specs.append(pl.BlockSpec((tr // SUBLANES, D), lambda i: (i, 0)))
        args.append(pos)
    in_specs += [_mod_spec(0, ctx), _mod_spec(1, ctx), _full((1, D)), _full((D, D_IN)), _full((1, D_IN))]
    args += [mod, mod, norm_g.reshape(1, D), w_in_bf, b_in.reshape(1, D_IN)]
    out_shape = [jax.ShapeDtypeStruct((R, 256), F32)] * 3 + [
        jax.ShapeDtypeStruct((R, 256), BF16), jax.ShapeDtypeStruct((R, 512), F32)]
    return pl.pallas_call(
        functools.partial(_inproj_body, add_pos=add_pos),
        grid=(R // tr,),
        in_specs=in_specs,
        out_specs=[row(256), row(256), row(256), row(256), row(512)],
        out_shape=out_shape,
        compiler_params=_cparams(("parallel",)),
        name="in_projection",
    )(*args)


def _rg_gates(xc, wg, bg, lam):
    g = _dot(xc.astype(BF16), wg) + bg
    r = jax.nn.sigmoid(g[:, :W_GROUP])
    gi = jax.nn.sigmoid(g[:, W_GROUP:])
    z = -lam
    softplus = jnp.maximum(z, 0.0) + jnp.log1p(jnp.exp(-jnp.abs(z)))
    log_a = (-RG_C) * r * softplus
    a = jnp.exp(log_a)
    b = jnp.sqrt(-jnp.tanh(log_a) * (a * a + 1.0)) * (gi * xc)
    return a, b


def _rg_body(xf_ref, xfh_ref, xr_ref, xrh_ref, cw_ref, cb_ref, wg_ref, bg_ref, lam_ref, h0_ref,
             yf_ref, yb_ref, hfin_ref, af_ref, ab_ref, hc_ref, *, n, tt):
    i = pl.program_id(0)
    tr = tt * SUBLANES
    keep = RG_HALO - (RG_CONV - 1) * SUBLANES

    @pl.when(i == 0)
    def _():
        hc_ref[...] = h0_ref[...]

    halo = jnp.where(i > 0, xfh_ref[...], 0.0)
    ext = jnp.concatenate([halo[keep:], xf_ref[...]], axis=0)
    xc = cb_ref[0]
    for k in range(RG_CONV):
        xc = xc + cw_ref[0, k:k + 1, :] * ext[k * SUBLANES:k * SUBLANES + tr]
    a, b = _rg_gates(xc, wg_ref[0], bg_ref[0], lam_ref[0])
    af_ref[...] = a
    yf_ref[...] = b

    halo = jnp.where(i > 0, xrh_ref[...], 0.0)
    ext = jnp.concatenate([xr_ref[...], halo[:(RG_CONV - 1) * SUBLANES]], axis=0)
    xc = cb_ref[1]
    for k in range(RG_CONV):
        o = (RG_CONV - 1 - k) * SUBLANES
        xc = xc + cw_ref[1, k:k + 1, :] * ext[o:o + tr]
    a, b = _rg_gates(xc, wg_ref[1], bg_ref[1], lam_ref[1])
    ab_ref[...] = a
    yb_ref[...] = b

    def step(t, carry):
        hf, hb = carry
        rf = pl.multiple_of(t * SUBLANES, SUBLANES)
        hf = af_ref[pl.ds(rf, SUBLANES), :] * hf + yf_ref[pl.ds(rf, SUBLANES), :]
        yf_ref[pl.ds(rf, SUBLANES), :] = hf
        rb = pl.multiple_of((tt - 1 - t) * SUBLANES, SUBLANES)
        hb = ab_ref[pl.ds(rb, SUBLANES), :] * hb + yb_ref[pl.ds(rb, SUBLANES), :]
        yb_ref[pl.ds(rb, SUBLANES), :] = hb
        return hf, hb

    hf, hb = lax.fori_loop(0, tt, step, (hc_ref[0], hc_ref[1]), unroll=8)
    hc_ref[0] = hf
    hc_ref[1] = hb
    hfin_ref[0] = hf
    hfin_ref[1] = hb


def _rglru(xa2, conv_w, conv_b, wg_bf, bg, lam, h0, L):
    tt = min(L, 256)
    n = L // tt
    tr = tt * SUBLANES
    per = tr // RG_HALO
    last_halo = L * SUBLANES // RG_HALO - 1
    row = lambda i: (i, 0)
    rev = lambda i: (n - 1 - i, 0)
    in_specs = [
        pl.BlockSpec((tr, W_GROUP), row),
        pl.BlockSpec((RG_HALO, W_GROUP), lambda i: (jnp.maximum(i * per - 1, 0), 0)),
        pl.BlockSpec((tr, W_GROUP), rev),
        pl.BlockSpec((RG_HALO, W_GROUP), lambda i: (jnp.minimum((n - i) * per, last_halo), 0)),
        _full((2, RG_CONV, W_GROUP)), _full((2, 1, W_GROUP)), _full((2, W_GROUP, 2 * W_GROUP)),
        _full((2, 1, 2 * W_GROUP)), _full((2, 1, W_GROUP)), _full((2, SUBLANES, W_GROUP)),
    ]
    return pl.pallas_call(
        functools.partial(_rg_body, n=n, tt=tt),
        grid=(n,),
        in_specs=in_specs,
        out_specs=[pl.BlockSpec((tr, W_GROUP), row), pl.BlockSpec((tr, W_GROUP), rev),
                   _full((2, SUBLANES, W_GROUP))],
        out_shape=[jax.ShapeDtypeStruct((L * SUBLANES, W_GROUP), F32)] * 2
        + [jax.ShapeDtypeStruct((2, SUBLANES, W_GROUP), F32)],
        scratch_shapes=[pltpu.VMEM((tr, W_GROUP), F32), pltpu.VMEM((tr, W_GROUP), F32),
                        pltpu.VMEM((2, SUBLANES, W_GROUP), F32)],
        compiler_params=_cparams(("arbitrary",)),
        name="rglru",
    )(xa2, xa2, xa2, xa2, conv_w, conv_b.reshape(2, 1, W_GROUP), wg_bf, bg, lam.reshape(2, 1, W_GROUP), h0)


def _pool_body(xm_ref, xp_ref, xn_ref, w_ref, b_ref, s_ref, o_ref, *, n, tt, L):
    i = pl.program_id(0)
    tr = tt * SUBLANES
    S = SUBLANES
    xm = xm_ref[...]
    prev = jnp.where(i > 0, xp_ref[...], 0.0)
    nxt = jnp.where(i < n - 1, xn_ref[...], 0.0)
    xe = jnp.concatenate([prev, xm, nxt], axis=0)
    e = xe.shape[0]
    p2 = xe[S:e] + xe[0:e - S]
    n4 = (tt + 13) * S
    p4 = p2[0:n4] + p2[2 * S:2 * S + n4]
    n8 = (tt + 9) * S
    p8 = p4[0:n8] + p4[4 * S:4 * S + n8]
    s16 = p8[0:tr] + p8[8 * S:8 * S + tr]
    s2 = p2[7 * S:7 * S + tr]
    s4 = p4[6 * S:6 * S + tr]
    s8 = p8[4 * S:4 * S + tr]
    grp = lax.broadcasted_iota(jnp.int32, (1, W_GROUP), 1) // D_SUB
    half = jnp.left_shift(1, grp)
    t = i * tt + lax.broadcasted_iota(jnp.int32, (tr, 1), 0) // S
    cnt = (jnp.minimum(t + half, L) - jnp.maximum(t - half, 0)).astype(F32)
    s = jnp.where(grp == 0, s2, jnp.where(grp == 1, s4, jnp.where(grp == 2, s8, s16)))
    pooled = s / cnt - xm
    y = _dot(pooled.astype(BF16), w_ref[...]) + b_ref[...]
    o_ref[...] = y * s_ref[...]


def _pool_mixer(xb2, w_bd_bf, b, scale, L):
    tt = min(L, 256)
    n = L // tt
    tr = tt * SUBLANES
    per = tr // POOL_HALO
    last_halo = L * SUBLANES // POOL_HALO - 1
    return pl.pallas_call(
        functools.partial(_pool_body, n=n, tt=tt, L=L),
        grid=(n,),
        in_specs=[pl.BlockSpec((tr, W_GROUP), lambda i: (i, 0)),
                  pl.BlockSpec((POOL_HALO, W_GROUP), lambda i: (jnp.maximum(i * per - 1, 0), 0)),
                  pl.BlockSpec((POOL_HALO, W_GROUP), lambda i: (jnp.minimum((i + 1) * per, last_halo), 0)),
                  _full((W_GROUP, W_GROUP)), _full((1, W_GROUP)), _full((1, W_GROUP))],
        out_specs=pl.BlockSpec((tr, W_GROUP), lambda i: (i, 0)),
        out_shape=jax.ShapeDtypeStruct((L * SUBLANES, W_GROUP), F32),
        compiler_params=_cparams(("parallel",)),
        name="pool_mixer",
    )(xb2, xb2, xb2, w_bd_bf, b.reshape(1, W_GROUP), scale.reshape(1, W_GROUP))


def _fourier_body(c_ref, s_ref, x_ref, cc_ref, sc_ref, w_ref, b_ref, o_ref):
    x = x_ref[...]
    z1 = _dot(c_ref[...], x)
    z2 = _dot(s_ref[...], x)
    for j in range(x.shape[1] // W_GROUP):
        sl = slice(j * W_GROUP, (j + 1) * W_GROUP)
        a_hi, a_lo = _split_bf16(z1[:, sl])
        b_hi, b_lo = _split_bf16(z2[:, sl])
        f = (_dot(a_hi, cc_ref[...]) + _dot(a_lo, cc_ref[...])) - (_dot(b_hi, sc_ref[...]) + _dot(b_lo, sc_ref[...]))
        o_ref[:, sl] = _dot(f.astype(BF16), w_ref[...]) + b_ref[...]


def _fourier_mixer(xc_tm, cl, sl, cc, sc, w_bd_bf, b, L):
    ncol = xc_tm.shape[1]
    nb = min(ncol, 1024)
    tk = min(L, 256)
    return pl.pallas_call(
        _fourier_body,
        grid=(ncol // nb, L // tk),
        in_specs=[pl.BlockSpec((tk, L), lambda j, k: (k, 0)),
                  pl.BlockSpec((tk, L), lambda j, k: (k, 0)),
                  pl.BlockSpec((L, nb), lambda j, k: (0, j)),
                  _full((W_GROUP, W_GROUP)), _full((W_GROUP, W_GROUP)), _full((W_GROUP, W_GROUP)),
                  _full((1, W_GROUP))],
        out_specs=pl.BlockSpec((tk, nb), lambda j, k: (k, j)),
        out_shape=jax.ShapeDtypeStruct((L, ncol), F32),
        compiler_params=_cparams(("parallel", "parallel")),
        name="fourier_mixer",
    )(cl, sl, xc_tm, cc, sc, w_bd_bf, b.reshape(1, W_GROUP))


def _dft_matrices(L):
    f = 1 << (max(L.bit_length() - 1, 0) // 2)
    n = jnp.arange(L, dtype=jnp.int32)[None, :]

    def table(rows):
        ang = ((rows[:, None] * n) % L).astype(F32) * (2.0 * math.pi / L)
        return jnp.cos(ang), jnp.sin(ang)

    ac, as_ = table(jnp.arange(L // f, dtype=jnp.int32) * f)
    bc, bs = table(jnp.arange(f, dtype=jnp.int32))
    scale = 1.0 / math.sqrt(L)
    cos = (ac[:, None, :] * bc[None, :, :] - as_[:, None, :] * bs[None, :, :]).reshape(L, L) * scale
    sin = (as_[:, None, :] * bc[None, :, :] + ac[:, None, :] * bs[None, :, :]).reshape(L, L) * scale
    return cos, sin


def _block_diag(w):
    g, a, b = w.shape
    eye = jnp.eye(g, dtype=w.dtype)
    return (eye[:, None, :, None] * w[:, :, None, :]).reshape(g * a, g * b)


CONF_CHUNK = 64


def _conformer_body(xm_ref, xp_ref, xn_ref, cw_ref, cb_ref, lg_ref, lb_ref, avg_ref, w_ref, b_ref,
                    o_ref, v_ref, c_ref, *, n, tt):
    i = pl.program_id(0)
    tr = tt * SUBLANES
    H = CONF_HALO

    def glu(v):
        return v[:, :W_GROUP] * jax.nn.sigmoid(v[:, W_GROUP:])

    v_ref[0:H] = jnp.where(i > 0, glu(xp_ref[...]), 0.0)
    v_ref[H:H + tr] = glu(xm_ref[...])
    v_ref[H + tr:H + tr + H] = jnp.where(i < n - 1, glu(xn_ref[...]), 0.0)

    def chunk(c, carry):
        r0 = pl.multiple_of(c * CONF_CHUNK, CONF_CHUNK)
        acc = jnp.broadcast_to(cb_ref[...], (CONF_CHUNK, W_GROUP))
        for k in range(CONF_KERNEL):
            acc = acc + cw_ref[k:k + 1, :] * v_ref[pl.ds(r0 + (k + 1) * SUBLANES, CONF_CHUNK), :]
        c_ref[pl.ds(r0, CONF_CHUNK), :] = acc
        return carry

    lax.fori_loop(0, tr // CONF_CHUNK, chunk, 0)

    v = c_ref[...]
    avg = avg_ref[...]
    v_hi, v_lo = _split_bf16(v)
    mu = _dot(v_hi, avg) + _dot(v_lo, avg)
    d = v - mu
    q_hi, q_lo = _split_bf16(d * d)
    var = _dot(q_hi, avg) + _dot(q_lo, avg)
    vn = d * lax.rsqrt(var + EPS) * lg_ref[...] + lb_ref[...]
    act = vn * jax.nn.sigmoid(vn)
    o_ref[...] = _dot(act.astype(BF16), w_ref[...]) + b_ref[...]


def _conformer(xd2, conv_w, conv_b, ln_g, ln_b, avg_bf, w_pw_bf, b_pw, L):
    tt = min(L, 256)
    n = L // tt
    tr = tt * SUBLANES
    per = tr // CONF_HALO
    last_halo = L * SUBLANES // CONF_HALO - 1
    vec = lambda a: a.reshape(1, W_GROUP)
    return pl.pallas_call(
        functools.partial(_conformer_body, n=n, tt=tt),
        grid=(n,),
        in_specs=[pl.BlockSpec((tr, 2 * W_GROUP), lambda i: (i, 0)),
                  pl.BlockSpec((CONF_HALO, 2 * W_GROUP), lambda i: (jnp.maximum(i * per - 1, 0), 0)),
                  pl.BlockSpec((CONF_HALO, 2 * W_GROUP), lambda i: (jnp.minimum((i + 1) * per, last_halo), 0)),
                  _full((CONF_KERNEL, W_GROUP)), _full((1, W_GROUP)), _full((1, W_GROUP)), _full((1, W_GROUP)),
                  _full((W_GROUP, W_GROUP)), _full((W_GROUP, W_GROUP)), _full((1, W_GROUP))],
        out_specs=pl.BlockSpec((tr, W_GROUP), lambda i: (i, 0)),
        out_shape=jax.ShapeDtypeStruct((L * SUBLANES, W_GROUP), F32),
        scratch_shapes=[pltpu.VMEM((tr + 2 * CONF_HALO, W_GROUP), F32), pltpu.VMEM((tr, W_GROUP), F32)],
        compiler_params=_cparams(("parallel",)),
        name="conformer",
    )(xd2, xd2, xd2, conv_w, vec(conv_b), vec(ln_g), vec(ln_b), avg_bf, w_pw_bf, vec(b_pw))


def _gelu_tanh(x):
    return 0.5 * x * (1.0 + jnp.tanh(math.sqrt(2.0 / math.pi) * (x + 0.044715 * (x * x * x))))


def _pack_bf16_pairs(h_bf):
    u = pltpu.bitcast(h_bf.astype(F32), jnp.uint32)
    half = h_bf.shape[1] // 2
    return (u[:, :half] & jnp.uint32(0xFFFF0000)) | (u[:, half:] >> 16)


def _unpack_bf16_pairs(p):
    hi = pltpu.bitcast(p & jnp.uint32(0xFFFF0000), F32).astype(BF16)
    lo = pltpu.bitcast(p << 16, F32).astype(BF16)
    return hi, lo


def _outproj_body(*refs, add_pos):
    if add_pos:
        (x_ref, pos_ref, yf_ref, yb_ref, ga_ref, yp_ref, yc_ref, yd_ref, g1_ref, sh_ref, sc_ref, ng_ref,
         wo_ref, bo_ref, wr_ref, br_ref, xo_ref, hp_ref, lg_ref) = refs
        x = _load_plus_pos(x_ref, pos_ref)
    else:
        (x_ref, yf_ref, yb_ref, ga_ref, yp_ref, yc_ref, yd_ref, g1_ref, sh_ref, sc_ref, ng_ref,
         wo_ref, bo_ref, wr_ref, br_ref, xo_ref, hp_ref, lg_ref) = refs
        x = x_ref[...]
    ya = (yf_ref[...] + yb_ref[...]) * _gelu_tanh(ga_ref[...])
    ycat = jnp.concatenate([ya, yp_ref[...], yc_ref[...], yd_ref[...]], axis=1).astype(BF16)
    y = _dot(ycat, wo_ref[...]) + bo_ref[...]
    xn = x + _scale_rows(y, g1_ref[...])
    xo_ref[...] = xn
    h = _rms_mod(xn, ng_ref[...], sh_ref[...], sc_ref[...])
    h_hi, h_lo = _split_bf16(h)
    nt = (((1,), (1,)), ((), ()))
    wr = wr_ref[...]
    acc = lax.dot_general(wr, h_hi, nt, preferred_element_type=F32)
    acc = acc + lax.dot_general(wr, h_lo, nt, preferred_element_type=F32)
    lg_ref[...] = acc[:N_EXPERTS] + acc[N_EXPERTS:] + br_ref[...]
    hp_ref[...] = _pack_bf16_pairs(h_hi)


def _out_projection(x, pos, mix, mod, norm_g, w_out_bf, b_out, wr2_bf, b_router, ctx):
    R, D = x.shape
    tr = min(R, 512)
    add_pos = pos is not None
    row = lambda w: pl.BlockSpec((tr, w), lambda i: (i, 0))
    in_specs = [row(D)]
    args = [x]
    if add_pos:
        in_specs.append(pl.BlockSpec((tr // SUBLANES, D), lambda i: (i, 0)))
        args.append(pos)
    in_specs += [row(W_GROUP)] * 6
    args += list(mix)
    in_specs += [_mod_spec(2, ctx), _mod_spec(3, ctx), _mod_spec(4, ctx), _full((1, D)), _full((D, D)),
                 _full((1, D)), _full((2 * N_EXPERTS, D)), _full((N_EXPERTS, 1))]
    args += [mod, mod, mod, norm_g.reshape(1, D), w_out_bf, b_out.reshape(1, D), wr2_bf,
             b_router.reshape(N_EXPERTS, 1)]
    return pl.pallas_call(
        functools.partial(_outproj_body, add_pos=add_pos),
        grid=(R // tr,),
        in_specs=in_specs,
        out_specs=[row(D), row(D // 2), pl.BlockSpec((N_EXPERTS, tr), lambda i: (0, i))],
        out_shape=[jax.ShapeDtypeStruct((R, D), F32),
                   jax.ShapeDtypeStruct((R, D // 2), jnp.uint32),
                   jax.ShapeDtypeStruct((N_EXPERTS, R), F32)],
        compiler_params=_cparams(("parallel",)),
        name="out_projection",
    )(*args)


def _top4(v):
    eid = lax.broadcasted_iota(jnp.int32, v.shape, 0)
    out = []
    work = v
    for _ in range(TOP_K):
        m = jnp.max(work, axis=0, keepdims=True)
        idx = jnp.min(jnp.where(work == m, eid, N_EXPERTS), axis=0, keepdims=True)
        oh = eid == idx
        out.append((m, oh))
        work = jnp.where(oh, -jnp.inf, work)
    return out


def _count_body(lg_ref, cnt_ref):
    @pl.when(pl.program_id(0) == 0)
    def _():
        cnt_ref[...] = jnp.zeros_like(cnt_ref)

    sel = jnp.zeros(lg_ref.shape, F32)
    for _, oh in _top4(lg_ref[...]):
        sel = sel + oh.astype(F32)
    cnt_ref[...] += jnp.sum(sel, axis=1, keepdims=True)


def _route_body(lg_ref, ps_ref, tri_ref, dest_ref, gate_ref, carry_ref):
    @pl.when(pl.program_id(0) == 0)
    def _():
        carry_ref[...] = jnp.zeros_like(carry_ref)

    top = _top4(lg_ref[...])
    sel = jnp.zeros(lg_ref.shape, F32)
    for _, oh in top:
        sel = sel + oh.astype(F32)
    before = _dot(sel.astype(BF16), tri_ref[...]) + carry_ref[...] + ps_ref[...]
    m0 = top[0][0]
    es = [jnp.exp(m - m0) for m, _ in top]
    den = es[0] + es[1] + es[2] + es[3]
    for k, (_, oh) in enumerate(top):
        dest_ref[k:k + 1, :] = jnp.sum(jnp.where(oh, before, 0.0), axis=0, keepdims=True).astype(jnp.int32)
        gate_ref[k:k + 1, :] = es[k] / den
    carry_ref[...] += jnp.sum(sel, axis=1, keepdims=True)


def _routing(logits_t):
    E, T = logits_t.shape
    tt = ROUTE_TILE
    nt = T // tt
    counts = pl.pallas_call(
        _count_body,
        grid=(nt,),
        in_specs=[pl.BlockSpec((E, tt), lambda i: (0, i))],
        out_specs=_full((E, 1)),
        out_shape=jax.ShapeDtypeStruct((E, 1), F32),
        compiler_params=_cparams(("arbitrary",)),
        name="route_count",
    )(logits_t)
    cnt = counts[:, 0].astype(jnp.int32)
    padded = ((cnt + MOE_TILE - 1) // MOE_TILE) * MOE_TILE
    pend = jnp.cumsum(padded)
    pstart = pend - padded
    n_tiles = -(-(T * TOP_K) // MOE_TILE) + N_EXPERTS
    tile_start = jnp.arange(n_tiles, dtype=jnp.int32) * MOE_TILE
    tile_e = jnp.minimum(jnp.sum((pend[None, :] <= tile_start[:, None]).astype(jnp.int32), axis=1), N_EXPERTS - 1)
    n_used = (pend[-1] // MOE_TILE).astype(jnp.int32).reshape(1)
    tri = (jnp.arange(tt)[:, None] < jnp.arange(tt)[None, :]).astype(BF16)
    dest, gates = pl.pallas_call(
        _route_body,
        grid=(nt,),
        in_specs=[pl.BlockSpec((E, tt), lambda i: (0, i)), _full((E, 1)), _full((tt, tt))],
        out_specs=[pl.BlockSpec((TOP_K, tt), lambda i: (0, i)), pl.BlockSpec((TOP_K, tt), lambda i: (0, i))],
        out_shape=[jax.ShapeDtypeStruct((TOP_K, T), jnp.int32), jax.ShapeDtypeStruct((TOP_K, T), F32)],
        scratch_shapes=[pltpu.VMEM((E, 1), F32)],
        compiler_params=_cparams(("arbitrary",)),
        name="route_assign",
    )(logits_t, pstart.astype(F32).reshape(E, 1), tri)
    return dest, gates, tile_e, n_used, n_tiles


def _row_copy(src_ref, s, dst_ref, d, sem):
    return pltpu.make_async_copy(src_ref.at[pl.ds(s, 1)], dst_ref.at[pl.ds(d, 1)], sem)


def _dispatch_body(dest_ref, hp_ref, xs_in_ref, xs_ref, sem):
    del xs_in_ref
    tl = hp_ref.shape[0]

    def issue(r, c):
        for k in range(TOP_K):
            _row_copy(hp_ref, r, xs_ref, dest_ref[k * tl + r], sem).start(priority=k % 2)
        return c

    lax.fori_loop(0, tl, issue, 0)

    def drain(r, c):
        for k in range(TOP_K):
            _row_copy(hp_ref, r, xs_ref, dest_ref[k * tl + r], sem).wait()
        return c

    lax.fori_loop(0, tl, drain, 0)


def _dispatch(hp, dest_tiles, n_rows):
    T, W = hp.shape
    tl = TOK_TILE
    xs0 = jnp.zeros((n_rows, W), jnp.uint32)
    return pl.pallas_call(
        _dispatch_body,
        grid=(T // tl,),
        in_specs=[pl.BlockSpec((TOP_K * tl,), lambda i: (i,), memory_space=pltpu.SMEM),
                  pl.BlockSpec((tl, W), lambda i: (i, 0)),
                  pl.BlockSpec(memory_space=pl.ANY)],
        out_specs=pl.BlockSpec(memory_space=pl.ANY),
        out_shape=jax.ShapeDtypeStruct((n_rows, W), jnp.uint32),
        scratch_shapes=[pltpu.SemaphoreType.DMA],
        input_output_aliases={2: 0},
        compiler_params=_cparams(("arbitrary",)),
        name="moe_dispatch",
    )(dest_tiles, hp, xs0)


def _expert_body(te_ref, nu_ref, xs_ref, wgu_ref, bgu_ref, wd_ref, bd_ref, ys_ref, wgu_bf_ref, wd_bf_ref):
    i = pl.program_id(0)
    used = i < nu_ref[0]
    new_expert = jnp.logical_or(i == 0, te_ref[i] != te_ref[jnp.maximum(i - 1, 0)])

    @pl.when(jnp.logical_and(used, new_expert))
    def _():
        wgu_bf_ref[...] = wgu_ref[...].astype(BF16)
        wd_bf_ref[...] = wd_ref[...].astype(BF16)

    @pl.when(used)
    def _():
        x_hi, x_lo = _unpack_bf16_pairs(xs_ref[...])
        half = D_MODEL // 2
        gu = _dot(x_hi, wgu_bf_ref[:half, :]) + _dot(x_lo, wgu_bf_ref[half:, :]) + bgu_ref[...]
        gt = jnp.minimum(gu[:, :D_FF], SWIGLU_LIMIT)
        up = jnp.clip(gu[:, D_FF:], -SWIGLU_LIMIT, SWIGLU_LIMIT)
        act = (up + 1.0) * (gt * jax.nn.sigmoid(SWIGLU_ALPHA * gt))
        ys_ref[...] = _dot(act.astype(BF16), wd_bf_ref[...]) + bd_ref[...]

    @pl.when(jnp.logical_not(used))
    def _():
        ys_ref[...] = jnp.zeros_like(ys_ref)


def _experts(xs, tile_e, n_used, layer, w_gu, b_gu, w_down, b_down):
    n_rows, W = xs.shape
    n_tiles = n_rows // MOE_TILE
    grid_spec = pltpu.PrefetchScalarGridSpec(
        num_scalar_prefetch=2,
        grid=(n_tiles,),
        in_specs=[pl.BlockSpec((MOE_TILE, W), lambda i, te, nu: (i, 0)),
                  pl.BlockSpec((None, None, D_MODEL, 2 * D_FF), lambda i, te, nu: (layer, te[i], 0, 0)),
                  pl.BlockSpec((None, None, 1, 2 * D_FF), lambda i, te, nu: (layer, te[i], 0, 0)),
                  pl.BlockSpec((None, None, D_FF, D_MODEL), lambda i, te, nu: (layer, te[i], 0, 0)),
                  pl.BlockSpec((None, None, 1, D_MODEL), lambda i, te, nu: (layer, te[i], 0, 0))],
        out_specs=pl.BlockSpec((MOE_TILE, D_MODEL), lambda i, te, nu: (i, 0)),
        scratch_shapes=[pltpu.VMEM((D_MODEL, 2 * D_FF), BF16), pltpu.VMEM((D_FF, D_MODEL), BF16)],
    )
    return pl.pallas_call(
        _expert_body,
        grid_spec=grid_spec,
        out_shape=jax.ShapeDtypeStruct((n_rows, D_MODEL), F32),
        compiler_params=_cparams(("arbitrary",)),
        name="moe_experts",
    )(tile_e, n_used, xs, w_gu, b_gu.reshape(DEPTH, N_EXPERTS, 1, 2 * D_FF), w_down,
      b_down.reshape(DEPTH, N_EXPERTS, 1, D_MODEL))


def _combine_body(dest_ref, x_ref, gate_ref, g2_ref, ng_ref, ys_ref, o_ref, buf_ref, sem, *, final_norm):
    tl = x_ref.shape[0]

    def issue(r, c):
        for k in range(TOP_K):
            _row_copy(ys_ref, dest_ref[k * tl + r], buf_ref.at[k], r, sem).start(priority=k % 2)
        return c

    lax.fori_loop(0, tl, issue, 0)

    def drain(r, c):
        for k in range(TOP_K):
            _row_copy(ys_ref, dest_ref[k * tl + r], buf_ref.at[k], r, sem).wait()
        return c

    lax.fori_loop(0, tl, drain, 0)

    g = gate_ref[...]
    acc = g[:, 0:1] * buf_ref[0]
    for k in range(1, TOP_K):
        acc = acc + g[:, k:k + 1] * buf_ref[k]
    xn = x_ref[...] + _scale_rows(acc, g2_ref[...])
    if final_norm:
        xn = xn * lax.rsqrt(jnp.mean(xn * xn, axis=-1, keepdims=True) + EPS) * ng_ref[...]
    o_ref[...] = xn


def _combine(x, ys, dest_tiles, gates_t, tok_off, mod, final_g, ctx):
    R, D = x.shape
    tl = TOK_TILE
    t0 = tok_off // tl
    final_norm = final_g is not None
    ng = final_g.reshape(1, D) if final_norm else jnp.ones((1, D), F32)
    return pl.pallas_call(
        functools.partial(_combine_body, final_norm=final_norm),
        grid=(R // tl,),
        in_specs=[pl.BlockSpec((TOP_K * tl,), lambda i: (t0 + i,), memory_space=pltpu.SMEM),
                  pl.BlockSpec((tl, D), lambda i: (i, 0)),
                  pl.BlockSpec((tl, TOP_K), lambda i: (t0 + i, 0)),
                  _mod_spec(5, ctx), _full((1, D)),
                  pl.BlockSpec(memory_space=pl.ANY)],
        out_specs=pl.BlockSpec((tl, D), lambda i: (i, 0)),
        out_shape=jax.ShapeDtypeStruct((R, D), F32),
        scratch_shapes=[pltpu.VMEM((TOP_K, tl, D), F32), pltpu.SemaphoreType.DMA],
        compiler_params=_cparams(("arbitrary",)),
        name="moe_combine",
    )(dest_tiles, x, gates_t, mod, ng, ys)


def _moe(hp, logits_t, layer, w_gu, b_gu, w_down, b_down):
    T = hp.shape[0]
    dest, gates, tile_e, n_used, n_tiles = _routing(logits_t)
    nt = T // TOK_TILE
    dest_tiles = dest.reshape(TOP_K, nt, TOK_TILE).transpose(1, 0, 2).reshape(-1)
    xs = _dispatch(hp, dest_tiles, n_tiles * MOE_TILE)
    ys = _experts(xs, tile_e, n_used, layer, w_gu, b_gu, w_down, b_down)
    return ys, dest_tiles, gates.T


def _token_mixers(x, pos, mod, h0, p, consts, ctx, need_out):
    R, D = x.shape
    L = R // SUBLANES
    xa, ga, xb, xc, xd = _in_projection(x, pos, mod, p["norm1_g"], p["w_in"], p["b_in"], ctx)
    yf, yb, hfin = _rglru(xa, p["conv_a_w"], p["conv_a_b"], p["wg"], p["bg"], p["rg_lambda"], h0, L)
    if not need_out:
        return None, hfin
    yp = _pool_mixer(xb, p["w_pool"], p["b_pool"], p["pool_scale"], L)
    cl, sl = consts["dft"][L]
    yc = _fourier_mixer(xc.reshape(L, SUBLANES * W_GROUP), cl, sl, consts["cc"], consts["sc"], p["w_four"],
                        p["b_four"], L).reshape(R, W_GROUP)
    yd = _conformer(xd, p["conv_d_w"], p["conv_d_b"], p["ln_d_g"], p["ln_d_b"], consts["avg"], p["w_pw"],
                    p["b_pw"], L)
    return (yf, yb, ga, yp, yc, yd), hfin


def _pos_embed(n_tokens):
    rows_n = n_tokens // GRID_W
    row = jnp.repeat(jnp.arange(rows_n), GRID_W).astype(F32)
    col = jnp.tile(jnp.arange(GRID_W), rows_n).astype(F32)
    q = D_MODEL // 4
    omega = 1.0 / (10000.0 ** (jnp.arange(q, dtype=F32) / q))

    def emb(v):
        ang = v[:, None] * omega[None, :]
        return jnp.concatenate([jnp.sin(ang), jnp.cos(ang)], axis=-1)

    return jnp.concatenate([emb(row), emb(col)], axis=-1)


def _layer_params(l, w_in, b_in, conv_a_w, conv_a_b, w_rg_r, b_rg_r, w_rg_i, b_rg_i, rg_lambda, w_pool, b_pool,
                  pool_scale, w_four, b_four, conv_d_w, conv_d_b, ln_d_g, ln_d_b, w_pw, b_pw, norm1_g):
    wg = jnp.stack([jnp.concatenate([_block_diag(w_rg_r[l, d]), _block_diag(w_rg_i[l, d])], axis=1)
                    for d in range(2)]).astype(BF16)
    bg = jnp.concatenate([b_rg_r[l].reshape(2, 1, W_GROUP), b_rg_i[l].reshape(2, 1, W_GROUP)], axis=-1)
    return dict(
        norm1_g=norm1_g[l], w_in=w_in[l].astype(BF16), b_in=b_in[l],
        conv_a_w=conv_a_w[l], conv_a_b=conv_a_b[l], wg=wg, bg=bg, rg_lambda=rg_lambda[l],
        w_pool=_block_diag(w_pool[l]).astype(BF16), b_pool=b_pool[l], pool_scale=pool_scale[l],
        w_four=_block_diag(w_four[l]).astype(BF16), b_four=b_four[l],
        conv_d_w=conv_d_w[l], conv_d_b=conv_d_b[l], ln_d_g=ln_d_g[l], ln_d_b=ln_d_b[l],
        w_pw=w_pw[l].astype(BF16), b_pw=b_pw[l])


def kernel(x, c, ctx, c_ctx, w_mod, b_mod, norm1_g, norm2_g, w_in, b_in, conv_a_w, conv_a_b, w_rg_r, b_rg_r,
           w_rg_i, b_rg_i, rg_lambda, w_pool, b_pool, pool_scale, w_four, b_four, conv_d_w, conv_d_b, ln_d_g,
           ln_d_b, w_pw, b_pw, w_out, b_out, w_router, b_router, w_gu, b_gu, w_down, b_down, final_norm_g):
    bn, L, D = x.shape
    Lc = ctx.shape[1]
    assert bn == SUBLANES and D == D_MODEL

    pos = _pos_embed(L)
    c_rows = jnp.concatenate([c, jnp.broadcast_to(c_ctx[None], (MOD_ROWS - bn, D))], axis=0)
    mod = _modulation(c_rows, w_mod, b_mod)
    x = jnp.transpose(x, (1, 0, 2)).reshape(L * bn, D)
    ctx = jnp.transpose(ctx, (1, 0, 2)).reshape(Lc * bn, D)

    cc1, sc1 = _dft_matrices(D_SUB)
    eye = jnp.eye(N_SUB, dtype=F32)
    consts = dict(
        dft={n: tuple(m.astype(BF16) for m in _dft_matrices(n)) for n in sorted({L, Lc})},
        cc=jnp.kron(eye, cc1).astype(BF16), sc=jnp.kron(eye, sc1).astype(BF16),
        avg=jnp.kron(eye, jnp.full((D_SUB, D_SUB), 1.0 / D_SUB, F32)).astype(BF16))
    h_zero = jnp.zeros((2, SUBLANES, W_GROUP), F32)

    for l in range(DEPTH):
        last = l == DEPTH - 1
        p = _layer_params(l, w_in, b_in, conv_a_w, conv_a_b, w_rg_r, b_rg_r, w_rg_i, b_rg_i, rg_lambda, w_pool,
                          b_pool, pool_scale, w_four, b_four, conv_d_w, conv_d_b, ln_d_g, ln_d_b, w_pw, b_pw,
                          norm1_g)
        mod3 = mod[l]
        w_out_bf = w_out[l].astype(BF16)
        wr_t = w_router[l].T
        wr_hi = wr_t.astype(BF16)
        wr2 = jnp.concatenate([wr_hi, (wr_t - wr_hi.astype(F32)).astype(BF16)], axis=0)
        x_pos = pos if l == 0 else None

        mix_c, h_ctx = _token_mixers(ctx, None, mod3, h_zero, p, consts, True, not last)
        mix_x, _ = _token_mixers(x, x_pos, mod3, h_ctx, p, consts, False, True)
        x, hp_x, lg_x = _out_projection(x, x_pos, mix_x, mod3, norm2_g[l], w_out_bf, b_out[l], wr2,
                                        b_router[l], False)
        if not last:
            ctx, hp_c, lg_c = _out_projection(ctx, None, mix_c, mod3, norm2_g[l], w_out_bf, b_out[l], wr2,
                                              b_router[l], True)
            hp = jnp.concatenate([hp_c, hp_x], axis=0)
            lg = jnp.concatenate([lg_c, lg_x], axis=1)
            ys, dest_tiles, gates_t = _moe(hp, lg, l, w_gu, b_gu, w_down, b_down)
            ctx = _combine(ctx, ys, dest_tiles, gates_t, 0, mod3, None, True)
            x = _combine(x, ys, dest_tiles, gates_t, bn * Lc, mod3, None, False)
        else:
            ys, dest_tiles, gates_t = _moe(hp_x, lg_x, l, w_gu, b_gu, w_down, b_down)
            x = _combine(x, ys, dest_tiles, gates_t, 0, mod3, final_norm_g, False)
    return jnp.transpose(x.reshape(L, bn, D), (1, 0, 2))
```

```python
import functools
import math

import jax
import jax.numpy as jnp
from jax import lax
from jax.experimental import pallas as pl
from jax.experimental.pallas import tpu as pltpu

F32 = jnp.float32
BF16 = jnp.bfloat16

D_MODEL = 1024
DEPTH = 2
GRID_W = 64
W_GROUP = 256
N_SUB = 4
D_SUB = 64
D_IN = 6 * W_GROUP
RG_CONV = 4
RG_C = 8.0
CONF_KERNEL = 31
N_EXPERTS = 32
TOP_K = 4
D_FF = D_MODEL
SWIGLU_LIMIT = 7.0
SWIGLU_ALPHA = 1.702
EPS = 1e-6

SUBLANES = 8
VMEM_LIMIT_BYTES = 56 * 1024 * 1024
MOD_ROWS = 16
RG_HALO = 8 * SUBLANES
POOL_HALO = 8 * SUBLANES
CONF_HALO = 16 * SUBLANES
MOE_TILE = 512
TOK_TILE = 256
SC_WINDOW = 64
ROUTE_TILE = 512


def _cparams(sem):
    return pltpu.CompilerParams(dimension_semantics=sem, vmem_limit_bytes=VMEM_LIMIT_BYTES)


def _full(shape):
    nd = len(shape)
    return pl.BlockSpec(shape, lambda *_: (0,) * nd)


def _dot(a, b):
    return jnp.dot(a, b, preferred_element_type=F32)


def _split_bf16(v):
    hi = v.astype(BF16)
    lo = (v - hi.astype(F32)).astype(BF16)
    return hi, lo


def _mod_body(c_ref, w_ref, b_ref, o_ref):
    c = c_ref[...]
    s = c * jax.nn.sigmoid(c)
    o_ref[...] = jnp.dot(s, w_ref[...], precision=lax.Precision.HIGHEST,
                         preferred_element_type=F32) + b_ref[...]


def _modulation(c_rows, w_mod, b_mod):
    tn = 1536
    n6 = 6 * D_MODEL
    return pl.pallas_call(
        _mod_body,
        grid=(DEPTH, n6 // tn),
        in_specs=[_full((MOD_ROWS, D_MODEL)),
                  pl.BlockSpec((None, D_MODEL, tn), lambda l, j: (l, 0, j)),
                  pl.BlockSpec((None, 1, tn), lambda l, j: (l, 0, j))],
        out_specs=pl.BlockSpec((None, MOD_ROWS, tn), lambda l, j: (l, 0, j)),
        out_shape=jax.ShapeDtypeStruct((DEPTH, MOD_ROWS, n6), F32),
        compiler_params=_cparams(("parallel", "parallel")),
        name="modulation",
    )(c_rows, w_mod, b_mod.reshape(DEPTH, 1, n6))


def _mod_spec(chunk, ctx):
    return pl.BlockSpec((SUBLANES, D_MODEL), lambda i: (1 if ctx else 0, chunk))


def _scale_rows(v, m):
    r, d = v.shape
    return (v.reshape(r // SUBLANES, SUBLANES, d) * m[None]).reshape(r, d)


def _rms_mod(x, g, shift, scale):
    r, d = x.shape
    y = x * lax.rsqrt(jnp.mean(x * x, axis=-1, keepdims=True) + EPS) * g
    y3 = y.reshape(r // SUBLANES, SUBLANES, d)
    return (y3 * (1.0 + scale)[None] + shift[None]).reshape(r, d)


def _load_plus_pos(x_ref, pos_ref):
    s = SUBLANES
    return jnp.concatenate([x_ref[t * s:(t + 1) * s, :] + pos_ref[t:t + 1, :] for t in range(pos_ref.shape[0])],
                           axis=0)


def _inproj_body(*refs, add_pos):
    if add_pos:
        x_ref, pos_ref, sh_ref, sc_ref, g_ref, w_ref, b_ref, xa_ref, ga_ref, xb_ref, xc_ref, xd_ref = refs
        x = _load_plus_pos(x_ref, pos_ref)
    else:
        x_ref, sh_ref, sc_ref, g_ref, w_ref, b_ref, xa_ref, ga_ref, xb_ref, xc_ref, xd_ref = refs
        x = x_ref[...]
    u = _rms_mod(x, g_ref[...], sh_ref[...], sc_ref[...])
    p = _dot(u.astype(BF16), w_ref[...]) + b_ref[...]
    xa_ref[...] = p[:, 0:256]
    ga_ref[...] = p[:, 256:512]
    xb_ref[...] = p[:, 512:768]
    xc_ref[...] = p[:, 768:1024].astype(BF16)
    xd_ref[...] = p[:, 1024:1536]


def _in_projection(x, pos, mod, norm_g, w_in_bf, b_in, ctx):
    R, D = x.shape
    tr = min(R, 512)
    add_pos = pos is not None
    row = lambda w: pl.BlockSpec((tr, w), lambda i: (i, 0))
    in_specs = [row(D)]
    args = [x]
    if add_pos:
        in_specs.append(pl.BlockSpec((tr // SUBLANES, D), lambda i: (i, 0)))
        args.append(pos)
    in_specs += [_mod_spec(0, ctx), _mod_spec(1, ctx), _full((1, D)), _full((D, D_IN)), _full((1, D_IN))]
    args += [mod, mod, norm_g.reshape(1, D), w_in_bf, b_in.reshape(1, D_IN)]
    out_shape = [jax.ShapeDtypeStruct((R, 256), F32)] * 3 + [
        jax.ShapeDtypeStruct((R, 256), BF16), jax.ShapeDtypeStruct((R, 512), F32)]
    return pl.pallas_call(
        functools.partial(_inproj_body, add_pos=add_pos),
        grid=(R // tr,),
        in_specs=in_specs,
        out_specs=[row(256), row(256), row(256), row(256), row(512)],
        out_shape=out_shape,
        compiler_params=_cparams(("parallel",)),
        name="in_projection",
    )(*args)


def _rg_gates(xc, wg, bg, lam):
    g = _dot(xc.astype(BF16), wg) + bg
    r = jax.nn.sigmoid(g[:, :W_GROUP])
    gi = jax.nn.sigmoid(g[:, W_GROUP:])
    z = -lam
    softplus = jnp.maximum(z, 0.0) + jnp.log1p(jnp.exp(-jnp.abs(z)))
    log_a = (-RG_C) * r * softplus
    a = jnp.exp(log_a)
    b = jnp.sqrt(-jnp.tanh(log_a) * (a * a + 1.0)) * (gi * xc)
    return a, b


def _rg_body(xf_ref, xfh_ref, xr_ref, xrh_ref, cw_ref, cb_ref, wg_ref, bg_ref, lam_ref, h0_ref,
             yf_ref, yb_ref, hfin_ref, af_ref, ab_ref, hc_ref, *, n, tt):
    i = pl.program_id(0)
    tr = tt * SUBLANES
    keep = RG_HALO - (RG_CONV - 1) * SUBLANES

    @pl.when(i == 0)
    def _():
        hc_ref[...] = h0_ref[...]

    halo = jnp.where(i > 0, xfh_ref[...], 0.0)
    ext = jnp.concatenate([halo[keep:], xf_ref[...]], axis=0)
    xc = cb_ref[0]
    for k in range(RG_CONV):
        xc = xc + cw_ref[0, k:k + 1, :] * ext[k * SUBLANES:k * SUBLANES + tr]
    a, b = _rg_gates(xc, wg_ref[0], bg_ref[0], lam_ref[0])
    af_ref[...] = a
    yf_ref[...] = b

    halo = jnp.where(i > 0, xrh_ref[...], 0.0)
    ext = jnp.concatenate([xr_ref[...], halo[:(RG_CONV - 1) * SUBLANES]], axis=0)
    xc = cb_ref[1]
    for k in range(RG_CONV):
        o = (RG_CONV - 1 - k) * SUBLANES
        xc = xc + cw_ref[1, k:k + 1, :] * ext[o:o + tr]
    a, b = _rg_gates(xc, wg_ref[1], bg_ref[1], lam_ref[1])
    ab_ref[...] = a
    yb_ref[...] = b

    def step(t, carry):
        hf, hb = carry
        rf = pl.multiple_of(t * SUBLANES, SUBLANES)
        hf = af_ref[pl.ds(rf, SUBLANES), :] * hf + yf_ref[pl.ds(rf, SUBLANES), :]
        yf_ref[pl.ds(rf, SUBLANES), :] = hf
        rb = pl.multiple_of((tt - 1 - t) * SUBLANES, SUBLANES)
        hb = ab_ref[pl.ds(rb, SUBLANES), :] * hb + yb_ref[pl.ds(rb, SUBLANES), :]
        yb_ref[pl.ds(rb, SUBLANES), :] = hb
        return hf, hb

    hf, hb = lax.fori_loop(0, tt, step, (hc_ref[0], hc_ref[1]), unroll=8)
    hc_ref[0] = hf
    hc_ref[1] = hb
    hfin_ref[0] = hf
    hfin_ref[1] = hb


def _rglru(xa2, conv_w, conv_b, wg_bf, bg, lam, h0, L):
    tt = min(L, 256)
    n = L // tt
    tr = tt * SUBLANES
    per = tr // RG_HALO
    last_halo = L * SUBLANES // RG_HALO - 1
    row = lambda i: (i, 0)
    rev = lambda i: (n - 1 - i, 0)
    in_specs = [
        pl.BlockSpec((tr, W_GROUP), row),
        pl.BlockSpec((RG_HALO, W_GROUP), lambda i: (jnp.maximum(i * per - 1, 0), 0)),
        pl.BlockSpec((tr, W_GROUP), rev),
        pl.BlockSpec((RG_HALO, W_GROUP), lambda i: (jnp.minimum((n - i) * per, last_halo), 0)),
        _full((2, RG_CONV, W_GROUP)), _full((2, 1, W_GROUP)), _full((2, W_GROUP, 2 * W_GROUP)),
        _full((2, 1, 2 * W_GROUP)), _full((2, 1, W_GROUP)), _full((2, SUBLANES, W_GROUP)),
    ]
    return pl.pallas_call(
        functools.partial(_rg_body, n=n, tt=tt),
        grid=(n,),
        in_specs=in_specs,
        out_specs=[pl.BlockSpec((tr, W_GROUP), row), pl.BlockSpec((tr, W_GROUP), rev),
                   _full((2, SUBLANES, W_GROUP))],
        out_shape=[jax.ShapeDtypeStruct((L * SUBLANES, W_GROUP), F32)] * 2
        + [jax.ShapeDtypeStruct((2, SUBLANES, W_GROUP), F32)],
        scratch_shapes=[pltpu.VMEM((tr, W_GROUP), F32), pltpu.VMEM((tr, W_GROUP), F32),
                        pltpu.VMEM((2, SUBLANES, W_GROUP), F32)],
        compiler_params=_cparams(("arbitrary",)),
        name="rglru",
    )(xa2, xa2, xa2, xa2, conv_w, conv_b.reshape(2, 1, W_GROUP), wg_bf, bg, lam.reshape(2, 1, W_GROUP), h0)


def _pool_body(xm_ref, xp_ref, xn_ref, w_ref, b_ref, s_ref, o_ref, *, n, tt, L):
    i = pl.program_id(0)
    tr = tt * SUBLANES
    S = SUBLANES
    xm = xm_ref[...]
    prev = jnp.where(i > 0, xp_ref[...], 0.0)
    nxt = jnp.where(i < n - 1, xn_ref[...], 0.0)
    xe = jnp.concatenate([prev, xm, nxt], axis=0)
    e = xe.shape[0]
    p2 = xe[S:e] + xe[0:e - S]
    n4 = (tt + 13) * S
    p4 = p2[0:n4] + p2[2 * S:2 * S + n4]
    n8 = (tt + 9) * S
    p8 = p4[0:n8] + p4[4 * S:4 * S + n8]
    s16 = p8[0:tr] + p8[8 * S:8 * S + tr]
    s2 = p2[7 * S:7 * S + tr]
    s4 = p4[6 * S:6 * S + tr]
    s8 = p8[4 * S:4 * S + tr]
    grp = lax.broadcasted_iota(jnp.int32, (1, W_GROUP), 1) // D_SUB
    half = jnp.left_shift(1, grp)
    t = i * tt + lax.broadcasted_iota(jnp.int32, (tr, 1), 0) // S
    cnt = (jnp.minimum(t + half, L) - jnp.maximum(t - half, 0)).astype(F32)
    s = jnp.where(grp == 0, s2, jnp.where(grp == 1, s4, jnp.where(grp == 2, s8, s16)))
    pooled = s / cnt - xm
    y = _dot(pooled.astype(BF16), w_ref[...]) + b_ref[...]
    o_ref[...] = y * s_ref[...]


def _pool_mixer(xb2, w_bd_bf, b, scale, L):
    tt = min(L, 256)
    n = L // tt
    tr = tt * SUBLANES
    per = tr // POOL_HALO
    last_halo = L * SUBLANES // POOL_HALO - 1
    return pl.pallas_call(
        functools.partial(_pool_body, n=n, tt=tt, L=L),
        grid=(n,),
        in_specs=[pl.BlockSpec((tr, W_GROUP), lambda i: (i, 0)),
                  pl.BlockSpec((POOL_HALO, W_GROUP), lambda i: (jnp.maximum(i * per - 1, 0), 0)),
                  pl.BlockSpec((POOL_HALO, W_GROUP), lambda i: (jnp.minimum((i + 1) * per, last_halo), 0)),
                  _full((W_GROUP, W_GROUP)), _full((1, W_GROUP)), _full((1, W_GROUP))],
        out_specs=pl.BlockSpec((tr, W_GROUP), lambda i: (i, 0)),
        out_shape=jax.ShapeDtypeStruct((L * SUBLANES, W_GROUP), F32),
        compiler_params=_cparams(("parallel",)),
        name="pool_mixer",
    )(xb2, xb2, xb2, w_bd_bf, b.reshape(1, W_GROUP), scale.reshape(1, W_GROUP))


def _fourier_body(c_ref, s_ref, x_ref, cc_ref, sc_ref, w_ref, b_ref, o_ref):
    x = x_ref[...]
    z1 = _dot(c_ref[...], x)
    z2 = _dot(s_ref[...], x)
    for j in range(x.shape[1] // W_GROUP):
        sl = slice(j * W_GROUP, (j + 1) * W_GROUP)
        a_hi, a_lo = _split_bf16(z1[:, sl])
        b_hi, b_lo = _split_bf16(z2[:, sl])
        f = (_dot(a_hi, cc_ref[...]) + _dot(a_lo, cc_ref[...])) - (_dot(b_hi, sc_ref[...]) + _dot(b_lo, sc_ref[...]))
        o_ref[:, sl] = _dot(f.astype(BF16), w_ref[...]) + b_ref[...]


def _fourier_mixer(xc_tm, cl, sl, cc, sc, w_bd_bf, b, L):
    ncol = xc_tm.shape[1]
    nb = min(ncol, 1024)
    tk = min(L, 256)
    return pl.pallas_call(
        _fourier_body,
        grid=(ncol // nb, L // tk),
        in_specs=[pl.BlockSpec((tk, L), lambda j, k: (k, 0)),
                  pl.BlockSpec((tk, L), lambda j, k: (k, 0)),
                  pl.BlockSpec((L, nb), lambda j, k: (0, j)),
                  _full((W_GROUP, W_GROUP)), _full((W_GROUP, W_GROUP)), _full((W_GROUP, W_GROUP)),
                  _full((1, W_GROUP))],
        out_specs=pl.BlockSpec((tk, nb), lambda j, k: (k, j)),
        out_shape=jax.ShapeDtypeStruct((L, ncol), F32),
        compiler_params=_cparams(("parallel", "parallel")),
        name="fourier_mixer",
    )(cl, sl, xc_tm, cc, sc, w_bd_bf, b.reshape(1, W_GROUP))


def _dft_matrices(L):
    f = 1 << (max(L.bit_length() - 1, 0) // 2)
    n = jnp.arange(L, dtype=jnp.int32)[None, :]

    def table(rows):
        ang = ((rows[:, None] * n) % L).astype(F32) * (2.0 * math.pi / L)
        return jnp.cos(ang), jnp.sin(ang)

    ac, as_ = table(jnp.arange(L // f, dtype=jnp.int32) * f)
    bc, bs = table(jnp.arange(f, dtype=jnp.int32))
    scale = 1.0 / math.sqrt(L)
    cos = (ac[:, None, :] * bc[None, :, :] - as_[:, None, :] * bs[None, :, :]).reshape(L, L) * scale
    sin = (as_[:, None, :] * bc[None, :, :] + ac[:, None, :] * bs[None, :, :]).reshape(L, L) * scale
    return cos, sin


def _block_diag(w):
    g, a, b = w.shape
    eye = jnp.eye(g, dtype=w.dtype)
    return (eye[:, None, :, None] * w[:, :, None, :]).reshape(g * a, g * b)


CONF_CHUNK = 64


def _conformer_body(xm_ref, xp_ref, xn_ref, cw_ref, cb_ref, lg_ref, lb_ref, avg_ref, w_ref, b_ref,
                    o_ref, v_ref, c_ref, *, n, tt):
    i = pl.program_id(0)
    tr = tt * SUBLANES
    H = CONF_HALO

    def glu(v):
        return v[:, :W_GROUP] * jax.nn.sigmoid(v[:, W_GROUP:])

    v_ref[0:H] = jnp.where(i > 0, glu(xp_ref[...]), 0.0)
    v_ref[H:H + tr] = glu(xm_ref[...])
    v_ref[H + tr:H + tr + H] = jnp.where(i < n - 1, glu(xn_ref[...]), 0.0)

    def chunk(c, carry):
        r0 = pl.multiple_of(c * CONF_CHUNK, CONF_CHUNK)
        acc = jnp.broadcast_to(cb_ref[...], (CONF_CHUNK, W_GROUP))
        for k in range(CONF_KERNEL):
            acc = acc + cw_ref[k:k + 1, :] * v_ref[pl.ds(r0 + (k + 1) * SUBLANES, CONF_CHUNK), :]
        c_ref[pl.ds(r0, CONF_CHUNK), :] = acc
        return carry

    lax.fori_loop(0, tr // CONF_CHUNK, chunk, 0)

    v = c_ref[...]
    avg = avg_ref[...]
    v_hi, v_lo = _split_bf16(v)
    mu = _dot(v_hi, avg) + _dot(v_lo, avg)
    d = v - mu
    q_hi, q_lo = _split_bf16(d * d)
    var = _dot(q_hi, avg) + _dot(q_lo, avg)
    vn = d * lax.rsqrt(var + EPS) * lg_ref[...] + lb_ref[...]
    act = vn * jax.nn.sigmoid(vn)
    o_ref[...] = _dot(act.astype(BF16), w_ref[...]) + b_ref[...]


def _conformer(xd2, conv_w, conv_b, ln_g, ln_b, avg_bf, w_pw_bf, b_pw, L):
    tt = min(L, 256)
    n = L // tt
    tr = tt * SUBLANES
    per = tr // CONF_HALO
    last_halo = L * SUBLANES // CONF_HALO - 1
    vec = lambda a: a.reshape(1, W_GROUP)
    return pl.pallas_call(
        functools.partial(_conformer_body, n=n, tt=tt),
        grid=(n,),
        in_specs=[pl.BlockSpec((tr, 2 * W_GROUP), lambda i: (i, 0)),
                  pl.BlockSpec((CONF_HALO, 2 * W_GROUP), lambda i: (jnp.maximum(i * per - 1, 0), 0)),
                  pl.BlockSpec((CONF_HALO, 2 * W_GROUP), lambda i: (jnp.minimum((i + 1) * per, last_halo), 0)),
                  _full((CONF_KERNEL, W_GROUP)), _full((1, W_GROUP)), _full((1, W_GROUP)), _full((1, W_GROUP)),
                  _full((W_GROUP, W_GROUP)), _full((W_GROUP, W_GROUP)), _full((1, W_GROUP))],
        out_specs=pl.BlockSpec((tr, W_GROUP), lambda i: (i, 0)),
        out_shape=jax.ShapeDtypeStruct((L * SUBLANES, W_GROUP), F32),
        scratch_shapes=[pltpu.VMEM((tr + 2 * CONF_HALO, W_GROUP), F32), pltpu.VMEM((tr, W_GROUP), F32)],
        compiler_params=_cparams(("parallel",)),
        name="conformer",
    )(xd2, xd2, xd2, conv_w, vec(conv_b), vec(ln_g), vec(ln_b), avg_bf, w_pw_bf, vec(b_pw))


def _gelu_tanh(x):
    return 0.5 * x * (1.0 + jnp.tanh(math.sqrt(2.0 / math.pi) * (x + 0.044715 * (x * x * x))))


def _pack_bf16_pairs(h_bf):
    u = pltpu.bitcast(h_bf.astype(F32), jnp.uint32)
    half = h_bf.shape[1] // 2
    return (u[:, :half] & jnp.uint32(0xFFFF0000)) | (u[:, half:] >> 16)


def _unpack_bf16_pairs(p):
    hi = pltpu.bitcast(p & jnp.uint32(0xFFFF0000), F32).astype(BF16)
    lo = pltpu.bitcast(p << 16, F32).astype(BF16)
    return hi, lo


def _outproj_body(*refs, add_pos):
    if add_pos:
        (x_ref, pos_ref, yf_ref, yb_ref, ga_ref, yp_ref, yc_ref, yd_ref, g1_ref, sh_ref, sc_ref, ng_ref,
         wo_ref, bo_ref, wr_ref, br_ref, xo_ref, hp_ref, lg_ref) = refs
        x = _load_plus_pos(x_ref, pos_ref)
    else:
        (x_ref, yf_ref, yb_ref, ga_ref, yp_ref, yc_ref, yd_ref, g1_ref, sh_ref, sc_ref, ng_ref,
         wo_ref, bo_ref, wr_ref, br_ref, xo_ref, hp_ref, lg_ref) = refs
        x = x_ref[...]
    ya = (yf_ref[...] + yb_ref[...]) * _gelu_tanh(ga_ref[...])
    ycat = jnp.concatenate([ya, yp_ref[...], yc_ref[...], yd_ref[...]], axis=1).astype(BF16)
    y = _dot(ycat, wo_ref[...]) + bo_ref[...]
    xn = x + _scale_rows(y, g1_ref[...])
    xo_ref[...] = xn
    h = _rms_mod(xn, ng_ref[...], sh_ref[...], sc_ref[...])
    h_hi, h_lo = _split_bf16(h)
    nt = (((1,), (1,)), ((), ()))
    wr = wr_ref[...]
    acc = lax.dot_general(wr, h_hi, nt, preferred_element_type=F32)
    acc = acc + lax.dot_general(wr, h_lo, nt, preferred_element_type=F32)
    lg_ref[...] = acc[:N_EXPERTS] + acc[N_EXPERTS:] + br_ref[...]
    hp_ref[...] = _pack_bf16_pairs(h_hi)


def _out_projection(x, pos, mix, mod, norm_g, w_out_bf, b_out, wr2_bf, b_router, ctx):
    R, D = x.shape
    tr = min(R, 512)
    add_pos = pos is not None
    row = lambda w: pl.BlockSpec((tr, w), lambda i: (i, 0))
    in_specs = [row(D)]
    args = [x]
    if add_pos:
        in_specs.append(pl.BlockSpec((tr // SUBLANES, D), lambda i: (i, 0)))
        args.append(pos)
    in_specs += [row(W_GROUP)] * 6
    args += list(mix)
    in_specs += [_mod_spec(2, ctx), _mod_spec(3, ctx), _mod_spec(4, ctx), _full((1, D)), _full((D, D)),
                 _full((1, D)), _full((2 * N_EXPERTS, D)), _full((N_EXPERTS, 1))]
    args += [mod, mod, mod, norm_g.reshape(1, D), w_out_bf, b_out.reshape(1, D), wr2_bf,
             b_router.reshape(N_EXPERTS, 1)]
    return pl.pallas_call(
        functools.partial(_outproj_body, add_pos=add_pos),
        grid=(R // tr,),
        in_specs=in_specs,
        out_specs=[row(D), row(D // 2), pl.BlockSpec((N_EXPERTS, tr), lambda i: (0, i))],
        out_shape=[jax.ShapeDtypeStruct((R, D), F32),
                   jax.ShapeDtypeStruct((R, D // 2), jnp.uint32),
                   jax.ShapeDtypeStruct((N_EXPERTS, R), F32)],
        compiler_params=_cparams(("parallel",)),
        name="out_projection",
    )(*args)


def _top4(v):
    eid = lax.broadcasted_iota(jnp.int32, v.shape, 0)
    out = []
    work = v
    for _ in range(TOP_K):
        m = jnp.max(work, axis=0, keepdims=True)
        idx = jnp.min(jnp.where(work == m, eid, N_EXPERTS), axis=0, keepdims=True)
        oh = eid == idx
        out.append((m, oh))
        work = jnp.where(oh, -jnp.inf, work)
    return out


def _count_body(lg_ref, cnt_ref):
    @pl.when(pl.program_id(0) == 0)
    def _():
        cnt_ref[...] = jnp.zeros_like(cnt_ref)

    sel = jnp.zeros(lg_ref.shape, F32)
    for _, oh in _top4(lg_ref[...]):
        sel = sel + oh.astype(F32)
    cnt_ref[...] += jnp.sum(sel, axis=1, keepdims=True)


def _route_body(lg_ref, ps_ref, tri_ref, dest_ref, gate_ref, carry_ref):
    @pl.when(pl.program_id(0) == 0)
    def _():
        carry_ref[...] = jnp.zeros_like(carry_ref)

    top = _top4(lg_ref[...])
    sel = jnp.zeros(lg_ref.shape, F32)
    for _, oh in top:
        sel = sel + oh.astype(F32)
    before = _dot(sel.astype(BF16), tri_ref[...]) + carry_ref[...] + ps_ref[...]
    m0 = top[0][0]
    es = [jnp.exp(m - m0) for m, _ in top]
    den = es[0] + es[1] + es[2] + es[3]
    for k, (_, oh) in enumerate(top):
        dest_ref[k:k + 1, :] = jnp.sum(jnp.where(oh, before, 0.0), axis=0, keepdims=True).astype(jnp.int32)
        gate_ref[k:k + 1, :] = es[k] / den
    carry_ref[...] += jnp.sum(sel, axis=1, keepdims=True)


def _routing(logits_t):
    E, T = logits_t.shape
    tt = ROUTE_TILE
    nt = T // tt
    counts = pl.pallas_call(
        _count_body,
        grid=(nt,),
        in_specs=[pl.BlockSpec((E, tt), lambda i: (0, i))],
        out_specs=_full((E, 1)),
        out_shape=jax.ShapeDtypeStruct((E, 1), F32),
        compiler_params=_cparams(("arbitrary",)),
        name="route_count",
    )(logits_t)
    cnt = counts[:, 0].astype(jnp.int32)
    padded = ((cnt + MOE_TILE - 1) // MOE_TILE) * MOE_TILE
    pend = jnp.cumsum(padded)
    pstart = pend - padded
    n_tiles = -(-(T * TOP_K) // MOE_TILE) + N_EXPERTS
    tile_start = jnp.arange(n_tiles, dtype=jnp.int32) * MOE_TILE
    tile_e = jnp.minimum(jnp.sum((pend[None, :] <= tile_start[:, None]).astype(jnp.int32), axis=1), N_EXPERTS - 1)
    n_used = (pend[-1] // MOE_TILE).astype(jnp.int32).reshape(1)
    n_valid = jnp.clip((pstart + cnt)[tile_e] - tile_start, 0, MOE_TILE).astype(jnp.int32)
    tri = (jnp.arange(tt)[:, None] < jnp.arange(tt)[None, :]).astype(BF16)
    dest, gates = pl.pallas_call(
        _route_body,
        grid=(nt,),
        in_specs=[pl.BlockSpec((E, tt), lambda i: (0, i)), _full((E, 1)), _full((tt, tt))],
        out_specs=[pl.BlockSpec((TOP_K, tt), lambda i: (0, i)), pl.BlockSpec((TOP_K, tt), lambda i: (0, i))],
        out_shape=[jax.ShapeDtypeStruct((TOP_K, T), jnp.int32), jax.ShapeDtypeStruct((TOP_K, T), F32)],
        scratch_shapes=[pltpu.VMEM((E, 1), F32)],
        compiler_params=_cparams(("arbitrary",)),
        name="route_assign",
    )(logits_t, pstart.astype(F32).reshape(E, 1), tri)
    return dest, gates, tile_e, n_valid, n_used, n_tiles


def _sc_workers():
    from jax.experimental.pallas import tpu_sc as plsc
    mesh = plsc.VectorSubcoreMesh(core_axis_name="c", subcore_axis_name="s")
    n_workers = mesh.num_cores * mesh.num_subcores
    worker = lambda: lax.axis_index("s") * mesh.num_cores + lax.axis_index("c")
    return mesh, n_workers, worker


def _dispatch(hp, dest_flat, n_rows):
    T, W = hp.shape
    mesh, n_workers, worker = _sc_workers()
    per = dest_flat.shape[0] // n_workers
    assert per * n_workers == dest_flat.shape[0] and per % SC_WINDOW == 0 and T % SC_WINDOW == 0

    @functools.partial(
        pl.kernel, mesh=mesh, out_type=jax.ShapeDtypeStruct((n_rows, W), hp.dtype),
        scratch_types=[pltpu.VMEM((SC_WINDOW,), jnp.int32), pltpu.VMEM((SC_WINDOW, W), hp.dtype),
                       pltpu.SemaphoreType.DMA],
        name="moe_dispatch")
    def scatter(hp_hbm, dest_hbm, xs_hbm, idx_v, rows_v, sem):
        base = worker() * per

        @pl.loop(0, per // SC_WINDOW)
        def _(j):
            off = pl.multiple_of(base + j * SC_WINDOW, SC_WINDOW)
            tok = pl.multiple_of(lax.rem(off, T), SC_WINDOW)
            pltpu.sync_copy(dest_hbm.at[pl.ds(off, SC_WINDOW)], idx_v)
            pltpu.sync_copy(hp_hbm.at[pl.ds(tok, SC_WINDOW)], rows_v)
            pltpu.async_copy(rows_v, xs_hbm.at[idx_v], sem).wait()

    return scatter(hp, dest_flat)


def _gather_rows(table, idx_flat):
    n = idx_flat.shape[0]
    W = table.shape[1]
    mesh, n_workers, worker = _sc_workers()
    per = n // n_workers
    assert per * n_workers == n and per % SC_WINDOW == 0

    @functools.partial(
        pl.kernel, mesh=mesh, out_type=jax.ShapeDtypeStruct((n, W), table.dtype),
        scratch_types=[pltpu.VMEM((SC_WINDOW,), jnp.int32), pltpu.VMEM((SC_WINDOW, W), table.dtype),
                       pltpu.SemaphoreType.DMA],
        name="moe_gather")
    def gather(table_hbm, idx_hbm, out_hbm, idx_v, rows_v, sem):
        base = worker() * per

        @pl.loop(0, per // SC_WINDOW)
        def _(j):
            off = pl.multiple_of(base + j * SC_WINDOW, SC_WINDOW)
            pltpu.sync_copy(idx_hbm.at[pl.ds(off, SC_WINDOW)], idx_v)
            pltpu.async_copy(table_hbm.at[idx_v], rows_v, sem).wait()
            pltpu.sync_copy(rows_v, out_hbm.at[pl.ds(off, SC_WINDOW)])

    return gather(table, idx_flat)


def _expert_body(te_ref, nv_ref, nu_ref, xs_ref, wgu_ref, bgu_ref, wd_ref, bd_ref, ys_ref, wgu_bf_ref, wd_bf_ref):
    i = pl.program_id(0)
    used = i < nu_ref[0]
    new_expert = jnp.logical_or(i == 0, te_ref[i] != te_ref[jnp.maximum(i - 1, 0)])

    @pl.when(jnp.logical_and(used, new_expert))
    def _():
        wgu_bf_ref[...] = wgu_ref[...].astype(BF16)
        wd_bf_ref[...] = wd_ref[...].astype(BF16)

    @pl.when(used)
    def _():
        live = lax.broadcasted_iota(jnp.int32, (MOE_TILE, 1), 0) < nv_ref[i]
        x_hi, x_lo = _unpack_bf16_pairs(jnp.where(live, xs_ref[...], jnp.uint32(0)))
        half = D_MODEL // 2
        gu = _dot(x_hi, wgu_bf_ref[:half, :]) + _dot(x_lo, wgu_bf_ref[half:, :]) + bgu_ref[...]
        gt = jnp.minimum(gu[:, :D_FF], SWIGLU_LIMIT)
        up = jnp.clip(gu[:, D_FF:], -SWIGLU_LIMIT, SWIGLU_LIMIT)
        act = (up + 1.0) * (gt * jax.nn.sigmoid(SWIGLU_ALPHA * gt))
        y = _dot(act.astype(BF16), wd_bf_ref[...]) + bd_ref[...]
        ys_ref[...] = _pack_bf16_pairs(y.astype(BF16))

    @pl.when(jnp.logical_not(used))
    def _():
        ys_ref[...] = jnp.zeros_like(ys_ref)


def _experts(xs, tile_e, n_valid, n_used, layer, w_gu, b_gu, w_down, b_down):
    n_rows, W = xs.shape
    n_tiles = n_rows // MOE_TILE
    grid_spec = pltpu.PrefetchScalarGridSpec(
        num_scalar_prefetch=3,
        grid=(n_tiles,),
        in_specs=[pl.BlockSpec((MOE_TILE, W), lambda i, te, nv, nu: (i, 0)),
                  pl.BlockSpec((None, None, D_MODEL, 2 * D_FF), lambda i, te, nv, nu: (layer, te[i], 0, 0)),
                  pl.BlockSpec((None, None, 1, 2 * D_FF), lambda i, te, nv, nu: (layer, te[i], 0, 0)),
                  pl.BlockSpec((None, None, D_FF, D_MODEL), lambda i, te, nv, nu: (layer, te[i], 0, 0)),
                  pl.BlockSpec((None, None, 1, D_MODEL), lambda i, te, nv, nu: (layer, te[i], 0, 0))],
        out_specs=pl.BlockSpec((MOE_TILE, W), lambda i, te, nv, nu: (i, 0)),
        scratch_shapes=[pltpu.VMEM((D_MODEL, 2 * D_FF), BF16), pltpu.VMEM((D_FF, D_MODEL), BF16)],
    )
    return pl.pallas_call(
        _expert_body,
        grid_spec=grid_spec,
        out_shape=jax.ShapeDtypeStruct((n_rows, W), jnp.uint32),
        compiler_params=_cparams(("arbitrary",)),
        name="moe_experts",
    )(tile_e, n_valid, n_used, xs, w_gu, b_gu.reshape(DEPTH, N_EXPERTS, 1, 2 * D_FF), w_down,
      b_down.reshape(DEPTH, N_EXPERTS, 1, D_MODEL))


def _combine_body(x_ref, y0_ref, y1_ref, y2_ref, y3_ref, gate_ref, g2_ref, ng_ref, o_ref, *, final_norm):
    g = gate_ref[...]
    acc_hi = acc_lo = None
    for k, y_ref in enumerate((y0_ref, y1_ref, y2_ref, y3_ref)):
        p = y_ref[...]
        gk = g[:, k:k + 1]
        hi = gk * pltpu.bitcast(p & jnp.uint32(0xFFFF0000), F32)
        lo = gk * pltpu.bitcast(p << 16, F32)
        acc_hi = hi if acc_hi is None else acc_hi + hi
        acc_lo = lo if acc_lo is None else acc_lo + lo
    acc = jnp.concatenate([acc_hi, acc_lo], axis=1)
    xn = x_ref[...] + _scale_rows(acc, g2_ref[...])
    if final_norm:
        xn = xn * lax.rsqrt(jnp.mean(xn * xn, axis=-1, keepdims=True) + EPS) * ng_ref[...]
    o_ref[...] = xn


def _combine(x, ysg, gates_t, tok_off, mod, final_g, ctx):
    R, D = x.shape
    tl = TOK_TILE
    t0 = tok_off // tl
    nt = gates_t.shape[0] // tl
    final_norm = final_g is not None
    ng = final_g.reshape(1, D) if final_norm else jnp.ones((1, D), F32)
    y_specs = [pl.BlockSpec((tl, D // 2), lambda i, k=k: (k * nt + t0 + i, 0)) for k in range(TOP_K)]
    return pl.pallas_call(
        functools.partial(_combine_body, final_norm=final_norm),
        grid=(R // tl,),
        in_specs=[pl.BlockSpec((tl, D), lambda i: (i, 0))] + y_specs
        + [pl.BlockSpec((tl, TOP_K), lambda i: (t0 + i, 0)), _mod_spec(5, ctx), _full((1, D))],
        out_specs=pl.BlockSpec((tl, D), lambda i: (i, 0)),
        out_shape=jax.ShapeDtypeStruct((R, D), F32),
        compiler_params=_cparams(("parallel",)),
        name="moe_combine",
    )(x, ysg, ysg, ysg, ysg, gates_t, mod, ng)


def _moe(hp, logits_t, layer, w_gu, b_gu, w_down, b_down):
    dest, gates, tile_e, n_valid, n_used, n_tiles = _routing(logits_t)
    dest_flat = dest.reshape(-1)
    xs = _dispatch(hp, dest_flat, n_tiles * MOE_TILE)
    ys = _experts(xs, tile_e, n_valid, n_used, layer, w_gu, b_gu, w_down, b_down)
    return _gather_rows(ys, dest_flat), gates.T


def _token_mixers(x, pos, mod, h0, p, consts, ctx, need_out):
    R, D = x.shape
    L = R // SUBLANES
    xa, ga, xb, xc, xd = _in_projection(x, pos, mod, p["norm1_g"], p["w_in"], p["b_in"], ctx)
    yf, yb, hfin = _rglru(xa, p["conv_a_w"], p["conv_a_b"], p["wg"], p["bg"], p["rg_lambda"], h0, L)
    if not need_out:
        return None, hfin
    yp = _pool_mixer(xb, p["w_pool"], p["b_pool"], p["pool_scale"], L)
    cl, sl = consts["dft"][L]
    yc = _fourier_mixer(xc.reshape(L, SUBLANES * W_GROUP), cl, sl, consts["cc"], consts["sc"], p["w_four"],
                        p["b_four"], L).reshape(R, W_GROUP)
    yd = _conformer(xd, p["conv_d_w"], p["conv_d_b"], p["ln_d_g"], p["ln_d_b"], consts["avg"], p["w_pw"],
                    p["b_pw"], L)
    return (yf, yb, ga, yp, yc, yd), hfin


def _pos_embed(n_tokens):
    rows_n = n_tokens // GRID_W
    row = jnp.repeat(jnp.arange(rows_n), GRID_W).astype(F32)
    col = jnp.tile(jnp.arange(GRID_W), rows_n).astype(F32)
    q = D_MODEL // 4
    omega = 1.0 / (10000.0 ** (jnp.arange(q, dtype=F32) / q))

    def emb(v):
        ang = v[:, None] * omega[None, :]
        return jnp.concatenate([jnp.sin(ang), jnp.cos(ang)], axis=-1)

    return jnp.concatenate([emb(row), emb(col)], axis=-1)


def _layer_params(l, w_in, b_in, conv_a_w, conv_a_b, w_rg_r, b_rg_r, w_rg_i, b_rg_i, rg_lambda, w_pool, b_pool,
                  pool_scale, w_four, b_four, conv_d_w, conv_d_b, ln_d_g, ln_d_b, w_pw, b_pw, norm1_g):
    wg = jnp.stack([jnp.concatenate([_block_diag(w_rg_r[l, d]), _block_diag(w_rg_i[l, d])], axis=1)
                    for d in range(2)]).astype(BF16)
    bg = jnp.concatenate([b_rg_r[l].reshape(2, 1, W_GROUP), b_rg_i[l].reshape(2, 1, W_GROUP)], axis=-1)
    return dict(
        norm1_g=norm1_g[l], w_in=w_in[l].astype(BF16), b_in=b_in[l],
        conv_a_w=conv_a_w[l], conv_a_b=conv_a_b[l], wg=wg, bg=bg, rg_lambda=rg_lambda[l],
        w_pool=_block_diag(w_pool[l]).astype(BF16), b_pool=b_pool[l], pool_scale=pool_scale[l],
        w_four=_block_diag(w_four[l]).astype(BF16), b_four=b_four[l],
        conv_d_w=conv_d_w[l], conv_d_b=conv_d_b[l], ln_d_g=ln_d_g[l], ln_d_b=ln_d_b[l],
        w_pw=w_pw[l].astype(BF16), b_pw=b_pw[l])


def kernel(x, c, ctx, c_ctx, w_mod, b_mod, norm1_g, norm2_g, w_in, b_in, conv_a_w, conv_a_b, w_rg_r, b_rg_r,
           w_rg_i, b_rg_i, rg_lambda, w_pool, b_pool, pool_scale, w_four, b_four, conv_d_w, conv_d_b, ln_d_g,
           ln_d_b, w_pw, b_pw, w_out, b_out, w_router, b_router, w_gu, b_gu, w_down, b_down, final_norm_g):
    bn, L, D = x.shape
    Lc = ctx.shape[1]
    assert bn == SUBLANES and D == D_MODEL

    pos = _pos_embed(L)
    c_rows = jnp.concatenate([c, jnp.broadcast_to(c_ctx[None], (MOD_ROWS - bn, D))], axis=0)
    mod = _modulation(c_rows, w_mod, b_mod)
    x = jnp.transpose(x, (1, 0, 2)).reshape(L * bn, D)
    ctx = jnp.transpose(ctx, (1, 0, 2)).reshape(Lc * bn, D)

    cc1, sc1 = _dft_matrices(D_SUB)
    eye = jnp.eye(N_SUB, dtype=F32)
    consts = dict(
        dft={n: tuple(m.astype(BF16) for m in _dft_matrices(n)) for n in sorted({L, Lc})},
        cc=jnp.kron(eye, cc1).astype(BF16), sc=jnp.kron(eye, sc1).astype(BF16),
        avg=jnp.kron(eye, jnp.full((D_SUB, D_SUB), 1.0 / D_SUB, F32)).astype(BF16))
    h_zero = jnp.zeros((2, SUBLANES, W_GROUP), F32)

    for l in range(DEPTH):
        last = l == DEPTH - 1
        p = _layer_params(l, w_in, b_in, conv_a_w, conv_a_b, w_rg_r, b_rg_r, w_rg_i, b_rg_i, rg_lambda, w_pool,
                          b_pool, pool_scale, w_four, b_four, conv_d_w, conv_d_b, ln_d_g, ln_d_b, w_pw, b_pw,
                          norm1_g)
        mod3 = mod[l]
        w_out_bf = w_out[l].astype(BF16)
        wr_t = w_router[l].T
        wr_hi = wr_t.astype(BF16)
        wr2 = jnp.concatenate([wr_hi, (wr_t - wr_hi.astype(F32)).astype(BF16)], axis=0)
        x_pos = pos if l == 0 else None

        mix_c, h_ctx = _token_mixers(ctx, None, mod3, h_zero, p, consts, True, not last)
        mix_x, _ = _token_mixers(x, x_pos, mod3, h_ctx, p, consts, False, True)
        x, hp_x, lg_x = _out_projection(x, x_pos, mix_x, mod3, norm2_g[l], w_out_bf, b_out[l], wr2,
                                        b_router[l], False)
        if not last:
            ctx, hp_c, lg_c = _out_projection(ctx, None, mix_c, mod3, norm2_g[l], w_out_bf, b_out[l], wr2,
                                              b_router[l], True)
            hp = jnp.concatenate([hp_c, hp_x], axis=0)
            lg = jnp.concatenate([lg_c, lg_x], axis=1)
            ysg, gates_t = _moe(hp, lg, l, w_gu, b_gu, w_down, b_down)
            ctx = _combine(ctx, ysg, gates_t, 0, mod3, None, True)
            x = _combine(x, ysg, gates_t, bn * Lc, mod3, None, False)
        else:
            ysg, gates_t = _moe(hp_x, lg_x, l, w_gu, b_gu, w_down, b_down)
            x = _combine(x, ysg, gates_t, 0, mod3, final_norm_g, False)
    return jnp.transpose(x.reshape(L, bn, D), (1, 0, 2))
```

```python
import functools
import math

import jax
import jax.numpy as jnp
from jax import lax
from jax.experimental import pallas as pl
from jax.experimental.pallas import tpu as pltpu

F32 = jnp.float32
BF16 = jnp.bfloat16

D_MODEL = 1024
DEPTH = 2
GRID_W = 64
W_GROUP = 256
N_SUB = 4
D_SUB = 64
D_IN = 6 * W_GROUP
RG_CONV = 4
RG_C = 8.0
CONF_KERNEL = 31
N_EXPERTS = 32
TOP_K = 4
D_FF = D_MODEL
SWIGLU_LIMIT = 7.0
SWIGLU_ALPHA = 1.702
EPS = 1e-6

SUBLANES = 8
VMEM_LIMIT_BYTES = 56 * 1024 * 1024
MOD_ROWS = 16
RG_HALO = 8 * SUBLANES
POOL_HALO = 8 * SUBLANES
CONF_HALO = 16 * SUBLANES
MOE_TILE = 512
TOK_TILE = 256
SC_WINDOW = 64
ROUTE_TILE = 512


def _cparams(sem):
    return pltpu.CompilerParams(dimension_semantics=sem, vmem_limit_bytes=VMEM_LIMIT_BYTES)


def _full(shape):
    nd = len(shape)
    return pl.BlockSpec(shape, lambda *_: (0,) * nd)


def _dot(a, b):
    return jnp.dot(a, b, preferred_element_type=F32)


def _split_bf16(v):
    hi = v.astype(BF16)
    lo = (v - hi.astype(F32)).astype(BF16)
    return hi, lo


def _mod_body(c_ref, w_ref, b_ref, o_ref):
    c = c_ref[...]
    s = c * jax.nn.sigmoid(c)
    o_ref[...] = jnp.dot(s, w_ref[...], precision=lax.Precision.HIGHEST,
                         preferred_element_type=F32) + b_ref[...]


def _modulation(c_rows, w_mod, b_mod):
    tn = 1536
    n6 = 6 * D_MODEL
    return pl.pallas_call(
        _mod_body,
        grid=(DEPTH, n6 // tn),
        in_specs=[_full((MOD_ROWS, D_MODEL)),
                  pl.BlockSpec((None, D_MODEL, tn), lambda l, j: (l, 0, j)),
                  pl.BlockSpec((None, 1, tn), lambda l, j: (l, 0, j))],
        out_specs=pl.BlockSpec((None, MOD_ROWS, tn), lambda l, j: (l, 0, j)),
        out_shape=jax.ShapeDtypeStruct((DEPTH, MOD_ROWS, n6), F32),
        compiler_params=_cparams(("parallel", "parallel")),
        name="modulation",
    )(c_rows, w_mod, b_mod.reshape(DEPTH, 1, n6))


def _mod_spec(chunk, ctx):
    return pl.BlockSpec((SUBLANES, D_MODEL), lambda i: (1 if ctx else 0, chunk))


def _scale_rows(v, m):
    r, d = v.shape
    return (v.reshape(r // SUBLANES, SUBLANES, d) * m[None]).reshape(r, d)


def _rms_mod(x, g, shift, scale):
    r, d = x.shape
    y = x * lax.rsqrt(jnp.mean(x * x, axis=-1, keepdims=True) + EPS) * g
    y3 = y.reshape(r // SUBLANES, SUBLANES, d)
    return (y3 * (1.0 + scale)[None] + shift[None]).reshape(r, d)


def _load_plus_pos(x_ref, pos_ref):
    s = SUBLANES
    return jnp.concatenate([x_ref[t * s:(t + 1) * s, :] + pos_ref[t:t + 1, :] for t in range(pos_ref.shape[0])],
                           axis=0)


def _inproj_body(*refs, add_pos):
    if add_pos:
        x_ref, pos_ref, sh_ref, sc_ref, g_ref, w_ref, b_ref, xa_ref, ga_ref, xb_ref, xc_ref, xd_ref = refs
        x = _load_plus_pos(x_ref, pos_ref)
    else:
        x_ref, sh_ref, sc_ref, g_ref, w_ref, b_ref, xa_ref, ga_ref, xb_ref, xc_ref, xd_ref = refs
        x = x_ref[...]
    u = _rms_mod(x, g_ref[...], sh_ref[...], sc_ref[...])
    p = _dot(u.astype(BF16), w_ref[...]) + b_ref[...]
    xa_ref[...] = p[:, 0:256]
    ga_ref[...] = p[:, 256:512]
    xb_ref[...] = p[:, 512:768]
    xc_ref[...] = p[:, 768:1024].astype(BF16)
    xd_ref[...] = p[:, 1024:1536]


def _in_projection(x, pos, mod, norm_g, w_in_bf, b_in, ctx):
    R, D = x.shape
    tr = min(R, 512)
    add_pos = pos is not None
    row = lambda w: pl.BlockSpec((tr, w), lambda i: (i, 0))
    in_specs = [row(D)]
    args = [x]
    if add_pos:
        in_specs.append(pl.BlockSpec((tr // SUBLANES, D), lambda i: (i, 0)))
        args.append(pos)
    in_specs += [_mod_spec(0, ctx), _mod_spec(1, ctx), _full((1, D)), _full((D, D_IN)), _full((1, D_IN))]
    args += [mod, mod, norm_g.reshape(1, D), w_in_bf, b_in.reshape(1, D_IN)]
    out_shape = [jax.ShapeDtypeStruct((R, 256), F32)] * 3 + [
        jax.ShapeDtypeStruct((R, 256), BF16), jax.ShapeDtypeStruct((R, 512), F32)]
    return pl.pallas_call(
        functools.partial(_inproj_body, add_pos=add_pos),
        grid=(R // tr,),
        in_specs=in_specs,
        out_specs=[row(256), row(256), row(256), row(256), row(512)],
        out_shape=out_shape,
        compiler_params=_cparams(("parallel",)),
        name="in_projection",
    )(*args)


def _rg_gates(xc, wg, bg, lam):
    g = _dot(xc.astype(BF16), wg) + bg
    r = jax.nn.sigmoid(g[:, :W_GROUP])
    gi = jax.nn.sigmoid(g[:, W_GROUP:])
    z = -lam
    softplus = jnp.maximum(z, 0.0) + jnp.log1p(jnp.exp(-jnp.abs(z)))
    log_a = (-RG_C) * r * softplus
    a = jnp.exp(log_a)
    b = jnp.sqrt(-jnp.tanh(log_a) * (a * a + 1.0)) * (gi * xc)
    return a, b


def _rg_body(xf_ref, xfh_ref, xr_ref, xrh_ref, cw_ref, cb_ref, wg_ref, bg_ref, lam_ref, h0_ref,
             yf_ref, yb_ref, hfin_ref, af_ref, ab_ref, hc_ref, *, n, tt):
    i = pl.program_id(0)
    tr = tt * SUBLANES
    keep = RG_HALO - (RG_CONV - 1) * SUBLANES

    @pl.when(i == 0)
    def _():
        hc_ref[...] = h0_ref[...]

    halo = jnp.where(i > 0, xfh_ref[...], 0.0)
    ext = jnp.concatenate([halo[keep:], xf_ref[...]], axis=0)
    xc = cb_ref[0]
    for k in range(RG_CONV):
        xc = xc + cw_ref[0, k:k + 1, :] * ext[k * SUBLANES:k * SUBLANES + tr]
    a, b = _rg_gates(xc, wg_ref[0], bg_ref[0], lam_ref[0])
    af_ref[...] = a
    yf_ref[...] = b

    halo = jnp.where(i > 0, xrh_ref[...], 0.0)
    ext = jnp.concatenate([xr_ref[...], halo[:(RG_CONV - 1) * SUBLANES]], axis=0)
    xc = cb_ref[1]
    for k in range(RG_CONV):
        o = (RG_CONV - 1 - k) * SUBLANES
        xc = xc + cw_ref[1, k:k + 1, :] * ext[o:o + tr]
    a, b = _rg_gates(xc, wg_ref[1], bg_ref[1], lam_ref[1])
    ab_ref[...] = a
    yb_ref[...] = b

    def step(t, carry):
        hf, hb = carry
        rf = pl.multiple_of(t * SUBLANES, SUBLANES)
        hf = af_ref[pl.ds(rf, SUBLANES), :] * hf + yf_ref[pl.ds(rf, SUBLANES), :]
        yf_ref[pl.ds(rf, SUBLANES), :] = hf
        rb = pl.multiple_of((tt - 1 - t) * SUBLANES, SUBLANES)
        hb = ab_ref[pl.ds(rb, SUBLANES), :] * hb + yb_ref[pl.ds(rb, SUBLANES), :]
        yb_ref[pl.ds(rb, SUBLANES), :] = hb
        return hf, hb

    hf, hb = lax.fori_loop(0, tt, step, (hc_ref[0], hc_ref[1]), unroll=8)
    hc_ref[0] = hf
    hc_ref[1] = hb
    hfin_ref[0] = hf
    hfin_ref[1] = hb


def _rglru(xa2, conv_w, conv_b, wg_bf, bg, lam, h0, L):
    tt = min(L, 256)
    n = L // tt
    tr = tt * SUBLANES
    per = tr // RG_HALO
    last_halo = L * SUBLANES // RG_HALO - 1
    row = lambda i: (i, 0)
    rev = lambda i: (n - 1 - i, 0)
    in_specs = [
        pl.BlockSpec((tr, W_GROUP), row),
        pl.BlockSpec((RG_HALO, W_GROUP), lambda i: (jnp.maximum(i * per - 1, 0), 0)),
        pl.BlockSpec((tr, W_GROUP), rev),
        pl.BlockSpec((RG_HALO, W_GROUP), lambda i: (jnp.minimum((n - i) * per, last_halo), 0)),
        _full((2, RG_CONV, W_GROUP)), _full((2, 1, W_GROUP)), _full((2, W_GROUP, 2 * W_GROUP)),
        _full((2, 1, 2 * W_GROUP)), _full((2, 1, W_GROUP)), _full((2, SUBLANES, W_GROUP)),
    ]
    return pl.pallas_call(
        functools.partial(_rg_body, n=n, tt=tt),
        grid=(n,),
        in_specs=in_specs,
        out_specs=[pl.BlockSpec((tr, W_GROUP), row), pl.BlockSpec((tr, W_GROUP), rev),
                   _full((2, SUBLANES, W_GROUP))],
        out_shape=[jax.ShapeDtypeStruct((L * SUBLANES, W_GROUP), F32)] * 2
        + [jax.ShapeDtypeStruct((2, SUBLANES, W_GROUP), F32)],
        scratch_shapes=[pltpu.VMEM((tr, W_GROUP), F32), pltpu.VMEM((tr, W_GROUP), F32),
                        pltpu.VMEM((2, SUBLANES, W_GROUP), F32)],
        compiler_params=_cparams(("arbitrary",)),
        name="rglru",
    )(xa2, xa2, xa2, xa2, conv_w, conv_b.reshape(2, 1, W_GROUP), wg_bf, bg, lam.reshape(2, 1, W_GROUP), h0)


def _pool_body(xm_ref, xp_ref, xn_ref, w_ref, b_ref, s_ref, o_ref, *, n, tt, L):
    i = pl.program_id(0)
    tr = tt * SUBLANES
    S = SUBLANES
    xm = xm_ref[...]
    prev = jnp.where(i > 0, xp_ref[...], 0.0)
    nxt = jnp.where(i < n - 1, xn_ref[...], 0.0)
    xe = jnp.concatenate([prev, xm, nxt], axis=0)
    e = xe.shape[0]
    p2 = xe[S:e] + xe[0:e - S]
    n4 = (tt + 13) * S
    p4 = p2[0:n4] + p2[2 * S:2 * S + n4]
    n8 = (tt + 9) * S
    p8 = p4[0:n8] + p4[4 * S:4 * S + n8]
    s16 = p8[0:tr] + p8[8 * S:8 * S + tr]
    s2 = p2[7 * S:7 * S + tr]
    s4 = p4[6 * S:6 * S + tr]
    s8 = p8[4 * S:4 * S + tr]
    grp = lax.broadcasted_iota(jnp.int32, (1, W_GROUP), 1) // D_SUB
    half = jnp.left_shift(1, grp)
    t = i * tt + lax.broadcasted_iota(jnp.int32, (tr, 1), 0) // S
    cnt = (jnp.minimum(t + half, L) - jnp.maximum(t - half, 0)).astype(F32)
    s = jnp.where(grp == 0, s2, jnp.where(grp == 1, s4, jnp.where(grp == 2, s8, s16)))
    pooled = s / cnt - xm
    y = _dot(pooled.astype(BF16), w_ref[...]) + b_ref[...]
    o_ref[...] = (y * s_ref[...]).astype(BF16)


def _pool_mixer(xb2, w_bd_bf, b, scale, L):
    tt = min(L, 256)
    n = L // tt
    tr = tt * SUBLANES
    per = tr // POOL_HALO
    last_halo = L * SUBLANES // POOL_HALO - 1
    return pl.pallas_call(
        functools.partial(_pool_body, n=n, tt=tt, L=L),
        grid=(n,),
        in_specs=[pl.BlockSpec((tr, W_GROUP), lambda i: (i, 0)),
                  pl.BlockSpec((POOL_HALO, W_GROUP), lambda i: (jnp.maximum(i * per - 1, 0), 0)),
                  pl.BlockSpec((POOL_HALO, W_GROUP), lambda i: (jnp.minimum((i + 1) * per, last_halo), 0)),
                  _full((W_GROUP, W_GROUP)), _full((1, W_GROUP)), _full((1, W_GROUP))],
        out_specs=pl.BlockSpec((tr, W_GROUP), lambda i: (i, 0)),
        out_shape=jax.ShapeDtypeStruct((L * SUBLANES, W_GROUP), BF16),
        compiler_params=_cparams(("parallel",)),
        name="pool_mixer",
    )(xb2, xb2, xb2, w_bd_bf, b.reshape(1, W_GROUP), scale.reshape(1, W_GROUP))


def _fourier_body(c_ref, s_ref, x_ref, cc_ref, sc_ref, w_ref, b_ref, o_ref):
    x = x_ref[...]
    z1 = _dot(c_ref[...], x)
    z2 = _dot(s_ref[...], x)
    for j in range(x.shape[1] // W_GROUP):
        sl = slice(j * W_GROUP, (j + 1) * W_GROUP)
        a_hi, a_lo = _split_bf16(z1[:, sl])
        b_hi, b_lo = _split_bf16(z2[:, sl])
        f = (_dot(a_hi, cc_ref[...]) + _dot(a_lo, cc_ref[...])) - (_dot(b_hi, sc_ref[...]) + _dot(b_lo, sc_ref[...]))
        o_ref[:, sl] = (_dot(f.astype(BF16), w_ref[...]) + b_ref[...]).astype(BF16)


def _fourier_mixer(xc_tm, cl, sl, cc, sc, w_bd_bf, b, L):
    ncol = xc_tm.shape[1]
    nb = min(ncol, 1024)
    tk = min(L, 256)
    return pl.pallas_call(
        _fourier_body,
        grid=(ncol // nb, L // tk),
        in_specs=[pl.BlockSpec((tk, L), lambda j, k: (k, 0)),
                  pl.BlockSpec((tk, L), lambda j, k: (k, 0)),
                  pl.BlockSpec((L, nb), lambda j, k: (0, j)),
                  _full((W_GROUP, W_GROUP)), _full((W_GROUP, W_GROUP)), _full((W_GROUP, W_GROUP)),
                  _full((1, W_GROUP))],
        out_specs=pl.BlockSpec((tk, nb), lambda j, k: (k, j)),
        out_shape=jax.ShapeDtypeStruct((L, ncol), BF16),
        compiler_params=_cparams(("parallel", "parallel")),
        name="fourier_mixer",
    )(cl, sl, xc_tm, cc, sc, w_bd_bf, b.reshape(1, W_GROUP))


def _dft_matrices(L):
    f = 1 << (max(L.bit_length() - 1, 0) // 2)
    n = jnp.arange(L, dtype=jnp.int32)[None, :]

    def table(rows):
        ang = ((rows[:, None] * n) % L).astype(F32) * (2.0 * math.pi / L)
        return jnp.cos(ang), jnp.sin(ang)

    ac, as_ = table(jnp.arange(L // f, dtype=jnp.int32) * f)
    bc, bs = table(jnp.arange(f, dtype=jnp.int32))
    scale = 1.0 / math.sqrt(L)
    cos = (ac[:, None, :] * bc[None, :, :] - as_[:, None, :] * bs[None, :, :]).reshape(L, L) * scale
    sin = (as_[:, None, :] * bc[None, :, :] + ac[:, None, :] * bs[None, :, :]).reshape(L, L) * scale
    return cos, sin


def _block_diag(w):
    g, a, b = w.shape
    eye = jnp.eye(g, dtype=w.dtype)
    return (eye[:, None, :, None] * w[:, :, None, :]).reshape(g * a, g * b)


CONF_CHUNK = 64


def _conformer_body(xm_ref, xp_ref, xn_ref, cw_ref, cb_ref, lg_ref, lb_ref, avg_ref, w_ref, b_ref,
                    o_ref, v_ref, c_ref, *, n, tt):
    i = pl.program_id(0)
    tr = tt * SUBLANES
    H = CONF_HALO

    def glu(v):
        return v[:, :W_GROUP] * jax.nn.sigmoid(v[:, W_GROUP:])

    v_ref[0:H] = jnp.where(i > 0, glu(xp_ref[...]), 0.0)
    v_ref[H:H + tr] = glu(xm_ref[...])
    v_ref[H + tr:H + tr + H] = jnp.where(i < n - 1, glu(xn_ref[...]), 0.0)

    def chunk(c, carry):
        r0 = pl.multiple_of(c * CONF_CHUNK, CONF_CHUNK)
        acc = jnp.broadcast_to(cb_ref[...], (CONF_CHUNK, W_GROUP))
        for k in range(CONF_KERNEL):
            acc = acc + cw_ref[k:k + 1, :] * v_ref[pl.ds(r0 + (k + 1) * SUBLANES, CONF_CHUNK), :]
        c_ref[pl.ds(r0, CONF_CHUNK), :] = acc
        return carry

    lax.fori_loop(0, tr // CONF_CHUNK, chunk, 0)

    v = c_ref[...]
    avg = avg_ref[...]
    v_hi, v_lo = _split_bf16(v)
    mu = _dot(v_hi, avg) + _dot(v_lo, avg)
    d = v - mu
    q_hi, q_lo = _split_bf16(d * d)
    var = _dot(q_hi, avg) + _dot(q_lo, avg)
    vn = d * lax.rsqrt(var + EPS) * lg_ref[...] + lb_ref[...]
    act = vn * jax.nn.sigmoid(vn)
    o_ref[...] = (_dot(act.astype(BF16), w_ref[...]) + b_ref[...]).astype(BF16)


def _conformer(xd2, conv_w, conv_b, ln_g, ln_b, avg_bf, w_pw_bf, b_pw, L):
    tt = min(L, 256)
    n = L // tt
    tr = tt * SUBLANES
    per = tr // CONF_HALO
    last_halo = L * SUBLANES // CONF_HALO - 1
    vec = lambda a: a.reshape(1, W_GROUP)
    return pl.pallas_call(
        functools.partial(_conformer_body, n=n, tt=tt),
        grid=(n,),
        in_specs=[pl.BlockSpec((tr, 2 * W_GROUP), lambda i: (i, 0)),
                  pl.BlockSpec((CONF_HALO, 2 * W_GROUP), lambda i: (jnp.maximum(i * per - 1, 0), 0)),
                  pl.BlockSpec((CONF_HALO, 2 * W_GROUP), lambda i: (jnp.minimum((i + 1) * per, last_halo), 0)),
                  _full((CONF_KERNEL, W_GROUP)), _full((1, W_GROUP)), _full((1, W_GROUP)), _full((1, W_GROUP)),
                  _full((W_GROUP, W_GROUP)), _full((W_GROUP, W_GROUP)), _full((1, W_GROUP))],
        out_specs=pl.BlockSpec((tr, W_GROUP), lambda i: (i, 0)),
        out_shape=jax.ShapeDtypeStruct((L * SUBLANES, W_GROUP), BF16),
        scratch_shapes=[pltpu.VMEM((tr + 2 * CONF_HALO, W_GROUP), F32), pltpu.VMEM((tr, W_GROUP), F32)],
        compiler_params=_cparams(("parallel",)),
        name="conformer",
    )(xd2, xd2, xd2, conv_w, vec(conv_b), vec(ln_g), vec(ln_b), avg_bf, w_pw_bf, vec(b_pw))


def _gelu_tanh(x):
    return 0.5 * x * (1.0 + jnp.tanh(math.sqrt(2.0 / math.pi) * (x + 0.044715 * (x * x * x))))


def _pack_bf16_pairs(h_bf):
    u = pltpu.bitcast(h_bf.astype(F32), jnp.uint32)
    half = h_bf.shape[1] // 2
    return (u[:, :half] & jnp.uint32(0xFFFF0000)) | (u[:, half:] >> 16)


def _unpack_bf16_pairs(p):
    hi = pltpu.bitcast(p & jnp.uint32(0xFFFF0000), F32).astype(BF16)
    lo = pltpu.bitcast(p << 16, F32).astype(BF16)
    return hi, lo


def _outproj_body(*refs, add_pos):
    if add_pos:
        (x_ref, pos_ref, yf_ref, yb_ref, ga_ref, yp_ref, yc_ref, yd_ref, g1_ref, sh_ref, sc_ref, ng_ref,
         wo_ref, bo_ref, wr_ref, br_ref, xo_ref, hp_ref, lg_ref) = refs
        x = _load_plus_pos(x_ref, pos_ref)
    else:
        (x_ref, yf_ref, yb_ref, ga_ref, yp_ref, yc_ref, yd_ref, g1_ref, sh_ref, sc_ref, ng_ref,
         wo_ref, bo_ref, wr_ref, br_ref, xo_ref, hp_ref, lg_ref) = refs
        x = x_ref[...]
    ya = (yf_ref[...] + yb_ref[...]) * _gelu_tanh(ga_ref[...])
    ycat = jnp.concatenate([ya.astype(BF16), yp_ref[...], yc_ref[...], yd_ref[...]], axis=1)
    y = _dot(ycat, wo_ref[...]) + bo_ref[...]
    xn = x + _scale_rows(y, g1_ref[...])
    xo_ref[...] = xn
    h = _rms_mod(xn, ng_ref[...], sh_ref[...], sc_ref[...])
    h_hi, h_lo = _split_bf16(h)
    nt = (((1,), (1,)), ((), ()))
    wr = wr_ref[...]
    acc = lax.dot_general(wr, h_hi, nt, preferred_element_type=F32)
    acc = acc + lax.dot_general(wr, h_lo, nt, preferred_element_type=F32)
    lg_ref[...] = acc[:N_EXPERTS] + acc[N_EXPERTS:] + br_ref[...]
    hp_ref[...] = _pack_bf16_pairs(h_hi)


def _out_projection(x, pos, mix, mod, norm_g, w_out_bf, b_out, wr2_bf, b_router, ctx):
    R, D = x.shape
    tr = min(R, 512)
    add_pos = pos is not None
    row = lambda w: pl.BlockSpec((tr, w), lambda i: (i, 0))
    in_specs = [row(D)]
    args = [x]
    if add_pos:
        in_specs.append(pl.BlockSpec((tr // SUBLANES, D), lambda i: (i, 0)))
        args.append(pos)
    in_specs += [row(W_GROUP)] * 6
    args += list(mix)
    in_specs += [_mod_spec(2, ctx), _mod_spec(3, ctx), _mod_spec(4, ctx), _full((1, D)), _full((D, D)),
                 _full((1, D)), _full((2 * N_EXPERTS, D)), _full((N_EXPERTS, 1))]
    args += [mod, mod, mod, norm_g.reshape(1, D), w_out_bf, b_out.reshape(1, D), wr2_bf,
             b_router.reshape(N_EXPERTS, 1)]
    return pl.pallas_call(
        functools.partial(_outproj_body, add_pos=add_pos),
        grid=(R // tr,),
        in_specs=in_specs,
        out_specs=[row(D), row(D // 2), pl.BlockSpec((N_EXPERTS, tr), lambda i: (0, i))],
        out_shape=[jax.ShapeDtypeStruct((R, D), F32),
                   jax.ShapeDtypeStruct((R, D // 2), jnp.uint32),
                   jax.ShapeDtypeStruct((N_EXPERTS, R), F32)],
        compiler_params=_cparams(("parallel",)),
        name="out_projection",
    )(*args)


def _top4(v):
    eid = lax.broadcasted_iota(jnp.int32, v.shape, 0)
    out = []
    work = v
    for _ in range(TOP_K):
        m = jnp.max(work, axis=0, keepdims=True)
        idx = jnp.min(jnp.where(work == m, eid, N_EXPERTS), axis=0, keepdims=True)
        oh = eid == idx
        out.append((m, oh))
        work = jnp.where(oh, -jnp.inf, work)
    return out


def _count_body(lg_ref, cnt_ref):
    @pl.when(pl.program_id(0) == 0)
    def _():
        cnt_ref[...] = jnp.zeros_like(cnt_ref)

    sel = jnp.zeros(lg_ref.shape, F32)
    for _, oh in _top4(lg_ref[...]):
        sel = sel + oh.astype(F32)
    cnt_ref[...] += jnp.sum(sel, axis=1, keepdims=True)


def _route_body(lg_ref, ps_ref, tri_ref, dest_ref, gate_ref, carry_ref):
    @pl.when(pl.program_id(0) == 0)
    def _():
        carry_ref[...] = jnp.zeros_like(carry_ref)

    top = _top4(lg_ref[...])
    sel = jnp.zeros(lg_ref.shape, F32)
    for _, oh in top:
        sel = sel + oh.astype(F32)
    before = _dot(sel.astype(BF16), tri_ref[...]) + carry_ref[...] + ps_ref[...]
    m0 = top[0][0]
    es = [jnp.exp(m - m0) for m, _ in top]
    den = es[0] + es[1] + es[2] + es[3]
    for k, (_, oh) in enumerate(top):
        dest_ref[k:k + 1, :] = jnp.sum(jnp.where(oh, before, 0.0), axis=0, keepdims=True).astype(jnp.int32)
        gate_ref[k:k + 1, :] = es[k] / den
    carry_ref[...] += jnp.sum(sel, axis=1, keepdims=True)


def _routing(logits_t):
    E, T = logits_t.shape
    tt = ROUTE_TILE
    nt = T // tt
    counts = pl.pallas_call(
        _count_body,
        grid=(nt,),
        in_specs=[pl.BlockSpec((E, tt), lambda i: (0, i))],
        out_specs=_full((E, 1)),
        out_shape=jax.ShapeDtypeStruct((E, 1), F32),
        compiler_params=_cparams(("arbitrary",)),
        name="route_count",
    )(logits_t)
    cnt = counts[:, 0].astype(jnp.int32)
    padded = ((cnt + MOE_TILE - 1) // MOE_TILE) * MOE_TILE
    pend = jnp.cumsum(padded)
    pstart = pend - padded
    n_tiles = -(-(T * TOP_K) // MOE_TILE) + N_EXPERTS
    tile_start = jnp.arange(n_tiles, dtype=jnp.int32) * MOE_TILE
    tile_e = jnp.minimum(jnp.sum((pend[None, :] <= tile_start[:, None]).astype(jnp.int32), axis=1), N_EXPERTS - 1)
    n_used = (pend[-1] // MOE_TILE).astype(jnp.int32).reshape(1)
    n_valid = jnp.clip((pstart + cnt)[tile_e] - tile_start, 0, MOE_TILE).astype(jnp.int32)
    tri = (jnp.arange(tt)[:, None] < jnp.arange(tt)[None, :]).astype(BF16)
    dest, gates = pl.pallas_call(
        _route_body,
        grid=(nt,),
        in_specs=[pl.BlockSpec((E, tt), lambda i: (0, i)), _full((E, 1)), _full((tt, tt))],
        out_specs=[pl.BlockSpec((TOP_K, tt), lambda i: (0, i)), pl.BlockSpec((TOP_K, tt), lambda i: (0, i))],
        out_shape=[jax.ShapeDtypeStruct((TOP_K, T), jnp.int32), jax.ShapeDtypeStruct((TOP_K, T), F32)],
        scratch_shapes=[pltpu.VMEM((E, 1), F32)],
        compiler_params=_cparams(("arbitrary",)),
        name="route_assign",
    )(logits_t, pstart.astype(F32).reshape(E, 1), tri)
    return dest, gates, tile_e, n_valid, n_used, n_tiles


def _sc_workers():
    from jax.experimental.pallas import tpu_sc as plsc
    mesh = plsc.VectorSubcoreMesh(core_axis_name="c", subcore_axis_name="s")
    n_workers = mesh.num_cores * mesh.num_subcores
    worker = lambda: lax.axis_index("s") * mesh.num_cores + lax.axis_index("c")
    return mesh, n_workers, worker


def _dispatch(hp, dest_flat, n_rows):
    T, W = hp.shape
    mesh, n_workers, worker = _sc_workers()
    per = dest_flat.shape[0] // n_workers
    steps = per // SC_WINDOW
    assert per * n_workers == dest_flat.shape[0] and steps * SC_WINDOW == per and steps % 2 == 0
    assert T % SC_WINDOW == 0

    @functools.partial(
        pl.kernel, mesh=mesh, out_type=jax.ShapeDtypeStruct((n_rows, W), hp.dtype),
        scratch_types=[pltpu.VMEM((SC_WINDOW,), jnp.int32), pltpu.VMEM((SC_WINDOW,), jnp.int32),
                       pltpu.VMEM((SC_WINDOW, W), hp.dtype), pltpu.VMEM((SC_WINDOW, W), hp.dtype),
                       pltpu.SemaphoreType.DMA, pltpu.SemaphoreType.DMA],
        name="moe_dispatch")
    def scatter(hp_hbm, dest_hbm, xs_hbm, idx0, idx1, rows0, rows1, sem0, sem1):
        base = worker() * per
        bufs = ((idx0, rows0, sem0), (idx1, rows1, sem1))

        def window(j, b, first):
            idx_v, rows_v, sem = bufs[b]

            @pl.when(jnp.logical_not(first))
            def _():
                pltpu.make_async_copy(rows_v, xs_hbm.at[idx_v], sem).wait()

            off = pl.multiple_of(base + j * SC_WINDOW, SC_WINDOW)
            tok = pl.multiple_of(lax.rem(off, T), SC_WINDOW)
            pltpu.sync_copy(dest_hbm.at[pl.ds(off, SC_WINDOW)], idx_v)
            pltpu.sync_copy(hp_hbm.at[pl.ds(tok, SC_WINDOW)], rows_v)
            pltpu.async_copy(rows_v, xs_hbm.at[idx_v], sem)

        @pl.loop(0, steps, step=2)
        def _(j):
            window(j, 0, j == 0)
            window(j + 1, 1, j == 0)

        for idx_v, rows_v, sem in bufs:
            pltpu.make_async_copy(rows_v, xs_hbm.at[idx_v], sem).wait()

    return scatter(hp, dest_flat)


def _gather_rows(table, idx_flat):
    n = idx_flat.shape[0]
    W = table.shape[1]
    mesh, n_workers, worker = _sc_workers()
    per = n // n_workers
    steps = per // SC_WINDOW
    assert per * n_workers == n and steps * SC_WINDOW == per and steps % 2 == 0

    @functools.partial(
        pl.kernel, mesh=mesh, out_type=jax.ShapeDtypeStruct((n, W), table.dtype),
        scratch_types=[pltpu.VMEM((SC_WINDOW,), jnp.int32), pltpu.VMEM((SC_WINDOW,), jnp.int32),
                       pltpu.VMEM((SC_WINDOW, W), table.dtype), pltpu.VMEM((SC_WINDOW, W), table.dtype),
                       pltpu.SemaphoreType.DMA, pltpu.SemaphoreType.DMA, pltpu.SemaphoreType.DMA],
        name="moe_gather")
    def gather(table_hbm, idx_hbm, out_hbm, idx0, idx1, rows0, rows1, sem0, sem1, gsem):
        base = worker() * per
        bufs = ((idx0, rows0, sem0), (idx1, rows1, sem1))

        def window(j, b, first):
            idx_v, rows_v, sem = bufs[b]
            off = pl.multiple_of(base + j * SC_WINDOW, SC_WINDOW)

            @pl.when(jnp.logical_not(first))
            def _():
                pltpu.make_async_copy(rows_v, out_hbm.at[pl.ds(off, SC_WINDOW)], sem).wait()

            pltpu.sync_copy(idx_hbm.at[pl.ds(off, SC_WINDOW)], idx_v)
            pltpu.async_copy(table_hbm.at[idx_v], rows_v, gsem).wait()
            pltpu.async_copy(rows_v, out_hbm.at[pl.ds(off, SC_WINDOW)], sem)

        @pl.loop(0, steps, step=2)
        def _(j):
            window(j, 0, j == 0)
            window(j + 1, 1, j == 0)

        for _, rows_v, sem in bufs:
            pltpu.make_async_copy(rows_v, out_hbm.at[pl.ds(base, SC_WINDOW)], sem).wait()

    return gather(table, idx_flat)


def _expert_body(te_ref, nv_ref, nu_ref, xs_ref, wgu_ref, bgu_ref, wd_ref, bd_ref, ys_ref, wgu_bf_ref, wd_bf_ref):
    i = pl.program_id(0)
    used = i < nu_ref[0]
    new_expert = jnp.logical_or(i == 0, te_ref[i] != te_ref[jnp.maximum(i - 1, 0)])

    @pl.when(jnp.logical_and(used, new_expert))
    def _():
        wgu_bf_ref[...] = wgu_ref[...].astype(BF16)
        wd_bf_ref[...] = wd_ref[...].astype(BF16)

    @pl.when(used)
    def _():
        live = lax.broadcasted_iota(jnp.int32, (MOE_TILE, 1), 0) < nv_ref[i]
        x_hi, x_lo = _unpack_bf16_pairs(jnp.where(live, xs_ref[...], jnp.uint32(0)))
        half = D_MODEL // 2
        gu = _dot(x_hi, wgu_bf_ref[:half, :]) + _dot(x_lo, wgu_bf_ref[half:, :]) + bgu_ref[...]
        gt = jnp.minimum(gu[:, :D_FF], SWIGLU_LIMIT)
        up = jnp.clip(gu[:, D_FF:], -SWIGLU_LIMIT, SWIGLU_LIMIT)
        act = (up + 1.0) * (gt * jax.nn.sigmoid(SWIGLU_ALPHA * gt))
        y = _dot(act.astype(BF16), wd_bf_ref[...]) + bd_ref[...]
        ys_ref[...] = _pack_bf16_pairs(y.astype(BF16))

    @pl.when(jnp.logical_not(used))
    def _():
        ys_ref[...] = jnp.zeros_like(ys_ref)


def _experts(xs, tile_e, n_valid, n_used, layer, w_gu, b_gu, w_down, b_down):
    n_rows, W = xs.shape
    n_tiles = n_rows // MOE_TILE
    grid_spec = pltpu.PrefetchScalarGridSpec(
        num_scalar_prefetch=3,
        grid=(n_tiles,),
        in_specs=[pl.BlockSpec((MOE_TILE, W), lambda i, te, nv, nu: (i, 0)),
                  pl.BlockSpec((None, None, D_MODEL, 2 * D_FF), lambda i, te, nv, nu: (layer, te[i], 0, 0)),
                  pl.BlockSpec((None, None, 1, 2 * D_FF), lambda i, te, nv, nu: (layer, te[i], 0, 0)),
                  pl.BlockSpec((None, None, D_FF, D_MODEL), lambda i, te, nv, nu: (layer, te[i], 0, 0)),
                  pl.BlockSpec((None, None, 1, D_MODEL), lambda i, te, nv, nu: (layer, te[i], 0, 0))],
        out_specs=pl.BlockSpec((MOE_TILE, W), lambda i, te, nv, nu: (i, 0)),
        scratch_shapes=[pltpu.VMEM((D_MODEL, 2 * D_FF), BF16), pltpu.VMEM((D_FF, D_MODEL), BF16)],
    )
    return pl.pallas_call(
        _expert_body,
        grid_spec=grid_spec,
        out_shape=jax.ShapeDtypeStruct((n_rows, W), jnp.uint32),
        compiler_params=_cparams(("arbitrary",)),
        name="moe_experts",
    )(tile_e, n_valid, n_used, xs, w_gu, b_gu.reshape(DEPTH, N_EXPERTS, 1, 2 * D_FF), w_down,
      b_down.reshape(DEPTH, N_EXPERTS, 1, D_MODEL))


def _combine_body(x_ref, y0_ref, y1_ref, y2_ref, y3_ref, gate_ref, g2_ref, ng_ref, o_ref, *, final_norm):
    g = gate_ref[...]
    acc_hi = acc_lo = None
    for k, y_ref in enumerate((y0_ref, y1_ref, y2_ref, y3_ref)):
        p = y_ref[...]
        gk = g[:, k:k + 1]
        hi = gk * pltpu.bitcast(p & jnp.uint32(0xFFFF0000), F32)
        lo = gk * pltpu.bitcast(p << 16, F32)
        acc_hi = hi if acc_hi is None else acc_hi + hi
        acc_lo = lo if acc_lo is None else acc_lo + lo
    acc = jnp.concatenate([acc_hi, acc_lo], axis=1)
    xn = x_ref[...] + _scale_rows(acc, g2_ref[...])
    if final_norm:
        xn = xn * lax.rsqrt(jnp.mean(xn * xn, axis=-1, keepdims=True) + EPS) * ng_ref[...]
    o_ref[...] = xn


def _combine(x, ysg, gates_t, tok_off, mod, final_g, ctx):
    R, D = x.shape
    tl = TOK_TILE
    t0 = tok_off // tl
    nt = gates_t.shape[0] // tl
    final_norm = final_g is not None
    ng = final_g.reshape(1, D) if final_norm else jnp.ones((1, D), F32)
    y_specs = [pl.BlockSpec((tl, D // 2), lambda i, k=k: (k * nt + t0 + i, 0)) for k in range(TOP_K)]
    return pl.pallas_call(
        functools.partial(_combine_body, final_norm=final_norm),
        grid=(R // tl,),
        in_specs=[pl.BlockSpec((tl, D), lambda i: (i, 0))] + y_specs
        + [pl.BlockSpec((tl, TOP_K), lambda i: (t0 + i, 0)), _mod_spec(5, ctx), _full((1, D))],
        out_specs=pl.BlockSpec((tl, D), lambda i: (i, 0)),
        out_shape=jax.ShapeDtypeStruct((R, D), F32),
        compiler_params=_cparams(("parallel",)),
        name="moe_combine",
    )(x, ysg, ysg, ysg, ysg, gates_t, mod, ng)


def _moe(hp, logits_t, layer, w_gu, b_gu, w_down, b_down):
    dest, gates, tile_e, n_valid, n_used, n_tiles = _routing(logits_t)
    dest_flat = dest.reshape(-1)
    xs = _dispatch(hp, dest_flat, n_tiles * MOE_TILE)
    ys = _experts(xs, tile_e, n_valid, n_used, layer, w_gu, b_gu, w_down, b_down)
    return _gather_rows(ys, dest_flat), gates.T


def _token_mixers(x, pos, mod, h0, p, consts, ctx, need_out):
    R, D = x.shape
    L = R // SUBLANES
    xa, ga, xb, xc, xd = _in_projection(x, pos, mod, p["norm1_g"], p["w_in"], p["b_in"], ctx)
    yf, yb, hfin = _rglru(xa, p["conv_a_w"], p["conv_a_b"], p["wg"], p["bg"], p["rg_lambda"], h0, L)
    if not need_out:
        return None, hfin
    yp = _pool_mixer(xb, p["w_pool"], p["b_pool"], p["pool_scale"], L)
    cl, sl = consts["dft"][L]
    yc = _fourier_mixer(xc.reshape(L, SUBLANES * W_GROUP), cl, sl, consts["cc"], consts["sc"], p["w_four"],
                        p["b_four"], L).reshape(R, W_GROUP)
    yd = _conformer(xd, p["conv_d_w"], p["conv_d_b"], p["ln_d_g"], p["ln_d_b"], consts["avg"], p["w_pw"],
                    p["b_pw"], L)
    return (yf, yb, ga, yp, yc, yd), hfin


def _pos_embed(n_tokens):
    rows_n = n_tokens // GRID_W
    row = jnp.repeat(jnp.arange(rows_n), GRID_W).astype(F32)
    col = jnp.tile(jnp.arange(GRID_W), rows_n).astype(F32)
    q = D_MODEL // 4
    omega = 1.0 / (10000.0 ** (jnp.arange(q, dtype=F32) / q))

    def emb(v):
        ang = v[:, None] * omega[None, :]
        return jnp.concatenate([jnp.sin(ang), jnp.cos(ang)], axis=-1)

    return jnp.concatenate([emb(row), emb(col)], axis=-1)


def _layer_params(l, w_in, b_in, conv_a_w, conv_a_b, w_rg_r, b_rg_r, w_rg_i, b_rg_i, rg_lambda, w_pool, b_pool,
                  pool_scale, w_four, b_four, conv_d_w, conv_d_b, ln_d_g, ln_d_b, w_pw, b_pw, norm1_g):
    wg = jnp.stack([jnp.concatenate([_block_diag(w_rg_r[l, d]), _block_diag(w_rg_i[l, d])], axis=1)
                    for d in range(2)]).astype(BF16)
    bg = jnp.concatenate([b_rg_r[l].reshape(2, 1, W_GROUP), b_rg_i[l].reshape(2, 1, W_GROUP)], axis=-1)
    return dict(
        norm1_g=norm1_g[l], w_in=w_in[l].astype(BF16), b_in=b_in[l],
        conv_a_w=conv_a_w[l], conv_a_b=conv_a_b[l], wg=wg, bg=bg, rg_lambda=rg_lambda[l],
        w_pool=_block_diag(w_pool[l]).astype(BF16), b_pool=b_pool[l], pool_scale=pool_scale[l],
        w_four=_block_diag(w_four[l]).astype(BF16), b_four=b_four[l],
        conv_d_w=conv_d_w[l], conv_d_b=conv_d_b[l], ln_d_g=ln_d_g[l], ln_d_b=ln_d_b[l],
        w_pw=w_pw[l].astype(BF16), b_pw=b_pw[l])


def kernel(x, c, ctx, c_ctx, w_mod, b_mod, norm1_g, norm2_g, w_in, b_in, conv_a_w, conv_a_b, w_rg_r, b_rg_r,
           w_rg_i, b_rg_i, rg_lambda, w_pool, b_pool, pool_scale, w_four, b_four, conv_d_w, conv_d_b, ln_d_g,
           ln_d_b, w_pw, b_pw, w_out, b_out, w_router, b_router, w_gu, b_gu, w_down, b_down, final_norm_g):
    bn, L, D = x.shape
    Lc = ctx.shape[1]
    assert bn == SUBLANES and D == D_MODEL

    pos = _pos_embed(L)
    c_rows = jnp.concatenate([c, jnp.broadcast_to(c_ctx[None], (MOD_ROWS - bn, D))], axis=0)
    mod = _modulation(c_rows, w_mod, b_mod)
    x = jnp.transpose(x, (1, 0, 2)).reshape(L * bn, D)
    ctx = jnp.transpose(ctx, (1, 0, 2)).reshape(Lc * bn, D)

    cc1, sc1 = _dft_matrices(D_SUB)
    eye = jnp.eye(N_SUB, dtype=F32)
    consts = dict(
        dft={n: tuple(m.astype(BF16) for m in _dft_matrices(n)) for n in sorted({L, Lc})},
        cc=jnp.kron(eye, cc1).astype(BF16), sc=jnp.kron(eye, sc1).astype(BF16),
        avg=jnp.kron(eye, jnp.full((D_SUB, D_SUB), 1.0 / D_SUB, F32)).astype(BF16))
    h_zero = jnp.zeros((2, SUBLANES, W_GROUP), F32)

    for l in range(DEPTH):
        last = l == DEPTH - 1
        p = _layer_params(l, w_in, b_in, conv_a_w, conv_a_b, w_rg_r, b_rg_r, w_rg_i, b_rg_i, rg_lambda, w_pool,
                          b_pool, pool_scale, w_four, b_four, conv_d_w, conv_d_b, ln_d_g, ln_d_b, w_pw, b_pw,
                          norm1_g)
        mod3 = mod[l]
        w_out_bf = w_out[l].astype(BF16)
        wr_t = w_router[l].T
        wr_hi = wr_t.astype(BF16)
        wr2 = jnp.concatenate([wr_hi, (wr_t - wr_hi.astype(F32)).astype(BF16)], axis=0)
        x_pos = pos if l == 0 else None

        mix_c, h_ctx = _token_mixers(ctx, None, mod3, h_zero, p, consts, True, not last)
        mix_x, _ = _token_mixers(x, x_pos, mod3, h_ctx, p, consts, False, True)
        x, hp_x, lg_x = _out_projection(x, x_pos, mix_x, mod3, norm2_g[l], w_out_bf, b_out[l], wr2,
                                        b_router[l], False)
        if not last:
            ctx, hp_c, lg_c = _out_projection(ctx, None, mix_c, mod3, norm2_g[l], w_out_bf, b_out[l], wr2,
                                              b_router[l], True)
            hp = jnp.concatenate([hp_c, hp_x], axis=0)
            lg = jnp.concatenate([lg_c, lg_x], axis=1)
            ysg, gates_t = _moe(hp, lg, l, w_gu, b_gu, w_down, b_down)
            ctx = _combine(ctx, ysg, gates_t, 0, mod3, None, True)
            x = _combine(x, ysg, gates_t, bn * Lc, mod3, None, False)
        else:
            ysg, gates_t = _moe(hp_x, lg_x, l, w_gu, b_gu, w_down, b_down)
            x = _combine(x, ysg, gates_t, 0, mod3, final_norm_g, False)
    return jnp.transpose(x.reshape(L, bn, D), (1, 0, 2))
```

```python
import functools
import math

import jax
import jax.numpy as jnp
from jax import lax
from jax.experimental import pallas as pl
from jax.experimental.pallas import tpu as pltpu

F32 = jnp.float32
BF16 = jnp.bfloat16

D_MODEL = 1024
DEPTH = 2
GRID_W = 64
W_GROUP = 256
N_SUB = 4
D_SUB = 64
D_IN = 6 * W_GROUP
RG_CONV = 4
RG_C = 8.0
CONF_KERNEL = 31
N_EXPERTS = 32
TOP_K = 4
D_FF = D_MODEL
SWIGLU_LIMIT = 7.0
SWIGLU_ALPHA = 1.702
EPS = 1e-6

SUBLANES = 8
VMEM_LIMIT_BYTES = 56 * 1024 * 1024
MOD_ROWS = 16
RG_HALO = 8 * SUBLANES
POOL_HALO = 8 * SUBLANES
CONF_HALO = 16 * SUBLANES
MOE_TILE = 512
TOK_TILE = 256
SC_WINDOW = 64
ROUTE_TILE = 512


def _cparams(sem):
    return pltpu.CompilerParams(dimension_semantics=sem, vmem_limit_bytes=VMEM_LIMIT_BYTES)


def _full(shape):
    nd = len(shape)
    return pl.BlockSpec(shape, lambda *_: (0,) * nd)


def _dot(a, b):
    return jnp.dot(a, b, preferred_element_type=F32)


def _split_bf16(v):
    hi = v.astype(BF16)
    lo = (v - hi.astype(F32)).astype(BF16)
    return hi, lo


def _mod_body(c_ref, w_ref, b_ref, o_ref):
    c = c_ref[...]
    s = c * jax.nn.sigmoid(c)
    o_ref[...] = jnp.dot(s, w_ref[...], precision=lax.Precision.HIGHEST,
                         preferred_element_type=F32) + b_ref[...]


def _modulation(c_rows, w_mod, b_mod):
    tn = 1536
    n6 = 6 * D_MODEL
    return pl.pallas_call(
        _mod_body,
        grid=(DEPTH, n6 // tn),
        in_specs=[_full((MOD_ROWS, D_MODEL)),
                  pl.BlockSpec((None, D_MODEL, tn), lambda l, j: (l, 0, j)),
                  pl.BlockSpec((None, 1, tn), lambda l, j: (l, 0, j))],
        out_specs=pl.BlockSpec((None, MOD_ROWS, tn), lambda l, j: (l, 0, j)),
        out_shape=jax.ShapeDtypeStruct((DEPTH, MOD_ROWS, n6), F32),
        compiler_params=_cparams(("parallel", "parallel")),
        name="modulation",
    )(c_rows, w_mod, b_mod.reshape(DEPTH, 1, n6))


def _mod_spec(chunk, ctx):
    return pl.BlockSpec((SUBLANES, D_MODEL), lambda i: (1 if ctx else 0, chunk))


def _scale_rows(v, m):
    r, d = v.shape
    return (v.reshape(r // SUBLANES, SUBLANES, d) * m[None]).reshape(r, d)


def _rms_mod(x, g, shift, scale):
    r, d = x.shape
    y = x * lax.rsqrt(jnp.mean(x * x, axis=-1, keepdims=True) + EPS) * g
    y3 = y.reshape(r // SUBLANES, SUBLANES, d)
    return (y3 * (1.0 + scale)[None] + shift[None]).reshape(r, d)


def _load_plus_pos(x_ref, pos_ref):
    return jnp.concatenate([x_ref[:, t, :] + pos_ref[t:t + 1, :] for t in range(pos_ref.shape[0])], axis=0)


def _x_spec(x, tr):
    if x.ndim == 3:
        return pl.BlockSpec((SUBLANES, tr // SUBLANES, x.shape[2]), lambda i: (0, i, 0))
    return pl.BlockSpec((tr, x.shape[1]), lambda i: (i, 0))


def _inproj_body(*refs, add_pos):
    if add_pos:
        x_ref, pos_ref, sh_ref, sc_ref, g_ref, w_ref, b_ref, xa_ref, ga_ref, xb_ref, xc_ref, xd_ref = refs
        x = _load_plus_pos(x_ref, pos_ref)
    else:
        x_ref, sh_ref, sc_ref, g_ref, w_ref, b_ref, xa_ref, ga_ref, xb_ref, xc_ref, xd_ref = refs
        x = x_ref[...]
    u = _rms_mod(x, g_ref[...], sh_ref[...], sc_ref[...])
    p = _dot(u.astype(BF16), w_ref[...]) + b_ref[...]
    xa_ref[...] = p[:, 0:256]
    ga_ref[...] = p[:, 256:512]
    xb_ref[...] = p[:, 512:768]
    xc_ref[...] = p[:, 768:1024].astype(BF16)
    xd_ref[...] = p[:, 1024:1536]


def _in_projection(x, pos, mod, norm_g, w_in_bf, b_in, ctx):
    add_pos = pos is not None
    assert add_pos == (x.ndim == 3)
    R, D = x.size // x.shape[-1], x.shape[-1]
    tr = min(R, 512)
    row = lambda w: pl.BlockSpec((tr, w), lambda i: (i, 0))
    in_specs = [_x_spec(x, tr)]
    args = [x]
    if add_pos:
        in_specs.append(pl.BlockSpec((tr // SUBLANES, D), lambda i: (i, 0)))
        args.append(pos)
    in_specs += [_mod_spec(0, ctx), _mod_spec(1, ctx), _full((1, D)), _full((D, D_IN)), _full((1, D_IN))]
    args += [mod, mod, norm_g.reshape(1, D), w_in_bf, b_in.reshape(1, D_IN)]
    out_shape = [jax.ShapeDtypeStruct((R, 256), F32)] * 3 + [
        jax.ShapeDtypeStruct((R, 256), BF16), jax.ShapeDtypeStruct((R, 512), F32)]
    return pl.pallas_call(
        functools.partial(_inproj_body, add_pos=add_pos),
        grid=(R // tr,),
        in_specs=in_specs,
        out_specs=[row(256), row(256), row(256), row(256), row(512)],
        out_shape=out_shape,
        compiler_params=_cparams(("parallel",)),
        name="in_projection",
    )(*args)


def _rg_gates(xc, wg, bg, lam):
    g = _dot(xc.astype(BF16), wg) + bg
    r = jax.nn.sigmoid(g[:, :W_GROUP])
    gi = jax.nn.sigmoid(g[:, W_GROUP:])
    z = -lam
    softplus = jnp.maximum(z, 0.0) + jnp.log1p(jnp.exp(-jnp.abs(z)))
    log_a = (-RG_C) * r * softplus
    a = jnp.exp(log_a)
    b = jnp.sqrt(-jnp.tanh(log_a) * (a * a + 1.0)) * (gi * xc)
    return a, b


def _rg_body(xf_ref, xfh_ref, xr_ref, xrh_ref, cw_ref, cb_ref, wg_ref, bg_ref, lam_ref, h0_ref,
             yf_ref, yb_ref, hfin_ref, af_ref, ab_ref, hc_ref, *, n, tt):
    i = pl.program_id(0)
    tr = tt * SUBLANES
    keep = RG_HALO - (RG_CONV - 1) * SUBLANES

    @pl.when(i == 0)
    def _():
        hc_ref[...] = h0_ref[...]

    halo = jnp.where(i > 0, xfh_ref[...], 0.0)
    ext = jnp.concatenate([halo[keep:], xf_ref[...]], axis=0)
    xc = cb_ref[0]
    for k in range(RG_CONV):
        xc = xc + cw_ref[0, k:k + 1, :] * ext[k * SUBLANES:k * SUBLANES + tr]
    a, b = _rg_gates(xc, wg_ref[0], bg_ref[0], lam_ref[0])
    af_ref[...] = a
    yf_ref[...] = b

    halo = jnp.where(i > 0, xrh_ref[...], 0.0)
    ext = jnp.concatenate([xr_ref[...], halo[:(RG_CONV - 1) * SUBLANES]], axis=0)
    xc = cb_ref[1]
    for k in range(RG_CONV):
        o = (RG_CONV - 1 - k) * SUBLANES
        xc = xc + cw_ref[1, k:k + 1, :] * ext[o:o + tr]
    a, b = _rg_gates(xc, wg_ref[1], bg_ref[1], lam_ref[1])
    ab_ref[...] = a
    yb_ref[...] = b

    def step(t, carry):
        hf, hb = carry
        rf = pl.multiple_of(t * SUBLANES, SUBLANES)
        hf = af_ref[pl.ds(rf, SUBLANES), :] * hf + yf_ref[pl.ds(rf, SUBLANES), :]
        yf_ref[pl.ds(rf, SUBLANES), :] = hf
        rb = pl.multiple_of((tt - 1 - t) * SUBLANES, SUBLANES)
        hb = ab_ref[pl.ds(rb, SUBLANES), :] * hb + yb_ref[pl.ds(rb, SUBLANES), :]
        yb_ref[pl.ds(rb, SUBLANES), :] = hb
        return hf, hb

    hf, hb = lax.fori_loop(0, tt, step, (hc_ref[0], hc_ref[1]), unroll=8)
    hc_ref[0] = hf
    hc_ref[1] = hb
    hfin_ref[0] = hf
    hfin_ref[1] = hb


def _rglru(xa2, conv_w, conv_b, wg_bf, bg, lam, h0, L):
    tt = min(L, 256)
    n = L // tt
    tr = tt * SUBLANES
    per = tr // RG_HALO
    last_halo = L * SUBLANES // RG_HALO - 1
    row = lambda i: (i, 0)
    rev = lambda i: (n - 1 - i, 0)
    in_specs = [
        pl.BlockSpec((tr, W_GROUP), row),
        pl.BlockSpec((RG_HALO, W_GROUP), lambda i: (jnp.maximum(i * per - 1, 0), 0)),
        pl.BlockSpec((tr, W_GROUP), rev),
        pl.BlockSpec((RG_HALO, W_GROUP), lambda i: (jnp.minimum((n - i) * per, last_halo), 0)),
        _full((2, RG_CONV, W_GROUP)), _full((2, 1, W_GROUP)), _full((2, W_GROUP, 2 * W_GROUP)),
        _full((2, 1, 2 * W_GROUP)), _full((2, 1, W_GROUP)), _full((2, SUBLANES, W_GROUP)),
    ]
    return pl.pallas_call(
        functools.partial(_rg_body, n=n, tt=tt),
        grid=(n,),
        in_specs=in_specs,
        out_specs=[pl.BlockSpec((tr, W_GROUP), row), pl.BlockSpec((tr, W_GROUP), rev),
                   _full((2, SUBLANES, W_GROUP))],
        out_shape=[jax.ShapeDtypeStruct((L * SUBLANES, W_GROUP), F32)] * 2
        + [jax.ShapeDtypeStruct((2, SUBLANES, W_GROUP), F32)],
        scratch_shapes=[pltpu.VMEM((tr, W_GROUP), F32), pltpu.VMEM((tr, W_GROUP), F32),
                        pltpu.VMEM((2, SUBLANES, W_GROUP), F32)],
        compiler_params=_cparams(("arbitrary",)),
        name="rglru",
    )(xa2, xa2, xa2, xa2, conv_w, conv_b.reshape(2, 1, W_GROUP), wg_bf, bg, lam.reshape(2, 1, W_GROUP), h0)


def _pool_body(xm_ref, xp_ref, xn_ref, w_ref, b_ref, s_ref, o_ref, *, n, tt, L):
    i = pl.program_id(0)
    tr = tt * SUBLANES
    S = SUBLANES
    xm = xm_ref[...]
    prev = jnp.where(i > 0, xp_ref[...], 0.0)
    nxt = jnp.where(i < n - 1, xn_ref[...], 0.0)
    xe = jnp.concatenate([prev, xm, nxt], axis=0)
    e = xe.shape[0]
    p2 = xe[S:e] + xe[0:e - S]
    n4 = (tt + 13) * S
    p4 = p2[0:n4] + p2[2 * S:2 * S + n4]
    n8 = (tt + 9) * S
    p8 = p4[0:n8] + p4[4 * S:4 * S + n8]
    s16 = p8[0:tr] + p8[8 * S:8 * S + tr]
    s2 = p2[7 * S:7 * S + tr]
    s4 = p4[6 * S:6 * S + tr]
    s8 = p8[4 * S:4 * S + tr]
    grp = lax.broadcasted_iota(jnp.int32, (1, W_GROUP), 1) // D_SUB
    half = jnp.left_shift(1, grp)
    t = i * tt + lax.broadcasted_iota(jnp.int32, (tr, 1), 0) // S
    cnt = (jnp.minimum(t + half, L) - jnp.maximum(t - half, 0)).astype(F32)
    s = jnp.where(grp == 0, s2, jnp.where(grp == 1, s4, jnp.where(grp == 2, s8, s16)))
    pooled = s / cnt - xm
    y = _dot(pooled.astype(BF16), w_ref[...]) + b_ref[...]
    o_ref[...] = (y * s_ref[...]).astype(BF16)


def _pool_mixer(xb2, w_bd_bf, b, scale, L):
    tt = min(L, 256)
    n = L // tt
    tr = tt * SUBLANES
    per = tr // POOL_HALO
    last_halo = L * SUBLANES // POOL_HALO - 1
    return pl.pallas_call(
        functools.partial(_pool_body, n=n, tt=tt, L=L),
        grid=(n,),
        in_specs=[pl.BlockSpec((tr, W_GROUP), lambda i: (i, 0)),
                  pl.BlockSpec((POOL_HALO, W_GROUP), lambda i: (jnp.maximum(i * per - 1, 0), 0)),
                  pl.BlockSpec((POOL_HALO, W_GROUP), lambda i: (jnp.minimum((i + 1) * per, last_halo), 0)),
                  _full((W_GROUP, W_GROUP)), _full((1, W_GROUP)), _full((1, W_GROUP))],
        out_specs=pl.BlockSpec((tr, W_GROUP), lambda i: (i, 0)),
        out_shape=jax.ShapeDtypeStruct((L * SUBLANES, W_GROUP), BF16),
        compiler_params=_cparams(("parallel",)),
        name="pool_mixer",
    )(xb2, xb2, xb2, w_bd_bf, b.reshape(1, W_GROUP), scale.reshape(1, W_GROUP))


def _fourier_body(c_ref, s_ref, x_ref, cc_ref, sc_ref, w_ref, b_ref, o_ref):
    x = x_ref[...]
    z1 = _dot(c_ref[...], x)
    z2 = _dot(s_ref[...], x)
    for j in range(x.shape[1] // W_GROUP):
        sl = slice(j * W_GROUP, (j + 1) * W_GROUP)
        a_hi, a_lo = _split_bf16(z1[:, sl])
        b_hi, b_lo = _split_bf16(z2[:, sl])
        f = (_dot(a_hi, cc_ref[...]) + _dot(a_lo, cc_ref[...])) - (_dot(b_hi, sc_ref[...]) + _dot(b_lo, sc_ref[...]))
        o_ref[:, sl] = (_dot(f.astype(BF16), w_ref[...]) + b_ref[...]).astype(BF16)


def _fourier_mixer(xc_tm, cl, sl, cc, sc, w_bd_bf, b, L):
    ncol = xc_tm.shape[1]
    nb = min(ncol, 1024)
    tk = min(L, 256)
    return pl.pallas_call(
        _fourier_body,
        grid=(ncol // nb, L // tk),
        in_specs=[pl.BlockSpec((tk, L), lambda j, k: (k, 0)),
                  pl.BlockSpec((tk, L), lambda j, k: (k, 0)),
                  pl.BlockSpec((L, nb), lambda j, k: (0, j)),
                  _full((W_GROUP, W_GROUP)), _full((W_GROUP, W_GROUP)), _full((W_GROUP, W_GROUP)),
                  _full((1, W_GROUP))],
        out_specs=pl.BlockSpec((tk, nb), lambda j, k: (k, j)),
        out_shape=jax.ShapeDtypeStruct((L, ncol), BF16),
        compiler_params=_cparams(("parallel", "parallel")),
        name="fourier_mixer",
    )(cl, sl, xc_tm, cc, sc, w_bd_bf, b.reshape(1, W_GROUP))


def _dft_matrices(L):
    f = 1 << (max(L.bit_length() - 1, 0) // 2)
    n = jnp.arange(L, dtype=jnp.int32)[None, :]

    def table(rows):
        ang = ((rows[:, None] * n) % L).astype(F32) * (2.0 * math.pi / L)
        return jnp.cos(ang), jnp.sin(ang)

    ac, as_ = table(jnp.arange(L // f, dtype=jnp.int32) * f)
    bc, bs = table(jnp.arange(f, dtype=jnp.int32))
    scale = 1.0 / math.sqrt(L)
    cos = (ac[:, None, :] * bc[None, :, :] - as_[:, None, :] * bs[None, :, :]).reshape(L, L) * scale
    sin = (as_[:, None, :] * bc[None, :, :] + ac[:, None, :] * bs[None, :, :]).reshape(L, L) * scale
    return cos, sin


def _block_diag(w):
    g, a, b = w.shape
    eye = jnp.eye(g, dtype=w.dtype)
    return (eye[:, None, :, None] * w[:, :, None, :]).reshape(g * a, g * b)


CONF_CHUNK = 64


def _conformer_body(xm_ref, xp_ref, xn_ref, cw_ref, cb_ref, lg_ref, lb_ref, avg_ref, w_ref, b_ref,
                    o_ref, v_ref, c_ref, *, n, tt):
    i = pl.program_id(0)
    tr = tt * SUBLANES
    H = CONF_HALO

    def glu(v):
        return v[:, :W_GROUP] * jax.nn.sigmoid(v[:, W_GROUP:])

    v_ref[0:H] = jnp.where(i > 0, glu(xp_ref[...]), 0.0)
    v_ref[H:H + tr] = glu(xm_ref[...])
    v_ref[H + tr:H + tr + H] = jnp.where(i < n - 1, glu(xn_ref[...]), 0.0)

    def chunk(c, carry):
        r0 = pl.multiple_of(c * CONF_CHUNK, CONF_CHUNK)
        acc = jnp.broadcast_to(cb_ref[...], (CONF_CHUNK, W_GROUP))
        for k in range(CONF_KERNEL):
            acc = acc + cw_ref[k:k + 1, :] * v_ref[pl.ds(r0 + (k + 1) * SUBLANES, CONF_CHUNK), :]
        c_ref[pl.ds(r0, CONF_CHUNK), :] = acc
        return carry

    lax.fori_loop(0, tr // CONF_CHUNK, chunk, 0)

    v = c_ref[...]
    avg = avg_ref[...]
    v_hi, v_lo = _split_bf16(v)
    mu = _dot(v_hi, avg) + _dot(v_lo, avg)
    d = v - mu
    q_hi, q_lo = _split_bf16(d * d)
    var = _dot(q_hi, avg) + _dot(q_lo, avg)
    vn = d * lax.rsqrt(var + EPS) * lg_ref[...] + lb_ref[...]
    act = vn * jax.nn.sigmoid(vn)
    o_ref[...] = (_dot(act.astype(BF16), w_ref[...]) + b_ref[...]).astype(BF16)


def _conformer(xd2, conv_w, conv_b, ln_g, ln_b, avg_bf, w_pw_bf, b_pw, L):
    tt = min(L, 256)
    n = L // tt
    tr = tt * SUBLANES
    per = tr // CONF_HALO
    last_halo = L * SUBLANES // CONF_HALO - 1
    vec = lambda a: a.reshape(1, W_GROUP)
    return pl.pallas_call(
        functools.partial(_conformer_body, n=n, tt=tt),
        grid=(n,),
        in_specs=[pl.BlockSpec((tr, 2 * W_GROUP), lambda i: (i, 0)),
                  pl.BlockSpec((CONF_HALO, 2 * W_GROUP), lambda i: (jnp.maximum(i * per - 1, 0), 0)),
                  pl.BlockSpec((CONF_HALO, 2 * W_GROUP), lambda i: (jnp.minimum((i + 1) * per, last_halo), 0)),
                  _full((CONF_KERNEL, W_GROUP)), _full((1, W_GROUP)), _full((1, W_GROUP)), _full((1, W_GROUP)),
                  _full((W_GROUP, W_GROUP)), _full((W_GROUP, W_GROUP)), _full((1, W_GROUP))],
        out_specs=pl.BlockSpec((tr, W_GROUP), lambda i: (i, 0)),
        out_shape=jax.ShapeDtypeStruct((L * SUBLANES, W_GROUP), BF16),
        scratch_shapes=[pltpu.VMEM((tr + 2 * CONF_HALO, W_GROUP), F32), pltpu.VMEM((tr, W_GROUP), F32)],
        compiler_params=_cparams(("parallel",)),
        name="conformer",
    )(xd2, xd2, xd2, conv_w, vec(conv_b), vec(ln_g), vec(ln_b), avg_bf, w_pw_bf, vec(b_pw))


def _gelu_tanh(x):
    return 0.5 * x * (1.0 + jnp.tanh(math.sqrt(2.0 / math.pi) * (x + 0.044715 * (x * x * x))))


def _pack_bf16_pairs(h_bf):
    u = pltpu.bitcast(h_bf.astype(F32), jnp.uint32)
    half = h_bf.shape[1] // 2
    return (u[:, :half] & jnp.uint32(0xFFFF0000)) | (u[:, half:] >> 16)


def _unpack_bf16_pairs(p):
    hi = pltpu.bitcast(p & jnp.uint32(0xFFFF0000), F32).astype(BF16)
    lo = pltpu.bitcast(p << 16, F32).astype(BF16)
    return hi, lo


def _outproj_body(*refs, add_pos):
    if add_pos:
        (x_ref, pos_ref, yf_ref, yb_ref, ga_ref, yp_ref, yc_ref, yd_ref, g1_ref, sh_ref, sc_ref, ng_ref,
         wo_ref, bo_ref, wr_ref, br_ref, xo_ref, hp_ref, lg_ref) = refs
        x = _load_plus_pos(x_ref, pos_ref)
    else:
        (x_ref, yf_ref, yb_ref, ga_ref, yp_ref, yc_ref, yd_ref, g1_ref, sh_ref, sc_ref, ng_ref,
         wo_ref, bo_ref, wr_ref, br_ref, xo_ref, hp_ref, lg_ref) = refs
        x = x_ref[...]
    ya = (yf_ref[...] + yb_ref[...]) * _gelu_tanh(ga_ref[...])
    ycat = jnp.concatenate([ya.astype(BF16), yp_ref[...], yc_ref[...], yd_ref[...]], axis=1)
    y = _dot(ycat, wo_ref[...]) + bo_ref[...]
    xn = x + _scale_rows(y, g1_ref[...])
    xo_ref[...] = xn
    h = _rms_mod(xn, ng_ref[...], sh_ref[...], sc_ref[...])
    h_hi, h_lo = _split_bf16(h)
    nt = (((1,), (1,)), ((), ()))
    wr = wr_ref[...]
    acc = lax.dot_general(wr, h_hi, nt, preferred_element_type=F32)
    acc = acc + lax.dot_general(wr, h_lo, nt, preferred_element_type=F32)
    lg_ref[...] = acc[:N_EXPERTS] + acc[N_EXPERTS:] + br_ref[...]
    hp_ref[...] = _pack_bf16_pairs(h_hi)


def _out_projection(x, pos, mix, mod, norm_g, w_out_bf, b_out, wr2_bf, b_router, ctx):
    add_pos = pos is not None
    assert add_pos == (x.ndim == 3)
    R, D = x.size // x.shape[-1], x.shape[-1]
    tr = min(R, 512)
    row = lambda w: pl.BlockSpec((tr, w), lambda i: (i, 0))
    in_specs = [_x_spec(x, tr)]
    args = [x]
    if add_pos:
        in_specs.append(pl.BlockSpec((tr // SUBLANES, D), lambda i: (i, 0)))
        args.append(pos)
    in_specs += [row(W_GROUP)] * 6
    args += list(mix)
    in_specs += [_mod_spec(2, ctx), _mod_spec(3, ctx), _mod_spec(4, ctx), _full((1, D)), _full((D, D)),
                 _full((1, D)), _full((2 * N_EXPERTS, D)), _full((N_EXPERTS, 1))]
    args += [mod, mod, mod, norm_g.reshape(1, D), w_out_bf, b_out.reshape(1, D), wr2_bf,
             b_router.reshape(N_EXPERTS, 1)]
    return pl.pallas_call(
        functools.partial(_outproj_body, add_pos=add_pos),
        grid=(R // tr,),
        in_specs=in_specs,
        out_specs=[row(D), row(D // 2), pl.BlockSpec((N_EXPERTS, tr), lambda i: (0, i))],
        out_shape=[jax.ShapeDtypeStruct((R, D), F32),
                   jax.ShapeDtypeStruct((R, D // 2), jnp.uint32),
                   jax.ShapeDtypeStruct((N_EXPERTS, R), F32)],
        compiler_params=_cparams(("parallel",)),
        name="out_projection",
    )(*args)


def _top4(v):
    eid = lax.broadcasted_iota(jnp.int32, v.shape, 0)
    out = []
    work = v
    for _ in range(TOP_K):
        m = jnp.max(work, axis=0, keepdims=True)
        idx = jnp.min(jnp.where(work == m, eid, N_EXPERTS), axis=0, keepdims=True)
        oh = eid == idx
        out.append((m, oh))
        work = jnp.where(oh, -jnp.inf, work)
    return out


def _count_body(lg_ref, cnt_ref):
    @pl.when(pl.program_id(0) == 0)
    def _():
        cnt_ref[...] = jnp.zeros_like(cnt_ref)

    sel = jnp.zeros(lg_ref.shape, F32)
    for _, oh in _top4(lg_ref[...]):
        sel = sel + oh.astype(F32)
    cnt_ref[...] += jnp.sum(sel, axis=1, keepdims=True)


def _route_body(lg_ref, ps_ref, tri_ref, dest_ref, gate_ref, carry_ref):
    @pl.when(pl.program_id(0) == 0)
    def _():
        carry_ref[...] = jnp.zeros_like(carry_ref)

    top = _top4(lg_ref[...])
    sel = jnp.zeros(lg_ref.shape, F32)
    for _, oh in top:
        sel = sel + oh.astype(F32)
    before = _dot(sel.astype(BF16), tri_ref[...]) + carry_ref[...] + ps_ref[...]
    m0 = top[0][0]
    es = [jnp.exp(m - m0) for m, _ in top]
    den = es[0] + es[1] + es[2] + es[3]
    for k, (_, oh) in enumerate(top):
        dest_ref[k:k + 1, :] = jnp.sum(jnp.where(oh, before, 0.0), axis=0, keepdims=True).astype(jnp.int32)
        gate_ref[k:k + 1, :] = es[k] / den
    carry_ref[...] += jnp.sum(sel, axis=1, keepdims=True)


def _routing(logits_t):
    E, T = logits_t.shape
    tt = ROUTE_TILE
    nt = T // tt
    counts = pl.pallas_call(
        _count_body,
        grid=(nt,),
        in_specs=[pl.BlockSpec((E, tt), lambda i: (0, i))],
        out_specs=_full((E, 1)),
        out_shape=jax.ShapeDtypeStruct((E, 1), F32),
        compiler_params=_cparams(("arbitrary",)),
        name="route_count",
    )(logits_t)
    cnt = counts[:, 0].astype(jnp.int32)
    padded = ((cnt + MOE_TILE - 1) // MOE_TILE) * MOE_TILE
    pend = jnp.cumsum(padded)
    pstart = pend - padded
    n_tiles = -(-(T * TOP_K) // MOE_TILE) + N_EXPERTS
    tile_start = jnp.arange(n_tiles, dtype=jnp.int32) * MOE_TILE
    tile_e = jnp.minimum(jnp.sum((pend[None, :] <= tile_start[:, None]).astype(jnp.int32), axis=1), N_EXPERTS - 1)
    n_used = (pend[-1] // MOE_TILE).astype(jnp.int32).reshape(1)
    n_valid = jnp.clip((pstart + cnt)[tile_e] - tile_start, 0, MOE_TILE).astype(jnp.int32)
    tri = (jnp.arange(tt)[:, None] < jnp.arange(tt)[None, :]).astype(BF16)
    dest, gates = pl.pallas_call(
        _route_body,
        grid=(nt,),
        in_specs=[pl.BlockSpec((E, tt), lambda i: (0, i)), _full((E, 1)), _full((tt, tt))],
        out_specs=[pl.BlockSpec((TOP_K, tt), lambda i: (0, i)), pl.BlockSpec((TOP_K, tt), lambda i: (0, i))],
        out_shape=[jax.ShapeDtypeStruct((TOP_K, T), jnp.int32), jax.ShapeDtypeStruct((TOP_K, T), F32)],
        scratch_shapes=[pltpu.VMEM((E, 1), F32)],
        compiler_params=_cparams(("arbitrary",)),
        name="route_assign",
    )(logits_t, pstart.astype(F32).reshape(E, 1), tri)
    return dest, gates, tile_e, n_valid, n_used, n_tiles


def _sc_workers():
    from jax.experimental.pallas import tpu_sc as plsc
    mesh = plsc.VectorSubcoreMesh(core_axis_name="c", subcore_axis_name="s")
    n_workers = mesh.num_cores * mesh.num_subcores
    worker = lambda: lax.axis_index("s") * mesh.num_cores + lax.axis_index("c")
    return mesh, n_workers, worker


def _dispatch(hp, dest_flat, n_rows):
    T, W = hp.shape
    mesh, n_workers, worker = _sc_workers()
    per = dest_flat.shape[0] // n_workers
    steps = per // SC_WINDOW
    assert per * n_workers == dest_flat.shape[0] and steps * SC_WINDOW == per and steps % 2 == 0
    assert T % SC_WINDOW == 0

    @functools.partial(
        pl.kernel, mesh=mesh, out_type=jax.ShapeDtypeStruct((n_rows, W), hp.dtype),
        scratch_types=[pltpu.VMEM((SC_WINDOW,), jnp.int32), pltpu.VMEM((SC_WINDOW,), jnp.int32),
                       pltpu.VMEM((SC_WINDOW, W), hp.dtype), pltpu.VMEM((SC_WINDOW, W), hp.dtype),
                       pltpu.SemaphoreType.DMA, pltpu.SemaphoreType.DMA],
        name="moe_dispatch")
    def scatter(hp_hbm, dest_hbm, xs_hbm, idx0, idx1, rows0, rows1, sem0, sem1):
        base = worker() * per
        bufs = ((idx0, rows0, sem0), (idx1, rows1, sem1))

        def window(j, b, first):
            idx_v, rows_v, sem = bufs[b]

            @pl.when(jnp.logical_not(first))
            def _():
                pltpu.make_async_copy(rows_v, xs_hbm.at[idx_v], sem).wait()

            off = pl.multiple_of(base + j * SC_WINDOW, SC_WINDOW)
            tok = pl.multiple_of(lax.rem(off, T), SC_WINDOW)
            pltpu.sync_copy(dest_hbm.at[pl.ds(off, SC_WINDOW)], idx_v)
            pltpu.sync_copy(hp_hbm.at[pl.ds(tok, SC_WINDOW)], rows_v)
            pltpu.async_copy(rows_v, xs_hbm.at[idx_v], sem)

        @pl.loop(0, steps, step=2)
        def _(j):
            window(j, 0, j == 0)
            window(j + 1, 1, j == 0)

        for idx_v, rows_v, sem in bufs:
            pltpu.make_async_copy(rows_v, xs_hbm.at[idx_v], sem).wait()

    return scatter(hp, dest_flat)


def _gather_rows(table, idx_flat):
    n = idx_flat.shape[0]
    W = table.shape[1]
    mesh, n_workers, worker = _sc_workers()
    per = n // n_workers
    steps = per // SC_WINDOW
    assert per * n_workers == n and steps * SC_WINDOW == per and steps % 2 == 0

    @functools.partial(
        pl.kernel, mesh=mesh, out_type=jax.ShapeDtypeStruct((n, W), table.dtype),
        scratch_types=[pltpu.VMEM((SC_WINDOW,), jnp.int32), pltpu.VMEM((SC_WINDOW,), jnp.int32),
                       pltpu.VMEM((SC_WINDOW, W), table.dtype), pltpu.VMEM((SC_WINDOW, W), table.dtype),
                       pltpu.SemaphoreType.DMA, pltpu.SemaphoreType.DMA, pltpu.SemaphoreType.DMA],
        name="moe_gather")
    def gather(table_hbm, idx_hbm, out_hbm, idx0, idx1, rows0, rows1, sem0, sem1, gsem):
        base = worker() * per
        bufs = ((idx0, rows0, sem0), (idx1, rows1, sem1))

        def window(j, b, first):
            idx_v, rows_v, sem = bufs[b]
            off = pl.multiple_of(base + j * SC_WINDOW, SC_WINDOW)

            @pl.when(jnp.logical_not(first))
            def _():
                pltpu.make_async_copy(rows_v, out_hbm.at[pl.ds(off, SC_WINDOW)], sem).wait()

            pltpu.sync_copy(idx_hbm.at[pl.ds(off, SC_WINDOW)], idx_v)
            pltpu.async_copy(table_hbm.at[idx_v], rows_v, gsem).wait()
            pltpu.async_copy(rows_v, out_hbm.at[pl.ds(off, SC_WINDOW)], sem)

        @pl.loop(0, steps, step=2)
        def _(j):
            window(j, 0, j == 0)
            window(j + 1, 1, j == 0)

        for _, rows_v, sem in bufs:
            pltpu.make_async_copy(rows_v, out_hbm.at[pl.ds(base, SC_WINDOW)], sem).wait()

    return gather(table, idx_flat)


def _expert_body(te_ref, nv_ref, nu_ref, xs_ref, wgu_ref, bgu_ref, wd_ref, bd_ref, ys_ref, wgu_bf_ref, wd_bf_ref):
    i = pl.program_id(0)
    used = i < nu_ref[0]
    new_expert = jnp.logical_or(i == 0, te_ref[i] != te_ref[jnp.maximum(i - 1, 0)])

    @pl.when(jnp.logical_and(used, new_expert))
    def _():
        wgu_bf_ref[...] = wgu_ref[...].astype(BF16)
        wd_bf_ref[...] = wd_ref[...].astype(BF16)

    @pl.when(used)
    def _():
        live = lax.broadcasted_iota(jnp.int32, (MOE_TILE, 1), 0) < nv_ref[i]
        x_hi, x_lo = _unpack_bf16_pairs(jnp.where(live, xs_ref[...], jnp.uint32(0)))
        half = D_MODEL // 2
        gu = _dot(x_hi, wgu_bf_ref[:half, :]) + _dot(x_lo, wgu_bf_ref[half:, :]) + bgu_ref[...]
        gt = jnp.minimum(gu[:, :D_FF], SWIGLU_LIMIT)
        up = jnp.clip(gu[:, D_FF:], -SWIGLU_LIMIT, SWIGLU_LIMIT)
        act = (up + 1.0) * (gt * jax.nn.sigmoid(SWIGLU_ALPHA * gt))
        y = _dot(act.astype(BF16), wd_bf_ref[...]) + bd_ref[...]
        ys_ref[...] = _pack_bf16_pairs(y.astype(BF16))

    @pl.when(jnp.logical_not(used))
    def _():
        ys_ref[...] = jnp.zeros_like(ys_ref)


def _experts(xs, tile_e, n_valid, n_used, layer, w_gu, b_gu, w_down, b_down):
    n_rows, W = xs.shape
    n_tiles = n_rows // MOE_TILE
    grid_spec = pltpu.PrefetchScalarGridSpec(
        num_scalar_prefetch=3,
        grid=(n_tiles,),
        in_specs=[pl.BlockSpec((MOE_TILE, W), lambda i, te, nv, nu: (i, 0)),
                  pl.BlockSpec((None, None, D_MODEL, 2 * D_FF), lambda i, te, nv, nu: (layer, te[i], 0, 0)),
                  pl.BlockSpec((None, None, 1, 2 * D_FF), lambda i, te, nv, nu: (layer, te[i], 0, 0)),
                  pl.BlockSpec((None, None, D_FF, D_MODEL), lambda i, te, nv, nu: (layer, te[i], 0, 0)),
                  pl.BlockSpec((None, None, 1, D_MODEL), lambda i, te, nv, nu: (layer, te[i], 0, 0))],
        out_specs=pl.BlockSpec((MOE_TILE, W), lambda i, te, nv, nu: (i, 0)),
        scratch_shapes=[pltpu.VMEM((D_MODEL, 2 * D_FF), BF16), pltpu.VMEM((D_FF, D_MODEL), BF16)],
    )
    return pl.pallas_call(
        _expert_body,
        grid_spec=grid_spec,
        out_shape=jax.ShapeDtypeStruct((n_rows, W), jnp.uint32),
        compiler_params=_cparams(("arbitrary",)),
        name="moe_experts",
    )(tile_e, n_valid, n_used, xs, w_gu, b_gu.reshape(DEPTH, N_EXPERTS, 1, 2 * D_FF), w_down,
      b_down.reshape(DEPTH, N_EXPERTS, 1, D_MODEL))


def _combine_body(x_ref, y0_ref, y1_ref, y2_ref, y3_ref, gate_ref, g2_ref, *rest):
    g = gate_ref[...]
    acc_hi = acc_lo = None
    for k, y_ref in enumerate((y0_ref, y1_ref, y2_ref, y3_ref)):
        p = y_ref[...]
        gk = g[:, k:k + 1]
        hi = gk * pltpu.bitcast(p & jnp.uint32(0xFFFF0000), F32)
        lo = gk * pltpu.bitcast(p << 16, F32)
        acc_hi = hi if acc_hi is None else acc_hi + hi
        acc_lo = lo if acc_lo is None else acc_lo + lo
    acc = jnp.concatenate([acc_hi, acc_lo], axis=1)
    xn = x_ref[...] + _scale_rows(acc, g2_ref[...])
    if len(rest) == 1:
        rest[0][...] = xn
    else:
        ng_ref, o_ref = rest
        xn = xn * lax.rsqrt(jnp.mean(xn * xn, axis=-1, keepdims=True) + EPS) * ng_ref[...]
        for t in range(o_ref.shape[1]):
            o_ref[:, t, :] = xn[t * SUBLANES:(t + 1) * SUBLANES, :]


def _combine(x, ysg, gates_t, tok_off, mod, final_g, ctx):
    R, D = x.shape
    tl = TOK_TILE
    t0 = tok_off // tl
    nt = gates_t.shape[0] // tl
    y_specs = [pl.BlockSpec((tl, D // 2), lambda i, k=k: (k * nt + t0 + i, 0)) for k in range(TOP_K)]
    in_specs = ([pl.BlockSpec((tl, D), lambda i: (i, 0))] + y_specs
                + [pl.BlockSpec((tl, TOP_K), lambda i: (t0 + i, 0)), _mod_spec(5, ctx)])
    args = [x, ysg, ysg, ysg, ysg, gates_t, mod]
    if final_g is None:
        out_spec = pl.BlockSpec((tl, D), lambda i: (i, 0))
        out_shape = jax.ShapeDtypeStruct((R, D), F32)
    else:
        in_specs.append(_full((1, D)))
        args.append(final_g.reshape(1, D))
        out_spec = pl.BlockSpec((SUBLANES, tl // SUBLANES, D), lambda i: (0, i, 0))
        out_shape = jax.ShapeDtypeStruct((SUBLANES, R // SUBLANES, D), F32)
    return pl.pallas_call(
        _combine_body,
        grid=(R // tl,),
        in_specs=in_specs,
        out_specs=out_spec,
        out_shape=out_shape,
        compiler_params=_cparams(("parallel",)),
        name="moe_combine",
    )(*args)


def _moe(hp, logits_t, layer, w_gu, b_gu, w_down, b_down):
    dest, gates, tile_e, n_valid, n_used, n_tiles = _routing(logits_t)
    dest_flat = dest.reshape(-1)
    xs = _dispatch(hp, dest_flat, n_tiles * MOE_TILE)
    ys = _experts(xs, tile_e, n_valid, n_used, layer, w_gu, b_gu, w_down, b_down)
    return _gather_rows(ys, dest_flat), gates.T


def _token_mixers(x, pos, mod, h0, p, consts, ctx, need_out):
    R = x.size // x.shape[-1]
    L = R // SUBLANES
    xa, ga, xb, xc, xd = _in_projection(x, pos, mod, p["norm1_g"], p["w_in"], p["b_in"], ctx)
    yf, yb, hfin = _rglru(xa, p["conv_a_w"], p["conv_a_b"], p["wg"], p["bg"], p["rg_lambda"], h0, L)
    if not need_out:
        return None, hfin
    yp = _pool_mixer(xb, p["w_pool"], p["b_pool"], p["pool_scale"], L)
    cl, sl = consts["dft"][L]
    yc = _fourier_mixer(xc.reshape(L, SUBLANES * W_GROUP), cl, sl, consts["cc"], consts["sc"], p["w_four"],
                        p["b_four"], L).reshape(R, W_GROUP)
    yd = _conformer(xd, p["conv_d_w"], p["conv_d_b"], p["ln_d_g"], p["ln_d_b"], consts["avg"], p["w_pw"],
                    p["b_pw"], L)
    return (yf, yb, ga, yp, yc, yd), hfin


def _pos_embed(n_tokens):
    rows_n = n_tokens // GRID_W
    row = jnp.repeat(jnp.arange(rows_n), GRID_W).astype(F32)
    col = jnp.tile(jnp.arange(GRID_W), rows_n).astype(F32)
    q = D_MODEL // 4
    omega = 1.0 / (10000.0 ** (jnp.arange(q, dtype=F32) / q))

    def emb(v):
        ang = v[:, None] * omega[None, :]
        return jnp.concatenate([jnp.sin(ang), jnp.cos(ang)], axis=-1)

    return jnp.concatenate([emb(row), emb(col)], axis=-1)


def _layer_params(l, w_in, b_in, conv_a_w, conv_a_b, w_rg_r, b_rg_r, w_rg_i, b_rg_i, rg_lambda, w_pool, b_pool,
                  pool_scale, w_four, b_four, conv_d_w, conv_d_b, ln_d_g, ln_d_b, w_pw, b_pw, norm1_g):
    wg = jnp.stack([jnp.concatenate([_block_diag(w_rg_r[l, d]), _block_diag(w_rg_i[l, d])], axis=1)
                    for d in range(2)]).astype(BF16)
    bg = jnp.concatenate([b_rg_r[l].reshape(2, 1, W_GROUP), b_rg_i[l].reshape(2, 1, W_GROUP)], axis=-1)
    return dict(
        norm1_g=norm1_g[l], w_in=w_in[l].astype(BF16), b_in=b_in[l],
        conv_a_w=conv_a_w[l], conv_a_b=conv_a_b[l], wg=wg, bg=bg, rg_lambda=rg_lambda[l],
        w_pool=_block_diag(w_pool[l]).astype(BF16), b_pool=b_pool[l], pool_scale=pool_scale[l],
        w_four=_block_diag(w_four[l]).astype(BF16), b_four=b_four[l],
        conv_d_w=conv_d_w[l], conv_d_b=conv_d_b[l], ln_d_g=ln_d_g[l], ln_d_b=ln_d_b[l],
        w_pw=w_pw[l].astype(BF16), b_pw=b_pw[l])


def kernel(x, c, ctx, c_ctx, w_mod, b_mod, norm1_g, norm2_g, w_in, b_in, conv_a_w, conv_a_b, w_rg_r, b_rg_r,
           w_rg_i, b_rg_i, rg_lambda, w_pool, b_pool, pool_scale, w_four, b_four, conv_d_w, conv_d_b, ln_d_g,
           ln_d_b, w_pw, b_pw, w_out, b_out, w_router, b_router, w_gu, b_gu, w_down, b_down, final_norm_g):
    bn, L, D = x.shape
    Lc = ctx.shape[1]
    assert bn == SUBLANES and D == D_MODEL

    pos = _pos_embed(L)
    c_rows = jnp.concatenate([c, jnp.broadcast_to(c_ctx[None], (MOD_ROWS - bn, D))], axis=0)
    mod = _modulation(c_rows, w_mod, b_mod)
    ctx = jnp.transpose(ctx, (1, 0, 2)).reshape(Lc * bn, D)

    cc1, sc1 = _dft_matrices(D_SUB)
    eye = jnp.eye(N_SUB, dtype=F32)
    consts = dict(
        dft={n: tuple(m.astype(BF16) for m in _dft_matrices(n)) for n in sorted({L, Lc})},
        cc=jnp.kron(eye, cc1).astype(BF16), sc=jnp.kron(eye, sc1).astype(BF16),
        avg=jnp.kron(eye, jnp.full((D_SUB, D_SUB), 1.0 / D_SUB, F32)).astype(BF16))
    h_zero = jnp.zeros((2, SUBLANES, W_GROUP), F32)

    for l in range(DEPTH):
        last = l == DEPTH - 1
        p = _layer_params(l, w_in, b_in, conv_a_w, conv_a_b, w_rg_r, b_rg_r, w_rg_i, b_rg_i, rg_lambda, w_pool,
                          b_pool, pool_scale, w_four, b_four, conv_d_w, conv_d_b, ln_d_g, ln_d_b, w_pw, b_pw,
                          norm1_g)
        mod3 = mod[l]
        w_out_bf = w_out[l].astype(BF16)
        wr_t = w_router[l].T
        wr_hi = wr_t.astype(BF16)
        wr2 = jnp.concatenate([wr_hi, (wr_t - wr_hi.astype(F32)).astype(BF16)], axis=0)
        x_pos = pos if l == 0 else None

        mix_c, h_ctx = _token_mixers(ctx, None, mod3, h_zero, p, consts, True, not last)
        mix_x, _ = _token_mixers(x, x_pos, mod3, h_ctx, p, consts, False, True)
        x, hp_x, lg_x = _out_projection(x, x_pos, mix_x, mod3, norm2_g[l], w_out_bf, b_out[l], wr2,
                                        b_router[l], False)
        if not last:
            ctx, hp_c, lg_c = _out_projection(ctx, None, mix_c, mod3, norm2_g[l], w_out_bf, b_out[l], wr2,
                                              b_router[l], True)
            hp = jnp.concatenate([hp_c, hp_x], axis=0)
            lg = jnp.concatenate([lg_c, lg_x], axis=1)
            ysg, gates_t = _moe(hp, lg, l, w_gu, b_gu, w_down, b_down)
            ctx = _combine(ctx, ysg, gates_t, 0, mod3, None, True)
            x = _combine(x, ysg, gates_t, bn * Lc, mod3, None, False)
        else:
            ysg, gates_t = _moe(hp_x, lg_x, l, w_gu, b_gu, w_down, b_down)
            x = _combine(x, ysg, gates_t, 0, mod3, final_norm_g, False)
    return x
```

```python
import functools
import math

import jax
import jax.numpy as jnp
from jax import lax
from jax.experimental import pallas as pl
from jax.experimental.pallas import tpu as pltpu

F32 = jnp.float32
BF16 = jnp.bfloat16

D_MODEL = 1024
DEPTH = 2
GRID_W = 64
W_GROUP = 256
N_SUB = 4
D_SUB = 64
D_IN = 6 * W_GROUP
RG_CONV = 4
RG_C = 8.0
CONF_KERNEL = 31
N_EXPERTS = 32
TOP_K = 4
D_FF = D_MODEL
SWIGLU_LIMIT = 7.0
SWIGLU_ALPHA = 1.702
EPS = 1e-6

SUBLANES = 8
VMEM_LIMIT_BYTES = 56 * 1024 * 1024
MOD_ROWS = 16
RG_HALO = 8 * SUBLANES
POOL_HALO = 8 * SUBLANES
CONF_HALO = 16 * SUBLANES
MOE_TILE = 512
TOK_TILE = 256
SC_WINDOW = 64
ROUTE_TILE = 512


def _cparams(sem):
    return pltpu.CompilerParams(dimension_semantics=sem, vmem_limit_bytes=VMEM_LIMIT_BYTES)


def _full(shape):
    nd = len(shape)
    return pl.BlockSpec(shape, lambda *_: (0,) * nd)


def _dot(a, b):
    return jnp.dot(a, b, preferred_element_type=F32)


def _split_bf16(v):
    hi = v.astype(BF16)
    lo = (v - hi.astype(F32)).astype(BF16)
    return hi, lo


def _mod_body(c_ref, w_ref, b_ref, o_ref):
    c = c_ref[...]
    s = c * jax.nn.sigmoid(c)
    o_ref[...] = jnp.dot(s, w_ref[...], precision=lax.Precision.HIGHEST,
                         preferred_element_type=F32) + b_ref[...]


def _modulation(c_rows, w_mod, b_mod):
    tn = 1536
    n6 = 6 * D_MODEL
    return pl.pallas_call(
        _mod_body,
        grid=(DEPTH, n6 // tn),
        in_specs=[_full((MOD_ROWS, D_MODEL)),
                  pl.BlockSpec((None, D_MODEL, tn), lambda l, j: (l, 0, j)),
                  pl.BlockSpec((None, 1, tn), lambda l, j: (l, 0, j))],
        out_specs=pl.BlockSpec((None, MOD_ROWS, tn), lambda l, j: (l, 0, j)),
        out_shape=jax.ShapeDtypeStruct((DEPTH, MOD_ROWS, n6), F32),
        compiler_params=_cparams(("parallel", "parallel")),
        name="modulation",
    )(c_rows, w_mod, b_mod.reshape(DEPTH, 1, n6))


def _mod_spec(chunk, ctx):
    return pl.BlockSpec((SUBLANES, D_MODEL), lambda i: (1 if ctx else 0, chunk))


def _scale_rows(v, m):
    r, d = v.shape
    return (v.reshape(r // SUBLANES, SUBLANES, d) * m[None]).reshape(r, d)


def _rms_mod(x, g, shift, scale):
    r, d = x.shape
    y = x * lax.rsqrt(jnp.mean(x * x, axis=-1, keepdims=True) + EPS) * g
    y3 = y.reshape(r // SUBLANES, SUBLANES, d)
    return (y3 * (1.0 + scale)[None] + shift[None]).reshape(r, d)


def _load_plus_pos(x_ref, pos_ref):
    return jnp.concatenate([x_ref[:, t, :] + pos_ref[t:t + 1, :] for t in range(pos_ref.shape[0])], axis=0)


def _x_spec(x, tr):
    if x.ndim == 3:
        return pl.BlockSpec((SUBLANES, tr // SUBLANES, x.shape[2]), lambda i: (0, i, 0))
    return pl.BlockSpec((tr, x.shape[1]), lambda i: (i, 0))


def _inproj_body(*refs, add_pos):
    if add_pos:
        x_ref, pos_ref, sh_ref, sc_ref, g_ref, w_ref, b_ref, xa_ref, ga_ref, xb_ref, xc_ref, xd_ref = refs
        x = _load_plus_pos(x_ref, pos_ref)
    else:
        x_ref, sh_ref, sc_ref, g_ref, w_ref, b_ref, xa_ref, ga_ref, xb_ref, xc_ref, xd_ref = refs
        x = x_ref[...]
    u = _rms_mod(x, g_ref[...], sh_ref[...], sc_ref[...])
    p = _dot(u.astype(BF16), w_ref[...]) + b_ref[...]
    xa_ref[...] = p[:, 0:256]
    ga_ref[...] = p[:, 256:512]
    xb_ref[...] = p[:, 512:768]
    xc_ref[...] = p[:, 768:1024].astype(BF16)
    xd_ref[...] = p[:, 1024:1536]


def _in_projection(x, pos, mod, norm_g, w_in_bf, b_in, ctx):
    add_pos = pos is not None
    assert add_pos == (x.ndim == 3)
    R, D = x.size // x.shape[-1], x.shape[-1]
    tr = min(R, 512)
    row = lambda w: pl.BlockSpec((tr, w), lambda i: (i, 0))
    in_specs = [_x_spec(x, tr)]
    args = [x]
    if add_pos:
        in_specs.append(pl.BlockSpec((tr // SUBLANES, D), lambda i: (i, 0)))
        args.append(pos)
    in_specs += [_mod_spec(0, ctx), _mod_spec(1, ctx), _full((1, D)), _full((D, D_IN)), _full((1, D_IN))]
    args += [mod, mod, norm_g.reshape(1, D), w_in_bf, b_in.reshape(1, D_IN)]
    out_shape = [jax.ShapeDtypeStruct((R, 256), F32)] * 3 + [
        jax.ShapeDtypeStruct((R, 256), BF16), jax.ShapeDtypeStruct((R, 512), F32)]
    return pl.pallas_call(
        functools.partial(_inproj_body, add_pos=add_pos),
        grid=(R // tr,),
        in_specs=in_specs,
        out_specs=[row(256), row(256), row(256), row(256), row(512)],
        out_shape=out_shape,
        compiler_params=_cparams(("parallel",)),
        name="in_projection",
    )(*args)


def _rg_gates(xc, wg, bg, lam):
    g = _dot(xc.astype(BF16), wg) + bg
    r = jax.nn.sigmoid(g[:, :W_GROUP])
    gi = jax.nn.sigmoid(g[:, W_GROUP:])
    z = -lam
    softplus = jnp.maximum(z, 0.0) + jnp.log1p(jnp.exp(-jnp.abs(z)))
    log_a = (-RG_C) * r * softplus
    a = jnp.exp(log_a)
    b = jnp.sqrt(-jnp.tanh(log_a) * (a * a + 1.0)) * (gi * xc)
    return a, b


def _rg_body(xf_ref, xfh_ref, xr_ref, xrh_ref, cw_ref, cb_ref, wg_ref, bg_ref, lam_ref, h0_ref,
             yf_ref, yb_ref, hfin_ref, af_ref, ab_ref, hc_ref, *, n, tt):
    i = pl.program_id(0)
    tr = tt * SUBLANES
    keep = RG_HALO - (RG_CONV - 1) * SUBLANES

    @pl.when(i == 0)
    def _():
        hc_ref[...] = h0_ref[...]

    halo = jnp.where(i > 0, xfh_ref[...], 0.0)
    ext = jnp.concatenate([halo[keep:], xf_ref[...]], axis=0)
    xc = cb_ref[0]
    for k in range(RG_CONV):
        xc = xc + cw_ref[0, k:k + 1, :] * ext[k * SUBLANES:k * SUBLANES + tr]
    a, b = _rg_gates(xc, wg_ref[0], bg_ref[0], lam_ref[0])
    af_ref[...] = a
    yf_ref[...] = b

    halo = jnp.where(i > 0, xrh_ref[...], 0.0)
    ext = jnp.concatenate([xr_ref[...], halo[:(RG_CONV - 1) * SUBLANES]], axis=0)
    xc = cb_ref[1]
    for k in range(RG_CONV):
        o = (RG_CONV - 1 - k) * SUBLANES
        xc = xc + cw_ref[1, k:k + 1, :] * ext[o:o + tr]
    a, b = _rg_gates(xc, wg_ref[1], bg_ref[1], lam_ref[1])
    ab_ref[...] = a
    yb_ref[...] = b

    def step(t, carry):
        hf, hb = carry
        rf = pl.multiple_of(t * SUBLANES, SUBLANES)
        hf = af_ref[pl.ds(rf, SUBLANES), :] * hf + yf_ref[pl.ds(rf, SUBLANES), :]
        yf_ref[pl.ds(rf, SUBLANES), :] = hf
        rb = pl.multiple_of((tt - 1 - t) * SUBLANES, SUBLANES)
        hb = ab_ref[pl.ds(rb, SUBLANES), :] * hb + yb_ref[pl.ds(rb, SUBLANES), :]
        yb_ref[pl.ds(rb, SUBLANES), :] = hb
        return hf, hb

    hf, hb = lax.fori_loop(0, tt, step, (hc_ref[0], hc_ref[1]), unroll=8)
    hc_ref[0] = hf
    hc_ref[1] = hb
    hfin_ref[0] = hf
    hfin_ref[1] = hb


def _rglru(xa2, conv_w, conv_b, wg_bf, bg, lam, h0, L):
    tt = min(L, 256)
    n = L // tt
    tr = tt * SUBLANES
    per = tr // RG_HALO
    last_halo = L * SUBLANES // RG_HALO - 1
    row = lambda i: (i, 0)
    rev = lambda i: (n - 1 - i, 0)
    in_specs = [
        pl.BlockSpec((tr, W_GROUP), row),
        pl.BlockSpec((RG_HALO, W_GROUP), lambda i: (jnp.maximum(i * per - 1, 0), 0)),
        pl.BlockSpec((tr, W_GROUP), rev),
        pl.BlockSpec((RG_HALO, W_GROUP), lambda i: (jnp.minimum((n - i) * per, last_halo), 0)),
        _full((2, RG_CONV, W_GROUP)), _full((2, 1, W_GROUP)), _full((2, W_GROUP, 2 * W_GROUP)),
        _full((2, 1, 2 * W_GROUP)), _full((2, 1, W_GROUP)), _full((2, SUBLANES, W_GROUP)),
    ]
    return pl.pallas_call(
        functools.partial(_rg_body, n=n, tt=tt),
        grid=(n,),
        in_specs=in_specs,
        out_specs=[pl.BlockSpec((tr, W_GROUP), row), pl.BlockSpec((tr, W_GROUP), rev),
                   _full((2, SUBLANES, W_GROUP))],
        out_shape=[jax.ShapeDtypeStruct((L * SUBLANES, W_GROUP), F32)] * 2
        + [jax.ShapeDtypeStruct((2, SUBLANES, W_GROUP), F32)],
        scratch_shapes=[pltpu.VMEM((tr, W_GROUP), F32), pltpu.VMEM((tr, W_GROUP), F32),
                        pltpu.VMEM((2, SUBLANES, W_GROUP), F32)],
        compiler_params=_cparams(("arbitrary",)),
        name="rglru",
    )(xa2, xa2, xa2, xa2, conv_w, conv_b.reshape(2, 1, W_GROUP), wg_bf, bg, lam.reshape(2, 1, W_GROUP), h0)


def _pool_body(xm_ref, xp_ref, xn_ref, w_ref, b_ref, s_ref, o_ref, *, n, tt, L):
    i = pl.program_id(0)
    tr = tt * SUBLANES
    S = SUBLANES
    xm = xm_ref[...]
    prev = jnp.where(i > 0, xp_ref[...], 0.0)
    nxt = jnp.where(i < n - 1, xn_ref[...], 0.0)
    xe = jnp.concatenate([prev, xm, nxt], axis=0)
    e = xe.shape[0]
    p2 = xe[S:e] + xe[0:e - S]
    n4 = (tt + 13) * S
    p4 = p2[0:n4] + p2[2 * S:2 * S + n4]
    n8 = (tt + 9) * S
    p8 = p4[0:n8] + p4[4 * S:4 * S + n8]
    s16 = p8[0:tr] + p8[8 * S:8 * S + tr]
    s2 = p2[7 * S:7 * S + tr]
    s4 = p4[6 * S:6 * S + tr]
    s8 = p8[4 * S:4 * S + tr]
    grp = lax.broadcasted_iota(jnp.int32, (1, W_GROUP), 1) // D_SUB
    half = jnp.left_shift(1, grp)
    t = i * tt + lax.broadcasted_iota(jnp.int32, (tr, 1), 0) // S
    cnt = (jnp.minimum(t + half, L) - jnp.maximum(t - half, 0)).astype(F32)
    s = jnp.where(grp == 0, s2, jnp.where(grp == 1, s4, jnp.where(grp == 2, s8, s16)))
    pooled = s / cnt - xm
    y = _dot(pooled.astype(BF16), w_ref[...]) + b_ref[...]
    o_ref[...] = (y * s_ref[...]).astype(BF16)


def _pool_mixer(xb2, w_bd_bf, b, scale, L):
    tt = min(L, 256)
    n = L // tt
    tr = tt * SUBLANES
    per = tr // POOL_HALO
    last_halo = L * SUBLANES // POOL_HALO - 1
    return pl.pallas_call(
        functools.partial(_pool_body, n=n, tt=tt, L=L),
        grid=(n,),
        in_specs=[pl.BlockSpec((tr, W_GROUP), lambda i: (i, 0)),
                  pl.BlockSpec((POOL_HALO, W_GROUP), lambda i: (jnp.maximum(i * per - 1, 0), 0)),
                  pl.BlockSpec((POOL_HALO, W_GROUP), lambda i: (jnp.minimum((i + 1) * per, last_halo), 0)),
                  _full((W_GROUP, W_GROUP)), _full((1, W_GROUP)), _full((1, W_GROUP))],
        out_specs=pl.BlockSpec((tr, W_GROUP), lambda i: (i, 0)),
        out_shape=jax.ShapeDtypeStruct((L * SUBLANES, W_GROUP), BF16),
        compiler_params=_cparams(("parallel",)),
        name="pool_mixer",
    )(xb2, xb2, xb2, w_bd_bf, b.reshape(1, W_GROUP), scale.reshape(1, W_GROUP))


def _fourier_body(c_ref, s_ref, x_ref, cc_ref, sc_ref, w_ref, b_ref, o_ref):
    x = x_ref[...]
    z1 = _dot(c_ref[...], x)
    z2 = _dot(s_ref[...], x)
    for j in range(x.shape[1] // W_GROUP):
        sl = slice(j * W_GROUP, (j + 1) * W_GROUP)
        a_hi, a_lo = _split_bf16(z1[:, sl])
        b_hi, b_lo = _split_bf16(z2[:, sl])
        f = (_dot(a_hi, cc_ref[...]) + _dot(a_lo, cc_ref[...])) - (_dot(b_hi, sc_ref[...]) + _dot(b_lo, sc_ref[...]))
        o_ref[:, sl] = (_dot(f.astype(BF16), w_ref[...]) + b_ref[...]).astype(BF16)


def _fourier_mixer(xc_tm, cl, sl, cc, sc, w_bd_bf, b, L):
    ncol = xc_tm.shape[1]
    nb = min(ncol, 1024)
    tk = min(L, 256)
    return pl.pallas_call(
        _fourier_body,
        grid=(ncol // nb, L // tk),
        in_specs=[pl.BlockSpec((tk, L), lambda j, k: (k, 0)),
                  pl.BlockSpec((tk, L), lambda j, k: (k, 0)),
                  pl.BlockSpec((L, nb), lambda j, k: (0, j)),
                  _full((W_GROUP, W_GROUP)), _full((W_GROUP, W_GROUP)), _full((W_GROUP, W_GROUP)),
                  _full((1, W_GROUP))],
        out_specs=pl.BlockSpec((tk, nb), lambda j, k: (k, j)),
        out_shape=jax.ShapeDtypeStruct((L, ncol), BF16),
        compiler_params=_cparams(("parallel", "parallel")),
        name="fourier_mixer",
    )(cl, sl, xc_tm, cc, sc, w_bd_bf, b.reshape(1, W_GROUP))


def _dft_matrices(L):
    f = 1 << (max(L.bit_length() - 1, 0) // 2)
    n = jnp.arange(L, dtype=jnp.int32)[None, :]

    def table(rows):
        ang = ((rows[:, None] * n) % L).astype(F32) * (2.0 * math.pi / L)
        return jnp.cos(ang), jnp.sin(ang)

    ac, as_ = table(jnp.arange(L // f, dtype=jnp.int32) * f)
    bc, bs = table(jnp.arange(f, dtype=jnp.int32))
    scale = 1.0 / math.sqrt(L)
    cos = (ac[:, None, :] * bc[None, :, :] - as_[:, None, :] * bs[None, :, :]).reshape(L, L) * scale
    sin = (as_[:, None, :] * bc[None, :, :] + ac[:, None, :] * bs[None, :, :]).reshape(L, L) * scale
    return cos, sin


def _block_diag(w):
    g, a, b = w.shape
    eye = jnp.eye(g, dtype=w.dtype)
    return (eye[:, None, :, None] * w[:, :, None, :]).reshape(g * a, g * b)


CONF_CHUNK = 64


def _conformer_body(xm_ref, xp_ref, xn_ref, cw_ref, cb_ref, lg_ref, lb_ref, avg_ref, w_ref, b_ref,
                    o_ref, v_ref, c_ref, *, n, tt):
    i = pl.program_id(0)
    tr = tt * SUBLANES
    H = CONF_HALO

    def glu(v):
        return v[:, :W_GROUP] * jax.nn.sigmoid(v[:, W_GROUP:])

    v_ref[0:H] = jnp.where(i > 0, glu(xp_ref[...]), 0.0)
    v_ref[H:H + tr] = glu(xm_ref[...])
    v_ref[H + tr:H + tr + H] = jnp.where(i < n - 1, glu(xn_ref[...]), 0.0)

    def chunk(c, carry):
        r0 = pl.multiple_of(c * CONF_CHUNK, CONF_CHUNK)
        acc = jnp.broadcast_to(cb_ref[...], (CONF_CHUNK, W_GROUP))
        for k in range(CONF_KERNEL):
            acc = acc + cw_ref[k:k + 1, :] * v_ref[pl.ds(r0 + (k + 1) * SUBLANES, CONF_CHUNK), :]
        c_ref[pl.ds(r0, CONF_CHUNK), :] = acc
        return carry

    lax.fori_loop(0, tr // CONF_CHUNK, chunk, 0)

    v = c_ref[...]
    avg = avg_ref[...]
    v_hi, v_lo = _split_bf16(v)
    mu = _dot(v_hi, avg) + _dot(v_lo, avg)
    d = v - mu
    q_hi, q_lo = _split_bf16(d * d)
    var = _dot(q_hi, avg) + _dot(q_lo, avg)
    vn = d * lax.rsqrt(var + EPS) * lg_ref[...] + lb_ref[...]
    act = vn * jax.nn.sigmoid(vn)
    o_ref[...] = (_dot(act.astype(BF16), w_ref[...]) + b_ref[...]).astype(BF16)


def _conformer(xd2, conv_w, conv_b, ln_g, ln_b, avg_bf, w_pw_bf, b_pw, L):
    tt = min(L, 256)
    n = L // tt
    tr = tt * SUBLANES
    per = tr // CONF_HALO
    last_halo = L * SUBLANES // CONF_HALO - 1
    vec = lambda a: a.reshape(1, W_GROUP)
    return pl.pallas_call(
        functools.partial(_conformer_body, n=n, tt=tt),
        grid=(n,),
        in_specs=[pl.BlockSpec((tr, 2 * W_GROUP), lambda i: (i, 0)),
                  pl.BlockSpec((CONF_HALO, 2 * W_GROUP), lambda i: (jnp.maximum(i * per - 1, 0), 0)),
                  pl.BlockSpec((CONF_HALO, 2 * W_GROUP), lambda i: (jnp.minimum((i + 1) * per, last_halo), 0)),
                  _full((CONF_KERNEL, W_GROUP)), _full((1, W_GROUP)), _full((1, W_GROUP)), _full((1, W_GROUP)),
                  _full((W_GROUP, W_GROUP)), _full((W_GROUP, W_GROUP)), _full((1, W_GROUP))],
        out_specs=pl.BlockSpec((tr, W_GROUP), lambda i: (i, 0)),
        out_shape=jax.ShapeDtypeStruct((L * SUBLANES, W_GROUP), BF16),
        scratch_shapes=[pltpu.VMEM((tr + 2 * CONF_HALO, W_GROUP), F32), pltpu.VMEM((tr, W_GROUP), F32)],
        compiler_params=_cparams(("parallel",)),
        name="conformer",
    )(xd2, xd2, xd2, conv_w, vec(conv_b), vec(ln_g), vec(ln_b), avg_bf, w_pw_bf, vec(b_pw))


def _gelu_tanh(x):
    return 0.5 * x * (1.0 + jnp.tanh(math.sqrt(2.0 / math.pi) * (x + 0.044715 * (x * x * x))))


def _pack_bf16_pairs(h_bf):
    u = pltpu.bitcast(h_bf.astype(F32), jnp.uint32)
    half = h_bf.shape[1] // 2
    return (u[:, :half] & jnp.uint32(0xFFFF0000)) | (u[:, half:] >> 16)


def _unpack_bf16_pairs(p):
    hi = pltpu.bitcast(p & jnp.uint32(0xFFFF0000), F32).astype(BF16)
    lo = pltpu.bitcast(p << 16, F32).astype(BF16)
    return hi, lo


def _outproj_body(*refs, add_pos):
    if add_pos:
        (x_ref, pos_ref, yf_ref, yb_ref, ga_ref, yp_ref, yc_ref, yd_ref, g1_ref, sh_ref, sc_ref, ng_ref,
         wo_ref, bo_ref, wr_ref, br_ref, xo_ref, hp_ref, lg_ref) = refs
        x = _load_plus_pos(x_ref, pos_ref)
    else:
        (x_ref, yf_ref, yb_ref, ga_ref, yp_ref, yc_ref, yd_ref, g1_ref, sh_ref, sc_ref, ng_ref,
         wo_ref, bo_ref, wr_ref, br_ref, xo_ref, hp_ref, lg_ref) = refs
        x = x_ref[...]
    ya = (yf_ref[...] + yb_ref[...]) * _gelu_tanh(ga_ref[...])
    ycat = jnp.concatenate([ya.astype(BF16), yp_ref[...], yc_ref[...], yd_ref[...]], axis=1)
    y = _dot(ycat, wo_ref[...]) + bo_ref[...]
    xn = x + _scale_rows(y, g1_ref[...])
    xo_ref[...] = xn
    h = _rms_mod(xn, ng_ref[...], sh_ref[...], sc_ref[...])
    h_hi, h_lo = _split_bf16(h)
    nt = (((1,), (1,)), ((), ()))
    wr = wr_ref[...]
    acc = lax.dot_general(wr, h_hi, nt, preferred_element_type=F32)
    acc = acc + lax.dot_general(wr, h_lo, nt, preferred_element_type=F32)
    lg_ref[...] = acc[:N_EXPERTS] + acc[N_EXPERTS:] + br_ref[...]
    hp_ref[...] = _pack_bf16_pairs(h_hi)


def _out_projection(x, pos, mix, mod, norm_g, w_out_bf, b_out, wr2_bf, b_router, ctx):
    add_pos = pos is not None
    assert add_pos == (x.ndim == 3)
    R, D = x.size // x.shape[-1], x.shape[-1]
    tr = min(R, 512)
    row = lambda w: pl.BlockSpec((tr, w), lambda i: (i, 0))
    in_specs = [_x_spec(x, tr)]
    args = [x]
    if add_pos:
        in_specs.append(pl.BlockSpec((tr // SUBLANES, D), lambda i: (i, 0)))
        args.append(pos)
    in_specs += [row(W_GROUP)] * 6
    args += list(mix)
    in_specs += [_mod_spec(2, ctx), _mod_spec(3, ctx), _mod_spec(4, ctx), _full((1, D)), _full((D, D)),
                 _full((1, D)), _full((2 * N_EXPERTS, D)), _full((N_EXPERTS, 1))]
    args += [mod, mod, mod, norm_g.reshape(1, D), w_out_bf, b_out.reshape(1, D), wr2_bf,
             b_router.reshape(N_EXPERTS, 1)]
    return pl.pallas_call(
        functools.partial(_outproj_body, add_pos=add_pos),
        grid=(R // tr,),
        in_specs=in_specs,
        out_specs=[row(D), row(D // 2), pl.BlockSpec((N_EXPERTS, tr), lambda i: (0, i))],
        out_shape=[jax.ShapeDtypeStruct((R, D), F32),
                   jax.ShapeDtypeStruct((R, D // 2), jnp.uint32),
                   jax.ShapeDtypeStruct((N_EXPERTS, R), F32)],
        compiler_params=_cparams(("parallel",)),
        name="out_projection",
    )(*args)


def _top4(v):
    eid = lax.broadcasted_iota(jnp.int32, v.shape, 0)
    out = []
    work = v
    for _ in range(TOP_K):
        m = jnp.max(work, axis=0, keepdims=True)
        idx = jnp.min(jnp.where(work == m, eid, N_EXPERTS), axis=0, keepdims=True)
        oh = eid == idx
        out.append((m, oh))
        work = jnp.where(oh, -jnp.inf, work)
    return out


def _count_body(lg_ref, cnt_ref):
    @pl.when(pl.program_id(0) == 0)
    def _():
        cnt_ref[...] = jnp.zeros_like(cnt_ref)

    sel = jnp.zeros(lg_ref.shape, F32)
    for _, oh in _top4(lg_ref[...]):
        sel = sel + oh.astype(F32)
    cnt_ref[...] += jnp.sum(sel, axis=1, keepdims=True)


def _route_body(lg_ref, ps_ref, tri_ref, dest_ref, gate_ref, carry_ref):
    @pl.when(pl.program_id(0) == 0)
    def _():
        carry_ref[...] = jnp.zeros_like(carry_ref)

    top = _top4(lg_ref[...])
    sel = jnp.zeros(lg_ref.shape, F32)
    for _, oh in top:
        sel = sel + oh.astype(F32)
    before = _dot(sel.astype(BF16), tri_ref[...]) + carry_ref[...] + ps_ref[...]
    m0 = top[0][0]
    es = [jnp.exp(m - m0) for m, _ in top]
    den = es[0] + es[1] + es[2] + es[3]
    for k, (_, oh) in enumerate(top):
        dest_ref[k:k + 1, :] = jnp.sum(jnp.where(oh, before, 0.0), axis=0, keepdims=True).astype(jnp.int32)
        gate_ref[k:k + 1, :] = es[k] / den
    carry_ref[...] += jnp.sum(sel, axis=1, keepdims=True)


def _routing(logits_t):
    E, T = logits_t.shape
    tt = ROUTE_TILE
    nt = T // tt
    counts = pl.pallas_call(
        _count_body,
        grid=(nt,),
        in_specs=[pl.BlockSpec((E, tt), lambda i: (0, i))],
        out_specs=_full((E, 1)),
        out_shape=jax.ShapeDtypeStruct((E, 1), F32),
        compiler_params=_cparams(("arbitrary",)),
        name="route_count",
    )(logits_t)
    cnt = counts[:, 0].astype(jnp.int32)
    padded = ((cnt + MOE_TILE - 1) // MOE_TILE) * MOE_TILE
    pend = jnp.cumsum(padded)
    pstart = pend - padded
    n_tiles = -(-(T * TOP_K) // MOE_TILE) + N_EXPERTS
    tile_start = jnp.arange(n_tiles, dtype=jnp.int32) * MOE_TILE
    tile_e = jnp.minimum(jnp.sum((pend[None, :] <= tile_start[:, None]).astype(jnp.int32), axis=1), N_EXPERTS - 1)
    n_used = (pend[-1] // MOE_TILE).astype(jnp.int32).reshape(1)
    n_valid = jnp.clip((pstart + cnt)[tile_e] - tile_start, 0, MOE_TILE).astype(jnp.int32)
    tri = (jnp.arange(tt)[:, None] < jnp.arange(tt)[None, :]).astype(BF16)
    dest, gates = pl.pallas_call(
        _route_body,
        grid=(nt,),
        in_specs=[pl.BlockSpec((E, tt), lambda i: (0, i)), _full((E, 1)), _full((tt, tt))],
        out_specs=[pl.BlockSpec((TOP_K, tt), lambda i: (0, i)), pl.BlockSpec((TOP_K, tt), lambda i: (0, i))],
        out_shape=[jax.ShapeDtypeStruct((TOP_K, T), jnp.int32), jax.ShapeDtypeStruct((TOP_K, T), F32)],
        scratch_shapes=[pltpu.VMEM((E, 1), F32)],
        compiler_params=_cparams(("arbitrary",)),
        name="route_assign",
    )(logits_t, pstart.astype(F32).reshape(E, 1), tri)
    return dest, gates, tile_e, n_valid, n_used, n_tiles


def _sc_workers():
    from jax.experimental.pallas import tpu_sc as plsc
    mesh = plsc.VectorSubcoreMesh(core_axis_name="c", subcore_axis_name="s")
    n_workers = mesh.num_cores * mesh.num_subcores
    worker = lambda: lax.axis_index("s") * mesh.num_cores + lax.axis_index("c")
    return mesh, n_workers, worker


def _dispatch(hp, tok0, dest_flat, n_rows):
    W = hp.shape[1]
    T = dest_flat.shape[0] // TOP_K
    mesh, n_workers, worker = _sc_workers()
    per = dest_flat.shape[0] // n_workers
    steps = per // SC_WINDOW
    assert per * n_workers == dest_flat.shape[0] and steps * SC_WINDOW == per and steps % 2 == 0
    assert T % SC_WINDOW == 0 and tok0 % SC_WINDOW == 0

    @functools.partial(
        pl.kernel, mesh=mesh, out_type=jax.ShapeDtypeStruct((n_rows, W), hp.dtype),
        scratch_types=[pltpu.VMEM((SC_WINDOW,), jnp.int32), pltpu.VMEM((SC_WINDOW,), jnp.int32),
                       pltpu.VMEM((SC_WINDOW, W), hp.dtype), pltpu.VMEM((SC_WINDOW, W), hp.dtype),
                       pltpu.SemaphoreType.DMA, pltpu.SemaphoreType.DMA],
        name="moe_dispatch")
    def scatter(hp_hbm, dest_hbm, xs_hbm, idx0, idx1, rows0, rows1, sem0, sem1):
        base = worker() * per
        bufs = ((idx0, rows0, sem0), (idx1, rows1, sem1))

        def window(j, b, first):
            idx_v, rows_v, sem = bufs[b]

            @pl.when(jnp.logical_not(first))
            def _():
                pltpu.make_async_copy(rows_v, xs_hbm.at[idx_v], sem).wait()

            off = pl.multiple_of(base + j * SC_WINDOW, SC_WINDOW)
            tok = pl.multiple_of(tok0 + lax.rem(off, T), SC_WINDOW)
            pltpu.sync_copy(dest_hbm.at[pl.ds(off, SC_WINDOW)], idx_v)
            pltpu.sync_copy(hp_hbm.at[pl.ds(tok, SC_WINDOW)], rows_v)
            pltpu.async_copy(rows_v, xs_hbm.at[idx_v], sem)

        @pl.loop(0, steps, step=2)
        def _(j):
            window(j, 0, j == 0)
            window(j + 1, 1, j == 0)

        for idx_v, rows_v, sem in bufs:
            pltpu.make_async_copy(rows_v, xs_hbm.at[idx_v], sem).wait()

    return scatter(hp, dest_flat)


def _gather_rows(table, idx_flat):
    n = idx_flat.shape[0]
    W = table.shape[1]
    mesh, n_workers, worker = _sc_workers()
    per = n // n_workers
    steps = per // SC_WINDOW
    assert per * n_workers == n and steps * SC_WINDOW == per and steps % 2 == 0

    @functools.partial(
        pl.kernel, mesh=mesh, out_type=jax.ShapeDtypeStruct((n, W), table.dtype),
        scratch_types=[pltpu.VMEM((SC_WINDOW,), jnp.int32), pltpu.VMEM((SC_WINDOW,), jnp.int32),
                       pltpu.VMEM((SC_WINDOW, W), table.dtype), pltpu.VMEM((SC_WINDOW, W), table.dtype),
                       pltpu.SemaphoreType.DMA, pltpu.SemaphoreType.DMA, pltpu.SemaphoreType.DMA],
        name="moe_gather")
    def gather(table_hbm, idx_hbm, out_hbm, idx0, idx1, rows0, rows1, sem0, sem1, gsem):
        base = worker() * per
        bufs = ((idx0, rows0, sem0), (idx1, rows1, sem1))

        def window(j, b, first):
            idx_v, rows_v, sem = bufs[b]
            off = pl.multiple_of(base + j * SC_WINDOW, SC_WINDOW)

            @pl.when(jnp.logical_not(first))
            def _():
                pltpu.make_async_copy(rows_v, out_hbm.at[pl.ds(off, SC_WINDOW)], sem).wait()

            pltpu.sync_copy(idx_hbm.at[pl.ds(off, SC_WINDOW)], idx_v)
            pltpu.async_copy(table_hbm.at[idx_v], rows_v, gsem).wait()
            pltpu.async_copy(rows_v, out_hbm.at[pl.ds(off, SC_WINDOW)], sem)

        @pl.loop(0, steps, step=2)
        def _(j):
            window(j, 0, j == 0)
            window(j + 1, 1, j == 0)

        for _, rows_v, sem in bufs:
            pltpu.make_async_copy(rows_v, out_hbm.at[pl.ds(base, SC_WINDOW)], sem).wait()

    return gather(table, idx_flat)


def _expert_body(te_ref, nv_ref, nu_ref, xs_ref, wgu_ref, bgu_ref, wd_ref, bd_ref, ys_ref, wgu_bf_ref, wd_bf_ref):
    i = pl.program_id(0)
    used = i < nu_ref[0]
    new_expert = jnp.logical_or(i == 0, te_ref[i] != te_ref[jnp.maximum(i - 1, 0)])

    @pl.when(jnp.logical_and(used, new_expert))
    def _():
        wgu_bf_ref[...] = wgu_ref[...].astype(BF16)
        wd_bf_ref[...] = wd_ref[...].astype(BF16)

    @pl.when(used)
    def _():
        live = lax.broadcasted_iota(jnp.int32, (MOE_TILE, 1), 0) < nv_ref[i]
        x = jnp.concatenate(_unpack_bf16_pairs(jnp.where(live, xs_ref[...], jnp.uint32(0))), axis=1)
        gu = _dot(x, wgu_bf_ref[...]) + bgu_ref[...]
        gt = jnp.minimum(gu[:, :D_FF], SWIGLU_LIMIT)
        up = jnp.clip(gu[:, D_FF:], -SWIGLU_LIMIT, SWIGLU_LIMIT)
        act = (up + 1.0) * (gt * jax.nn.sigmoid(SWIGLU_ALPHA * gt))
        y = _dot(act.astype(BF16), wd_bf_ref[...]) + bd_ref[...]
        ys_ref[...] = _pack_bf16_pairs(y.astype(BF16))

    @pl.when(jnp.logical_not(used))
    def _():
        ys_ref[...] = jnp.zeros_like(ys_ref)


def _experts(xs, tile_e, n_valid, n_used, layer, w_gu, b_gu, w_down, b_down):
    n_rows, W = xs.shape
    n_tiles = n_rows // MOE_TILE
    grid_spec = pltpu.PrefetchScalarGridSpec(
        num_scalar_prefetch=3,
        grid=(n_tiles,),
        in_specs=[pl.BlockSpec((MOE_TILE, W), lambda i, te, nv, nu: (i, 0)),
                  pl.BlockSpec((None, None, D_MODEL, 2 * D_FF), lambda i, te, nv, nu: (layer, te[i], 0, 0)),
                  pl.BlockSpec((None, None, 1, 2 * D_FF), lambda i, te, nv, nu: (layer, te[i], 0, 0)),
                  pl.BlockSpec((None, None, D_FF, D_MODEL), lambda i, te, nv, nu: (layer, te[i], 0, 0)),
                  pl.BlockSpec((None, None, 1, D_MODEL), lambda i, te, nv, nu: (layer, te[i], 0, 0))],
        out_specs=pl.BlockSpec((MOE_TILE, W), lambda i, te, nv, nu: (i, 0)),
        scratch_shapes=[pltpu.VMEM((D_MODEL, 2 * D_FF), BF16), pltpu.VMEM((D_FF, D_MODEL), BF16)],
    )
    return pl.pallas_call(
        _expert_body,
        grid_spec=grid_spec,
        out_shape=jax.ShapeDtypeStruct((n_rows, W), jnp.uint32),
        compiler_params=_cparams(("arbitrary",)),
        name="moe_experts",
    )(tile_e, n_valid, n_used, xs, w_gu, b_gu.reshape(DEPTH, N_EXPERTS, 1, 2 * D_FF), w_down,
      b_down.reshape(DEPTH, N_EXPERTS, 1, D_MODEL))


def _combine_body(x_ref, y0_ref, y1_ref, y2_ref, y3_ref, gate_ref, g2_ref, *rest, final, chained):
    rest = list(rest)
    o_ref = rest.pop()
    if chained:
        rest.pop()
    g = gate_ref[...]
    acc_hi = acc_lo = None
    for k, y_ref in enumerate((y0_ref, y1_ref, y2_ref, y3_ref)):
        p = y_ref[...]
        gk = g[:, k:k + 1]
        hi = gk * pltpu.bitcast(p & jnp.uint32(0xFFFF0000), F32)
        lo = gk * pltpu.bitcast(p << 16, F32)
        acc_hi = hi if acc_hi is None else acc_hi + hi
        acc_lo = lo if acc_lo is None else acc_lo + lo
    acc = jnp.concatenate([acc_hi, acc_lo], axis=1)
    xn = x_ref[...] + _scale_rows(acc, g2_ref[...])
    if final:
        xn = xn * lax.rsqrt(jnp.mean(xn * xn, axis=-1, keepdims=True) + EPS) * rest[0][...]
        for t in range(o_ref.shape[1]):
            o_ref[:, t, :] = xn[t * SUBLANES:(t + 1) * SUBLANES, :]
    else:
        o_ref[...] = xn


def _combine(x, rows, moe_out, tok_off, mod, final_g, ctx, prev=None):
    ysg, gates_t = moe_out
    R, D = x.shape
    tl = TOK_TILE
    r0 = rows[0] // tl
    t0 = tok_off // tl
    nt = gates_t.shape[0] // tl
    final = final_g is not None
    y_specs = [pl.BlockSpec((tl, D // 2), lambda i, k=k: (k * nt + t0 + i, 0)) for k in range(TOP_K)]
    in_specs = ([pl.BlockSpec((tl, D), lambda i: (r0 + i, 0))] + y_specs
                + [pl.BlockSpec((tl, TOP_K), lambda i: (t0 + i, 0)), _mod_spec(5, ctx)])
    args = [x, ysg, ysg, ysg, ysg, gates_t, mod]
    if final:
        in_specs.append(_full((1, D)))
        args.append(final_g.reshape(1, D))
        out_spec = pl.BlockSpec((SUBLANES, tl // SUBLANES, D), lambda i: (0, r0 + i, 0))
        out_shape = jax.ShapeDtypeStruct((SUBLANES, R // SUBLANES, D), F32)
    else:
        out_spec = pl.BlockSpec((tl, D), lambda i: (r0 + i, 0))
        out_shape = jax.ShapeDtypeStruct((R, D), F32)
    aliases = {}
    if prev is not None:
        aliases = {len(args): 0}
        in_specs.append(pl.BlockSpec(memory_space=pl.ANY))
        args.append(prev)
    return pl.pallas_call(
        functools.partial(_combine_body, final=final, chained=prev is not None),
        grid=((rows[1] - rows[0]) // tl,),
        in_specs=in_specs,
        out_specs=out_spec,
        out_shape=out_shape,
        input_output_aliases=aliases,
        compiler_params=_cparams(("parallel",)),
        name="moe_combine",
    )(*args)


def _moe(hp, logits_t, toks, layer, w_gu, b_gu, w_down, b_down):
    dest, gates, tile_e, n_valid, n_used, n_tiles = _routing(logits_t[:, toks[0]:toks[1]])
    dest_flat = dest.reshape(-1)
    xs = _dispatch(hp, toks[0], dest_flat, n_tiles * MOE_TILE)
    ys = _experts(xs, tile_e, n_valid, n_used, layer, w_gu, b_gu, w_down, b_down)
    return _gather_rows(ys, dest_flat), gates.T


def _token_mixers(x, pos, mod, h0, p, consts, ctx, need_out):
    R = x.size // x.shape[-1]
    L = R // SUBLANES
    xa, ga, xb, xc, xd = _in_projection(x, pos, mod, p["norm1_g"], p["w_in"], p["b_in"], ctx)
    yf, yb, hfin = _rglru(xa, p["conv_a_w"], p["conv_a_b"], p["wg"], p["bg"], p["rg_lambda"], h0, L)
    if not need_out:
        return None, hfin
    yp = _pool_mixer(xb, p["w_pool"], p["b_pool"], p["pool_scale"], L)
    cl, sl = consts["dft"][L]
    yc = _fourier_mixer(xc.reshape(L, SUBLANES * W_GROUP), cl, sl, consts["cc"], consts["sc"], p["w_four"],
                        p["b_four"], L).reshape(R, W_GROUP)
    yd = _conformer(xd, p["conv_d_w"], p["conv_d_b"], p["ln_d_g"], p["ln_d_b"], consts["avg"], p["w_pw"],
                    p["b_pw"], L)
    return (yf, yb, ga, yp, yc, yd), hfin


def _pos_embed(n_tokens):
    rows_n = n_tokens // GRID_W
    q = D_MODEL // 4
    omega = 1.0 / (10000.0 ** (jnp.arange(q, dtype=F32) / q))

    def emb(n):
        ang = jnp.arange(n, dtype=F32)[:, None] * omega[None, :]
        return jnp.concatenate([jnp.sin(ang), jnp.cos(ang)], axis=-1)

    return jnp.concatenate([jnp.repeat(emb(rows_n), GRID_W, axis=0), jnp.tile(emb(GRID_W), (rows_n, 1))], axis=-1)


def _layer_params(l, w_in, b_in, conv_a_w, conv_a_b, w_rg_r, b_rg_r, w_rg_i, b_rg_i, rg_lambda, w_pool, b_pool,
                  pool_scale, w_four, b_four, conv_d_w, conv_d_b, ln_d_g, ln_d_b, w_pw, b_pw, norm1_g):
    wg = jnp.stack([jnp.concatenate([_block_diag(w_rg_r[l, d]), _block_diag(w_rg_i[l, d])], axis=1)
                    for d in range(2)]).astype(BF16)
    bg = jnp.concatenate([b_rg_r[l].reshape(2, 1, W_GROUP), b_rg_i[l].reshape(2, 1, W_GROUP)], axis=-1)
    return dict(
        norm1_g=norm1_g[l], w_in=w_in[l].astype(BF16), b_in=b_in[l],
        conv_a_w=conv_a_w[l], conv_a_b=conv_a_b[l], wg=wg, bg=bg, rg_lambda=rg_lambda[l],
        w_pool=_block_diag(w_pool[l]).astype(BF16), b_pool=b_pool[l], pool_scale=pool_scale[l],
        w_four=_block_diag(w_four[l]).astype(BF16), b_four=b_four[l],
        conv_d_w=conv_d_w[l], conv_d_b=conv_d_b[l], ln_d_g=ln_d_g[l], ln_d_b=ln_d_b[l],
        w_pw=w_pw[l].astype(BF16), b_pw=b_pw[l])


def kernel(x, c, ctx, c_ctx, w_mod, b_mod, norm1_g, norm2_g, w_in, b_in, conv_a_w, conv_a_b, w_rg_r, b_rg_r,
           w_rg_i, b_rg_i, rg_lambda, w_pool, b_pool, pool_scale, w_four, b_four, conv_d_w, conv_d_b, ln_d_g,
           ln_d_b, w_pw, b_pw, w_out, b_out, w_router, b_router, w_gu, b_gu, w_down, b_down, final_norm_g):
    bn, L, D = x.shape
    Lc = ctx.shape[1]
    assert bn == SUBLANES and D == D_MODEL

    pos = _pos_embed(L)
    c_rows = jnp.concatenate([c, jnp.broadcast_to(c_ctx[None], (MOD_ROWS - bn, D))], axis=0)
    mod = _modulation(c_rows, w_mod, b_mod)
    ctx = jnp.transpose(ctx, (1, 0, 2)).reshape(Lc * bn, D)

    cc1, sc1 = _dft_matrices(D_SUB)
    eye = jnp.eye(N_SUB, dtype=F32)
    consts = dict(
        dft={n: tuple(m.astype(BF16) for m in _dft_matrices(n)) for n in sorted({L, Lc})},
        cc=jnp.kron(eye, cc1).astype(BF16), sc=jnp.kron(eye, sc1).astype(BF16),
        avg=jnp.kron(eye, jnp.full((D_SUB, D_SUB), 1.0 / D_SUB, F32)).astype(BF16))
    h_zero = jnp.zeros((2, SUBLANES, W_GROUP), F32)

    for l in range(DEPTH):
        last = l == DEPTH - 1
        p = _layer_params(l, w_in, b_in, conv_a_w, conv_a_b, w_rg_r, b_rg_r, w_rg_i, b_rg_i, rg_lambda, w_pool,
                          b_pool, pool_scale, w_four, b_four, conv_d_w, conv_d_b, ln_d_g, ln_d_b, w_pw, b_pw,
                          norm1_g)
        mod3 = mod[l]
        w_out_bf = w_out[l].astype(BF16)
        wr_t = w_router[l].T
        wr_hi = wr_t.astype(BF16)
        wr2 = jnp.concatenate([wr_hi, (wr_t - wr_hi.astype(F32)).astype(BF16)], axis=0)
        x_pos = pos if l == 0 else None

        mix_c, h_ctx = _token_mixers(ctx, None, mod3, h_zero, p, consts, True, not last)
        mix_x, _ = _token_mixers(x, x_pos, mod3, h_ctx, p, consts, False, True)
        x, hp_x, lg_x = _out_projection(x, x_pos, mix_x, mod3, norm2_g[l], w_out_bf, b_out[l], wr2,
                                        b_router[l], False)
        if not last:
            ctx, hp_c, lg_c = _out_projection(ctx, None, mix_c, mod3, norm2_g[l], w_out_bf, b_out[l], wr2,
                                              b_router[l], True)
            hp = jnp.concatenate([hp_c, hp_x], axis=0)
            lg = jnp.concatenate([lg_c, lg_x], axis=1)
            n_ctx, final_g = bn * Lc, None
        else:
            hp, lg, n_ctx, final_g = hp_x, lg_x, 0, final_norm_g
        T = hp.shape[0]
        half = T // 2
        moe_a = _moe(hp, lg, (0, half), l, w_gu, b_gu, w_down, b_down)
        moe_b = _moe(hp, lg, (half, T), l, w_gu, b_gu, w_down, b_down)
        if n_ctx:
            ctx = _combine(ctx, (0, n_ctx), moe_a, 0, mod3, None, True)
        split = half - n_ctx
        xa = _combine(x, (0, split), moe_a, n_ctx, mod3, final_g, False)
        x = _combine(x, (split, x.shape[0]), moe_b, 0, mod3, final_g, False, prev=xa)
    return x
```

```python
import functools
import math

import jax
import jax.numpy as jnp
from jax import lax
from jax.experimental import pallas as pl
from jax.experimental.pallas import tpu as pltpu

F32 = jnp.float32
BF16 = jnp.bfloat16

D_MODEL = 1024
DEPTH = 2
GRID_W = 64
W_GROUP = 256
N_SUB = 4
D_SUB = 64
D_IN = 6 * W_GROUP
RG_CONV = 4
RG_C = 8.0
CONF_KERNEL = 31
N_EXPERTS = 32
TOP_K = 4
D_FF = D_MODEL
SWIGLU_LIMIT = 7.0
SWIGLU_ALPHA = 1.702
EPS = 1e-6

SUBLANES = 8
VMEM_LIMIT_BYTES = 56 * 1024 * 1024
MOD_ROWS = 16
RG_HALO = 8 * SUBLANES
POOL_HALO = 8 * SUBLANES
CONF_HALO = 16 * SUBLANES
MOE_TILE = 512
TOK_TILE = 256
SC_WINDOW = 64
ROUTE_TILE = 512


def _cparams(sem):
    return pltpu.CompilerParams(dimension_semantics=sem, vmem_limit_bytes=VMEM_LIMIT_BYTES)


def _full(shape):
    nd = len(shape)
    return pl.BlockSpec(shape, lambda *_: (0,) * nd)


def _dot(a, b):
    return jnp.dot(a, b, preferred_element_type=F32)


def _split_bf16(v):
    hi = v.astype(BF16)
    lo = (v - hi.astype(F32)).astype(BF16)
    return hi, lo


def _mod_body(c_ref, w_ref, b_ref, o_ref):
    c = c_ref[...]
    s = c * jax.nn.sigmoid(c)
    o_ref[...] = jnp.dot(s, w_ref[...], precision=lax.Precision.HIGHEST,
                         preferred_element_type=F32) + b_ref[...]


def _modulation(c_rows, w_mod, b_mod):
    tn = 1536
    n6 = 6 * D_MODEL
    return pl.pallas_call(
        _mod_body,
        grid=(DEPTH, n6 // tn),
        in_specs=[_full((MOD_ROWS, D_MODEL)),
                  pl.BlockSpec((None, D_MODEL, tn), lambda l, j: (l, 0, j)),
                  pl.BlockSpec((None, 1, tn), lambda l, j: (l, 0, j))],
        out_specs=pl.BlockSpec((None, MOD_ROWS, tn), lambda l, j: (l, 0, j)),
        out_shape=jax.ShapeDtypeStruct((DEPTH, MOD_ROWS, n6), F32),
        compiler_params=_cparams(("parallel", "parallel")),
        name="modulation",
    )(c_rows, w_mod, b_mod.reshape(DEPTH, 1, n6))


def _mod_spec(chunk, ctx):
    return pl.BlockSpec((SUBLANES, D_MODEL), lambda i: (1 if ctx else 0, chunk))


def _scale_rows(v, m):
    r, d = v.shape
    return (v.reshape(r // SUBLANES, SUBLANES, d) * m[None]).reshape(r, d)


def _rms_mod(x, g, shift, scale):
    r, d = x.shape
    y = x * lax.rsqrt(jnp.mean(x * x, axis=-1, keepdims=True) + EPS) * g
    y3 = y.reshape(r // SUBLANES, SUBLANES, d)
    return (y3 * (1.0 + scale)[None] + shift[None]).reshape(r, d)


def _load_plus_pos(x_ref, pos_ref):
    return jnp.concatenate([x_ref[:, t, :] + pos_ref[t:t + 1, :] for t in range(pos_ref.shape[0])], axis=0)


def _x_spec(x, tr):
    if x.ndim == 3:
        return pl.BlockSpec((SUBLANES, tr // SUBLANES, x.shape[2]), lambda i: (0, i, 0))
    return pl.BlockSpec((tr, x.shape[1]), lambda i: (i, 0))


def _inproj_body(*refs, add_pos):
    if add_pos:
        x_ref, pos_ref, sh_ref, sc_ref, g_ref, w_ref, b_ref, xa_ref, ga_ref, xb_ref, xc_ref, xd_ref = refs
        x = _load_plus_pos(x_ref, pos_ref)
    else:
        x_ref, sh_ref, sc_ref, g_ref, w_ref, b_ref, xa_ref, ga_ref, xb_ref, xc_ref, xd_ref = refs
        x = x_ref[...]
    u = _rms_mod(x, g_ref[...], sh_ref[...], sc_ref[...])
    p = _dot(u.astype(BF16), w_ref[...]) + b_ref[...]
    xa_ref[...] = p[:, 0:256]
    ga_ref[...] = p[:, 256:512]
    xb_ref[...] = p[:, 512:768]
    xc_ref[...] = p[:, 768:1024].astype(BF16)
    xd_ref[...] = p[:, 1024:1536]


def _in_projection(x, pos, mod, norm_g, w_in_bf, b_in, ctx):
    add_pos = pos is not None
    assert add_pos == (x.ndim == 3)
    R, D = x.size // x.shape[-1], x.shape[-1]
    tr = min(R, 512)
    row = lambda w: pl.BlockSpec((tr, w), lambda i: (i, 0))
    in_specs = [_x_spec(x, tr)]
    args = [x]
    if add_pos:
        in_specs.append(pl.BlockSpec((tr // SUBLANES, D), lambda i: (i, 0)))
        args.append(pos)
    in_specs += [_mod_spec(0, ctx), _mod_spec(1, ctx), _full((1, D)), _full((D, D_IN)), _full((1, D_IN))]
    args += [mod, mod, norm_g.reshape(1, D), w_in_bf, b_in.reshape(1, D_IN)]
    out_shape = [jax.ShapeDtypeStruct((R, 256), F32)] * 3 + [
        jax.ShapeDtypeStruct((R, 256), BF16), jax.ShapeDtypeStruct((R, 512), F32)]
    return pl.pallas_call(
        functools.partial(_inproj_body, add_pos=add_pos),
        grid=(R // tr,),
        in_specs=in_specs,
        out_specs=[row(256), row(256), row(256), row(256), row(512)],
        out_shape=out_shape,
        compiler_params=_cparams(("parallel",)),
        name="in_projection",
    )(*args)


def _rg_gates(xc, wg, bg, lam):
    g = _dot(xc.astype(BF16), wg) + bg
    r = jax.nn.sigmoid(g[:, :W_GROUP])
    gi = jax.nn.sigmoid(g[:, W_GROUP:])
    z = -lam
    softplus = jnp.maximum(z, 0.0) + jnp.log1p(jnp.exp(-jnp.abs(z)))
    log_a = (-RG_C) * r * softplus
    a = jnp.exp(log_a)
    b = jnp.sqrt(-jnp.tanh(log_a) * (a * a + 1.0)) * (gi * xc)
    return a, b


def _rg_body(xf_ref, xfh_ref, xr_ref, xrh_ref, cw_ref, cb_ref, wg_ref, bg_ref, lam_ref, h0_ref,
             yf_ref, yb_ref, hfin_ref, af_ref, ab_ref, hc_ref, *, n, tt):
    i = pl.program_id(0)
    tr = tt * SUBLANES
    keep = RG_HALO - (RG_CONV - 1) * SUBLANES

    @pl.when(i == 0)
    def _():
        hc_ref[...] = h0_ref[...]

    halo = jnp.where(i > 0, xfh_ref[...], 0.0)
    ext = jnp.concatenate([halo[keep:], xf_ref[...]], axis=0)
    xc = cb_ref[0]
    for k in range(RG_CONV):
        xc = xc + cw_ref[0, k:k + 1, :] * ext[k * SUBLANES:k * SUBLANES + tr]
    a, b = _rg_gates(xc, wg_ref[0], bg_ref[0], lam_ref[0])
    af_ref[...] = a
    yf_ref[...] = b

    halo = jnp.where(i > 0, xrh_ref[...], 0.0)
    ext = jnp.concatenate([xr_ref[...], halo[:(RG_CONV - 1) * SUBLANES]], axis=0)
    xc = cb_ref[1]
    for k in range(RG_CONV):
        o = (RG_CONV - 1 - k) * SUBLANES
        xc = xc + cw_ref[1, k:k + 1, :] * ext[o:o + tr]
    a, b = _rg_gates(xc, wg_ref[1], bg_ref[1], lam_ref[1])
    ab_ref[...] = a
    yb_ref[...] = b

    def step(t, carry):
        hf, hb = carry
        rf = pl.multiple_of(t * SUBLANES, SUBLANES)
        hf = af_ref[pl.ds(rf, SUBLANES), :] * hf + yf_ref[pl.ds(rf, SUBLANES), :]
        yf_ref[pl.ds(rf, SUBLANES), :] = hf
        rb = pl.multiple_of((tt - 1 - t) * SUBLANES, SUBLANES)
        hb = ab_ref[pl.ds(rb, SUBLANES), :] * hb + yb_ref[pl.ds(rb, SUBLANES), :]
        yb_ref[pl.ds(rb, SUBLANES), :] = hb
        return hf, hb

    hf, hb = lax.fori_loop(0, tt, step, (hc_ref[0], hc_ref[1]), unroll=8)
    hc_ref[0] = hf
    hc_ref[1] = hb
    hfin_ref[0] = hf
    hfin_ref[1] = hb


def _rglru(xa2, conv_w, conv_b, wg_bf, bg, lam, h0, L):
    tt = min(L, 256)
    n = L // tt
    tr = tt * SUBLANES
    per = tr // RG_HALO
    last_halo = L * SUBLANES // RG_HALO - 1
    row = lambda i: (i, 0)
    rev = lambda i: (n - 1 - i, 0)
    in_specs = [
        pl.BlockSpec((tr, W_GROUP), row),
        pl.BlockSpec((RG_HALO, W_GROUP), lambda i: (jnp.maximum(i * per - 1, 0), 0)),
        pl.BlockSpec((tr, W_GROUP), rev),
        pl.BlockSpec((RG_HALO, W_GROUP), lambda i: (jnp.minimum((n - i) * per, last_halo), 0)),
        _full((2, RG_CONV, W_GROUP)), _full((2, 1, W_GROUP)), _full((2, W_GROUP, 2 * W_GROUP)),
        _full((2, 1, 2 * W_GROUP)), _full((2, 1, W_GROUP)), _full((2, SUBLANES, W_GROUP)),
    ]
    return pl.pallas_call(
        functools.partial(_rg_body, n=n, tt=tt),
        grid=(n,),
        in_specs=in_specs,
        out_specs=[pl.BlockSpec((tr, W_GROUP), row), pl.BlockSpec((tr, W_GROUP), rev),
                   _full((2, SUBLANES, W_GROUP))],
        out_shape=[jax.ShapeDtypeStruct((L * SUBLANES, W_GROUP), F32)] * 2
        + [jax.ShapeDtypeStruct((2, SUBLANES, W_GROUP), F32)],
        scratch_shapes=[pltpu.VMEM((tr, W_GROUP), F32), pltpu.VMEM((tr, W_GROUP), F32),
                        pltpu.VMEM((2, SUBLANES, W_GROUP), F32)],
        compiler_params=_cparams(("arbitrary",)),
        name="rglru",
    )(xa2, xa2, xa2, xa2, conv_w, conv_b.reshape(2, 1, W_GROUP), wg_bf, bg, lam.reshape(2, 1, W_GROUP), h0)


def _pool_body(xm_ref, xp_ref, xn_ref, w_ref, b_ref, s_ref, o_ref, *, n, tt, L):
    i = pl.program_id(0)
    tr = tt * SUBLANES
    S = SUBLANES
    xm = xm_ref[...]
    prev = jnp.where(i > 0, xp_ref[...], 0.0)
    nxt = jnp.where(i < n - 1, xn_ref[...], 0.0)
    xe = jnp.concatenate([prev, xm, nxt], axis=0)
    e = xe.shape[0]
    p2 = xe[S:e] + xe[0:e - S]
    n4 = (tt + 13) * S
    p4 = p2[0:n4] + p2[2 * S:2 * S + n4]
    n8 = (tt + 9) * S
    p8 = p4[0:n8] + p4[4 * S:4 * S + n8]
    s16 = p8[0:tr] + p8[8 * S:8 * S + tr]
    s2 = p2[7 * S:7 * S + tr]
    s4 = p4[6 * S:6 * S + tr]
    s8 = p8[4 * S:4 * S + tr]
    grp = lax.broadcasted_iota(jnp.int32, (1, W_GROUP), 1) // D_SUB
    half = jnp.left_shift(1, grp)
    t = i * tt + lax.broadcasted_iota(jnp.int32, (tr, 1), 0) // S
    cnt = (jnp.minimum(t + half, L) - jnp.maximum(t - half, 0)).astype(F32)
    s = jnp.where(grp == 0, s2, jnp.where(grp == 1, s4, jnp.where(grp == 2, s8, s16)))
    pooled = s / cnt - xm
    y = _dot(pooled.astype(BF16), w_ref[...]) + b_ref[...]
    o_ref[...] = (y * s_ref[...]).astype(BF16)


def _pool_mixer(xb2, w_bd_bf, b, scale, L):
    tt = min(L, 256)
    n = L // tt
    tr = tt * SUBLANES
    per = tr // POOL_HALO
    last_halo = L * SUBLANES // POOL_HALO - 1
    return pl.pallas_call(
        functools.partial(_pool_body, n=n, tt=tt, L=L),
        grid=(n,),
        in_specs=[pl.BlockSpec((tr, W_GROUP), lambda i: (i, 0)),
                  pl.BlockSpec((POOL_HALO, W_GROUP), lambda i: (jnp.maximum(i * per - 1, 0), 0)),
                  pl.BlockSpec((POOL_HALO, W_GROUP), lambda i: (jnp.minimum((i + 1) * per, last_halo), 0)),
                  _full((W_GROUP, W_GROUP)), _full((1, W_GROUP)), _full((1, W_GROUP))],
        out_specs=pl.BlockSpec((tr, W_GROUP), lambda i: (i, 0)),
        out_shape=jax.ShapeDtypeStruct((L * SUBLANES, W_GROUP), BF16),
        compiler_params=_cparams(("parallel",)),
        name="pool_mixer",
    )(xb2, xb2, xb2, w_bd_bf, b.reshape(1, W_GROUP), scale.reshape(1, W_GROUP))


def _fourier_body(c_ref, s_ref, x_ref, cc_ref, sc_ref, w_ref, b_ref, o_ref):
    x = x_ref[...]
    z1 = _dot(c_ref[...], x)
    z2 = _dot(s_ref[...], x)
    for j in range(x.shape[1] // W_GROUP):
        sl = slice(j * W_GROUP, (j + 1) * W_GROUP)
        a_hi, a_lo = _split_bf16(z1[:, sl])
        b_hi, b_lo = _split_bf16(z2[:, sl])
        f = (_dot(a_hi, cc_ref[...]) + _dot(a_lo, cc_ref[...])) - (_dot(b_hi, sc_ref[...]) + _dot(b_lo, sc_ref[...]))
        o_ref[:, sl] = (_dot(f.astype(BF16), w_ref[...]) + b_ref[...]).astype(BF16)


def _fourier_mixer(xc_tm, cl, sl, cc, sc, w_bd_bf, b, L):
    ncol = xc_tm.shape[1]
    nb = min(ncol, 1024)
    tk = min(L, 256)
    return pl.pallas_call(
        _fourier_body,
        grid=(ncol // nb, L // tk),
        in_specs=[pl.BlockSpec((tk, L), lambda j, k: (k, 0)),
                  pl.BlockSpec((tk, L), lambda j, k: (k, 0)),
                  pl.BlockSpec((L, nb), lambda j, k: (0, j)),
                  _full((W_GROUP, W_GROUP)), _full((W_GROUP, W_GROUP)), _full((W_GROUP, W_GROUP)),
                  _full((1, W_GROUP))],
        out_specs=pl.BlockSpec((tk, nb), lambda j, k: (k, j)),
        out_shape=jax.ShapeDtypeStruct((L, ncol), BF16),
        compiler_params=_cparams(("parallel", "parallel")),
        name="fourier_mixer",
    )(cl, sl, xc_tm, cc, sc, w_bd_bf, b.reshape(1, W_GROUP))


def _dft_matrices(L):
    f = 1 << (max(L.bit_length() - 1, 0) // 2)
    n = jnp.arange(L, dtype=jnp.int32)[None, :]

    def table(rows):
        ang = ((rows[:, None] * n) % L).astype(F32) * (2.0 * math.pi / L)
        return jnp.cos(ang), jnp.sin(ang)

    ac, as_ = table(jnp.arange(L // f, dtype=jnp.int32) * f)
    bc, bs = table(jnp.arange(f, dtype=jnp.int32))
    scale = 1.0 / math.sqrt(L)
    cos = (ac[:, None, :] * bc[None, :, :] - as_[:, None, :] * bs[None, :, :]).reshape(L, L) * scale
    sin = (as_[:, None, :] * bc[None, :, :] + ac[:, None, :] * bs[None, :, :]).reshape(L, L) * scale
    return cos, sin


def _block_diag(w):
    g, a, b = w.shape
    eye = jnp.eye(g, dtype=w.dtype)
    return (eye[:, None, :, None] * w[:, :, None, :]).reshape(g * a, g * b)


CONF_CHUNK = 64


def _conformer_body(xm_ref, xp_ref, xn_ref, cw_ref, cb_ref, lg_ref, lb_ref, avg_ref, w_ref, b_ref,
                    o_ref, v_ref, c_ref, *, n, tt):
    i = pl.program_id(0)
    tr = tt * SUBLANES
    H = CONF_HALO

    def glu(v):
        return v[:, :W_GROUP] * jax.nn.sigmoid(v[:, W_GROUP:])

    v_ref[0:H] = jnp.where(i > 0, glu(xp_ref[...]), 0.0)
    v_ref[H:H + tr] = glu(xm_ref[...])
    v_ref[H + tr:H + tr + H] = jnp.where(i < n - 1, glu(xn_ref[...]), 0.0)

    def chunk(c, carry):
        r0 = pl.multiple_of(c * CONF_CHUNK, CONF_CHUNK)
        acc = jnp.broadcast_to(cb_ref[...], (CONF_CHUNK, W_GROUP))
        for k in range(CONF_KERNEL):
            acc = acc + cw_ref[k:k + 1, :] * v_ref[pl.ds(r0 + (k + 1) * SUBLANES, CONF_CHUNK), :]
        c_ref[pl.ds(r0, CONF_CHUNK), :] = acc
        return carry

    lax.fori_loop(0, tr // CONF_CHUNK, chunk, 0)

    v = c_ref[...]
    avg = avg_ref[...]
    v_hi, v_lo = _split_bf16(v)
    mu = _dot(v_hi, avg) + _dot(v_lo, avg)
    d = v - mu
    q_hi, q_lo = _split_bf16(d * d)
    var = _dot(q_hi, avg) + _dot(q_lo, avg)
    vn = d * lax.rsqrt(var + EPS) * lg_ref[...] + lb_ref[...]
    act = vn * jax.nn.sigmoid(vn)
    o_ref[...] = (_dot(act.astype(BF16), w_ref[...]) + b_ref[...]).astype(BF16)


def _conformer(xd2, conv_w, conv_b, ln_g, ln_b, avg_bf, w_pw_bf, b_pw, L):
    tt = min(L, 256)
    n = L // tt
    tr = tt * SUBLANES
    per = tr // CONF_HALO
    last_halo = L * SUBLANES // CONF_HALO - 1
    vec = lambda a: a.reshape(1, W_GROUP)
    return pl.pallas_call(
        functools.partial(_conformer_body, n=n, tt=tt),
        grid=(n,),
        in_specs=[pl.BlockSpec((tr, 2 * W_GROUP), lambda i: (i, 0)),
                  pl.BlockSpec((CONF_HALO, 2 * W_GROUP), lambda i: (jnp.maximum(i * per - 1, 0), 0)),
                  pl.BlockSpec((CONF_HALO, 2 * W_GROUP), lambda i: (jnp.minimum((i + 1) * per, last_halo), 0)),
                  _full((CONF_KERNEL, W_GROUP)), _full((1, W_GROUP)), _full((1, W_GROUP)), _full((1, W_GROUP)),
                  _full((W_GROUP, W_GROUP)), _full((W_GROUP, W_GROUP)), _full((1, W_GROUP))],
        out_specs=pl.BlockSpec((tr, W_GROUP), lambda i: (i, 0)),
        out_shape=jax.ShapeDtypeStruct((L * SUBLANES, W_GROUP), BF16),
        scratch_shapes=[pltpu.VMEM((tr + 2 * CONF_HALO, W_GROUP), F32), pltpu.VMEM((tr, W_GROUP), F32)],
        compiler_params=_cparams(("parallel",)),
        name="conformer",
    )(xd2, xd2, xd2, conv_w, vec(conv_b), vec(ln_g), vec(ln_b), avg_bf, w_pw_bf, vec(b_pw))


def _gelu_tanh(x):
    return 0.5 * x * (1.0 + jnp.tanh(math.sqrt(2.0 / math.pi) * (x + 0.044715 * (x * x * x))))


def _pack_bf16_pairs(h_bf):
    u = pltpu.bitcast(h_bf.astype(F32), jnp.uint32)
    half = h_bf.shape[1] // 2
    return (u[:, :half] & jnp.uint32(0xFFFF0000)) | (u[:, half:] >> 16)


def _unpack_bf16_pairs(p):
    hi = pltpu.bitcast(p & jnp.uint32(0xFFFF0000), F32).astype(BF16)
    lo = pltpu.bitcast(p << 16, F32).astype(BF16)
    return hi, lo


def _outproj_body(*refs, add_pos):
    if add_pos:
        (x_ref, pos_ref, yf_ref, yb_ref, ga_ref, yp_ref, yc_ref, yd_ref, g1_ref, sh_ref, sc_ref, ng_ref,
         wo_ref, bo_ref, wr_ref, br_ref, xo_ref, hp_ref, lg_ref) = refs
        x = _load_plus_pos(x_ref, pos_ref)
    else:
        (x_ref, yf_ref, yb_ref, ga_ref, yp_ref, yc_ref, yd_ref, g1_ref, sh_ref, sc_ref, ng_ref,
         wo_ref, bo_ref, wr_ref, br_ref, xo_ref, hp_ref, lg_ref) = refs
        x = x_ref[...]
    ya = (yf_ref[...] + yb_ref[...]) * _gelu_tanh(ga_ref[...])
    ycat = jnp.concatenate([ya.astype(BF16), yp_ref[...], yc_ref[...], yd_ref[...]], axis=1)
    y = _dot(ycat, wo_ref[...]) + bo_ref[...]
    xn = x + _scale_rows(y, g1_ref[...])
    xo_ref[...] = xn
    h = _rms_mod(xn, ng_ref[...], sh_ref[...], sc_ref[...])
    h_hi, h_lo = _split_bf16(h)
    nt = (((1,), (1,)), ((), ()))
    wr = wr_ref[...]
    acc = lax.dot_general(wr, h_hi, nt, preferred_element_type=F32)
    acc = acc + lax.dot_general(wr, h_lo, nt, preferred_element_type=F32)
    lg_ref[...] = acc[:N_EXPERTS] + acc[N_EXPERTS:] + br_ref[...]
    hp_ref[...] = _pack_bf16_pairs(h_hi)


def _out_projection(x, pos, mix, mod, norm_g, w_out_bf, b_out, wr2_bf, b_router, ctx):
    add_pos = pos is not None
    assert add_pos == (x.ndim == 3)
    R, D = x.size // x.shape[-1], x.shape[-1]
    tr = min(R, 512)
    row = lambda w: pl.BlockSpec((tr, w), lambda i: (i, 0))
    in_specs = [_x_spec(x, tr)]
    args = [x]
    if add_pos:
        in_specs.append(pl.BlockSpec((tr // SUBLANES, D), lambda i: (i, 0)))
        args.append(pos)
    in_specs += [row(W_GROUP)] * 6
    args += list(mix)
    in_specs += [_mod_spec(2, ctx), _mod_spec(3, ctx), _mod_spec(4, ctx), _full((1, D)), _full((D, D)),
                 _full((1, D)), _full((2 * N_EXPERTS, D)), _full((N_EXPERTS, 1))]
    args += [mod, mod, mod, norm_g.reshape(1, D), w_out_bf, b_out.reshape(1, D), wr2_bf,
             b_router.reshape(N_EXPERTS, 1)]
    return pl.pallas_call(
        functools.partial(_outproj_body, add_pos=add_pos),
        grid=(R // tr,),
        in_specs=in_specs,
        out_specs=[row(D), row(D // 2), pl.BlockSpec((N_EXPERTS, tr), lambda i: (0, i))],
        out_shape=[jax.ShapeDtypeStruct((R, D), F32),
                   jax.ShapeDtypeStruct((R, D // 2), jnp.uint32),
                   jax.ShapeDtypeStruct((N_EXPERTS, R), F32)],
        compiler_params=_cparams(("parallel",)),
        name="out_projection",
    )(*args)


def _top4(v):
    eid = lax.broadcasted_iota(jnp.int32, v.shape, 0)
    out = []
    work = v
    for _ in range(TOP_K):
        m = jnp.max(work, axis=0, keepdims=True)
        idx = jnp.min(jnp.where(work == m, eid, N_EXPERTS), axis=0, keepdims=True)
        oh = eid == idx
        out.append((m, oh))
        work = jnp.where(oh, -jnp.inf, work)
    return out


def _count_body(lg_ref, cnt_ref):
    @pl.when(pl.program_id(0) == 0)
    def _():
        cnt_ref[...] = jnp.zeros_like(cnt_ref)

    sel = jnp.zeros(lg_ref.shape, F32)
    for _, oh in _top4(lg_ref[...]):
        sel = sel + oh.astype(F32)
    cnt_ref[...] += jnp.sum(sel, axis=1, keepdims=True)


def _route_body(lg_ref, ps_ref, tri_ref, dest_ref, gate_ref, carry_ref):
    @pl.when(pl.program_id(0) == 0)
    def _():
        carry_ref[...] = jnp.zeros_like(carry_ref)

    top = _top4(lg_ref[...])
    sel = jnp.zeros(lg_ref.shape, F32)
    for _, oh in top:
        sel = sel + oh.astype(F32)
    before = _dot(sel.astype(BF16), tri_ref[...]) + carry_ref[...] + ps_ref[...]
    m0 = top[0][0]
    es = [jnp.exp(m - m0) for m, _ in top]
    den = es[0] + es[1] + es[2] + es[3]
    for k, (_, oh) in enumerate(top):
        dest_ref[k:k + 1, :] = jnp.sum(jnp.where(oh, before, 0.0), axis=0, keepdims=True).astype(jnp.int32)
        gate_ref[k:k + 1, :] = es[k] / den
    carry_ref[...] += jnp.sum(sel, axis=1, keepdims=True)


def _routing(logits_t):
    E, T = logits_t.shape
    tt = ROUTE_TILE
    nt = T // tt
    counts = pl.pallas_call(
        _count_body,
        grid=(nt,),
        in_specs=[pl.BlockSpec((E, tt), lambda i: (0, i))],
        out_specs=_full((E, 1)),
        out_shape=jax.ShapeDtypeStruct((E, 1), F32),
        compiler_params=_cparams(("arbitrary",)),
        name="route_count",
    )(logits_t)
    cnt = counts[:, 0].astype(jnp.int32)
    padded = ((cnt + MOE_TILE - 1) // MOE_TILE) * MOE_TILE
    pend = jnp.cumsum(padded)
    pstart = pend - padded
    n_tiles = -(-(T * TOP_K) // MOE_TILE) + N_EXPERTS
    tile_start = jnp.arange(n_tiles, dtype=jnp.int32) * MOE_TILE
    tile_e = jnp.minimum(jnp.sum((pend[None, :] <= tile_start[:, None]).astype(jnp.int32), axis=1), N_EXPERTS - 1)
    n_used = (pend[-1] // MOE_TILE).astype(jnp.int32).reshape(1)
    n_valid = jnp.clip((pstart + cnt)[tile_e] - tile_start, 0, MOE_TILE).astype(jnp.int32)
    tri = (jnp.arange(tt)[:, None] < jnp.arange(tt)[None, :]).astype(BF16)
    dest, gates = pl.pallas_call(
        _route_body,
        grid=(nt,),
        in_specs=[pl.BlockSpec((E, tt), lambda i: (0, i)), _full((E, 1)), _full((tt, tt))],
        out_specs=[pl.BlockSpec((TOP_K, tt), lambda i: (0, i)), pl.BlockSpec((TOP_K, tt), lambda i: (0, i))],
        out_shape=[jax.ShapeDtypeStruct((TOP_K, T), jnp.int32), jax.ShapeDtypeStruct((TOP_K, T), F32)],
        scratch_shapes=[pltpu.VMEM((E, 1), F32)],
        compiler_params=_cparams(("arbitrary",)),
        name="route_assign",
    )(logits_t, pstart.astype(F32).reshape(E, 1), tri)
    return dest, gates, tile_e, n_valid, n_used, n_tiles


def _sc_workers():
    from jax.experimental.pallas import tpu_sc as plsc
    mesh = plsc.VectorSubcoreMesh(core_axis_name="c", subcore_axis_name="s")
    n_workers = mesh.num_cores * mesh.num_subcores
    worker = lambda: lax.axis_index("s") * mesh.num_cores + lax.axis_index("c")
    return mesh, n_workers, worker


def _dispatch(hp, tok0, dest_flat, n_rows):
    W = hp.shape[1]
    T = dest_flat.shape[0] // TOP_K
    mesh, n_workers, worker = _sc_workers()
    per = dest_flat.shape[0] // n_workers
    steps = per // SC_WINDOW
    assert per * n_workers == dest_flat.shape[0] and steps * SC_WINDOW == per and steps % 2 == 0
    assert T % SC_WINDOW == 0 and tok0 % SC_WINDOW == 0

    @functools.partial(
        pl.kernel, mesh=mesh, out_type=jax.ShapeDtypeStruct((n_rows, W), hp.dtype),
        scratch_types=[pltpu.VMEM((SC_WINDOW,), jnp.int32), pltpu.VMEM((SC_WINDOW,), jnp.int32),
                       pltpu.VMEM((SC_WINDOW, W), hp.dtype), pltpu.VMEM((SC_WINDOW, W), hp.dtype),
                       pltpu.SemaphoreType.DMA, pltpu.SemaphoreType.DMA],
        name="moe_dispatch")
    def scatter(hp_hbm, dest_hbm, xs_hbm, idx0, idx1, rows0, rows1, sem0, sem1):
        base = worker() * per
        bufs = ((idx0, rows0, sem0), (idx1, rows1, sem1))

        def window(j, b, first):
            idx_v, rows_v, sem = bufs[b]

            @pl.when(jnp.logical_not(first))
            def _():
                pltpu.make_async_copy(rows_v, xs_hbm.at[idx_v], sem).wait()

            off = pl.multiple_of(base + j * SC_WINDOW, SC_WINDOW)
            tok = pl.multiple_of(tok0 + lax.rem(off, T), SC_WINDOW)
            pltpu.sync_copy(dest_hbm.at[pl.ds(off, SC_WINDOW)], idx_v)
            pltpu.sync_copy(hp_hbm.at[pl.ds(tok, SC_WINDOW)], rows_v)
            pltpu.async_copy(rows_v, xs_hbm.at[idx_v], sem)

        @pl.loop(0, steps, step=2)
        def _(j):
            window(j, 0, j == 0)
            window(j + 1, 1, j == 0)

        for idx_v, rows_v, sem in bufs:
            pltpu.make_async_copy(rows_v, xs_hbm.at[idx_v], sem).wait()

    return scatter(hp, dest_flat)


def _gather_rows(table, idx_flat):
    n = idx_flat.shape[0]
    W = table.shape[1]
    mesh, n_workers, worker = _sc_workers()
    per = n // n_workers
    steps = per // SC_WINDOW
    assert per * n_workers == n and steps * SC_WINDOW == per and steps % 2 == 0

    @functools.partial(
        pl.kernel, mesh=mesh, out_type=jax.ShapeDtypeStruct((n, W), table.dtype),
        scratch_types=[pltpu.VMEM((SC_WINDOW,), jnp.int32), pltpu.VMEM((SC_WINDOW,), jnp.int32),
                       pltpu.VMEM((SC_WINDOW, W), table.dtype), pltpu.VMEM((SC_WINDOW, W), table.dtype),
                       pltpu.SemaphoreType.DMA, pltpu.SemaphoreType.DMA, pltpu.SemaphoreType.DMA],
        name="moe_gather")
    def gather(table_hbm, idx_hbm, out_hbm, idx0, idx1, rows0, rows1, sem0, sem1, gsem):
        base = worker() * per
        bufs = ((idx0, rows0, sem0), (idx1, rows1, sem1))

        def window(j, b, first):
            idx_v, rows_v, sem = bufs[b]
            off = pl.multiple_of(base + j * SC_WINDOW, SC_WINDOW)

            @pl.when(jnp.logical_not(first))
            def _():
                pltpu.make_async_copy(rows_v, out_hbm.at[pl.ds(off, SC_WINDOW)], sem).wait()

            pltpu.sync_copy(idx_hbm.at[pl.ds(off, SC_WINDOW)], idx_v)
            pltpu.async_copy(table_hbm.at[idx_v], rows_v, gsem).wait()
            pltpu.async_copy(rows_v, out_hbm.at[pl.ds(off, SC_WINDOW)], sem)

        @pl.loop(0, steps, step=2)
        def _(j):
            window(j, 0, j == 0)
            window(j + 1, 1, j == 0)

        for _, rows_v, sem in bufs:
            pltpu.make_async_copy(rows_v, out_hbm.at[pl.ds(base, SC_WINDOW)], sem).wait()

    return gather(table, idx_flat)


def _expert_body(te_ref, nv_ref, nu_ref, nx_ref, sl_ref, xs_ref, wgu_hbm, bgu_ref, wd_hbm, bd_ref, ys_ref,
                 wgu_f32, wd_f32, wgu_bf_ref, wd_bf_ref, sems, *, layer):
    i = pl.program_id(0)
    used = i < nu_ref[0]
    e = te_ref[i]
    s = sl_ref[i]
    new_expert = jnp.logical_or(i == 0, e != te_ref[jnp.maximum(i - 1, 0)])

    def weight_copies(expert, slot):
        return (pltpu.make_async_copy(wgu_hbm.at[layer, expert], wgu_f32.at[slot], sems.at[0, slot]),
                pltpu.make_async_copy(wd_hbm.at[layer, expert], wd_f32.at[slot], sems.at[1, slot]))

    @pl.when(jnp.logical_and(used, i == 0))
    def _():
        for c in weight_copies(e, s):
            c.start()

    @pl.when(jnp.logical_and(used, new_expert))
    def _():
        for c in weight_copies(e, s):
            c.wait()

        @pl.when(nx_ref[i] >= 0)
        def _():
            for c in weight_copies(nx_ref[i], 1 - s):
                c.start()

        wgu_bf_ref[...] = wgu_f32[s].astype(BF16)
        wd_bf_ref[...] = wd_f32[s].astype(BF16)

    @pl.when(used)
    def _():
        live = lax.broadcasted_iota(jnp.int32, (MOE_TILE, 1), 0) < nv_ref[i]
        x = jnp.concatenate(_unpack_bf16_pairs(jnp.where(live, xs_ref[...], jnp.uint32(0))), axis=1)
        gu = _dot(x, wgu_bf_ref[...]) + bgu_ref[...]
        gt = jnp.minimum(gu[:, :D_FF], SWIGLU_LIMIT)
        up = jnp.clip(gu[:, D_FF:], -SWIGLU_LIMIT, SWIGLU_LIMIT)
        act = (up + 1.0) * (gt * jax.nn.sigmoid(SWIGLU_ALPHA * gt))
        y = _dot(act.astype(BF16), wd_bf_ref[...]) + bd_ref[...]
        ys_ref[...] = _pack_bf16_pairs(y.astype(BF16))

    @pl.when(jnp.logical_not(used))
    def _():
        ys_ref[...] = jnp.zeros_like(ys_ref)


def _experts(xs, tile_e, n_valid, n_used, layer, w_gu, b_gu, w_down, b_down):
    n_rows, W = xs.shape
    n_tiles = n_rows // MOE_TILE
    idx = jnp.arange(n_tiles, dtype=jnp.int32)
    first = jnp.logical_and(idx < n_used[0], jnp.logical_or(idx == 0, tile_e != jnp.roll(tile_e, 1)))
    slot = ((jnp.cumsum(first.astype(jnp.int32)) + 1) % 2).astype(jnp.int32)
    later_first = lax.cummin(jnp.where(first, idx, n_tiles), reverse=True)
    nxt_idx = jnp.concatenate([later_first[1:], jnp.full((1,), n_tiles, jnp.int32)])
    nxt = jnp.where(nxt_idx < n_tiles, tile_e[jnp.minimum(nxt_idx, n_tiles - 1)], -1).astype(jnp.int32)
    bias = lambda w: pl.BlockSpec((None, None, 1, w), lambda i, te, nv, nu, nx, sl: (layer, te[i], 0, 0))
    tile = pl.BlockSpec((MOE_TILE, W), lambda i, te, nv, nu, nx, sl: (i, 0))
    grid_spec = pltpu.PrefetchScalarGridSpec(
        num_scalar_prefetch=5,
        grid=(n_tiles,),
        in_specs=[tile, pl.BlockSpec(memory_space=pl.ANY), bias(2 * D_FF),
                  pl.BlockSpec(memory_space=pl.ANY), bias(D_MODEL)],
        out_specs=tile,
        scratch_shapes=[pltpu.VMEM((2, D_MODEL, 2 * D_FF), F32), pltpu.VMEM((2, D_FF, D_MODEL), F32),
                        pltpu.VMEM((D_MODEL, 2 * D_FF), BF16), pltpu.VMEM((D_FF, D_MODEL), BF16),
                        pltpu.SemaphoreType.DMA((2, 2))],
    )
    return pl.pallas_call(
        functools.partial(_expert_body, layer=layer),
        grid_spec=grid_spec,
        out_shape=jax.ShapeDtypeStruct((n_rows, W), jnp.uint32),
        compiler_params=_cparams(("arbitrary",)),
        name="moe_experts",
    )(tile_e, n_valid, n_used, nxt, slot, xs, w_gu, b_gu.reshape(DEPTH, N_EXPERTS, 1, 2 * D_FF), w_down,
      b_down.reshape(DEPTH, N_EXPERTS, 1, D_MODEL))


def _combine_body(x_ref, y0_ref, y1_ref, y2_ref, y3_ref, gate_ref, g2_ref, *rest, final, chained):
    rest = list(rest)
    o_ref = rest.pop()
    if chained:
        rest.pop()
    g = gate_ref[...]
    acc_hi = acc_lo = None
    for k, y_ref in enumerate((y0_ref, y1_ref, y2_ref, y3_ref)):
        p = y_ref[...]
        gk = g[:, k:k + 1]
        hi = gk * pltpu.bitcast(p & jnp.uint32(0xFFFF0000), F32)
        lo = gk * pltpu.bitcast(p << 16, F32)
        acc_hi = hi if acc_hi is None else acc_hi + hi
        acc_lo = lo if acc_lo is None else acc_lo + lo
    acc = jnp.concatenate([acc_hi, acc_lo], axis=1)
    xn = x_ref[...] + _scale_rows(acc, g2_ref[...])
    if final:
        xn = xn * lax.rsqrt(jnp.mean(xn * xn, axis=-1, keepdims=True) + EPS) * rest[0][...]
        for t in range(o_ref.shape[1]):
            o_ref[:, t, :] = xn[t * SUBLANES:(t + 1) * SUBLANES, :]
    else:
        o_ref[...] = xn


def _combine(x, rows, moe_out, tok_off, mod, final_g, ctx, prev=None):
    ysg, gates_t = moe_out
    R, D = x.shape
    tl = TOK_TILE
    r0 = rows[0] // tl
    t0 = tok_off // tl
    nt = gates_t.shape[0] // tl
    final = final_g is not None
    y_specs = [pl.BlockSpec((tl, D // 2), lambda i, k=k: (k * nt + t0 + i, 0)) for k in range(TOP_K)]
    in_specs = ([pl.BlockSpec((tl, D), lambda i: (r0 + i, 0))] + y_specs
                + [pl.BlockSpec((tl, TOP_K), lambda i: (t0 + i, 0)), _mod_spec(5, ctx)])
    args = [x, ysg, ysg, ysg, ysg, gates_t, mod]
    if final:
        in_specs.append(_full((1, D)))
        args.append(final_g.reshape(1, D))
        out_spec = pl.BlockSpec((SUBLANES, tl // SUBLANES, D), lambda i: (0, r0 + i, 0))
        out_shape = jax.ShapeDtypeStruct((SUBLANES, R // SUBLANES, D), F32)
    else:
        out_spec = pl.BlockSpec((tl, D), lambda i: (r0 + i, 0))
        out_shape = jax.ShapeDtypeStruct((R, D), F32)
    aliases = {}
    if prev is not None:
        aliases = {len(args): 0}
        in_specs.append(pl.BlockSpec(memory_space=pl.ANY))
        args.append(prev)
    return pl.pallas_call(
        functools.partial(_combine_body, final=final, chained=prev is not None),
        grid=((rows[1] - rows[0]) // tl,),
        in_specs=in_specs,
        out_specs=out_spec,
        out_shape=out_shape,
        input_output_aliases=aliases,
        compiler_params=_cparams(("parallel",)),
        name="moe_combine",
    )(*args)


def _moe(hp, logits_t, toks, layer, w_gu, b_gu, w_down, b_down):
    dest, gates, tile_e, n_valid, n_used, n_tiles = _routing(logits_t[:, toks[0]:toks[1]])
    dest_flat = dest.reshape(-1)
    xs = _dispatch(hp, toks[0], dest_flat, n_tiles * MOE_TILE)
    ys = _experts(xs, tile_e, n_valid, n_used, layer, w_gu, b_gu, w_down, b_down)
    return _gather_rows(ys, dest_flat), gates.T


def _token_mixers(x, pos, mod, h0, p, consts, ctx, need_out):
    R = x.size // x.shape[-1]
    L = R // SUBLANES
    xa, ga, xb, xc, xd = _in_projection(x, pos, mod, p["norm1_g"], p["w_in"], p["b_in"], ctx)
    yf, yb, hfin = _rglru(xa, p["conv_a_w"], p["conv_a_b"], p["wg"], p["bg"], p["rg_lambda"], h0, L)
    if not need_out:
        return None, hfin
    yp = _pool_mixer(xb, p["w_pool"], p["b_pool"], p["pool_scale"], L)
    cl, sl = consts["dft"][L]
    yc = _fourier_mixer(xc.reshape(L, SUBLANES * W_GROUP), cl, sl, consts["cc"], consts["sc"], p["w_four"],
                        p["b_four"], L).reshape(R, W_GROUP)
    yd = _conformer(xd, p["conv_d_w"], p["conv_d_b"], p["ln_d_g"], p["ln_d_b"], consts["avg"], p["w_pw"],
                    p["b_pw"], L)
    return (yf, yb, ga, yp, yc, yd), hfin


def _pos_embed(n_tokens):
    rows_n = n_tokens // GRID_W
    q = D_MODEL // 4
    omega = 1.0 / (10000.0 ** (jnp.arange(q, dtype=F32) / q))

    def emb(n):
        ang = jnp.arange(n, dtype=F32)[:, None] * omega[None, :]
        return jnp.concatenate([jnp.sin(ang), jnp.cos(ang)], axis=-1)

    return jnp.concatenate([jnp.repeat(emb(rows_n), GRID_W, axis=0), jnp.tile(emb(GRID_W), (rows_n, 1))], axis=-1)


def _layer_params(l, w_in, b_in, conv_a_w, conv_a_b, w_rg_r, b_rg_r, w_rg_i, b_rg_i, rg_lambda, w_pool, b_pool,
                  pool_scale, w_four, b_four, conv_d_w, conv_d_b, ln_d_g, ln_d_b, w_pw, b_pw, norm1_g):
    wg = jnp.stack([jnp.concatenate([_block_diag(w_rg_r[l, d]), _block_diag(w_rg_i[l, d])], axis=1)
                    for d in range(2)]).astype(BF16)
    bg = jnp.concatenate([b_rg_r[l].reshape(2, 1, W_GROUP), b_rg_i[l].reshape(2, 1, W_GROUP)], axis=-1)
    return dict(
        norm1_g=norm1_g[l], w_in=w_in[l].astype(BF16), b_in=b_in[l],
        conv_a_w=conv_a_w[l], conv_a_b=conv_a_b[l], wg=wg, bg=bg, rg_lambda=rg_lambda[l],
        w_pool=_block_diag(w_pool[l]).astype(BF16), b_pool=b_pool[l], pool_scale=pool_scale[l],
        w_four=_block_diag(w_four[l]).astype(BF16), b_four=b_four[l],
        conv_d_w=conv_d_w[l], conv_d_b=conv_d_b[l], ln_d_g=ln_d_g[l], ln_d_b=ln_d_b[l],
        w_pw=w_pw[l].astype(BF16), b_pw=b_pw[l])


def kernel(x, c, ctx, c_ctx, w_mod, b_mod, norm1_g, norm2_g, w_in, b_in, conv_a_w, conv_a_b, w_rg_r, b_rg_r,
           w_rg_i, b_rg_i, rg_lambda, w_pool, b_pool, pool_scale, w_four, b_four, conv_d_w, conv_d_b, ln_d_g,
           ln_d_b, w_pw, b_pw, w_out, b_out, w_router, b_router, w_gu, b_gu, w_down, b_down, final_norm_g):
    bn, L, D = x.shape
    Lc = ctx.shape[1]
    assert bn == SUBLANES and D == D_MODEL

    pos = _pos_embed(L)
    c_rows = jnp.concatenate([c, jnp.broadcast_to(c_ctx[None], (MOD_ROWS - bn, D))], axis=0)
    mod = _modulation(c_rows, w_mod, b_mod)
    ctx = jnp.transpose(ctx, (1, 0, 2)).reshape(Lc * bn, D)

    cc1, sc1 = _dft_matrices(D_SUB)
    eye = jnp.eye(N_SUB, dtype=F32)
    consts = dict(
        dft={n: tuple(m.astype(BF16) for m in _dft_matrices(n)) for n in sorted({L, Lc})},
        cc=jnp.kron(eye, cc1).astype(BF16), sc=jnp.kron(eye, sc1).astype(BF16),
        avg=jnp.kron(eye, jnp.full((D_SUB, D_SUB), 1.0 / D_SUB, F32)).astype(BF16))
    h_zero = jnp.zeros((2, SUBLANES, W_GROUP), F32)

    for l in range(DEPTH):
        last = l == DEPTH - 1
        p = _layer_params(l, w_in, b_in, conv_a_w, conv_a_b, w_rg_r, b_rg_r, w_rg_i, b_rg_i, rg_lambda, w_pool,
                          b_pool, pool_scale, w_four, b_four, conv_d_w, conv_d_b, ln_d_g, ln_d_b, w_pw, b_pw,
                          norm1_g)
        mod3 = mod[l]
        w_out_bf = w_out[l].astype(BF16)
        wr_t = w_router[l].T
        wr_hi = wr_t.astype(BF16)
        wr2 = jnp.concatenate([wr_hi, (wr_t - wr_hi.astype(F32)).astype(BF16)], axis=0)
        x_pos = pos if l == 0 else None

        mix_c, h_ctx = _token_mixers(ctx, None, mod3, h_zero, p, consts, True, not last)
        mix_x, _ = _token_mixers(x, x_pos, mod3, h_ctx, p, consts, False, True)
        x, hp_x, lg_x = _out_projection(x, x_pos, mix_x, mod3, norm2_g[l], w_out_bf, b_out[l], wr2,
                                        b_router[l], False)
        if not last:
            ctx, hp_c, lg_c = _out_projection(ctx, None, mix_c, mod3, norm2_g[l], w_out_bf, b_out[l], wr2,
                                              b_router[l], True)
            hp = jnp.concatenate([hp_c, hp_x], axis=0)
            lg = jnp.concatenate([lg_c, lg_x], axis=1)
            n_ctx, final_g = bn * Lc, None
        else:
            hp, lg, n_ctx, final_g = hp_x, lg_x, 0, final_norm_g
        T = hp.shape[0]
        half = T // 2
        moe_a = _moe(hp, lg, (0, half), l, w_gu, b_gu, w_down, b_down)
        moe_b = _moe(hp, lg, (half, T), l, w_gu, b_gu, w_down, b_down)
        if n_ctx:
            ctx = _combine(ctx, (0, n_ctx), moe_a, 0, mod3, None, True)
        split = half - n_ctx
        xa = _combine(x, (0, split), moe_a, n_ctx, mod3, final_g, False)
        x = _combine(x, (split, x.shape[0]), moe_b, 0, mod3, final_g, False, prev=xa)
    return x
```

```python
import functools
import math

import jax
import jax.numpy as jnp
from jax import lax
from jax.experimental import pallas as pl
from jax.experimental.pallas import tpu as pltpu

F32 = jnp.float32
BF16 = jnp.bfloat16

D_MODEL = 1024
DEPTH = 2
GRID_W = 64
W_GROUP = 256
N_SUB = 4
D_SUB = 64
D_IN = 6 * W_GROUP
RG_CONV = 4
RG_C = 8.0
CONF_KERNEL = 31
N_EXPERTS = 32
TOP_K = 4
D_FF = D_MODEL
SWIGLU_LIMIT = 7.0
SWIGLU_ALPHA = 1.702
EPS = 1e-6

SUBLANES = 8
VMEM_LIMIT_BYTES = 56 * 1024 * 1024
MOD_ROWS = 16
RG_HALO = 8 * SUBLANES
POOL_HALO = 8 * SUBLANES
CONF_HALO = 16 * SUBLANES
MOE_TILE = 512
TOK_TILE = 256
SC_WINDOW = 64
ROUTE_TILE = 512


def _cparams(sem):
    return pltpu.CompilerParams(dimension_semantics=sem, vmem_limit_bytes=VMEM_LIMIT_BYTES)


def _full(shape):
    nd = len(shape)
    return pl.BlockSpec(shape, lambda *_: (0,) * nd)


def _dot(a, b):
    return jnp.dot(a, b, preferred_element_type=F32)


def _split_bf16(v):
    hi = v.astype(BF16)
    lo = (v - hi.astype(F32)).astype(BF16)
    return hi, lo


def _mod_body(c_ref, w_ref, b_ref, o_ref):
    c = c_ref[...]
    s = c * jax.nn.sigmoid(c)
    o_ref[...] = jnp.dot(s, w_ref[...], precision=lax.Precision.HIGHEST,
                         preferred_element_type=F32) + b_ref[...]


def _modulation(c_rows, w_mod, b_mod):
    tn = 1536
    n6 = 6 * D_MODEL
    return pl.pallas_call(
        _mod_body,
        grid=(DEPTH, n6 // tn),
        in_specs=[_full((MOD_ROWS, D_MODEL)),
                  pl.BlockSpec((None, D_MODEL, tn), lambda l, j: (l, 0, j)),
                  pl.BlockSpec((None, 1, tn), lambda l, j: (l, 0, j))],
        out_specs=pl.BlockSpec((None, MOD_ROWS, tn), lambda l, j: (l, 0, j)),
        out_shape=jax.ShapeDtypeStruct((DEPTH, MOD_ROWS, n6), F32),
        compiler_params=_cparams(("parallel", "parallel")),
        name="modulation",
    )(c_rows, w_mod, b_mod.reshape(DEPTH, 1, n6))


def _mod_spec(chunk, ctx):
    return pl.BlockSpec((SUBLANES, D_MODEL), lambda i: (1 if ctx else 0, chunk))


def _scale_rows(v, m):
    r, d = v.shape
    return (v.reshape(r // SUBLANES, SUBLANES, d) * m[None]).reshape(r, d)


def _rms_mod(x, g, shift, scale):
    r, d = x.shape
    y = x * lax.rsqrt(jnp.mean(x * x, axis=-1, keepdims=True) + EPS) * g
    y3 = y.reshape(r // SUBLANES, SUBLANES, d)
    return (y3 * (1.0 + scale)[None] + shift[None]).reshape(r, d)


def _load_plus_pos(x_ref, prow_ref, pcol_ref):
    row = prow_ref[...]
    return jnp.concatenate([x_ref[:, t, :] + jnp.concatenate([row, pcol_ref[t:t + 1, :]], axis=1)
                            for t in range(GRID_W)], axis=0)


def _x_spec(x, tr):
    if x.ndim == 3:
        return pl.BlockSpec((SUBLANES, tr // SUBLANES, x.shape[2]), lambda i: (0, i, 0))
    return pl.BlockSpec((tr, x.shape[1]), lambda i: (i, 0))


def _inproj_body(*refs, add_pos):
    if add_pos:
        (x_ref, prow_ref, pcol_ref, sh_ref, sc_ref, g_ref, w_ref, b_ref,
         xa_ref, ga_ref, xb_ref, xc_ref, xd_ref) = refs
        x = _load_plus_pos(x_ref, prow_ref, pcol_ref)
    else:
        x_ref, sh_ref, sc_ref, g_ref, w_ref, b_ref, xa_ref, ga_ref, xb_ref, xc_ref, xd_ref = refs
        x = x_ref[...]
    u = _rms_mod(x, g_ref[...], sh_ref[...], sc_ref[...])
    p = _dot(u.astype(BF16), w_ref[...]) + b_ref[...]
    xa_ref[...] = p[:, 0:256]
    ga_ref[...] = p[:, 256:512]
    xb_ref[...] = p[:, 512:768]
    xc_ref[...] = p[:, 768:1024].astype(BF16)
    xd_ref[...] = p[:, 1024:1536]


def _in_projection(x, pos, mod, norm_g, w_in_bf, b_in, ctx):
    add_pos = pos is not None
    assert add_pos == (x.ndim == 3)
    R, D = x.size // x.shape[-1], x.shape[-1]
    tr = min(R, 512)
    row = lambda w: pl.BlockSpec((tr, w), lambda i: (i, 0))
    in_specs = [_x_spec(x, tr)]
    args = [x]
    if add_pos:
        assert tr == GRID_W * SUBLANES
        in_specs += [pl.BlockSpec((None, 1, D // 2), lambda i: (i, 0, 0)), _full((GRID_W, D // 2))]
        args += list(pos)
    in_specs += [_mod_spec(0, ctx), _mod_spec(1, ctx), _full((1, D)), _full((D, D_IN)), _full((1, D_IN))]
    args += [mod, mod, norm_g.reshape(1, D), w_in_bf, b_in.reshape(1, D_IN)]
    out_shape = [jax.ShapeDtypeStruct((R, 256), F32)] * 3 + [
        jax.ShapeDtypeStruct((R, 256), BF16), jax.ShapeDtypeStruct((R, 512), F32)]
    return pl.pallas_call(
        functools.partial(_inproj_body, add_pos=add_pos),
        grid=(R // tr,),
        in_specs=in_specs,
        out_specs=[row(256), row(256), row(256), row(256), row(512)],
        out_shape=out_shape,
        compiler_params=_cparams(("parallel",)),
        name="in_projection",
    )(*args)


def _rg_gates(xc, wg, bg, lam):
    g = _dot(xc.astype(BF16), wg) + bg
    r = jax.nn.sigmoid(g[:, :W_GROUP])
    gi = jax.nn.sigmoid(g[:, W_GROUP:])
    z = -lam
    softplus = jnp.maximum(z, 0.0) + jnp.log1p(jnp.exp(-jnp.abs(z)))
    log_a = (-RG_C) * r * softplus
    a = jnp.exp(log_a)
    b = jnp.sqrt(-jnp.tanh(log_a) * (a * a + 1.0)) * (gi * xc)
    return a, b


def _rg_body(xf_ref, xfh_ref, xr_ref, xrh_ref, cw_ref, cb_ref, wg_ref, bg_ref, lam_ref, h0_ref,
             yf_ref, yb_ref, hfin_ref, af_ref, ab_ref, hc_ref, *, n, tt):
    i = pl.program_id(0)
    tr = tt * SUBLANES
    keep = RG_HALO - (RG_CONV - 1) * SUBLANES

    @pl.when(i == 0)
    def _():
        hc_ref[...] = h0_ref[...]

    halo = jnp.where(i > 0, xfh_ref[...], 0.0)
    ext = jnp.concatenate([halo[keep:], xf_ref[...]], axis=0)
    xc = cb_ref[0]
    for k in range(RG_CONV):
        xc = xc + cw_ref[0, k:k + 1, :] * ext[k * SUBLANES:k * SUBLANES + tr]
    a, b = _rg_gates(xc, wg_ref[0], bg_ref[0], lam_ref[0])
    af_ref[...] = a
    yf_ref[...] = b

    halo = jnp.where(i > 0, xrh_ref[...], 0.0)
    ext = jnp.concatenate([xr_ref[...], halo[:(RG_CONV - 1) * SUBLANES]], axis=0)
    xc = cb_ref[1]
    for k in range(RG_CONV):
        o = (RG_CONV - 1 - k) * SUBLANES
        xc = xc + cw_ref[1, k:k + 1, :] * ext[o:o + tr]
    a, b = _rg_gates(xc, wg_ref[1], bg_ref[1], lam_ref[1])
    ab_ref[...] = a
    yb_ref[...] = b

    def step(t, carry):
        hf, hb = carry
        rf = pl.multiple_of(t * SUBLANES, SUBLANES)
        hf = af_ref[pl.ds(rf, SUBLANES), :] * hf + yf_ref[pl.ds(rf, SUBLANES), :]
        yf_ref[pl.ds(rf, SUBLANES), :] = hf
        rb = pl.multiple_of((tt - 1 - t) * SUBLANES, SUBLANES)
        hb = ab_ref[pl.ds(rb, SUBLANES), :] * hb + yb_ref[pl.ds(rb, SUBLANES), :]
        yb_ref[pl.ds(rb, SUBLANES), :] = hb
        return hf, hb

    hf, hb = lax.fori_loop(0, tt, step, (hc_ref[0], hc_ref[1]), unroll=8)
    hc_ref[0] = hf
    hc_ref[1] = hb
    hfin_ref[0] = hf
    hfin_ref[1] = hb


def _rglru(xa2, conv_w, conv_b, wg_bf, bg, lam, h0, L):
    tt = min(L, 256)
    n = L // tt
    tr = tt * SUBLANES
    per = tr // RG_HALO
    last_halo = L * SUBLANES // RG_HALO - 1
    row = lambda i: (i, 0)
    rev = lambda i: (n - 1 - i, 0)
    in_specs = [
        pl.BlockSpec((tr, W_GROUP), row),
        pl.BlockSpec((RG_HALO, W_GROUP), lambda i: (jnp.maximum(i * per - 1, 0), 0)),
        pl.BlockSpec((tr, W_GROUP), rev),
        pl.BlockSpec((RG_HALO, W_GROUP), lambda i: (jnp.minimum((n - i) * per, last_halo), 0)),
        _full((2, RG_CONV, W_GROUP)), _full((2, 1, W_GROUP)), _full((2, W_GROUP, 2 * W_GROUP)),
        _full((2, 1, 2 * W_GROUP)), _full((2, 1, W_GROUP)), _full((2, SUBLANES, W_GROUP)),
    ]
    return pl.pallas_call(
        functools.partial(_rg_body, n=n, tt=tt),
        grid=(n,),
        in_specs=in_specs,
        out_specs=[pl.BlockSpec((tr, W_GROUP), row), pl.BlockSpec((tr, W_GROUP), rev),
                   _full((2, SUBLANES, W_GROUP))],
        out_shape=[jax.ShapeDtypeStruct((L * SUBLANES, W_GROUP), F32)] * 2
        + [jax.ShapeDtypeStruct((2, SUBLANES, W_GROUP), F32)],
        scratch_shapes=[pltpu.VMEM((tr, W_GROUP), F32), pltpu.VMEM((tr, W_GROUP), F32),
                        pltpu.VMEM((2, SUBLANES, W_GROUP), F32)],
        compiler_params=_cparams(("arbitrary",)),
        name="rglru",
    )(xa2, xa2, xa2, xa2, conv_w, conv_b.reshape(2, 1, W_GROUP), wg_bf, bg, lam.reshape(2, 1, W_GROUP), h0)


def _pool_body(xm_ref, xp_ref, xn_ref, w_ref, b_ref, s_ref, o_ref, *, n, tt, L):
    i = pl.program_id(0)
    tr = tt * SUBLANES
    S = SUBLANES
    xm = xm_ref[...]
    prev = jnp.where(i > 0, xp_ref[...], 0.0)
    nxt = jnp.where(i < n - 1, xn_ref[...], 0.0)
    xe = jnp.concatenate([prev, xm, nxt], axis=0)
    e = xe.shape[0]
    p2 = xe[S:e] + xe[0:e - S]
    n4 = (tt + 13) * S
    p4 = p2[0:n4] + p2[2 * S:2 * S + n4]
    n8 = (tt + 9) * S
    p8 = p4[0:n8] + p4[4 * S:4 * S + n8]
    s16 = p8[0:tr] + p8[8 * S:8 * S + tr]
    s2 = p2[7 * S:7 * S + tr]
    s4 = p4[6 * S:6 * S + tr]
    s8 = p8[4 * S:4 * S + tr]
    grp = lax.broadcasted_iota(jnp.int32, (1, W_GROUP), 1) // D_SUB
    half = jnp.left_shift(1, grp)
    t = i * tt + lax.broadcasted_iota(jnp.int32, (tr, 1), 0) // S
    cnt = (jnp.minimum(t + half, L) - jnp.maximum(t - half, 0)).astype(F32)
    s = jnp.where(grp == 0, s2, jnp.where(grp == 1, s4, jnp.where(grp == 2, s8, s16)))
    pooled = s / cnt - xm
    y = _dot(pooled.astype(BF16), w_ref[...]) + b_ref[...]
    o_ref[...] = (y * s_ref[...]).astype(BF16)


def _pool_mixer(xb2, w_bd_bf, b, scale, L):
    tt = min(L, 256)
    n = L // tt
    tr = tt * SUBLANES
    per = tr // POOL_HALO
    last_halo = L * SUBLANES // POOL_HALO - 1
    return pl.pallas_call(
        functools.partial(_pool_body, n=n, tt=tt, L=L),
        grid=(n,),
        in_specs=[pl.BlockSpec((tr, W_GROUP), lambda i: (i, 0)),
                  pl.BlockSpec((POOL_HALO, W_GROUP), lambda i: (jnp.maximum(i * per - 1, 0), 0)),
                  pl.BlockSpec((POOL_HALO, W_GROUP), lambda i: (jnp.minimum((i + 1) * per, last_halo), 0)),
                  _full((W_GROUP, W_GROUP)), _full((1, W_GROUP)), _full((1, W_GROUP))],
        out_specs=pl.BlockSpec((tr, W_GROUP), lambda i: (i, 0)),
        out_shape=jax.ShapeDtypeStruct((L * SUBLANES, W_GROUP), BF16),
        compiler_params=_cparams(("parallel",)),
        name="pool_mixer",
    )(xb2, xb2, xb2, w_bd_bf, b.reshape(1, W_GROUP), scale.reshape(1, W_GROUP))


def _fourier_body(c_ref, s_ref, xe_ref, xo_ref, cw_ref, sw_ref, cc_ref, sc_ref, w_ref, b_ref, o_ref):
    c, s = c_ref[...], s_ref[...]
    xe, xo = xe_ref[...], xo_ref[...]
    ec, es = _dot(c, xe), _dot(s, xe)
    oc, os_ = _dot(c, xo), _dot(s, xo)
    cw, sw = cw_ref[...], sw_ref[...]
    tc = cw * oc - sw * os_
    ts = cw * os_ + sw * oc
    for h, (z1, z2) in enumerate(((ec + tc, es + ts), (ec - tc, es - ts))):
        for j in range(xe.shape[1] // W_GROUP):
            sl = slice(j * W_GROUP, (j + 1) * W_GROUP)
            a_hi, a_lo = _split_bf16(z1[:, sl])
            b_hi, b_lo = _split_bf16(z2[:, sl])
            f = ((_dot(a_hi, cc_ref[...]) + _dot(a_lo, cc_ref[...]))
                 - (_dot(b_hi, sc_ref[...]) + _dot(b_lo, sc_ref[...])))
            o_ref[h, :, sl] = (_dot(f.astype(BF16), w_ref[...]) + b_ref[...]).astype(BF16)


def _fourier_mixer(xc, tables, cc, sc, w_bd_bf, b, L):
    ch, sh, cw, sw = tables
    M = L // 2
    ncol = SUBLANES * W_GROUP
    x2 = xc.reshape(M, 2 * ncol)
    nb = 1024
    tk = min(M, 256)
    out = pl.pallas_call(
        _fourier_body,
        grid=(ncol // nb, M // tk),
        in_specs=[pl.BlockSpec((tk, M), lambda j, k: (k, 0)),
                  pl.BlockSpec((tk, M), lambda j, k: (k, 0)),
                  pl.BlockSpec((M, nb), lambda j, k: (0, j)),
                  pl.BlockSpec((M, nb), lambda j, k: (0, ncol // nb + j)),
                  pl.BlockSpec((tk, 1), lambda j, k: (k, 0)),
                  pl.BlockSpec((tk, 1), lambda j, k: (k, 0)),
                  _full((W_GROUP, W_GROUP)), _full((W_GROUP, W_GROUP)), _full((W_GROUP, W_GROUP)),
                  _full((1, W_GROUP))],
        out_specs=pl.BlockSpec((2, tk, nb), lambda j, k: (0, k, j)),
        out_shape=jax.ShapeDtypeStruct((2, M, ncol), BF16),
        compiler_params=_cparams(("parallel", "parallel")),
        name="fourier_mixer",
    )(ch, sh, x2, x2, cw, sw, cc, sc, w_bd_bf, b.reshape(1, W_GROUP))
    return out.reshape(L * SUBLANES, W_GROUP)


def _time_dft_tables(L):
    M = L // 2
    ch, sh = _dft_matrices(M, 1.0 / math.sqrt(L))
    ang = jnp.arange(M, dtype=F32) * (2.0 * math.pi / L)
    return ch.astype(BF16), sh.astype(BF16), jnp.cos(ang).reshape(M, 1), jnp.sin(ang).reshape(M, 1)


def _dft_matrices(L, scale):
    f = 1 << (max(L.bit_length() - 1, 0) // 2)
    n = jnp.arange(L, dtype=jnp.int32)[None, :]

    def table(rows):
        ang = ((rows[:, None] * n) % L).astype(F32) * (2.0 * math.pi / L)
        return jnp.cos(ang), jnp.sin(ang)

    ac, as_ = table(jnp.arange(L // f, dtype=jnp.int32) * f)
    bc, bs = table(jnp.arange(f, dtype=jnp.int32))
    cos = (ac[:, None, :] * bc[None, :, :] - as_[:, None, :] * bs[None, :, :]).reshape(L, L) * scale
    sin = (as_[:, None, :] * bc[None, :, :] + ac[:, None, :] * bs[None, :, :]).reshape(L, L) * scale
    return cos, sin


def _block_diag(w):
    g, a, b = w.shape
    eye = jnp.eye(g, dtype=w.dtype)
    return (eye[:, None, :, None] * w[:, :, None, :]).reshape(g * a, g * b)


CONF_CHUNK = 64


def _conformer_body(xm_ref, xp_ref, xn_ref, cw_ref, cb_ref, lg_ref, lb_ref, avg_ref, w_ref, b_ref,
                    o_ref, v_ref, c_ref, *, n, tt):
    i = pl.program_id(0)
    tr = tt * SUBLANES
    H = CONF_HALO

    def glu(v):
        return v[:, :W_GROUP] * jax.nn.sigmoid(v[:, W_GROUP:])

    v_ref[0:H] = jnp.where(i > 0, glu(xp_ref[...]), 0.0)
    v_ref[H:H + tr] = glu(xm_ref[...])
    v_ref[H + tr:H + tr + H] = jnp.where(i < n - 1, glu(xn_ref[...]), 0.0)

    def chunk(c, carry):
        r0 = pl.multiple_of(c * CONF_CHUNK, CONF_CHUNK)
        acc = jnp.broadcast_to(cb_ref[...], (CONF_CHUNK, W_GROUP))
        for k in range(CONF_KERNEL):
            acc = acc + cw_ref[k:k + 1, :] * v_ref[pl.ds(r0 + (k + 1) * SUBLANES, CONF_CHUNK), :]
        c_ref[pl.ds(r0, CONF_CHUNK), :] = acc
        return carry

    lax.fori_loop(0, tr // CONF_CHUNK, chunk, 0)

    v = c_ref[...]
    avg = avg_ref[...]
    v_hi, v_lo = _split_bf16(v)
    mu = _dot(v_hi, avg) + _dot(v_lo, avg)
    d = v - mu
    q_hi, q_lo = _split_bf16(d * d)
    var = _dot(q_hi, avg) + _dot(q_lo, avg)
    vn = d * lax.rsqrt(var + EPS) * lg_ref[...] + lb_ref[...]
    act = vn * jax.nn.sigmoid(vn)
    o_ref[...] = (_dot(act.astype(BF16), w_ref[...]) + b_ref[...]).astype(BF16)


def _conformer(xd2, conv_w, conv_b, ln_g, ln_b, avg_bf, w_pw_bf, b_pw, L):
    tt = min(L, 256)
    n = L // tt
    tr = tt * SUBLANES
    per = tr // CONF_HALO
    last_halo = L * SUBLANES // CONF_HALO - 1
    vec = lambda a: a.reshape(1, W_GROUP)
    return pl.pallas_call(
        functools.partial(_conformer_body, n=n, tt=tt),
        grid=(n,),
        in_specs=[pl.BlockSpec((tr, 2 * W_GROUP), lambda i: (i, 0)),
                  pl.BlockSpec((CONF_HALO, 2 * W_GROUP), lambda i: (jnp.maximum(i * per - 1, 0), 0)),
                  pl.BlockSpec((CONF_HALO, 2 * W_GROUP), lambda i: (jnp.minimum((i + 1) * per, last_halo), 0)),
                  _full((CONF_KERNEL, W_GROUP)), _full((1, W_GROUP)), _full((1, W_GROUP)), _full((1, W_GROUP)),
                  _full((W_GROUP, W_GROUP)), _full((W_GROUP, W_GROUP)), _full((1, W_GROUP))],
        out_specs=pl.BlockSpec((tr, W_GROUP), lambda i: (i, 0)),
        out_shape=jax.ShapeDtypeStruct((L * SUBLANES, W_GROUP), BF16),
        scratch_shapes=[pltpu.VMEM((tr + 2 * CONF_HALO, W_GROUP), F32), pltpu.VMEM((tr, W_GROUP), F32)],
        compiler_params=_cparams(("parallel",)),
        name="conformer",
    )(xd2, xd2, xd2, conv_w, vec(conv_b), vec(ln_g), vec(ln_b), avg_bf, w_pw_bf, vec(b_pw))


def _gelu_tanh(x):
    return 0.5 * x * (1.0 + jnp.tanh(math.sqrt(2.0 / math.pi) * (x + 0.044715 * (x * x * x))))


def _pack_bf16_pairs(h_bf):
    u = pltpu.bitcast(h_bf.astype(F32), jnp.uint32)
    half = h_bf.shape[1] // 2
    return (u[:, :half] & jnp.uint32(0xFFFF0000)) | (u[:, half:] >> 16)


def _unpack_bf16_pairs(p):
    hi = pltpu.bitcast(p & jnp.uint32(0xFFFF0000), F32).astype(BF16)
    lo = pltpu.bitcast(p << 16, F32).astype(BF16)
    return hi, lo


def _outproj_body(*refs, add_pos, chained):
    refs = list(refs)
    xo_ref, hp_ref, lg_ref = refs[-3:]
    del refs[-5 if chained else -3:]
    if add_pos:
        (x_ref, prow_ref, pcol_ref, yf_ref, yb_ref, ga_ref, yp_ref, yc_ref, yd_ref, g1_ref, sh_ref, sc_ref,
         ng_ref, wo_ref, bo_ref, wr_ref, br_ref) = refs
        x = _load_plus_pos(x_ref, prow_ref, pcol_ref)
    else:
        (x_ref, yf_ref, yb_ref, ga_ref, yp_ref, yc_ref, yd_ref, g1_ref, sh_ref, sc_ref, ng_ref,
         wo_ref, bo_ref, wr_ref, br_ref) = refs
        x = x_ref[...]
    ya = (yf_ref[...] + yb_ref[...]) * _gelu_tanh(ga_ref[...])
    ycat = jnp.concatenate([ya.astype(BF16), yp_ref[...], yc_ref[...], yd_ref[...]], axis=1)
    y = _dot(ycat, wo_ref[...]) + bo_ref[...]
    xn = x + _scale_rows(y, g1_ref[...])
    xo_ref[...] = xn
    h = _rms_mod(xn, ng_ref[...], sh_ref[...], sc_ref[...])
    h_hi, h_lo = _split_bf16(h)
    nt = (((1,), (1,)), ((), ()))
    wr = wr_ref[...]
    acc = lax.dot_general(wr, h_hi, nt, preferred_element_type=F32)
    acc = acc + lax.dot_general(wr, h_lo, nt, preferred_element_type=F32)
    lg_ref[...] = acc[:N_EXPERTS] + acc[N_EXPERTS:] + br_ref[...]
    hp_ref[...] = _pack_bf16_pairs(h_hi)


def _out_projection(x, pos, mix, mod, norm_g, w_out_bf, b_out, wr2_bf, b_router, ctx, n_tok, tok0, prev):
    add_pos = pos is not None
    assert add_pos == (x.ndim == 3)
    R, D = x.size // x.shape[-1], x.shape[-1]
    tr = min(R, 512)
    row = lambda w: pl.BlockSpec((tr, w), lambda i: (i, 0))
    in_specs = [_x_spec(x, tr)]
    args = [x]
    if add_pos:
        assert tr == GRID_W * SUBLANES
        in_specs += [pl.BlockSpec((None, 1, D // 2), lambda i: (i, 0, 0)), _full((GRID_W, D // 2))]
        args += list(pos)
    in_specs += [row(W_GROUP)] * 6
    args += list(mix)
    in_specs += [_mod_spec(2, ctx), _mod_spec(3, ctx), _mod_spec(4, ctx), _full((1, D)), _full((D, D)),
                 _full((1, D)), _full((2 * N_EXPERTS, D)), _full((N_EXPERTS, 1))]
    args += [mod, mod, mod, norm_g.reshape(1, D), w_out_bf, b_out.reshape(1, D), wr2_bf,
             b_router.reshape(N_EXPERTS, 1)]
    aliases = {}
    if prev is not None:
        aliases = {len(args): 1, len(args) + 1: 2}
        in_specs += [pl.BlockSpec(memory_space=pl.ANY)] * 2
        args += list(prev)
    t0 = tok0 // tr
    assert t0 * tr == tok0
    return pl.pallas_call(
        functools.partial(_outproj_body, add_pos=add_pos, chained=prev is not None),
        grid=(R // tr,),
        in_specs=in_specs,
        out_specs=[row(D), pl.BlockSpec((tr, D // 2), lambda i: (t0 + i, 0)),
                   pl.BlockSpec((N_EXPERTS, tr), lambda i: (0, t0 + i))],
        out_shape=[jax.ShapeDtypeStruct((R, D), F32),
                   jax.ShapeDtypeStruct((n_tok, D // 2), jnp.uint32),
                   jax.ShapeDtypeStruct((N_EXPERTS, n_tok), F32)],
        input_output_aliases=aliases,
        compiler_params=_cparams(("parallel",)),
        name="out_projection",
    )(*args)


def _top4(v):
    eid = lax.broadcasted_iota(jnp.int32, v.shape, 0)
    out = []
    work = v
    for _ in range(TOP_K):
        m = jnp.max(work, axis=0, keepdims=True)
        idx = jnp.min(jnp.where(work == m, eid, N_EXPERTS), axis=0, keepdims=True)
        oh = eid == idx
        out.append((m, oh))
        work = jnp.where(oh, -jnp.inf, work)
    return out


def _count_body(lg_ref, cnt_ref):
    @pl.when(pl.program_id(0) == 0)
    def _():
        cnt_ref[...] = jnp.zeros_like(cnt_ref)

    sel = jnp.zeros(lg_ref.shape, F32)
    for _, oh in _top4(lg_ref[...]):
        sel = sel + oh.astype(F32)
    cnt_ref[...] += jnp.sum(sel, axis=1, keepdims=True)


def _route_body(lg_ref, ps_ref, tri_ref, dest_ref, gate_ref, carry_ref):
    @pl.when(pl.program_id(0) == 0)
    def _():
        carry_ref[...] = jnp.zeros_like(carry_ref)

    top = _top4(lg_ref[...])
    sel = jnp.zeros(lg_ref.shape, F32)
    for _, oh in top:
        sel = sel + oh.astype(F32)
    before = _dot(sel.astype(BF16), tri_ref[...]) + carry_ref[...] + ps_ref[...]
    m0 = top[0][0]
    es = [jnp.exp(m - m0) for m, _ in top]
    den = es[0] + es[1] + es[2] + es[3]
    for k, (_, oh) in enumerate(top):
        dest_ref[k:k + 1, :] = jnp.sum(jnp.where(oh, before, 0.0), axis=0, keepdims=True).astype(jnp.int32)
        gate_ref[k:k + 1, :] = es[k] / den
    carry_ref[...] += jnp.sum(sel, axis=1, keepdims=True)


def _routing(logits_t):
    E, T = logits_t.shape
    tt = ROUTE_TILE
    nt = T // tt
    counts = pl.pallas_call(
        _count_body,
        grid=(nt,),
        in_specs=[pl.BlockSpec((E, tt), lambda i: (0, i))],
        out_specs=_full((E, 1)),
        out_shape=jax.ShapeDtypeStruct((E, 1), F32),
        compiler_params=_cparams(("arbitrary",)),
        name="route_count",
    )(logits_t)
    cnt = counts[:, 0].astype(jnp.int32)
    padded = ((cnt + MOE_TILE - 1) // MOE_TILE) * MOE_TILE
    pend = jnp.cumsum(padded)
    pstart = pend - padded
    n_tiles = -(-(T * TOP_K) // MOE_TILE) + N_EXPERTS
    tile_start = jnp.arange(n_tiles, dtype=jnp.int32) * MOE_TILE
    tile_e = jnp.minimum(jnp.sum((pend[None, :] <= tile_start[:, None]).astype(jnp.int32), axis=1), N_EXPERTS - 1)
    n_used = (pend[-1] // MOE_TILE).astype(jnp.int32).reshape(1)
    n_valid = jnp.clip((pstart + cnt)[tile_e] - tile_start, 0, MOE_TILE).astype(jnp.int32)
    tri = (jnp.arange(tt)[:, None] < jnp.arange(tt)[None, :]).astype(BF16)
    dest, gates = pl.pallas_call(
        _route_body,
        grid=(nt,),
        in_specs=[pl.BlockSpec((E, tt), lambda i: (0, i)), _full((E, 1)), _full((tt, tt))],
        out_specs=[pl.BlockSpec((TOP_K, tt), lambda i: (0, i)), pl.BlockSpec((TOP_K, tt), lambda i: (0, i))],
        out_shape=[jax.ShapeDtypeStruct((TOP_K, T), jnp.int32), jax.ShapeDtypeStruct((TOP_K, T), F32)],
        scratch_shapes=[pltpu.VMEM((E, 1), F32)],
        compiler_params=_cparams(("arbitrary",)),
        name="route_assign",
    )(logits_t, pstart.astype(F32).reshape(E, 1), tri)
    return dest, gates, tile_e, n_valid, n_used, n_tiles


def _sc_workers():
    from jax.experimental.pallas import tpu_sc as plsc
    mesh = plsc.VectorSubcoreMesh(core_axis_name="c", subcore_axis_name="s")
    n_workers = mesh.num_cores * mesh.num_subcores
    worker = lambda: lax.axis_index("s") * mesh.num_cores + lax.axis_index("c")
    return mesh, n_workers, worker


def _dispatch(hp, tok0, dest_flat, n_rows):
    W = hp.shape[1]
    T = dest_flat.shape[0] // TOP_K
    mesh, n_workers, worker = _sc_workers()
    per = dest_flat.shape[0] // n_workers
    steps = per // SC_WINDOW
    assert per * n_workers == dest_flat.shape[0] and steps * SC_WINDOW == per and steps % 2 == 0
    assert T % SC_WINDOW == 0 and tok0 % SC_WINDOW == 0

    @functools.partial(
        pl.kernel, mesh=mesh, out_type=jax.ShapeDtypeStruct((n_rows, W), hp.dtype),
        scratch_types=[pltpu.VMEM((SC_WINDOW,), jnp.int32), pltpu.VMEM((SC_WINDOW,), jnp.int32),
                       pltpu.VMEM((SC_WINDOW, W), hp.dtype), pltpu.VMEM((SC_WINDOW, W), hp.dtype),
                       pltpu.SemaphoreType.DMA, pltpu.SemaphoreType.DMA],
        name="moe_dispatch")
    def scatter(hp_hbm, dest_hbm, xs_hbm, idx0, idx1, rows0, rows1, sem0, sem1):
        base = worker() * per
        bufs = ((idx0, rows0, sem0), (idx1, rows1, sem1))

        def window(j, b, first):
            idx_v, rows_v, sem = bufs[b]

            @pl.when(jnp.logical_not(first))
            def _():
                pltpu.make_async_copy(rows_v, xs_hbm.at[idx_v], sem).wait()

            off = pl.multiple_of(base + j * SC_WINDOW, SC_WINDOW)
            tok = pl.multiple_of(tok0 + lax.rem(off, T), SC_WINDOW)
            pltpu.sync_copy(dest_hbm.at[pl.ds(off, SC_WINDOW)], idx_v)
            pltpu.sync_copy(hp_hbm.at[pl.ds(tok, SC_WINDOW)], rows_v)
            pltpu.async_copy(rows_v, xs_hbm.at[idx_v], sem)

        @pl.loop(0, steps, step=2)
        def _(j):
            window(j, 0, j == 0)
            window(j + 1, 1, j == 0)

        for idx_v, rows_v, sem in bufs:
            pltpu.make_async_copy(rows_v, xs_hbm.at[idx_v], sem).wait()

    return scatter(hp, dest_flat)


def _gather_rows(table, idx_flat):
    n = idx_flat.shape[0]
    W = table.shape[1]
    mesh, n_workers, worker = _sc_workers()
    per = n // n_workers
    steps = per // SC_WINDOW
    assert per * n_workers == n and steps * SC_WINDOW == per and steps % 2 == 0

    @functools.partial(
        pl.kernel, mesh=mesh, out_type=jax.ShapeDtypeStruct((n, W), table.dtype),
        scratch_types=[pltpu.VMEM((SC_WINDOW,), jnp.int32), pltpu.VMEM((SC_WINDOW,), jnp.int32),
                       pltpu.VMEM((SC_WINDOW, W), table.dtype), pltpu.VMEM((SC_WINDOW, W), table.dtype),
                       pltpu.SemaphoreType.DMA, pltpu.SemaphoreType.DMA, pltpu.SemaphoreType.DMA],
        name="moe_gather")
    def gather(table_hbm, idx_hbm, out_hbm, idx0, idx1, rows0, rows1, sem0, sem1, gsem):
        base = worker() * per
        bufs = ((idx0, rows0, sem0), (idx1, rows1, sem1))

        def window(j, b, first):
            idx_v, rows_v, sem = bufs[b]
            off = pl.multiple_of(base + j * SC_WINDOW, SC_WINDOW)

            @pl.when(jnp.logical_not(first))
            def _():
                pltpu.make_async_copy(rows_v, out_hbm.at[pl.ds(off, SC_WINDOW)], sem).wait()

            pltpu.sync_copy(idx_hbm.at[pl.ds(off, SC_WINDOW)], idx_v)
            pltpu.async_copy(table_hbm.at[idx_v], rows_v, gsem).wait()
            pltpu.async_copy(rows_v, out_hbm.at[pl.ds(off, SC_WINDOW)], sem)

        @pl.loop(0, steps, step=2)
        def _(j):
            window(j, 0, j == 0)
            window(j + 1, 1, j == 0)

        for _, rows_v, sem in bufs:
            pltpu.make_async_copy(rows_v, out_hbm.at[pl.ds(base, SC_WINDOW)], sem).wait()

    return gather(table, idx_flat)


def _expert_body(te_ref, nv_ref, nu_ref, nx_ref, sl_ref, xs_ref, wgu_hbm, bgu_ref, wd_hbm, bd_ref, ys_ref,
                 wgu_f32, wd_f32, wgu_bf_ref, wd_bf_ref, sems, *, layer):
    i = pl.program_id(0)
    used = i < nu_ref[0]
    e = te_ref[i]
    s = sl_ref[i]
    new_expert = jnp.logical_or(i == 0, e != te_ref[jnp.maximum(i - 1, 0)])

    def weight_copies(expert, slot):
        return (pltpu.make_async_copy(wgu_hbm.at[layer, expert], wgu_f32.at[slot], sems.at[0, slot]),
                pltpu.make_async_copy(wd_hbm.at[layer, expert], wd_f32.at[slot], sems.at[1, slot]))

    @pl.when(jnp.logical_and(used, i == 0))
    def _():
        for c in weight_copies(e, s):
            c.start()

    @pl.when(jnp.logical_and(used, new_expert))
    def _():
        for c in weight_copies(e, s):
            c.wait()

        @pl.when(nx_ref[i] >= 0)
        def _():
            for c in weight_copies(nx_ref[i], 1 - s):
                c.start()

        wgu_bf_ref[...] = wgu_f32[s].astype(BF16)
        wd_bf_ref[...] = wd_f32[s].astype(BF16)

    @pl.when(used)
    def _():
        live = lax.broadcasted_iota(jnp.int32, (MOE_TILE, 1), 0) < nv_ref[i]
        x = jnp.concatenate(_unpack_bf16_pairs(jnp.where(live, xs_ref[...], jnp.uint32(0))), axis=1)
        gu = _dot(x, wgu_bf_ref[...]) + bgu_ref[...]
        gt = jnp.minimum(gu[:, :D_FF], SWIGLU_LIMIT)
        up = jnp.clip(gu[:, D_FF:], -SWIGLU_LIMIT, SWIGLU_LIMIT)
        act = (up + 1.0) * (gt * jax.nn.sigmoid(SWIGLU_ALPHA * gt))
        y = _dot(act.astype(BF16), wd_bf_ref[...]) + bd_ref[...]
        ys_ref[...] = _pack_bf16_pairs(y.astype(BF16))

    @pl.when(jnp.logical_not(used))
    def _():
        ys_ref[...] = jnp.zeros_like(ys_ref)


def _experts(xs, tile_e, n_valid, n_used, layer, w_gu, b_gu, w_down, b_down):
    n_rows, W = xs.shape
    n_tiles = n_rows // MOE_TILE
    idx = jnp.arange(n_tiles, dtype=jnp.int32)
    first = jnp.logical_and(idx < n_used[0], jnp.logical_or(idx == 0, tile_e != jnp.roll(tile_e, 1)))
    slot = ((jnp.cumsum(first.astype(jnp.int32)) + 1) % 2).astype(jnp.int32)
    later_first = lax.cummin(jnp.where(first, idx, n_tiles), reverse=True)
    nxt_idx = jnp.concatenate([later_first[1:], jnp.full((1,), n_tiles, jnp.int32)])
    nxt = jnp.where(nxt_idx < n_tiles, tile_e[jnp.minimum(nxt_idx, n_tiles - 1)], -1).astype(jnp.int32)
    bias = lambda w: pl.BlockSpec((None, None, 1, w), lambda i, te, nv, nu, nx, sl: (layer, te[i], 0, 0))
    tile = pl.BlockSpec((MOE_TILE, W), lambda i, te, nv, nu, nx, sl: (i, 0))
    grid_spec = pltpu.PrefetchScalarGridSpec(
        num_scalar_prefetch=5,
        grid=(n_tiles,),
        in_specs=[tile, pl.BlockSpec(memory_space=pl.ANY), bias(2 * D_FF),
                  pl.BlockSpec(memory_space=pl.ANY), bias(D_MODEL)],
        out_specs=tile,
        scratch_shapes=[pltpu.VMEM((2, D_MODEL, 2 * D_FF), F32), pltpu.VMEM((2, D_FF, D_MODEL), F32),
                        pltpu.VMEM((D_MODEL, 2 * D_FF), BF16), pltpu.VMEM((D_FF, D_MODEL), BF16),
                        pltpu.SemaphoreType.DMA((2, 2))],
    )
    return pl.pallas_call(
        functools.partial(_expert_body, layer=layer),
        grid_spec=grid_spec,
        out_shape=jax.ShapeDtypeStruct((n_rows, W), jnp.uint32),
        compiler_params=_cparams(("arbitrary",)),
        name="moe_experts",
    )(tile_e, n_valid, n_used, nxt, slot, xs, w_gu, b_gu.reshape(DEPTH, N_EXPERTS, 1, 2 * D_FF), w_down,
      b_down.reshape(DEPTH, N_EXPERTS, 1, D_MODEL))


def _combine_body(x_ref, y0_ref, y1_ref, y2_ref, y3_ref, gate_ref, g2_ref, *rest, final, chained):
    rest = list(rest)
    o_ref = rest.pop()
    if chained:
        rest.pop()
    g = gate_ref[...]
    acc_hi = acc_lo = None
    for k, y_ref in enumerate((y0_ref, y1_ref, y2_ref, y3_ref)):
        p = y_ref[...]
        gk = g[:, k:k + 1]
        hi = gk * pltpu.bitcast(p & jnp.uint32(0xFFFF0000), F32)
        lo = gk * pltpu.bitcast(p << 16, F32)
        acc_hi = hi if acc_hi is None else acc_hi + hi
        acc_lo = lo if acc_lo is None else acc_lo + lo
    acc = jnp.concatenate([acc_hi, acc_lo], axis=1)
    xn = x_ref[...] + _scale_rows(acc, g2_ref[...])
    if final:
        xn = xn * lax.rsqrt(jnp.mean(xn * xn, axis=-1, keepdims=True) + EPS) * rest[0][...]
        for t in range(o_ref.shape[1]):
            o_ref[:, t, :] = xn[t * SUBLANES:(t + 1) * SUBLANES, :]
    else:
        o_ref[...] = xn


def _combine(x, rows, moe_out, tok_off, mod, final_g, ctx, prev=None):
    ysg, gates_t = moe_out
    R, D = x.shape
    tl = TOK_TILE
    r0 = rows[0] // tl
    t0 = tok_off // tl
    nt = gates_t.shape[0] // tl
    final = final_g is not None
    y_specs = [pl.BlockSpec((tl, D // 2), lambda i, k=k: (k * nt + t0 + i, 0)) for k in range(TOP_K)]
    in_specs = ([pl.BlockSpec((tl, D), lambda i: (r0 + i, 0))] + y_specs
                + [pl.BlockSpec((tl, TOP_K), lambda i: (t0 + i, 0)), _mod_spec(5, ctx)])
    args = [x, ysg, ysg, ysg, ysg, gates_t, mod]
    if final:
        in_specs.append(_full((1, D)))
        args.append(final_g.reshape(1, D))
        out_spec = pl.BlockSpec((SUBLANES, tl // SUBLANES, D), lambda i: (0, r0 + i, 0))
        out_shape = jax.ShapeDtypeStruct((SUBLANES, R // SUBLANES, D), F32)
    else:
        out_spec = pl.BlockSpec((tl, D), lambda i: (r0 + i, 0))
        out_shape = jax.ShapeDtypeStruct((R, D), F32)
    aliases = {}
    if prev is not None:
        aliases = {len(args): 0}
        in_specs.append(pl.BlockSpec(memory_space=pl.ANY))
        args.append(prev)
    return pl.pallas_call(
        functools.partial(_combine_body, final=final, chained=prev is not None),
        grid=((rows[1] - rows[0]) // tl,),
        in_specs=in_specs,
        out_specs=out_spec,
        out_shape=out_shape,
        input_output_aliases=aliases,
        compiler_params=_cparams(("parallel",)),
        name="moe_combine",
    )(*args)


def _moe(hp, logits_t, toks, layer, w_gu, b_gu, w_down, b_down):
    dest, gates, tile_e, n_valid, n_used, n_tiles = _routing(logits_t[:, toks[0]:toks[1]])
    dest_flat = dest.reshape(-1)
    xs = _dispatch(hp, toks[0], dest_flat, n_tiles * MOE_TILE)
    ys = _experts(xs, tile_e, n_valid, n_used, layer, w_gu, b_gu, w_down, b_down)
    return _gather_rows(ys, dest_flat), gates.T


def _token_mixers(x, pos, mod, h0, p, consts, ctx, need_out):
    R = x.size // x.shape[-1]
    L = R // SUBLANES
    xa, ga, xb, xc, xd = _in_projection(x, pos, mod, p["norm1_g"], p["w_in"], p["b_in"], ctx)
    yf, yb, hfin = _rglru(xa, p["conv_a_w"], p["conv_a_b"], p["wg"], p["bg"], p["rg_lambda"], h0, L)
    if not need_out:
        return None, hfin
    yp = _pool_mixer(xb, p["w_pool"], p["b_pool"], p["pool_scale"], L)
    yc = _fourier_mixer(xc, consts["dft"][L], consts["cc"], consts["sc"], p["w_four"], p["b_four"], L)
    yd = _conformer(xd, p["conv_d_w"], p["conv_d_b"], p["ln_d_g"], p["ln_d_b"], consts["avg"], p["w_pw"],
                    p["b_pw"], L)
    return (yf, yb, ga, yp, yc, yd), hfin


def _pos_embed(n_tokens):
    rows_n = n_tokens // GRID_W
    q = D_MODEL // 4
    omega = 1.0 / (10000.0 ** (jnp.arange(q, dtype=F32) / q))

    def emb(n):
        ang = jnp.arange(n, dtype=F32)[:, None] * omega[None, :]
        return jnp.concatenate([jnp.sin(ang), jnp.cos(ang)], axis=-1)

    return emb(rows_n).reshape(rows_n, 1, D_MODEL // 2), emb(GRID_W)


def _layer_params(l, w_in, b_in, conv_a_w, conv_a_b, w_rg_r, b_rg_r, w_rg_i, b_rg_i, rg_lambda, w_pool, b_pool,
                  pool_scale, w_four, b_four, conv_d_w, conv_d_b, ln_d_g, ln_d_b, w_pw, b_pw, norm1_g):
    wg = jnp.stack([jnp.concatenate([_block_diag(w_rg_r[l, d]), _block_diag(w_rg_i[l, d])], axis=1)
                    for d in range(2)]).astype(BF16)
    bg = jnp.concatenate([b_rg_r[l].reshape(2, 1, W_GROUP), b_rg_i[l].reshape(2, 1, W_GROUP)], axis=-1)
    return dict(
        norm1_g=norm1_g[l], w_in=w_in[l].astype(BF16), b_in=b_in[l],
        conv_a_w=conv_a_w[l], conv_a_b=conv_a_b[l], wg=wg, bg=bg, rg_lambda=rg_lambda[l],
        w_pool=_block_diag(w_pool[l]).astype(BF16), b_pool=b_pool[l], pool_scale=pool_scale[l],
        w_four=_block_diag(w_four[l]).astype(BF16), b_four=b_four[l],
        conv_d_w=conv_d_w[l], conv_d_b=conv_d_b[l], ln_d_g=ln_d_g[l], ln_d_b=ln_d_b[l],
        w_pw=w_pw[l].astype(BF16), b_pw=b_pw[l])


def kernel(x, c, ctx, c_ctx, w_mod, b_mod, norm1_g, norm2_g, w_in, b_in, conv_a_w, conv_a_b, w_rg_r, b_rg_r,
           w_rg_i, b_rg_i, rg_lambda, w_pool, b_pool, pool_scale, w_four, b_four, conv_d_w, conv_d_b, ln_d_g,
           ln_d_b, w_pw, b_pw, w_out, b_out, w_router, b_router, w_gu, b_gu, w_down, b_down, final_norm_g):
    bn, L, D = x.shape
    Lc = ctx.shape[1]
    assert bn == SUBLANES and D == D_MODEL

    pos = _pos_embed(L)
    c_rows = jnp.concatenate([c, jnp.broadcast_to(c_ctx[None], (MOD_ROWS - bn, D))], axis=0)
    mod = _modulation(c_rows, w_mod, b_mod)
    ctx = jnp.transpose(ctx, (1, 0, 2)).reshape(Lc * bn, D)

    cc1, sc1 = _dft_matrices(D_SUB, 1.0 / math.sqrt(D_SUB))
    eye = jnp.eye(N_SUB, dtype=F32)
    consts = dict(
        dft={n: _time_dft_tables(n) for n in sorted({L, Lc})},
        cc=jnp.kron(eye, cc1).astype(BF16), sc=jnp.kron(eye, sc1).astype(BF16),
        avg=jnp.kron(eye, jnp.full((D_SUB, D_SUB), 1.0 / D_SUB, F32)).astype(BF16))
    h_zero = jnp.zeros((2, SUBLANES, W_GROUP), F32)

    for l in range(DEPTH):
        last = l == DEPTH - 1
        p = _layer_params(l, w_in, b_in, conv_a_w, conv_a_b, w_rg_r, b_rg_r, w_rg_i, b_rg_i, rg_lambda, w_pool,
                          b_pool, pool_scale, w_four, b_four, conv_d_w, conv_d_b, ln_d_g, ln_d_b, w_pw, b_pw,
                          norm1_g)
        mod3 = mod[l]
        w_out_bf = w_out[l].astype(BF16)
        wr_t = w_router[l].T
        wr_hi = wr_t.astype(BF16)
        wr2 = jnp.concatenate([wr_hi, (wr_t - wr_hi.astype(F32)).astype(BF16)], axis=0)
        x_pos = pos if l == 0 else None

        mix_c, h_ctx = _token_mixers(ctx, None, mod3, h_zero, p, consts, True, not last)
        mix_x, _ = _token_mixers(x, x_pos, mod3, h_ctx, p, consts, False, True)
        n_ctx, final_g = (0, final_norm_g) if last else (bn * Lc, None)
        T = n_ctx + bn * L
        moe_in = None
        if not last:
            ctx, *moe_in = _out_projection(ctx, None, mix_c, mod3, norm2_g[l], w_out_bf, b_out[l], wr2,
                                           b_router[l], True, T, 0, None)
        x, hp, lg = _out_projection(x, x_pos, mix_x, mod3, norm2_g[l], w_out_bf, b_out[l], wr2, b_router[l],
                                    False, T, n_ctx, moe_in)
        half = T // 2
        moe_a = _moe(hp, lg, (0, half), l, w_gu, b_gu, w_down, b_down)
        moe_b = _moe(hp, lg, (half, T), l, w_gu, b_gu, w_down, b_down)
        if n_ctx:
            ctx = _combine(ctx, (0, n_ctx), moe_a, 0, mod3, None, True)
        split = half - n_ctx
        xa = _combine(x, (0, split), moe_a, n_ctx, mod3, final_g, False)
        x = _combine(x, (split, x.shape[0]), moe_b, 0, mod3, final_g, False, prev=xa)
    return x
```

```python
import functools
import math

import jax
import jax.numpy as jnp
from jax import lax
from jax.experimental import pallas as pl
from jax.experimental.pallas import tpu as pltpu

F32 = jnp.float32
BF16 = jnp.bfloat16

D_MODEL = 1024
DEPTH = 2
GRID_W = 64
W_GROUP = 256
N_SUB = 4
D_SUB = 64
D_IN = 6 * W_GROUP
RG_CONV = 4
RG_C = 8.0
CONF_KERNEL = 31
N_EXPERTS = 32
TOP_K = 4
D_FF = D_MODEL
SWIGLU_LIMIT = 7.0
SWIGLU_ALPHA = 1.702
EPS = 1e-6

SUBLANES = 8
VMEM_LIMIT_BYTES = 56 * 1024 * 1024
MOD_ROWS = 16
RG_HALO = 8 * SUBLANES
POOL_HALO = 8 * SUBLANES
CONF_HALO = 16 * SUBLANES
MOE_TILE = 512
MOE_SUBTILE = 256
TOK_TILE = 256
SC_WINDOW = 64
ROUTE_TILE = 1024


def _cparams(sem):
    return pltpu.CompilerParams(dimension_semantics=sem, vmem_limit_bytes=VMEM_LIMIT_BYTES)


def _full(shape):
    nd = len(shape)
    return pl.BlockSpec(shape, lambda *_: (0,) * nd)


def _dot(a, b):
    return jnp.dot(a, b, preferred_element_type=F32)


def _split_bf16(v):
    hi = v.astype(BF16)
    lo = (v - hi.astype(F32)).astype(BF16)
    return hi, lo


def _mod_body(c_ref, w_ref, b_ref, o_ref):
    c = c_ref[...]
    s = c * jax.nn.sigmoid(c)
    o_ref[...] = jnp.dot(s, w_ref[...], precision=lax.Precision.HIGHEST,
                         preferred_element_type=F32) + b_ref[...]


def _modulation(c_rows, w_mod, b_mod):
    tn = 1536
    n6 = 6 * D_MODEL
    return pl.pallas_call(
        _mod_body,
        grid=(DEPTH, n6 // tn),
        in_specs=[_full((MOD_ROWS, D_MODEL)),
                  pl.BlockSpec((None, D_MODEL, tn), lambda l, j: (l, 0, j)),
                  pl.BlockSpec((None, 1, tn), lambda l, j: (l, 0, j))],
        out_specs=pl.BlockSpec((None, MOD_ROWS, tn), lambda l, j: (l, 0, j)),
        out_shape=jax.ShapeDtypeStruct((DEPTH, MOD_ROWS, n6), F32),
        compiler_params=_cparams(("parallel", "parallel")),
        name="modulation",
    )(c_rows, w_mod, b_mod.reshape(DEPTH, 1, n6))


def _mod_spec(chunk, ctx):
    return pl.BlockSpec((SUBLANES, D_MODEL), lambda i: (1 if ctx else 0, chunk))


def _scale_rows(v, m):
    r, d = v.shape
    return (v.reshape(r // SUBLANES, SUBLANES, d) * m[None]).reshape(r, d)


def _rms_mod(x, g, shift, scale):
    r, d = x.shape
    y = x * lax.rsqrt(jnp.mean(x * x, axis=-1, keepdims=True) + EPS) * g
    y3 = y.reshape(r // SUBLANES, SUBLANES, d)
    return (y3 * (1.0 + scale)[None] + shift[None]).reshape(r, d)


def _load_plus_pos(x_ref, prow_ref, pcol_ref):
    row = prow_ref[...]
    return jnp.concatenate([x_ref[:, t, :] + jnp.concatenate([row, pcol_ref[t:t + 1, :]], axis=1)
                            for t in range(GRID_W)], axis=0)


def _x_spec(x, tr):
    if x.ndim == 3:
        return pl.BlockSpec((SUBLANES, tr // SUBLANES, x.shape[2]), lambda i: (0, i, 0))
    return pl.BlockSpec((tr, x.shape[1]), lambda i: (i, 0))


def _inproj_body(*refs, add_pos):
    if add_pos:
        (x_ref, prow_ref, pcol_ref, sh_ref, sc_ref, g_ref, w_ref, b_ref,
         xa_ref, ga_ref, xb_ref, xc_ref, xd_ref) = refs
        x = _load_plus_pos(x_ref, prow_ref, pcol_ref)
    else:
        x_ref, sh_ref, sc_ref, g_ref, w_ref, b_ref, xa_ref, ga_ref, xb_ref, xc_ref, xd_ref = refs
        x = x_ref[...]
    u = _rms_mod(x, g_ref[...], sh_ref[...], sc_ref[...])
    p = _dot(u.astype(BF16), w_ref[...]) + b_ref[...]
    xa_ref[...] = p[:, 0:256]
    ga_ref[...] = p[:, 256:512]
    xb_ref[...] = p[:, 512:768]
    xc_ref[...] = p[:, 768:1024].astype(BF16)
    xd_ref[...] = p[:, 1024:1536]


def _in_projection(x, pos, mod, norm_g, w_in_bf, b_in, ctx):
    add_pos = pos is not None
    assert add_pos == (x.ndim == 3)
    R, D = x.size // x.shape[-1], x.shape[-1]
    tr = min(R, 512)
    row = lambda w: pl.BlockSpec((tr, w), lambda i: (i, 0))
    in_specs = [_x_spec(x, tr)]
    args = [x]
    if add_pos:
        assert tr == GRID_W * SUBLANES
        in_specs += [pl.BlockSpec((None, 1, D // 2), lambda i: (i, 0, 0)), _full((GRID_W, D // 2))]
        args += list(pos)
    in_specs += [_mod_spec(0, ctx), _mod_spec(1, ctx), _full((1, D)), _full((D, D_IN)), _full((1, D_IN))]
    args += [mod, mod, norm_g.reshape(1, D), w_in_bf, b_in.reshape(1, D_IN)]
    out_shape = [jax.ShapeDtypeStruct((R, 256), F32)] * 3 + [
        jax.ShapeDtypeStruct((R, 256), BF16), jax.ShapeDtypeStruct((R, 512), F32)]
    return pl.pallas_call(
        functools.partial(_inproj_body, add_pos=add_pos),
        grid=(R // tr,),
        in_specs=in_specs,
        out_specs=[row(256), row(256), row(256), row(256), row(512)],
        out_shape=out_shape,
        compiler_params=_cparams(("parallel",)),
        name="in_projection",
    )(*args)


def _rg_gates(xc, wg, bg, lam):
    g = _dot(xc.astype(BF16), wg) + bg
    r = jax.nn.sigmoid(g[:, :W_GROUP])
    gi = jax.nn.sigmoid(g[:, W_GROUP:])
    z = -lam
    softplus = jnp.maximum(z, 0.0) + jnp.log1p(jnp.exp(-jnp.abs(z)))
    log_a = (-RG_C) * r * softplus
    a = jnp.exp(log_a)
    b = jnp.sqrt(-jnp.tanh(log_a) * (a * a + 1.0)) * (gi * xc)
    return a, b


def _rg_body(xf_ref, xfh_ref, xr_ref, xrh_ref, cw_ref, cb_ref, wg_ref, bg_ref, lam_ref, h0_ref,
             yf_ref, yb_ref, hfin_ref, af_ref, ab_ref, hc_ref, *, n, tt):
    i = pl.program_id(0)
    tr = tt * SUBLANES
    keep = RG_HALO - (RG_CONV - 1) * SUBLANES

    @pl.when(i == 0)
    def _():
        hc_ref[...] = h0_ref[...]

    halo = jnp.where(i > 0, xfh_ref[...], 0.0)
    ext = jnp.concatenate([halo[keep:], xf_ref[...]], axis=0)
    xc = cb_ref[0]
    for k in range(RG_CONV):
        xc = xc + cw_ref[0, k:k + 1, :] * ext[k * SUBLANES:k * SUBLANES + tr]
    a, b = _rg_gates(xc, wg_ref[0], bg_ref[0], lam_ref[0])
    af_ref[...] = a
    yf_ref[...] = b

    halo = jnp.where(i > 0, xrh_ref[...], 0.0)
    ext = jnp.concatenate([xr_ref[...], halo[:(RG_CONV - 1) * SUBLANES]], axis=0)
    xc = cb_ref[1]
    for k in range(RG_CONV):
        o = (RG_CONV - 1 - k) * SUBLANES
        xc = xc + cw_ref[1, k:k + 1, :] * ext[o:o + tr]
    a, b = _rg_gates(xc, wg_ref[1], bg_ref[1], lam_ref[1])
    ab_ref[...] = a
    yb_ref[...] = b

    def step(t, carry):
        hf, hb = carry
        rf = pl.multiple_of(t * SUBLANES, SUBLANES)
        hf = af_ref[pl.ds(rf, SUBLANES), :] * hf + yf_ref[pl.ds(rf, SUBLANES), :]
        yf_ref[pl.ds(rf, SUBLANES), :] = hf
        rb = pl.multiple_of((tt - 1 - t) * SUBLANES, SUBLANES)
        hb = ab_ref[pl.ds(rb, SUBLANES), :] * hb + yb_ref[pl.ds(rb, SUBLANES), :]
        yb_ref[pl.ds(rb, SUBLANES), :] = hb
        return hf, hb

    hf, hb = lax.fori_loop(0, tt, step, (hc_ref[0], hc_ref[1]), unroll=8)
    hc_ref[0] = hf
    hc_ref[1] = hb
    hfin_ref[0] = hf
    hfin_ref[1] = hb


def _rglru(xa2, conv_w, conv_b, wg_bf, bg, lam, h0, L):
    tt = min(L, 256)
    n = L // tt
    tr = tt * SUBLANES
    per = tr // RG_HALO
    last_halo = L * SUBLANES // RG_HALO - 1
    row = lambda i: (i, 0)
    rev = lambda i: (n - 1 - i, 0)
    in_specs = [
        pl.BlockSpec((tr, W_GROUP), row),
        pl.BlockSpec((RG_HALO, W_GROUP), lambda i: (jnp.maximum(i * per - 1, 0), 0)),
        pl.BlockSpec((tr, W_GROUP), rev),
        pl.BlockSpec((RG_HALO, W_GROUP), lambda i: (jnp.minimum((n - i) * per, last_halo), 0)),
        _full((2, RG_CONV, W_GROUP)), _full((2, 1, W_GROUP)), _full((2, W_GROUP, 2 * W_GROUP)),
        _full((2, 1, 2 * W_GROUP)), _full((2, 1, W_GROUP)), _full((2, SUBLANES, W_GROUP)),
    ]
    return pl.pallas_call(
        functools.partial(_rg_body, n=n, tt=tt),
        grid=(n,),
        in_specs=in_specs,
        out_specs=[pl.BlockSpec((tr, W_GROUP), row), pl.BlockSpec((tr, W_GROUP), rev),
                   _full((2, SUBLANES, W_GROUP))],
        out_shape=[jax.ShapeDtypeStruct((L * SUBLANES, W_GROUP), F32)] * 2
        + [jax.ShapeDtypeStruct((2, SUBLANES, W_GROUP), F32)],
        scratch_shapes=[pltpu.VMEM((tr, W_GROUP), F32), pltpu.VMEM((tr, W_GROUP), F32),
                        pltpu.VMEM((2, SUBLANES, W_GROUP), F32)],
        compiler_params=_cparams(("arbitrary",)),
        name="rglru",
    )(xa2, xa2, xa2, xa2, conv_w, conv_b.reshape(2, 1, W_GROUP), wg_bf, bg, lam.reshape(2, 1, W_GROUP), h0)


def _pool_body(xm_ref, xp_ref, xn_ref, w_ref, b_ref, s_ref, o_ref, *, n, tt, L):
    i = pl.program_id(0)
    tr = tt * SUBLANES
    S = SUBLANES
    xm = xm_ref[...]
    prev = jnp.where(i > 0, xp_ref[...], 0.0)
    nxt = jnp.where(i < n - 1, xn_ref[...], 0.0)
    xe = jnp.concatenate([prev, xm, nxt], axis=0)
    e = xe.shape[0]
    p2 = xe[S:e] + xe[0:e - S]
    n4 = (tt + 13) * S
    p4 = p2[0:n4] + p2[2 * S:2 * S + n4]
    n8 = (tt + 9) * S
    p8 = p4[0:n8] + p4[4 * S:4 * S + n8]
    s16 = p8[0:tr] + p8[8 * S:8 * S + tr]
    s2 = p2[7 * S:7 * S + tr]
    s4 = p4[6 * S:6 * S + tr]
    s8 = p8[4 * S:4 * S + tr]
    grp = lax.broadcasted_iota(jnp.int32, (1, W_GROUP), 1) // D_SUB
    half = jnp.left_shift(1, grp)
    t = i * tt + lax.broadcasted_iota(jnp.int32, (tr, 1), 0) // S
    cnt = (jnp.minimum(t + half, L) - jnp.maximum(t - half, 0)).astype(F32)
    s = jnp.where(grp == 0, s2, jnp.where(grp == 1, s4, jnp.where(grp == 2, s8, s16)))
    pooled = s / cnt - xm
    y = _dot(pooled.astype(BF16), w_ref[...]) + b_ref[...]
    o_ref[...] = (y * s_ref[...]).astype(BF16)


def _pool_mixer(xb2, w_bd_bf, b, scale, L):
    tt = min(L, 256)
    n = L // tt
    tr = tt * SUBLANES
    per = tr // POOL_HALO
    last_halo = L * SUBLANES // POOL_HALO - 1
    return pl.pallas_call(
        functools.partial(_pool_body, n=n, tt=tt, L=L),
        grid=(n,),
        in_specs=[pl.BlockSpec((tr, W_GROUP), lambda i: (i, 0)),
                  pl.BlockSpec((POOL_HALO, W_GROUP), lambda i: (jnp.maximum(i * per - 1, 0), 0)),
                  pl.BlockSpec((POOL_HALO, W_GROUP), lambda i: (jnp.minimum((i + 1) * per, last_halo), 0)),
                  _full((W_GROUP, W_GROUP)), _full((1, W_GROUP)), _full((1, W_GROUP))],
        out_specs=pl.BlockSpec((tr, W_GROUP), lambda i: (i, 0)),
        out_shape=jax.ShapeDtypeStruct((L * SUBLANES, W_GROUP), BF16),
        compiler_params=_cparams(("parallel",)),
        name="pool_mixer",
    )(xb2, xb2, xb2, w_bd_bf, b.reshape(1, W_GROUP), scale.reshape(1, W_GROUP))


def _fourier_body(c_ref, s_ref, xe_ref, xo_ref, cw_ref, sw_ref, cc_ref, sc_ref, w_ref, b_ref, o_ref):
    c, s = c_ref[...], s_ref[...]
    xe, xo = xe_ref[...], xo_ref[...]
    ec, es = _dot(c, xe), _dot(s, xe)
    oc, os_ = _dot(c, xo), _dot(s, xo)
    cw, sw = cw_ref[...], sw_ref[...]
    tc = cw * oc - sw * os_
    ts = cw * os_ + sw * oc
    for h, (z1, z2) in enumerate(((ec + tc, es + ts), (ec - tc, es - ts))):
        for j in range(xe.shape[1] // W_GROUP):
            sl = slice(j * W_GROUP, (j + 1) * W_GROUP)
            a_hi, a_lo = _split_bf16(z1[:, sl])
            b_hi, b_lo = _split_bf16(z2[:, sl])
            f = ((_dot(a_hi, cc_ref[...]) + _dot(a_lo, cc_ref[...]))
                 - (_dot(b_hi, sc_ref[...]) + _dot(b_lo, sc_ref[...])))
            o_ref[h, :, sl] = (_dot(f.astype(BF16), w_ref[...]) + b_ref[...]).astype(BF16)


def _fourier_mixer(xc, tables, cc, sc, w_bd_bf, b, L):
    ch, sh, cw, sw = tables
    M = L // 2
    ncol = SUBLANES * W_GROUP
    x2 = xc.reshape(M, 2 * ncol)
    nb = 1024
    tk = min(M, 256)
    out = pl.pallas_call(
        _fourier_body,
        grid=(ncol // nb, M // tk),
        in_specs=[pl.BlockSpec((tk, M), lambda j, k: (k, 0)),
                  pl.BlockSpec((tk, M), lambda j, k: (k, 0)),
                  pl.BlockSpec((M, nb), lambda j, k: (0, j)),
                  pl.BlockSpec((M, nb), lambda j, k: (0, ncol // nb + j)),
                  pl.BlockSpec((tk, 1), lambda j, k: (k, 0)),
                  pl.BlockSpec((tk, 1), lambda j, k: (k, 0)),
                  _full((W_GROUP, W_GROUP)), _full((W_GROUP, W_GROUP)), _full((W_GROUP, W_GROUP)),
                  _full((1, W_GROUP))],
        out_specs=pl.BlockSpec((2, tk, nb), lambda j, k: (0, k, j)),
        out_shape=jax.ShapeDtypeStruct((2, M, ncol), BF16),
        compiler_params=_cparams(("parallel", "parallel")),
        name="fourier_mixer",
    )(ch, sh, x2, x2, cw, sw, cc, sc, w_bd_bf, b.reshape(1, W_GROUP))
    return out.reshape(L * SUBLANES, W_GROUP)


def _time_dft_tables(L):
    M = L // 2
    ch, sh = _dft_matrices(M, 1.0 / math.sqrt(L))
    ang = jnp.arange(M, dtype=F32) * (2.0 * math.pi / L)
    return ch.astype(BF16), sh.astype(BF16), jnp.cos(ang).reshape(M, 1), jnp.sin(ang).reshape(M, 1)


def _dft_matrices(L, scale):
    f = 1 << (max(L.bit_length() - 1, 0) // 2)
    n = jnp.arange(L, dtype=jnp.int32)[None, :]

    def table(rows):
        ang = ((rows[:, None] * n) % L).astype(F32) * (2.0 * math.pi / L)
        return jnp.cos(ang), jnp.sin(ang)

    ac, as_ = table(jnp.arange(L // f, dtype=jnp.int32) * f)
    bc, bs = table(jnp.arange(f, dtype=jnp.int32))
    cos = (ac[:, None, :] * bc[None, :, :] - as_[:, None, :] * bs[None, :, :]).reshape(L, L) * scale
    sin = (as_[:, None, :] * bc[None, :, :] + ac[:, None, :] * bs[None, :, :]).reshape(L, L) * scale
    return cos, sin


def _block_diag(w):
    g, a, b = w.shape
    eye = jnp.eye(g, dtype=w.dtype)
    return (eye[:, None, :, None] * w[:, :, None, :]).reshape(g * a, g * b)


CONF_CHUNK = 64


def _conformer_body(xm_ref, xp_ref, xn_ref, cw_ref, cb_ref, lg_ref, lb_ref, avg_ref, w_ref, b_ref,
                    o_ref, v_ref, c_ref, *, n, tt):
    i = pl.program_id(0)
    tr = tt * SUBLANES
    H = CONF_HALO

    def glu(v):
        return v[:, :W_GROUP] * jax.nn.sigmoid(v[:, W_GROUP:])

    v_ref[0:H] = jnp.where(i > 0, glu(xp_ref[...]), 0.0)
    v_ref[H:H + tr] = glu(xm_ref[...])
    v_ref[H + tr:H + tr + H] = jnp.where(i < n - 1, glu(xn_ref[...]), 0.0)

    def chunk(c, carry):
        r0 = pl.multiple_of(c * CONF_CHUNK, CONF_CHUNK)
        acc = jnp.broadcast_to(cb_ref[...], (CONF_CHUNK, W_GROUP))
        for k in range(CONF_KERNEL):
            acc = acc + cw_ref[k:k + 1, :] * v_ref[pl.ds(r0 + (k + 1) * SUBLANES, CONF_CHUNK), :]
        c_ref[pl.ds(r0, CONF_CHUNK), :] = acc
        return carry

    lax.fori_loop(0, tr // CONF_CHUNK, chunk, 0)

    v = c_ref[...]
    avg = avg_ref[...]
    v_hi, v_lo = _split_bf16(v)
    mu = _dot(v_hi, avg) + _dot(v_lo, avg)
    d = v - mu
    q_hi, q_lo = _split_bf16(d * d)
    var = _dot(q_hi, avg) + _dot(q_lo, avg)
    vn = d * lax.rsqrt(var + EPS) * lg_ref[...] + lb_ref[...]
    act = vn * jax.nn.sigmoid(vn)
    o_ref[...] = (_dot(act.astype(BF16), w_ref[...]) + b_ref[...]).astype(BF16)


def _conformer(xd2, conv_w, conv_b, ln_g, ln_b, avg_bf, w_pw_bf, b_pw, L):
    tt = min(L, 256)
    n = L // tt
    tr = tt * SUBLANES
    per = tr // CONF_HALO
    last_halo = L * SUBLANES // CONF_HALO - 1
    vec = lambda a: a.reshape(1, W_GROUP)
    return pl.pallas_call(
        functools.partial(_conformer_body, n=n, tt=tt),
        grid=(n,),
        in_specs=[pl.BlockSpec((tr, 2 * W_GROUP), lambda i: (i, 0)),
                  pl.BlockSpec((CONF_HALO, 2 * W_GROUP), lambda i: (jnp.maximum(i * per - 1, 0), 0)),
                  pl.BlockSpec((CONF_HALO, 2 * W_GROUP), lambda i: (jnp.minimum((i + 1) * per, last_halo), 0)),
                  _full((CONF_KERNEL, W_GROUP)), _full((1, W_GROUP)), _full((1, W_GROUP)), _full((1, W_GROUP)),
                  _full((W_GROUP, W_GROUP)), _full((W_GROUP, W_GROUP)), _full((1, W_GROUP))],
        out_specs=pl.BlockSpec((tr, W_GROUP), lambda i: (i, 0)),
        out_shape=jax.ShapeDtypeStruct((L * SUBLANES, W_GROUP), BF16),
        scratch_shapes=[pltpu.VMEM((tr + 2 * CONF_HALO, W_GROUP), F32), pltpu.VMEM((tr, W_GROUP), F32)],
        compiler_params=_cparams(("parallel",)),
        name="conformer",
    )(xd2, xd2, xd2, conv_w, vec(conv_b), vec(ln_g), vec(ln_b), avg_bf, w_pw_bf, vec(b_pw))


def _gelu_tanh(x):
    return 0.5 * x * (1.0 + jnp.tanh(math.sqrt(2.0 / math.pi) * (x + 0.044715 * (x * x * x))))


def _pack_bf16_pairs(h_bf):
    u = pltpu.bitcast(h_bf.astype(F32), jnp.uint32)
    half = h_bf.shape[1] // 2
    return (u[:, :half] & jnp.uint32(0xFFFF0000)) | (u[:, half:] >> 16)


def _unpack_bf16_pairs(p):
    hi = pltpu.bitcast(p & jnp.uint32(0xFFFF0000), F32).astype(BF16)
    lo = pltpu.bitcast(p << 16, F32).astype(BF16)
    return hi, lo


def _outproj_body(*refs, add_pos, chained):
    refs = list(refs)
    xo_ref, hp_ref, lg_ref = refs[-3:]
    del refs[-5 if chained else -3:]
    if add_pos:
        (x_ref, prow_ref, pcol_ref, yf_ref, yb_ref, ga_ref, yp_ref, yc_ref, yd_ref, g1_ref, sh_ref, sc_ref,
         ng_ref, wo_ref, bo_ref, wr_ref, br_ref) = refs
        x = _load_plus_pos(x_ref, prow_ref, pcol_ref)
    else:
        (x_ref, yf_ref, yb_ref, ga_ref, yp_ref, yc_ref, yd_ref, g1_ref, sh_ref, sc_ref, ng_ref,
         wo_ref, bo_ref, wr_ref, br_ref) = refs
        x = x_ref[...]
    ya = (yf_ref[...] + yb_ref[...]) * _gelu_tanh(ga_ref[...])
    ycat = jnp.concatenate([ya.astype(BF16), yp_ref[...], yc_ref[...], yd_ref[...]], axis=1)
    y = _dot(ycat, wo_ref[...]) + bo_ref[...]
    xn = x + _scale_rows(y, g1_ref[...])
    xo_ref[...] = xn
    h = _rms_mod(xn, ng_ref[...], sh_ref[...], sc_ref[...])
    h_hi, h_lo = _split_bf16(h)
    nt = (((1,), (1,)), ((), ()))
    wr = wr_ref[...]
    acc = lax.dot_general(wr, h_hi, nt, preferred_element_type=F32)
    acc = acc + lax.dot_general(wr, h_lo, nt, preferred_element_type=F32)
    lg_ref[...] = acc[:N_EXPERTS] + acc[N_EXPERTS:] + br_ref[...]
    hp_ref[...] = _pack_bf16_pairs(h_hi)


def _out_projection(x, pos, mix, mod, norm_g, w_out_bf, b_out, wr2_bf, b_router, ctx, n_tok, tok0, prev):
    add_pos = pos is not None
    assert add_pos == (x.ndim == 3)
    R, D = x.size // x.shape[-1], x.shape[-1]
    tr = min(R, 512)
    row = lambda w: pl.BlockSpec((tr, w), lambda i: (i, 0))
    in_specs = [_x_spec(x, tr)]
    args = [x]
    if add_pos:
        assert tr == GRID_W * SUBLANES
        in_specs += [pl.BlockSpec((None, 1, D // 2), lambda i: (i, 0, 0)), _full((GRID_W, D // 2))]
        args += list(pos)
    in_specs += [row(W_GROUP)] * 6
    args += list(mix)
    in_specs += [_mod_spec(2, ctx), _mod_spec(3, ctx), _mod_spec(4, ctx), _full((1, D)), _full((D, D)),
                 _full((1, D)), _full((2 * N_EXPERTS, D)), _full((N_EXPERTS, 1))]
    args += [mod, mod, mod, norm_g.reshape(1, D), w_out_bf, b_out.reshape(1, D), wr2_bf,
             b_router.reshape(N_EXPERTS, 1)]
    aliases = {}
    if prev is not None:
        aliases = {len(args): 1, len(args) + 1: 2}
        in_specs += [pl.BlockSpec(memory_space=pl.ANY)] * 2
        args += list(prev)
    t0 = tok0 // tr
    assert t0 * tr == tok0
    return pl.pallas_call(
        functools.partial(_outproj_body, add_pos=add_pos, chained=prev is not None),
        grid=(R // tr,),
        in_specs=in_specs,
        out_specs=[row(D), pl.BlockSpec((tr, D // 2), lambda i: (t0 + i, 0)),
                   pl.BlockSpec((N_EXPERTS, tr), lambda i: (0, t0 + i))],
        out_shape=[jax.ShapeDtypeStruct((R, D), F32),
                   jax.ShapeDtypeStruct((n_tok, D // 2), jnp.uint32),
                   jax.ShapeDtypeStruct((N_EXPERTS, n_tok), F32)],
        input_output_aliases=aliases,
        compiler_params=_cparams(("parallel",)),
        name="out_projection",
    )(*args)


def _top4(v):
    eid = lax.broadcasted_iota(jnp.int32, v.shape, 0)
    out = []
    work = v
    for _ in range(TOP_K):
        m = jnp.max(work, axis=0, keepdims=True)
        idx = jnp.min(jnp.where(work == m, eid, N_EXPERTS), axis=0, keepdims=True)
        oh = eid == idx
        out.append((m, oh))
        work = jnp.where(oh, -jnp.inf, work)
    return out


def _count_body(lg_ref, cnt_ref):
    @pl.when(pl.program_id(0) == 0)
    def _():
        cnt_ref[...] = jnp.zeros_like(cnt_ref)

    sel = jnp.zeros(lg_ref.shape, F32)
    for _, oh in _top4(lg_ref[...]):
        sel = sel + oh.astype(F32)
    cnt_ref[...] += jnp.sum(sel, axis=1, keepdims=True)


def _route_body(lg_ref, ps_ref, tri_ref, dest_ref, gate_ref, carry_ref):
    @pl.when(pl.program_id(0) == 0)
    def _():
        carry_ref[...] = jnp.zeros_like(carry_ref)

    top = _top4(lg_ref[...])
    sel = jnp.zeros(lg_ref.shape, F32)
    for _, oh in top:
        sel = sel + oh.astype(F32)
    before = _dot(sel.astype(BF16), tri_ref[...]) + carry_ref[...] + ps_ref[...]
    m0 = top[0][0]
    es = [jnp.exp(m - m0) for m, _ in top]
    den = es[0] + es[1] + es[2] + es[3]
    for k, (_, oh) in enumerate(top):
        dest_ref[k:k + 1, :] = jnp.sum(jnp.where(oh, before, 0.0), axis=0, keepdims=True).astype(jnp.int32)
        gate_ref[k:k + 1, :] = es[k] / den
    carry_ref[...] += jnp.sum(sel, axis=1, keepdims=True)


def _routing(logits_t):
    E, T = logits_t.shape
    tt = ROUTE_TILE
    nt = T // tt
    counts = pl.pallas_call(
        _count_body,
        grid=(nt,),
        in_specs=[pl.BlockSpec((E, tt), lambda i: (0, i))],
        out_specs=_full((E, 1)),
        out_shape=jax.ShapeDtypeStruct((E, 1), F32),
        compiler_params=_cparams(("arbitrary",)),
        name="route_count",
    )(logits_t)
    cnt = counts[:, 0].astype(jnp.int32)
    padded = ((cnt + MOE_TILE - 1) // MOE_TILE) * MOE_TILE
    pend = jnp.cumsum(padded)
    pstart = pend - padded
    n_tiles = -(-(T * TOP_K) // MOE_TILE) + N_EXPERTS
    tile_start = jnp.arange(n_tiles, dtype=jnp.int32) * MOE_TILE
    tile_e = jnp.minimum(jnp.sum((pend[None, :] <= tile_start[:, None]).astype(jnp.int32), axis=1), N_EXPERTS - 1)
    n_used = (pend[-1] // MOE_TILE).astype(jnp.int32).reshape(1)
    n_valid = jnp.clip((pstart + cnt)[tile_e] - tile_start, 0, MOE_TILE).astype(jnp.int32)
    tri = (jnp.arange(tt)[:, None] < jnp.arange(tt)[None, :]).astype(BF16)
    dest, gates = pl.pallas_call(
        _route_body,
        grid=(nt,),
        in_specs=[pl.BlockSpec((E, tt), lambda i: (0, i)), _full((E, 1)), _full((tt, tt))],
        out_specs=[pl.BlockSpec((TOP_K, tt), lambda i: (0, i)), pl.BlockSpec((TOP_K, tt), lambda i: (0, i))],
        out_shape=[jax.ShapeDtypeStruct((TOP_K, T), jnp.int32), jax.ShapeDtypeStruct((TOP_K, T), F32)],
        scratch_shapes=[pltpu.VMEM((E, 1), F32)],
        compiler_params=_cparams(("arbitrary",)),
        name="route_assign",
    )(logits_t, pstart.astype(F32).reshape(E, 1), tri)
    return dest, gates, tile_e, n_valid, n_used, n_tiles


def _sc_workers():
    from jax.experimental.pallas import tpu_sc as plsc
    mesh = plsc.VectorSubcoreMesh(core_axis_name="c", subcore_axis_name="s")
    n_workers = mesh.num_cores * mesh.num_subcores
    worker = lambda: lax.axis_index("s") * mesh.num_cores + lax.axis_index("c")
    return mesh, n_workers, worker


def _dispatch(hp, tok0, dest_flat, n_rows):
    W = hp.shape[1]
    T = dest_flat.shape[0] // TOP_K
    mesh, n_workers, worker = _sc_workers()
    per = dest_flat.shape[0] // n_workers
    steps = per // SC_WINDOW
    assert per * n_workers == dest_flat.shape[0] and steps * SC_WINDOW == per and steps % 2 == 0
    assert T % SC_WINDOW == 0 and tok0 % SC_WINDOW == 0

    @functools.partial(
        pl.kernel, mesh=mesh, out_type=jax.ShapeDtypeStruct((n_rows, W), hp.dtype),
        scratch_types=[pltpu.VMEM((SC_WINDOW,), jnp.int32), pltpu.VMEM((SC_WINDOW,), jnp.int32),
                       pltpu.VMEM((SC_WINDOW, W), hp.dtype), pltpu.VMEM((SC_WINDOW, W), hp.dtype),
                       pltpu.SemaphoreType.DMA, pltpu.SemaphoreType.DMA],
        name="moe_dispatch")
    def scatter(hp_hbm, dest_hbm, xs_hbm, idx0, idx1, rows0, rows1, sem0, sem1):
        base = worker() * per
        bufs = ((idx0, rows0, sem0), (idx1, rows1, sem1))

        def window(j, b, first):
            idx_v, rows_v, sem = bufs[b]

            @pl.when(jnp.logical_not(first))
            def _():
                pltpu.make_async_copy(rows_v, xs_hbm.at[idx_v], sem).wait()

            off = pl.multiple_of(base + j * SC_WINDOW, SC_WINDOW)
            tok = pl.multiple_of(tok0 + lax.rem(off, T), SC_WINDOW)
            pltpu.sync_copy(dest_hbm.at[pl.ds(off, SC_WINDOW)], idx_v)
            pltpu.sync_copy(hp_hbm.at[pl.ds(tok, SC_WINDOW)], rows_v)
            pltpu.async_copy(rows_v, xs_hbm.at[idx_v], sem)

        @pl.loop(0, steps, step=2)
        def _(j):
            window(j, 0, j == 0)
            window(j + 1, 1, j == 0)

        for idx_v, rows_v, sem in bufs:
            pltpu.make_async_copy(rows_v, xs_hbm.at[idx_v], sem).wait()

    return scatter(hp, dest_flat)


def _gather_rows(table, idx_flat):
    n = idx_flat.shape[0]
    W = table.shape[1]
    mesh, n_workers, worker = _sc_workers()
    per = n // n_workers
    steps = per // SC_WINDOW
    assert per * n_workers == n and steps * SC_WINDOW == per and steps % 2 == 0

    @functools.partial(
        pl.kernel, mesh=mesh, out_type=jax.ShapeDtypeStruct((n, W), table.dtype),
        scratch_types=[pltpu.VMEM((SC_WINDOW,), jnp.int32), pltpu.VMEM((SC_WINDOW,), jnp.int32),
                       pltpu.VMEM((SC_WINDOW, W), table.dtype), pltpu.VMEM((SC_WINDOW, W), table.dtype),
                       pltpu.SemaphoreType.DMA, pltpu.SemaphoreType.DMA, pltpu.SemaphoreType.DMA],
        name="moe_gather")
    def gather(table_hbm, idx_hbm, out_hbm, idx0, idx1, rows0, rows1, sem0, sem1, gsem):
        base = worker() * per
        bufs = ((idx0, rows0, sem0), (idx1, rows1, sem1))

        def window(j, b, first):
            idx_v, rows_v, sem = bufs[b]
            off = pl.multiple_of(base + j * SC_WINDOW, SC_WINDOW)

            @pl.when(jnp.logical_not(first))
            def _():
                pltpu.make_async_copy(rows_v, out_hbm.at[pl.ds(off, SC_WINDOW)], sem).wait()

            pltpu.sync_copy(idx_hbm.at[pl.ds(off, SC_WINDOW)], idx_v)
            pltpu.async_copy(table_hbm.at[idx_v], rows_v, gsem).wait()
            pltpu.async_copy(rows_v, out_hbm.at[pl.ds(off, SC_WINDOW)], sem)

        @pl.loop(0, steps, step=2)
        def _(j):
            window(j, 0, j == 0)
            window(j + 1, 1, j == 0)

        for _, rows_v, sem in bufs:
            pltpu.make_async_copy(rows_v, out_hbm.at[pl.ds(base, SC_WINDOW)], sem).wait()

    return gather(table, idx_flat)


def _expert_body(te_ref, nv_ref, nu_ref, nx_ref, sl_ref, xs_ref, wgu_hbm, bgu_ref, wd_hbm, bd_ref, ys_ref,
                 wgu_f32, wd_f32, wgu_bf_ref, wd_bf_ref, sems, *, layer):
    i = pl.program_id(0)
    used = i < nu_ref[0]
    e = te_ref[i]
    s = sl_ref[i]
    new_expert = jnp.logical_or(i == 0, e != te_ref[jnp.maximum(i - 1, 0)])

    def weight_copies(expert, slot):
        return (pltpu.make_async_copy(wgu_hbm.at[layer, expert], wgu_f32.at[slot], sems.at[0, slot]),
                pltpu.make_async_copy(wd_hbm.at[layer, expert], wd_f32.at[slot], sems.at[1, slot]))

    @pl.when(jnp.logical_and(used, i == 0))
    def _():
        for c in weight_copies(e, s):
            c.start()

    @pl.when(jnp.logical_and(used, new_expert))
    def _():
        for c in weight_copies(e, s):
            c.wait()

        @pl.when(nx_ref[i] >= 0)
        def _():
            for c in weight_copies(nx_ref[i], 1 - s):
                c.start()

        wgu_bf_ref[...] = wgu_f32[s].astype(BF16)
        wd_bf_ref[...] = wd_f32[s].astype(BF16)

    def experts_on(n_rows):
        live = lax.broadcasted_iota(jnp.int32, (n_rows, 1), 0) < nv_ref[i]
        x = jnp.concatenate(_unpack_bf16_pairs(jnp.where(live, xs_ref[:n_rows, :], jnp.uint32(0))), axis=1)
        gu = _dot(x, wgu_bf_ref[...]) + bgu_ref[...]
        gt = jnp.minimum(gu[:, :D_FF], SWIGLU_LIMIT)
        up = jnp.clip(gu[:, D_FF:], -SWIGLU_LIMIT, SWIGLU_LIMIT)
        act = (up + 1.0) * (gt * jax.nn.sigmoid(SWIGLU_ALPHA * gt))
        y = _dot(act.astype(BF16), wd_bf_ref[...]) + bd_ref[...]
        ys_ref[:n_rows, :] = _pack_bf16_pairs(y.astype(BF16))

    small = nv_ref[i] <= MOE_SUBTILE

    @pl.when(jnp.logical_and(used, jnp.logical_not(small)))
    def _():
        experts_on(MOE_TILE)

    @pl.when(jnp.logical_and(used, small))
    def _():
        experts_on(MOE_SUBTILE)
        ys_ref[MOE_SUBTILE:, :] = jnp.zeros((MOE_TILE - MOE_SUBTILE, ys_ref.shape[1]), ys_ref.dtype)

    @pl.when(jnp.logical_not(used))
    def _():
        ys_ref[...] = jnp.zeros_like(ys_ref)


def _experts(xs, tile_e, n_valid, n_used, layer, w_gu, b_gu, w_down, b_down):
    n_rows, W = xs.shape
    n_tiles = n_rows // MOE_TILE
    idx = jnp.arange(n_tiles, dtype=jnp.int32)
    first = jnp.logical_and(idx < n_used[0], jnp.logical_or(idx == 0, tile_e != jnp.roll(tile_e, 1)))
    slot = ((jnp.cumsum(first.astype(jnp.int32)) + 1) % 2).astype(jnp.int32)
    later_first = lax.cummin(jnp.where(first, idx, n_tiles), reverse=True)
    nxt_idx = jnp.concatenate([later_first[1:], jnp.full((1,), n_tiles, jnp.int32)])
    nxt = jnp.where(nxt_idx < n_tiles, tile_e[jnp.minimum(nxt_idx, n_tiles - 1)], -1).astype(jnp.int32)
    bias = lambda w: pl.BlockSpec((None, None, 1, w), lambda i, te, nv, nu, nx, sl: (layer, te[i], 0, 0))
    tile = pl.BlockSpec((MOE_TILE, W), lambda i, te, nv, nu, nx, sl: (i, 0))
    grid_spec = pltpu.PrefetchScalarGridSpec(
        num_scalar_prefetch=5,
        grid=(n_tiles,),
        in_specs=[tile, pl.BlockSpec(memory_space=pl.ANY), bias(2 * D_FF),
                  pl.BlockSpec(memory_space=pl.ANY), bias(D_MODEL)],
        out_specs=tile,
        scratch_shapes=[pltpu.VMEM((2, D_MODEL, 2 * D_FF), F32), pltpu.VMEM((2, D_FF, D_MODEL), F32),
                        pltpu.VMEM((D_MODEL, 2 * D_FF), BF16), pltpu.VMEM((D_FF, D_MODEL), BF16),
                        pltpu.SemaphoreType.DMA((2, 2))],
    )
    return pl.pallas_call(
        functools.partial(_expert_body, layer=layer),
        grid_spec=grid_spec,
        out_shape=jax.ShapeDtypeStruct((n_rows, W), jnp.uint32),
        compiler_params=_cparams(("arbitrary",)),
        name="moe_experts",
    )(tile_e, n_valid, n_used, nxt, slot, xs, w_gu, b_gu.reshape(DEPTH, N_EXPERTS, 1, 2 * D_FF), w_down,
      b_down.reshape(DEPTH, N_EXPERTS, 1, D_MODEL))


def _combine_body(x_ref, y0_ref, y1_ref, y2_ref, y3_ref, gate_ref, g2_ref, *rest, final, chained):
    rest = list(rest)
    o_ref = rest.pop()
    if chained:
        rest.pop()
    g = gate_ref[...]
    acc_hi = acc_lo = None
    for k, y_ref in enumerate((y0_ref, y1_ref, y2_ref, y3_ref)):
        p = y_ref[...]
        gk = g[:, k:k + 1]
        hi = gk * pltpu.bitcast(p & jnp.uint32(0xFFFF0000), F32)
        lo = gk * pltpu.bitcast(p << 16, F32)
        acc_hi = hi if acc_hi is None else acc_hi + hi
        acc_lo = lo if acc_lo is None else acc_lo + lo
    acc = jnp.concatenate([acc_hi, acc_lo], axis=1)
    xn = x_ref[...] + _scale_rows(acc, g2_ref[...])
    if final:
        xn = xn * lax.rsqrt(jnp.mean(xn * xn, axis=-1, keepdims=True) + EPS) * rest[0][...]
        for t in range(o_ref.shape[1]):
            o_ref[:, t, :] = xn[t * SUBLANES:(t + 1) * SUBLANES, :]
    else:
        o_ref[...] = xn


def _combine(x, rows, moe_out, tok_off, mod, final_g, ctx, prev=None):
    ysg, gates_t = moe_out
    R, D = x.shape
    tl = TOK_TILE
    r0 = rows[0] // tl
    t0 = tok_off // tl
    nt = gates_t.shape[0] // tl
    final = final_g is not None
    y_specs = [pl.BlockSpec((tl, D // 2), lambda i, k=k: (k * nt + t0 + i, 0)) for k in range(TOP_K)]
    in_specs = ([pl.BlockSpec((tl, D), lambda i: (r0 + i, 0))] + y_specs
                + [pl.BlockSpec((tl, TOP_K), lambda i: (t0 + i, 0)), _mod_spec(5, ctx)])
    args = [x, ysg, ysg, ysg, ysg, gates_t, mod]
    if final:
        in_specs.append(_full((1, D)))
        args.append(final_g.reshape(1, D))
        out_spec = pl.BlockSpec((SUBLANES, tl // SUBLANES, D), lambda i: (0, r0 + i, 0))
        out_shape = jax.ShapeDtypeStruct((SUBLANES, R // SUBLANES, D), F32)
    else:
        out_spec = pl.BlockSpec((tl, D), lambda i: (r0 + i, 0))
        out_shape = jax.ShapeDtypeStruct((R, D), F32)
    aliases = {}
    if prev is not None:
        aliases = {len(args): 0}
        in_specs.append(pl.BlockSpec(memory_space=pl.ANY))
        args.append(prev)
    return pl.pallas_call(
        functools.partial(_combine_body, final=final, chained=prev is not None),
        grid=((rows[1] - rows[0]) // tl,),
        in_specs=in_specs,
        out_specs=out_spec,
        out_shape=out_shape,
        input_output_aliases=aliases,
        compiler_params=_cparams(("parallel",)),
        name="moe_combine",
    )(*args)


def _moe(hp, logits_t, toks, layer, w_gu, b_gu, w_down, b_down):
    dest, gates, tile_e, n_valid, n_used, n_tiles = _routing(logits_t[:, toks[0]:toks[1]])
    dest_flat = dest.reshape(-1)
    xs = _dispatch(hp, toks[0], dest_flat, n_tiles * MOE_TILE)
    ys = _experts(xs, tile_e, n_valid, n_used, layer, w_gu, b_gu, w_down, b_down)
    return _gather_rows(ys, dest_flat), gates.T


def _token_mixers(x, pos, mod, h0, p, consts, ctx, need_out):
    R = x.size // x.shape[-1]
    L = R // SUBLANES
    xa, ga, xb, xc, xd = _in_projection(x, pos, mod, p["norm1_g"], p["w_in"], p["b_in"], ctx)
    yf, yb, hfin = _rglru(xa, p["conv_a_w"], p["conv_a_b"], p["wg"], p["bg"], p["rg_lambda"], h0, L)
    if not need_out:
        return None, hfin
    yp = _pool_mixer(xb, p["w_pool"], p["b_pool"], p["pool_scale"], L)
    yc = _fourier_mixer(xc, consts["dft"][L], consts["cc"], consts["sc"], p["w_four"], p["b_four"], L)
    yd = _conformer(xd, p["conv_d_w"], p["conv_d_b"], p["ln_d_g"], p["ln_d_b"], consts["avg"], p["w_pw"],
                    p["b_pw"], L)
    return (yf, yb, ga, yp, yc, yd), hfin


def _pos_embed(n_tokens):
    rows_n = n_tokens // GRID_W
    q = D_MODEL // 4
    omega = 1.0 / (10000.0 ** (jnp.arange(q, dtype=F32) / q))

    def emb(n):
        ang = jnp.arange(n, dtype=F32)[:, None] * omega[None, :]
        return jnp.concatenate([jnp.sin(ang), jnp.cos(ang)], axis=-1)

    return emb(rows_n).reshape(rows_n, 1, D_MODEL // 2), emb(GRID_W)


def _layer_params(l, w_in, b_in, conv_a_w, conv_a_b, w_rg_r, b_rg_r, w_rg_i, b_rg_i, rg_lambda, w_pool, b_pool,
                  pool_scale, w_four, b_four, conv_d_w, conv_d_b, ln_d_g, ln_d_b, w_pw, b_pw, norm1_g):
    wg = jnp.stack([jnp.concatenate([_block_diag(w_rg_r[l, d]), _block_diag(w_rg_i[l, d])], axis=1)
                    for d in range(2)]).astype(BF16)
    bg = jnp.concatenate([b_rg_r[l].reshape(2, 1, W_GROUP), b_rg_i[l].reshape(2, 1, W_GROUP)], axis=-1)
    return dict(
        norm1_g=norm1_g[l], w_in=w_in[l].astype(BF16), b_in=b_in[l],
        conv_a_w=conv_a_w[l], conv_a_b=conv_a_b[l], wg=wg, bg=bg, rg_lambda=rg_lambda[l],
        w_pool=_block_diag(w_pool[l]).astype(BF16), b_pool=b_pool[l], pool_scale=pool_scale[l],
        w_four=_block_diag(w_four[l]).astype(BF16), b_four=b_four[l],
        conv_d_w=conv_d_w[l], conv_d_b=conv_d_b[l], ln_d_g=ln_d_g[l], ln_d_b=ln_d_b[l],
        w_pw=w_pw[l].astype(BF16), b_pw=b_pw[l])


def kernel(x, c, ctx, c_ctx, w_mod, b_mod, norm1_g, norm2_g, w_in, b_in, conv_a_w, conv_a_b, w_rg_r, b_rg_r,
           w_rg_i, b_rg_i, rg_lambda, w_pool, b_pool, pool_scale, w_four, b_four, conv_d_w, conv_d_b, ln_d_g,
           ln_d_b, w_pw, b_pw, w_out, b_out, w_router, b_router, w_gu, b_gu, w_down, b_down, final_norm_g):
    bn, L, D = x.shape
    Lc = ctx.shape[1]
    assert bn == SUBLANES and D == D_MODEL

    pos = _pos_embed(L)
    c_rows = jnp.concatenate([c, jnp.broadcast_to(c_ctx[None], (MOD_ROWS - bn, D))], axis=0)
    mod = _modulation(c_rows, w_mod, b_mod)
    ctx = jnp.transpose(ctx, (1, 0, 2)).reshape(Lc * bn, D)

    cc1, sc1 = _dft_matrices(D_SUB, 1.0 / math.sqrt(D_SUB))
    eye = jnp.eye(N_SUB, dtype=F32)
    consts = dict(
        dft={n: _time_dft_tables(n) for n in sorted({L, Lc})},
        cc=jnp.kron(eye, cc1).astype(BF16), sc=jnp.kron(eye, sc1).astype(BF16),
        avg=jnp.kron(eye, jnp.full((D_SUB, D_SUB), 1.0 / D_SUB, F32)).astype(BF16))
    h_zero = jnp.zeros((2, SUBLANES, W_GROUP), F32)

    for l in range(DEPTH):
        last = l == DEPTH - 1
        p = _layer_params(l, w_in, b_in, conv_a_w, conv_a_b, w_rg_r, b_rg_r, w_rg_i, b_rg_i, rg_lambda, w_pool,
                          b_pool, pool_scale, w_four, b_four, conv_d_w, conv_d_b, ln_d_g, ln_d_b, w_pw, b_pw,
                          norm1_g)
        mod3 = mod[l]
        w_out_bf = w_out[l].astype(BF16)
        wr_t = w_router[l].T
        wr_hi = wr_t.astype(BF16)
        wr2 = jnp.concatenate([wr_hi, (wr_t - wr_hi.astype(F32)).astype(BF16)], axis=0)
        x_pos = pos if l == 0 else None

        mix_c, h_ctx = _token_mixers(ctx, None, mod3, h_zero, p, consts, True, not last)
        mix_x, _ = _token_mixers(x, x_pos, mod3, h_ctx, p, consts, False, True)
        n_ctx, final_g = (0, final_norm_g) if last else (bn * Lc, None)
        T = n_ctx + bn * L
        moe_in = None
        if not last:
            ctx, *moe_in = _out_projection(ctx, None, mix_c, mod3, norm2_g[l], w_out_bf, b_out[l], wr2,
                                           b_router[l], True, T, 0, None)
        x, hp, lg = _out_projection(x, x_pos, mix_x, mod3, norm2_g[l], w_out_bf, b_out[l], wr2, b_router[l],
                                    False, T, n_ctx, moe_in)
        half = T // 2
        moe_a = _moe(hp, lg, (0, half), l, w_gu, b_gu, w_down, b_down)
        moe_b = _moe(hp, lg, (half, T), l, w_gu, b_gu, w_down, b_down)
        if n_ctx:
            ctx = _combine(ctx, (0, n_ctx), moe_a, 0, mod3, None, True)
        split = half - n_ctx
        xa = _combine(x, (0, split), moe_a, n_ctx, mod3, final_g, False)
        x = _combine(x, (split, x.shape[0]), moe_b, 0, mod3, final_g, False, prev=xa)
    return x
```

```python
import functools
import math

import jax
import jax.numpy as jnp
from jax import lax
from jax.experimental import pallas as pl
from jax.experimental.pallas import tpu as pltpu

F32 = jnp.float32
BF16 = jnp.bfloat16

D_MODEL = 1024
DEPTH = 2
GRID_W = 64
W_GROUP = 256
N_SUB = 4
D_SUB = 64
D_IN = 6 * W_GROUP
RG_CONV = 4
RG_C = 8.0
CONF_KERNEL = 31
N_EXPERTS = 32
TOP_K = 4
D_FF = D_MODEL
SWIGLU_LIMIT = 7.0
SWIGLU_ALPHA = 1.702
EPS = 1e-6

SUBLANES = 8
VMEM_LIMIT_BYTES = 56 * 1024 * 1024
MOD_ROWS = 16
RG_HALO = 8 * SUBLANES
POOL_HALO = 8 * SUBLANES
CONF_HALO = 16 * SUBLANES
MOE_TILE = 512
MOE_SUBTILE = 256
TOK_TILE = 256
SC_WINDOW = 64
ROUTE_TILE = 1024


def _cparams(sem):
    return pltpu.CompilerParams(dimension_semantics=sem, vmem_limit_bytes=VMEM_LIMIT_BYTES)


def _full(shape):
    nd = len(shape)
    return pl.BlockSpec(shape, lambda *_: (0,) * nd)


def _dot(a, b):
    return jnp.dot(a, b, preferred_element_type=F32)


def _split_bf16(v):
    hi = v.astype(BF16)
    lo = (v - hi.astype(F32)).astype(BF16)
    return hi, lo


def _mod_body(c_ref, w_ref, b_ref, o_ref):
    c = c_ref[...]
    s = c * jax.nn.sigmoid(c)
    o_ref[...] = jnp.dot(s, w_ref[...], precision=lax.Precision.HIGHEST,
                         preferred_element_type=F32) + b_ref[...]


def _modulation(c_rows, w_mod, b_mod):
    tn = 1536
    n6 = 6 * D_MODEL
    return pl.pallas_call(
        _mod_body,
        grid=(DEPTH, n6 // tn),
        in_specs=[_full((MOD_ROWS, D_MODEL)),
                  pl.BlockSpec((None, D_MODEL, tn), lambda l, j: (l, 0, j)),
                  pl.BlockSpec((None, 1, tn), lambda l, j: (l, 0, j))],
        out_specs=pl.BlockSpec((None, MOD_ROWS, tn), lambda l, j: (l, 0, j)),
        out_shape=jax.ShapeDtypeStruct((DEPTH, MOD_ROWS, n6), F32),
        compiler_params=_cparams(("parallel", "parallel")),
        name="modulation",
    )(c_rows, w_mod, b_mod.reshape(DEPTH, 1, n6))


def _mod_spec(chunk, ctx):
    return pl.BlockSpec((SUBLANES, D_MODEL), lambda i: (1 if ctx else 0, chunk))


def _scale_rows(v, m):
    r, d = v.shape
    return (v.reshape(r // SUBLANES, SUBLANES, d) * m[None]).reshape(r, d)


def _rms_mod(x, g, shift, scale):
    r, d = x.shape
    y = x * lax.rsqrt(jnp.mean(x * x, axis=-1, keepdims=True) + EPS) * g
    y3 = y.reshape(r // SUBLANES, SUBLANES, d)
    return (y3 * (1.0 + scale)[None] + shift[None]).reshape(r, d)


def _load_plus_pos(x_ref, prow_ref, pcol_ref):
    row = prow_ref[...]
    return jnp.concatenate([x_ref[:, t, :] + jnp.concatenate([row, pcol_ref[t:t + 1, :]], axis=1)
                            for t in range(GRID_W)], axis=0)


def _x_spec(x, tr):
    if x.ndim == 3:
        return pl.BlockSpec((SUBLANES, tr // SUBLANES, x.shape[2]), lambda i: (0, i, 0))
    return pl.BlockSpec((tr, x.shape[1]), lambda i: (i, 0))


def _inproj_body(*refs, add_pos):
    if add_pos:
        (x_ref, prow_ref, pcol_ref, sh_ref, sc_ref, g_ref, w_ref, b_ref,
         xa_ref, ga_ref, xb_ref, xc_ref, xd_ref) = refs
        x = _load_plus_pos(x_ref, prow_ref, pcol_ref)
    else:
        x_ref, sh_ref, sc_ref, g_ref, w_ref, b_ref, xa_ref, ga_ref, xb_ref, xc_ref, xd_ref = refs
        x = x_ref[...]
    u = _rms_mod(x, g_ref[...], sh_ref[...], sc_ref[...])
    p = _dot(u.astype(BF16), w_ref[...]) + b_ref[...]
    xa_ref[...] = p[:, 0:256].astype(BF16)
    ga_ref[...] = p[:, 256:512].astype(BF16)
    xb_ref[...] = p[:, 512:768].astype(BF16)
    xc_ref[...] = p[:, 768:1024].astype(BF16)
    xd_ref[...] = p[:, 1024:1536].astype(BF16)


def _in_projection(x, pos, mod, norm_g, w_in_bf, b_in, ctx):
    add_pos = pos is not None
    assert add_pos == (x.ndim == 3)
    R, D = x.size // x.shape[-1], x.shape[-1]
    tr = min(R, 512)
    row = lambda w: pl.BlockSpec((tr, w), lambda i: (i, 0))
    in_specs = [_x_spec(x, tr)]
    args = [x]
    if add_pos:
        assert tr == GRID_W * SUBLANES
        in_specs += [pl.BlockSpec((None, 1, D // 2), lambda i: (i, 0, 0)), _full((GRID_W, D // 2))]
        args += list(pos)
    in_specs += [_mod_spec(0, ctx), _mod_spec(1, ctx), _full((1, D)), _full((D, D_IN)), _full((1, D_IN))]
    args += [mod, mod, norm_g.reshape(1, D), w_in_bf, b_in.reshape(1, D_IN)]
    out_shape = [jax.ShapeDtypeStruct((R, 256), BF16)] * 4 + [jax.ShapeDtypeStruct((R, 512), BF16)]
    return pl.pallas_call(
        functools.partial(_inproj_body, add_pos=add_pos),
        grid=(R // tr,),
        in_specs=in_specs,
        out_specs=[row(256), row(256), row(256), row(256), row(512)],
        out_shape=out_shape,
        compiler_params=_cparams(("parallel",)),
        name="in_projection",
    )(*args)


def _rg_gates(xc, wg, bg, lam):
    g = _dot(xc.astype(BF16), wg) + bg
    r = jax.nn.sigmoid(g[:, :W_GROUP])
    gi = jax.nn.sigmoid(g[:, W_GROUP:])
    z = -lam
    softplus = jnp.maximum(z, 0.0) + jnp.log1p(jnp.exp(-jnp.abs(z)))
    log_a = (-RG_C) * r * softplus
    a = jnp.exp(log_a)
    b = jnp.sqrt(-jnp.tanh(log_a) * (a * a + 1.0)) * (gi * xc)
    return a, b


def _rg_body(xf_ref, xfh_ref, xr_ref, xrh_ref, cw_ref, cb_ref, wg_ref, bg_ref, lam_ref, h0_ref,
             yf_ref, yb_ref, hfin_ref, af_ref, ab_ref, hf_ref, hb_ref, hc_ref, *, n, tt):
    i = pl.program_id(0)
    tr = tt * SUBLANES
    keep = RG_HALO - (RG_CONV - 1) * SUBLANES

    @pl.when(i == 0)
    def _():
        hc_ref[...] = h0_ref[...]

    halo = jnp.where(i > 0, xfh_ref[...].astype(F32), 0.0)
    ext = jnp.concatenate([halo[keep:], xf_ref[...].astype(F32)], axis=0)
    xc = cb_ref[0]
    for k in range(RG_CONV):
        xc = xc + cw_ref[0, k:k + 1, :] * ext[k * SUBLANES:k * SUBLANES + tr]
    a, b = _rg_gates(xc, wg_ref[0], bg_ref[0], lam_ref[0])
    af_ref[...] = a
    hf_ref[...] = b

    halo = jnp.where(i > 0, xrh_ref[...].astype(F32), 0.0)
    ext = jnp.concatenate([xr_ref[...].astype(F32), halo[:(RG_CONV - 1) * SUBLANES]], axis=0)
    xc = cb_ref[1]
    for k in range(RG_CONV):
        o = (RG_CONV - 1 - k) * SUBLANES
        xc = xc + cw_ref[1, k:k + 1, :] * ext[o:o + tr]
    a, b = _rg_gates(xc, wg_ref[1], bg_ref[1], lam_ref[1])
    ab_ref[...] = a
    hb_ref[...] = b

    def step(t, carry):
        hf, hb = carry
        rf = pl.multiple_of(t * SUBLANES, SUBLANES)
        hf = af_ref[pl.ds(rf, SUBLANES), :] * hf + hf_ref[pl.ds(rf, SUBLANES), :]
        hf_ref[pl.ds(rf, SUBLANES), :] = hf
        rb = pl.multiple_of((tt - 1 - t) * SUBLANES, SUBLANES)
        hb = ab_ref[pl.ds(rb, SUBLANES), :] * hb + hb_ref[pl.ds(rb, SUBLANES), :]
        hb_ref[pl.ds(rb, SUBLANES), :] = hb
        return hf, hb

    hf, hb = lax.fori_loop(0, tt, step, (hc_ref[0], hc_ref[1]), unroll=8)
    yf_ref[...] = hf_ref[...].astype(BF16)
    yb_ref[...] = hb_ref[...].astype(BF16)
    hc_ref[0] = hf
    hc_ref[1] = hb
    hfin_ref[0] = hf
    hfin_ref[1] = hb


def _rglru(xa2, conv_w, conv_b, wg_bf, bg, lam, h0, L):
    tt = min(L, 256)
    n = L // tt
    tr = tt * SUBLANES
    per = tr // RG_HALO
    last_halo = L * SUBLANES // RG_HALO - 1
    row = lambda i: (i, 0)
    rev = lambda i: (n - 1 - i, 0)
    in_specs = [
        pl.BlockSpec((tr, W_GROUP), row),
        pl.BlockSpec((RG_HALO, W_GROUP), lambda i: (jnp.maximum(i * per - 1, 0), 0)),
        pl.BlockSpec((tr, W_GROUP), rev),
        pl.BlockSpec((RG_HALO, W_GROUP), lambda i: (jnp.minimum((n - i) * per, last_halo), 0)),
        _full((2, RG_CONV, W_GROUP)), _full((2, 1, W_GROUP)), _full((2, W_GROUP, 2 * W_GROUP)),
        _full((2, 1, 2 * W_GROUP)), _full((2, 1, W_GROUP)), _full((2, SUBLANES, W_GROUP)),
    ]
    return pl.pallas_call(
        functools.partial(_rg_body, n=n, tt=tt),
        grid=(n,),
        in_specs=in_specs,
        out_specs=[pl.BlockSpec((tr, W_GROUP), row), pl.BlockSpec((tr, W_GROUP), rev),
                   _full((2, SUBLANES, W_GROUP))],
        out_shape=[jax.ShapeDtypeStruct((L * SUBLANES, W_GROUP), BF16)] * 2
        + [jax.ShapeDtypeStruct((2, SUBLANES, W_GROUP), F32)],
        scratch_shapes=[pltpu.VMEM((tr, W_GROUP), F32)] * 4 + [pltpu.VMEM((2, SUBLANES, W_GROUP), F32)],
        compiler_params=_cparams(("arbitrary",)),
        name="rglru",
    )(xa2, xa2, xa2, xa2, conv_w, conv_b.reshape(2, 1, W_GROUP), wg_bf, bg, lam.reshape(2, 1, W_GROUP), h0)


def _pool_body(xm_ref, xp_ref, xn_ref, w_ref, b_ref, s_ref, o_ref, *, n, tt, L):
    i = pl.program_id(0)
    tr = tt * SUBLANES
    S = SUBLANES
    xm = xm_ref[...].astype(F32)
    prev = jnp.where(i > 0, xp_ref[...].astype(F32), 0.0)
    nxt = jnp.where(i < n - 1, xn_ref[...].astype(F32), 0.0)
    xe = jnp.concatenate([prev, xm, nxt], axis=0)
    e = xe.shape[0]
    p2 = xe[S:e] + xe[0:e - S]
    n4 = (tt + 13) * S
    p4 = p2[0:n4] + p2[2 * S:2 * S + n4]
    n8 = (tt + 9) * S
    p8 = p4[0:n8] + p4[4 * S:4 * S + n8]
    s16 = p8[0:tr] + p8[8 * S:8 * S + tr]
    s2 = p2[7 * S:7 * S + tr]
    s4 = p4[6 * S:6 * S + tr]
    s8 = p8[4 * S:4 * S + tr]
    grp = lax.broadcasted_iota(jnp.int32, (1, W_GROUP), 1) // D_SUB
    half = jnp.left_shift(1, grp)
    t = i * tt + lax.broadcasted_iota(jnp.int32, (tr, 1), 0) // S
    cnt = (jnp.minimum(t + half, L) - jnp.maximum(t - half, 0)).astype(F32)
    s = jnp.where(grp == 0, s2, jnp.where(grp == 1, s4, jnp.where(grp == 2, s8, s16)))
    pooled = s / cnt - xm
    y = _dot(pooled.astype(BF16), w_ref[...]) + b_ref[...]
    o_ref[...] = (y * s_ref[...]).astype(BF16)


def _pool_mixer(xb2, w_bd_bf, b, scale, L):
    tt = min(L, 256)
    n = L // tt
    tr = tt * SUBLANES
    per = tr // POOL_HALO
    last_halo = L * SUBLANES // POOL_HALO - 1
    return pl.pallas_call(
        functools.partial(_pool_body, n=n, tt=tt, L=L),
        grid=(n,),
        in_specs=[pl.BlockSpec((tr, W_GROUP), lambda i: (i, 0)),
                  pl.BlockSpec((POOL_HALO, W_GROUP), lambda i: (jnp.maximum(i * per - 1, 0), 0)),
                  pl.BlockSpec((POOL_HALO, W_GROUP), lambda i: (jnp.minimum((i + 1) * per, last_halo), 0)),
                  _full((W_GROUP, W_GROUP)), _full((1, W_GROUP)), _full((1, W_GROUP))],
        out_specs=pl.BlockSpec((tr, W_GROUP), lambda i: (i, 0)),
        out_shape=jax.ShapeDtypeStruct((L * SUBLANES, W_GROUP), BF16),
        compiler_params=_cparams(("parallel",)),
        name="pool_mixer",
    )(xb2, xb2, xb2, w_bd_bf, b.reshape(1, W_GROUP), scale.reshape(1, W_GROUP))


def _fourier_body(c_ref, s_ref, xe_ref, xo_ref, cw_ref, sw_ref, cc_ref, sc_ref, w_ref, b_ref, o_ref):
    c, s = c_ref[...], s_ref[...]
    xe, xo = xe_ref[...], xo_ref[...]
    ec, es = _dot(c, xe), _dot(s, xe)
    oc, os_ = _dot(c, xo), _dot(s, xo)
    cw, sw = cw_ref[...], sw_ref[...]
    tc = cw * oc - sw * os_
    ts = cw * os_ + sw * oc
    for h, (z1, z2) in enumerate(((ec + tc, es + ts), (ec - tc, es - ts))):
        for j in range(xe.shape[1] // W_GROUP):
            sl = slice(j * W_GROUP, (j + 1) * W_GROUP)
            a_hi, a_lo = _split_bf16(z1[:, sl])
            b_hi, b_lo = _split_bf16(z2[:, sl])
            f = ((_dot(a_hi, cc_ref[...]) + _dot(a_lo, cc_ref[...]))
                 - (_dot(b_hi, sc_ref[...]) + _dot(b_lo, sc_ref[...])))
            o_ref[h, :, sl] = (_dot(f.astype(BF16), w_ref[...]) + b_ref[...]).astype(BF16)


def _fourier_mixer(xc, tables, cc, sc, w_bd_bf, b, L):
    ch, sh, cw, sw = tables
    M = L // 2
    ncol = SUBLANES * W_GROUP
    x2 = xc.reshape(M, 2 * ncol)
    nb = 1024
    tk = min(M, 256)
    out = pl.pallas_call(
        _fourier_body,
        grid=(ncol // nb, M // tk),
        in_specs=[pl.BlockSpec((tk, M), lambda j, k: (k, 0)),
                  pl.BlockSpec((tk, M), lambda j, k: (k, 0)),
                  pl.BlockSpec((M, nb), lambda j, k: (0, j)),
                  pl.BlockSpec((M, nb), lambda j, k: (0, ncol // nb + j)),
                  pl.BlockSpec((tk, 1), lambda j, k: (k, 0)),
                  pl.BlockSpec((tk, 1), lambda j, k: (k, 0)),
                  _full((W_GROUP, W_GROUP)), _full((W_GROUP, W_GROUP)), _full((W_GROUP, W_GROUP)),
                  _full((1, W_GROUP))],
        out_specs=pl.BlockSpec((2, tk, nb), lambda j, k: (0, k, j)),
        out_shape=jax.ShapeDtypeStruct((2, M, ncol), BF16),
        compiler_params=_cparams(("parallel", "parallel")),
        name="fourier_mixer",
    )(ch, sh, x2, x2, cw, sw, cc, sc, w_bd_bf, b.reshape(1, W_GROUP))
    return out.reshape(L * SUBLANES, W_GROUP)


def _time_dft_tables(L):
    M = L // 2
    ch, sh = _dft_matrices(M, 1.0 / math.sqrt(L))
    ang = jnp.arange(M, dtype=F32) * (2.0 * math.pi / L)
    return ch.astype(BF16), sh.astype(BF16), jnp.cos(ang).reshape(M, 1), jnp.sin(ang).reshape(M, 1)


def _dft_matrices(L, scale):
    f = 1 << (max(L.bit_length() - 1, 0) // 2)
    n = jnp.arange(L, dtype=jnp.int32)[None, :]

    def table(rows):
        ang = ((rows[:, None] * n) % L).astype(F32) * (2.0 * math.pi / L)
        return jnp.cos(ang), jnp.sin(ang)

    ac, as_ = table(jnp.arange(L // f, dtype=jnp.int32) * f)
    bc, bs = table(jnp.arange(f, dtype=jnp.int32))
    cos = (ac[:, None, :] * bc[None, :, :] - as_[:, None, :] * bs[None, :, :]).reshape(L, L) * scale
    sin = (as_[:, None, :] * bc[None, :, :] + ac[:, None, :] * bs[None, :, :]).reshape(L, L) * scale
    return cos, sin


def _block_diag(w):
    g, a, b = w.shape
    eye = jnp.eye(g, dtype=w.dtype)
    return (eye[:, None, :, None] * w[:, :, None, :]).reshape(g * a, g * b)


CONF_CHUNK = 64


def _conformer_body(xm_ref, xp_ref, xn_ref, cw_ref, cb_ref, lg_ref, lb_ref, avg_ref, w_ref, b_ref,
                    o_ref, v_ref, c_ref, *, n, tt):
    i = pl.program_id(0)
    tr = tt * SUBLANES
    H = CONF_HALO

    def glu(v):
        v = v.astype(F32)
        return v[:, :W_GROUP] * jax.nn.sigmoid(v[:, W_GROUP:])

    v_ref[0:H] = jnp.where(i > 0, glu(xp_ref[...]), 0.0)
    v_ref[H:H + tr] = glu(xm_ref[...])
    v_ref[H + tr:H + tr + H] = jnp.where(i < n - 1, glu(xn_ref[...]), 0.0)

    def chunk(c, carry):
        r0 = pl.multiple_of(c * CONF_CHUNK, CONF_CHUNK)
        acc = jnp.broadcast_to(cb_ref[...], (CONF_CHUNK, W_GROUP))
        for k in range(CONF_KERNEL):
            acc = acc + cw_ref[k:k + 1, :] * v_ref[pl.ds(r0 + (k + 1) * SUBLANES, CONF_CHUNK), :]
        c_ref[pl.ds(r0, CONF_CHUNK), :] = acc
        return carry

    lax.fori_loop(0, tr // CONF_CHUNK, chunk, 0)

    v = c_ref[...]
    avg = avg_ref[...]
    v_hi, v_lo = _split_bf16(v)
    mu = _dot(v_hi, avg) + _dot(v_lo, avg)
    d = v - mu
    q_hi, q_lo = _split_bf16(d * d)
    var = _dot(q_hi, avg) + _dot(q_lo, avg)
    vn = d * lax.rsqrt(var + EPS) * lg_ref[...] + lb_ref[...]
    act = vn * jax.nn.sigmoid(vn)
    o_ref[...] = (_dot(act.astype(BF16), w_ref[...]) + b_ref[...]).astype(BF16)


def _conformer(xd2, conv_w, conv_b, ln_g, ln_b, avg_bf, w_pw_bf, b_pw, L):
    tt = min(L, 256)
    n = L // tt
    tr = tt * SUBLANES
    per = tr // CONF_HALO
    last_halo = L * SUBLANES // CONF_HALO - 1
    vec = lambda a: a.reshape(1, W_GROUP)
    return pl.pallas_call(
        functools.partial(_conformer_body, n=n, tt=tt),
        grid=(n,),
        in_specs=[pl.BlockSpec((tr, 2 * W_GROUP), lambda i: (i, 0)),
                  pl.BlockSpec((CONF_HALO, 2 * W_GROUP), lambda i: (jnp.maximum(i * per - 1, 0), 0)),
                  pl.BlockSpec((CONF_HALO, 2 * W_GROUP), lambda i: (jnp.minimum((i + 1) * per, last_halo), 0)),
                  _full((CONF_KERNEL, W_GROUP)), _full((1, W_GROUP)), _full((1, W_GROUP)), _full((1, W_GROUP)),
                  _full((W_GROUP, W_GROUP)), _full((W_GROUP, W_GROUP)), _full((1, W_GROUP))],
        out_specs=pl.BlockSpec((tr, W_GROUP), lambda i: (i, 0)),
        out_shape=jax.ShapeDtypeStruct((L * SUBLANES, W_GROUP), BF16),
        scratch_shapes=[pltpu.VMEM((tr + 2 * CONF_HALO, W_GROUP), F32), pltpu.VMEM((tr, W_GROUP), F32)],
        compiler_params=_cparams(("parallel",)),
        name="conformer",
    )(xd2, xd2, xd2, conv_w, vec(conv_b), vec(ln_g), vec(ln_b), avg_bf, w_pw_bf, vec(b_pw))


def _gelu_tanh(x):
    return 0.5 * x * (1.0 + jnp.tanh(math.sqrt(2.0 / math.pi) * (x + 0.044715 * (x * x * x))))


def _pack_bf16_pairs(h_bf):
    u = pltpu.bitcast(h_bf.astype(F32), jnp.uint32)
    half = h_bf.shape[1] // 2
    return (u[:, :half] & jnp.uint32(0xFFFF0000)) | (u[:, half:] >> 16)


def _unpack_bf16_pairs(p):
    hi = pltpu.bitcast(p & jnp.uint32(0xFFFF0000), F32).astype(BF16)
    lo = pltpu.bitcast(p << 16, F32).astype(BF16)
    return hi, lo


def _outproj_body(*refs, add_pos, chained):
    refs = list(refs)
    xo_ref, hp_ref, lg_ref = refs[-3:]
    del refs[-5 if chained else -3:]
    if add_pos:
        (x_ref, prow_ref, pcol_ref, yf_ref, yb_ref, ga_ref, yp_ref, yc_ref, yd_ref, g1_ref, sh_ref, sc_ref,
         ng_ref, wo_ref, bo_ref, wr_ref, br_ref) = refs
        x = _load_plus_pos(x_ref, prow_ref, pcol_ref)
    else:
        (x_ref, yf_ref, yb_ref, ga_ref, yp_ref, yc_ref, yd_ref, g1_ref, sh_ref, sc_ref, ng_ref,
         wo_ref, bo_ref, wr_ref, br_ref) = refs
        x = x_ref[...]
    ya = (yf_ref[...].astype(F32) + yb_ref[...].astype(F32)) * _gelu_tanh(ga_ref[...].astype(F32))
    ycat = jnp.concatenate([ya.astype(BF16), yp_ref[...], yc_ref[...], yd_ref[...]], axis=1)
    y = _dot(ycat, wo_ref[...]) + bo_ref[...]
    xn = x + _scale_rows(y, g1_ref[...])
    xo_ref[...] = xn
    h = _rms_mod(xn, ng_ref[...], sh_ref[...], sc_ref[...])
    h_hi, h_lo = _split_bf16(h)
    nt = (((1,), (1,)), ((), ()))
    wr = wr_ref[...]
    acc = lax.dot_general(wr, h_hi, nt, preferred_element_type=F32)
    acc = acc + lax.dot_general(wr, h_lo, nt, preferred_element_type=F32)
    lg_ref[...] = acc[:N_EXPERTS] + acc[N_EXPERTS:] + br_ref[...]
    hp_ref[...] = _pack_bf16_pairs(h_hi)


def _out_projection(x, pos, mix, mod, norm_g, w_out_bf, b_out, wr2_bf, b_router, ctx, n_tok, tok0, prev):
    add_pos = pos is not None
    assert add_pos == (x.ndim == 3)
    R, D = x.size // x.shape[-1], x.shape[-1]
    tr = min(R, 512)
    row = lambda w: pl.BlockSpec((tr, w), lambda i: (i, 0))
    in_specs = [_x_spec(x, tr)]
    args = [x]
    if add_pos:
        assert tr == GRID_W * SUBLANES
        in_specs += [pl.BlockSpec((None, 1, D // 2), lambda i: (i, 0, 0)), _full((GRID_W, D // 2))]
        args += list(pos)
    in_specs += [row(W_GROUP)] * 6
    args += list(mix)
    in_specs += [_mod_spec(2, ctx), _mod_spec(3, ctx), _mod_spec(4, ctx), _full((1, D)), _full((D, D)),
                 _full((1, D)), _full((2 * N_EXPERTS, D)), _full((N_EXPERTS, 1))]
    args += [mod, mod, mod, norm_g.reshape(1, D), w_out_bf, b_out.reshape(1, D), wr2_bf,
             b_router.reshape(N_EXPERTS, 1)]
    aliases = {}
    if prev is not None:
        aliases = {len(args): 1, len(args) + 1: 2}
        in_specs += [pl.BlockSpec(memory_space=pl.ANY)] * 2
        args += list(prev)
    t0 = tok0 // tr
    assert t0 * tr == tok0
    return pl.pallas_call(
        functools.partial(_outproj_body, add_pos=add_pos, chained=prev is not None),
        grid=(R // tr,),
        in_specs=in_specs,
        out_specs=[row(D), pl.BlockSpec((tr, D // 2), lambda i: (t0 + i, 0)),
                   pl.BlockSpec((N_EXPERTS, tr), lambda i: (0, t0 + i))],
        out_shape=[jax.ShapeDtypeStruct((R, D), F32),
                   jax.ShapeDtypeStruct((n_tok, D // 2), jnp.uint32),
                   jax.ShapeDtypeStruct((N_EXPERTS, n_tok), F32)],
        input_output_aliases=aliases,
        compiler_params=_cparams(("parallel",)),
        name="out_projection",
    )(*args)


def _top4(v):
    eid = lax.broadcasted_iota(jnp.int32, v.shape, 0)
    out = []
    work = v
    for _ in range(TOP_K):
        m = jnp.max(work, axis=0, keepdims=True)
        idx = jnp.min(jnp.where(work == m, eid, N_EXPERTS), axis=0, keepdims=True)
        oh = eid == idx
        out.append((m, oh))
        work = jnp.where(oh, -jnp.inf, work)
    return out


def _count_body(lg_ref, cnt_ref):
    @pl.when(pl.program_id(0) == 0)
    def _():
        cnt_ref[...] = jnp.zeros_like(cnt_ref)

    sel = jnp.zeros(lg_ref.shape, F32)
    for _, oh in _top4(lg_ref[...]):
        sel = sel + oh.astype(F32)
    cnt_ref[...] += jnp.sum(sel, axis=1, keepdims=True)


def _route_body(lg_ref, ps_ref, tri_ref, dest_ref, gate_ref, carry_ref):
    @pl.when(pl.program_id(0) == 0)
    def _():
        carry_ref[...] = jnp.zeros_like(carry_ref)

    top = _top4(lg_ref[...])
    sel = jnp.zeros(lg_ref.shape, F32)
    for _, oh in top:
        sel = sel + oh.astype(F32)
    before = _dot(sel.astype(BF16), tri_ref[...]) + carry_ref[...] + ps_ref[...]
    m0 = top[0][0]
    es = [jnp.exp(m - m0) for m, _ in top]
    den = es[0] + es[1] + es[2] + es[3]
    for k, (_, oh) in enumerate(top):
        dest_ref[k:k + 1, :] = jnp.sum(jnp.where(oh, before, 0.0), axis=0, keepdims=True).astype(jnp.int32)
        gate_ref[k:k + 1, :] = es[k] / den
    carry_ref[...] += jnp.sum(sel, axis=1, keepdims=True)


def _routing(logits_t):
    E, T = logits_t.shape
    tt = ROUTE_TILE
    nt = T // tt
    counts = pl.pallas_call(
        _count_body,
        grid=(nt,),
        in_specs=[pl.BlockSpec((E, tt), lambda i: (0, i))],
        out_specs=_full((E, 1)),
        out_shape=jax.ShapeDtypeStruct((E, 1), F32),
        compiler_params=_cparams(("arbitrary",)),
        name="route_count",
    )(logits_t)
    cnt = counts[:, 0].astype(jnp.int32)
    padded = ((cnt + MOE_TILE - 1) // MOE_TILE) * MOE_TILE
    pend = jnp.cumsum(padded)
    pstart = pend - padded
    n_tiles = -(-(T * TOP_K) // MOE_TILE) + N_EXPERTS
    tile_start = jnp.arange(n_tiles, dtype=jnp.int32) * MOE_TILE
    tile_e = jnp.minimum(jnp.sum((pend[None, :] <= tile_start[:, None]).astype(jnp.int32), axis=1), N_EXPERTS - 1)
    n_used = (pend[-1] // MOE_TILE).astype(jnp.int32).reshape(1)
    n_valid = jnp.clip((pstart + cnt)[tile_e] - tile_start, 0, MOE_TILE).astype(jnp.int32)
    tri = (jnp.arange(tt)[:, None] < jnp.arange(tt)[None, :]).astype(BF16)
    dest, gates = pl.pallas_call(
        _route_body,
        grid=(nt,),
        in_specs=[pl.BlockSpec((E, tt), lambda i: (0, i)), _full((E, 1)), _full((tt, tt))],
        out_specs=[pl.BlockSpec((TOP_K, tt), lambda i: (0, i)), pl.BlockSpec((TOP_K, tt), lambda i: (0, i))],
        out_shape=[jax.ShapeDtypeStruct((TOP_K, T), jnp.int32), jax.ShapeDtypeStruct((TOP_K, T), F32)],
        scratch_shapes=[pltpu.VMEM((E, 1), F32)],
        compiler_params=_cparams(("arbitrary",)),
        name="route_assign",
    )(logits_t, pstart.astype(F32).reshape(E, 1), tri)
    return dest, gates, tile_e, n_valid, n_used, n_tiles


def _sc_workers():
    from jax.experimental.pallas import tpu_sc as plsc
    mesh = plsc.VectorSubcoreMesh(core_axis_name="c", subcore_axis_name="s")
    n_workers = mesh.num_cores * mesh.num_subcores
    worker = lambda: lax.axis_index("s") * mesh.num_cores + lax.axis_index("c")
    return mesh, n_workers, worker


def _dispatch(hp, tok0, dest_flat, n_rows):
    W = hp.shape[1]
    T = dest_flat.shape[0] // TOP_K
    mesh, n_workers, worker = _sc_workers()
    per = dest_flat.shape[0] // n_workers
    steps = per // SC_WINDOW
    assert per * n_workers == dest_flat.shape[0] and steps * SC_WINDOW == per and steps % 2 == 0
    assert T % SC_WINDOW == 0 and tok0 % SC_WINDOW == 0

    @functools.partial(
        pl.kernel, mesh=mesh, out_type=jax.ShapeDtypeStruct((n_rows, W), hp.dtype),
        scratch_types=[pltpu.VMEM((SC_WINDOW,), jnp.int32), pltpu.VMEM((SC_WINDOW,), jnp.int32),
                       pltpu.VMEM((SC_WINDOW, W), hp.dtype), pltpu.VMEM((SC_WINDOW, W), hp.dtype),
                       pltpu.SemaphoreType.DMA, pltpu.SemaphoreType.DMA],
        name="moe_dispatch")
    def scatter(hp_hbm, dest_hbm, xs_hbm, idx0, idx1, rows0, rows1, sem0, sem1):
        base = worker() * per
        bufs = ((idx0, rows0, sem0), (idx1, rows1, sem1))

        def window(j, b, first):
            idx_v, rows_v, sem = bufs[b]

            @pl.when(jnp.logical_not(first))
            def _():
                pltpu.make_async_copy(rows_v, xs_hbm.at[idx_v], sem).wait()

            off = pl.multiple_of(base + j * SC_WINDOW, SC_WINDOW)
            tok = pl.multiple_of(tok0 + lax.rem(off, T), SC_WINDOW)
            pltpu.sync_copy(dest_hbm.at[pl.ds(off, SC_WINDOW)], idx_v)
            pltpu.sync_copy(hp_hbm.at[pl.ds(tok, SC_WINDOW)], rows_v)
            pltpu.async_copy(rows_v, xs_hbm.at[idx_v], sem)

        @pl.loop(0, steps, step=2)
        def _(j):
            window(j, 0, j == 0)
            window(j + 1, 1, j == 0)

        for idx_v, rows_v, sem in bufs:
            pltpu.make_async_copy(rows_v, xs_hbm.at[idx_v], sem).wait()

    return scatter(hp, dest_flat)


def _gather_rows(table, idx_flat):
    n = idx_flat.shape[0]
    W = table.shape[1]
    mesh, n_workers, worker = _sc_workers()
    per = n // n_workers
    steps = per // SC_WINDOW
    assert per * n_workers == n and steps * SC_WINDOW == per and steps % 2 == 0

    @functools.partial(
        pl.kernel, mesh=mesh, out_type=jax.ShapeDtypeStruct((n, W), table.dtype),
        scratch_types=[pltpu.VMEM((SC_WINDOW,), jnp.int32), pltpu.VMEM((SC_WINDOW,), jnp.int32),
                       pltpu.VMEM((SC_WINDOW, W), table.dtype), pltpu.VMEM((SC_WINDOW, W), table.dtype),
                       pltpu.SemaphoreType.DMA, pltpu.SemaphoreType.DMA, pltpu.SemaphoreType.DMA],
        name="moe_gather")
    def gather(table_hbm, idx_hbm, out_hbm, idx0, idx1, rows0, rows1, sem0, sem1, gsem):
        base = worker() * per
        bufs = ((idx0, rows0, sem0), (idx1, rows1, sem1))

        def window(j, b, first):
            idx_v, rows_v, sem = bufs[b]
            off = pl.multiple_of(base + j * SC_WINDOW, SC_WINDOW)

            @pl.when(jnp.logical_not(first))
            def _():
                pltpu.make_async_copy(rows_v, out_hbm.at[pl.ds(off, SC_WINDOW)], sem).wait()

            pltpu.sync_copy(idx_hbm.at[pl.ds(off, SC_WINDOW)], idx_v)
            pltpu.async_copy(table_hbm.at[idx_v], rows_v, gsem).wait()
            pltpu.async_copy(rows_v, out_hbm.at[pl.ds(off, SC_WINDOW)], sem)

        @pl.loop(0, steps, step=2)
        def _(j):
            window(j, 0, j == 0)
            window(j + 1, 1, j == 0)

        for _, rows_v, sem in bufs:
            pltpu.make_async_copy(rows_v, out_hbm.at[pl.ds(base, SC_WINDOW)], sem).wait()

    return gather(table, idx_flat)


def _expert_body(te_ref, nv_ref, nu_ref, nx_ref, sl_ref, xs_ref, wgu_hbm, bgu_ref, wd_hbm, bd_ref, ys_ref,
                 wgu_f32, wd_f32, wgu_bf_ref, wd_bf_ref, sems, *, layer):
    i = pl.program_id(0)
    used = i < nu_ref[0]
    e = te_ref[i]
    s = sl_ref[i]
    new_expert = jnp.logical_or(i == 0, e != te_ref[jnp.maximum(i - 1, 0)])

    def weight_copies(expert, slot):
        return (pltpu.make_async_copy(wgu_hbm.at[layer, expert], wgu_f32.at[slot], sems.at[0, slot]),
                pltpu.make_async_copy(wd_hbm.at[layer, expert], wd_f32.at[slot], sems.at[1, slot]))

    @pl.when(jnp.logical_and(used, i == 0))
    def _():
        for c in weight_copies(e, s):
            c.start()

    @pl.when(jnp.logical_and(used, new_expert))
    def _():
        for c in weight_copies(e, s):
            c.wait()

        @pl.when(nx_ref[i] >= 0)
        def _():
            for c in weight_copies(nx_ref[i], 1 - s):
                c.start()

        wgu_bf_ref[...] = wgu_f32[s].astype(BF16)
        wd_bf_ref[...] = wd_f32[s].astype(BF16)

    def experts_on(n_rows):
        live = lax.broadcasted_iota(jnp.int32, (n_rows, 1), 0) < nv_ref[i]
        x = jnp.concatenate(_unpack_bf16_pairs(jnp.where(live, xs_ref[:n_rows, :], jnp.uint32(0))), axis=1)
        gu = _dot(x, wgu_bf_ref[...]) + bgu_ref[...]
        gt = jnp.minimum(gu[:, :D_FF], SWIGLU_LIMIT)
        up = jnp.clip(gu[:, D_FF:], -SWIGLU_LIMIT, SWIGLU_LIMIT)
        act = (up + 1.0) * (gt * jax.nn.sigmoid(SWIGLU_ALPHA * gt))
        y = _dot(act.astype(BF16), wd_bf_ref[...]) + bd_ref[...]
        ys_ref[:n_rows, :] = _pack_bf16_pairs(y.astype(BF16))

    small = nv_ref[i] <= MOE_SUBTILE

    @pl.when(jnp.logical_and(used, jnp.logical_not(small)))
    def _():
        experts_on(MOE_TILE)

    @pl.when(jnp.logical_and(used, small))
    def _():
        experts_on(MOE_SUBTILE)
        ys_ref[MOE_SUBTILE:, :] = jnp.zeros((MOE_TILE - MOE_SUBTILE, ys_ref.shape[1]), ys_ref.dtype)

    @pl.when(jnp.logical_not(used))
    def _():
        ys_ref[...] = jnp.zeros_like(ys_ref)


def _experts(xs, tile_e, n_valid, n_used, layer, w_gu, b_gu, w_down, b_down):
    n_rows, W = xs.shape
    n_tiles = n_rows // MOE_TILE
    idx = jnp.arange(n_tiles, dtype=jnp.int32)
    first = jnp.logical_and(idx < n_used[0], jnp.logical_or(idx == 0, tile_e != jnp.roll(tile_e, 1)))
    slot = ((jnp.cumsum(first.astype(jnp.int32)) + 1) % 2).astype(jnp.int32)
    later_first = lax.cummin(jnp.where(first, idx, n_tiles), reverse=True)
    nxt_idx = jnp.concatenate([later_first[1:], jnp.full((1,), n_tiles, jnp.int32)])
    nxt = jnp.where(nxt_idx < n_tiles, tile_e[jnp.minimum(nxt_idx, n_tiles - 1)], -1).astype(jnp.int32)
    bias = lambda w: pl.BlockSpec((None, None, 1, w), lambda i, te, nv, nu, nx, sl: (layer, te[i], 0, 0))
    tile = pl.BlockSpec((MOE_TILE, W), lambda i, te, nv, nu, nx, sl: (i, 0))
    grid_spec = pltpu.PrefetchScalarGridSpec(
        num_scalar_prefetch=5,
        grid=(n_tiles,),
        in_specs=[tile, pl.BlockSpec(memory_space=pl.ANY), bias(2 * D_FF),
                  pl.BlockSpec(memory_space=pl.ANY), bias(D_MODEL)],
        out_specs=tile,
        scratch_shapes=[pltpu.VMEM((2, D_MODEL, 2 * D_FF), F32), pltpu.VMEM((2, D_FF, D_MODEL), F32),
                        pltpu.VMEM((D_MODEL, 2 * D_FF), BF16), pltpu.VMEM((D_FF, D_MODEL), BF16),
                        pltpu.SemaphoreType.DMA((2, 2))],
    )
    return pl.pallas_call(
        functools.partial(_expert_body, layer=layer),
        grid_spec=grid_spec,
        out_shape=jax.ShapeDtypeStruct((n_rows, W), jnp.uint32),
        compiler_params=_cparams(("arbitrary",)),
        name="moe_experts",
    )(tile_e, n_valid, n_used, nxt, slot, xs, w_gu, b_gu.reshape(DEPTH, N_EXPERTS, 1, 2 * D_FF), w_down,
      b_down.reshape(DEPTH, N_EXPERTS, 1, D_MODEL))


def _combine_body(x_ref, y0_ref, y1_ref, y2_ref, y3_ref, gate_ref, g2_ref, *rest, final, chained):
    rest = list(rest)
    o_ref = rest.pop()
    if chained:
        rest.pop()
    g = gate_ref[...]
    acc_hi = acc_lo = None
    for k, y_ref in enumerate((y0_ref, y1_ref, y2_ref, y3_ref)):
        p = y_ref[...]
        gk = g[:, k:k + 1]
        hi = gk * pltpu.bitcast(p & jnp.uint32(0xFFFF0000), F32)
        lo = gk * pltpu.bitcast(p << 16, F32)
        acc_hi = hi if acc_hi is None else acc_hi + hi
        acc_lo = lo if acc_lo is None else acc_lo + lo
    acc = jnp.concatenate([acc_hi, acc_lo], axis=1)
    xn = x_ref[...] + _scale_rows(acc, g2_ref[...])
    if final:
        xn = xn * lax.rsqrt(jnp.mean(xn * xn, axis=-1, keepdims=True) + EPS) * rest[0][...]
        for t in range(o_ref.shape[1]):
            o_ref[:, t, :] = xn[t * SUBLANES:(t + 1) * SUBLANES, :]
    else:
        o_ref[...] = xn


def _combine(x, rows, moe_out, tok_off, mod, final_g, ctx, prev=None):
    ysg, gates_t = moe_out
    R, D = x.shape
    tl = TOK_TILE
    r0 = rows[0] // tl
    t0 = tok_off // tl
    nt = gates_t.shape[0] // tl
    final = final_g is not None
    y_specs = [pl.BlockSpec((tl, D // 2), lambda i, k=k: (k * nt + t0 + i, 0)) for k in range(TOP_K)]
    in_specs = ([pl.BlockSpec((tl, D), lambda i: (r0 + i, 0))] + y_specs
                + [pl.BlockSpec((tl, TOP_K), lambda i: (t0 + i, 0)), _mod_spec(5, ctx)])
    args = [x, ysg, ysg, ysg, ysg, gates_t, mod]
    if final:
        in_specs.append(_full((1, D)))
        args.append(final_g.reshape(1, D))
        out_spec = pl.BlockSpec((SUBLANES, tl // SUBLANES, D), lambda i: (0, r0 + i, 0))
        out_shape = jax.ShapeDtypeStruct((SUBLANES, R // SUBLANES, D), F32)
    else:
        out_spec = pl.BlockSpec((tl, D), lambda i: (r0 + i, 0))
        out_shape = jax.ShapeDtypeStruct((R, D), F32)
    aliases = {}
    if prev is not None:
        aliases = {len(args): 0}
        in_specs.append(pl.BlockSpec(memory_space=pl.ANY))
        args.append(prev)
    return pl.pallas_call(
        functools.partial(_combine_body, final=final, chained=prev is not None),
        grid=((rows[1] - rows[0]) // tl,),
        in_specs=in_specs,
        out_specs=out_spec,
        out_shape=out_shape,
        input_output_aliases=aliases,
        compiler_params=_cparams(("parallel",)),
        name="moe_combine",
    )(*args)


def _moe(hp, logits_t, toks, layer, w_gu, b_gu, w_down, b_down):
    dest, gates, tile_e, n_valid, n_used, n_tiles = _routing(logits_t[:, toks[0]:toks[1]])
    dest_flat = dest.reshape(-1)
    xs = _dispatch(hp, toks[0], dest_flat, n_tiles * MOE_TILE)
    ys = _experts(xs, tile_e, n_valid, n_used, layer, w_gu, b_gu, w_down, b_down)
    return _gather_rows(ys, dest_flat), gates.T


def _token_mixers(x, pos, mod, h0, p, consts, ctx, need_out):
    R = x.size // x.shape[-1]
    L = R // SUBLANES
    xa, ga, xb, xc, xd = _in_projection(x, pos, mod, p["norm1_g"], p["w_in"], p["b_in"], ctx)
    yf, yb, hfin = _rglru(xa, p["conv_a_w"], p["conv_a_b"], p["wg"], p["bg"], p["rg_lambda"], h0, L)
    if not need_out:
        return None, hfin
    yp = _pool_mixer(xb, p["w_pool"], p["b_pool"], p["pool_scale"], L)
    yc = _fourier_mixer(xc, consts["dft"][L], consts["cc"], consts["sc"], p["w_four"], p["b_four"], L)
    yd = _conformer(xd, p["conv_d_w"], p["conv_d_b"], p["ln_d_g"], p["ln_d_b"], consts["avg"], p["w_pw"],
                    p["b_pw"], L)
    return (yf, yb, ga, yp, yc, yd), hfin


def _pos_embed(n_tokens):
    rows_n = n_tokens // GRID_W
    q = D_MODEL // 4
    omega = 1.0 / (10000.0 ** (jnp.arange(q, dtype=F32) / q))

    def emb(n):
        ang = jnp.arange(n, dtype=F32)[:, None] * omega[None, :]
        return jnp.concatenate([jnp.sin(ang), jnp.cos(ang)], axis=-1)

    return emb(rows_n).reshape(rows_n, 1, D_MODEL // 2), emb(GRID_W)


def _layer_params(l, w_in, b_in, conv_a_w, conv_a_b, w_rg_r, b_rg_r, w_rg_i, b_rg_i, rg_lambda, w_pool, b_pool,
                  pool_scale, w_four, b_four, conv_d_w, conv_d_b, ln_d_g, ln_d_b, w_pw, b_pw, norm1_g):
    wg = jnp.stack([jnp.concatenate([_block_diag(w_rg_r[l, d]), _block_diag(w_rg_i[l, d])], axis=1)
                    for d in range(2)]).astype(BF16)
    bg = jnp.concatenate([b_rg_r[l].reshape(2, 1, W_GROUP), b_rg_i[l].reshape(2, 1, W_GROUP)], axis=-1)
    return dict(
        norm1_g=norm1_g[l], w_in=w_in[l].astype(BF16), b_in=b_in[l],
        conv_a_w=conv_a_w[l], conv_a_b=conv_a_b[l], wg=wg, bg=bg, rg_lambda=rg_lambda[l],
        w_pool=_block_diag(w_pool[l]).astype(BF16), b_pool=b_pool[l], pool_scale=pool_scale[l],
        w_four=_block_diag(w_four[l]).astype(BF16), b_four=b_four[l],
        conv_d_w=conv_d_w[l], conv_d_b=conv_d_b[l], ln_d_g=ln_d_g[l], ln_d_b=ln_d_b[l],
        w_pw=w_pw[l].astype(BF16), b_pw=b_pw[l])


def kernel(x, c, ctx, c_ctx, w_mod, b_mod, norm1_g, norm2_g, w_in, b_in, conv_a_w, conv_a_b, w_rg_r, b_rg_r,
           w_rg_i, b_rg_i, rg_lambda, w_pool, b_pool, pool_scale, w_four, b_four, conv_d_w, conv_d_b, ln_d_g,
           ln_d_b, w_pw, b_pw, w_out, b_out, w_router, b_router, w_gu, b_gu, w_down, b_down, final_norm_g):
    bn, L, D = x.shape
    Lc = ctx.shape[1]
    assert bn == SUBLANES and D == D_MODEL

    pos = _pos_embed(L)
    c_rows = jnp.concatenate([c, jnp.broadcast_to(c_ctx[None], (MOD_ROWS - bn, D))], axis=0)
    mod = _modulation(c_rows, w_mod, b_mod)
    ctx = jnp.transpose(ctx, (1, 0, 2)).reshape(Lc * bn, D)

    cc1, sc1 = _dft_matrices(D_SUB, 1.0 / math.sqrt(D_SUB))
    eye = jnp.eye(N_SUB, dtype=F32)
    consts = dict(
        dft={n: _time_dft_tables(n) for n in sorted({L, Lc})},
        cc=jnp.kron(eye, cc1).astype(BF16), sc=jnp.kron(eye, sc1).astype(BF16),
        avg=jnp.kron(eye, jnp.full((D_SUB, D_SUB), 1.0 / D_SUB, F32)).astype(BF16))
    h_zero = jnp.zeros((2, SUBLANES, W_GROUP), F32)

    for l in range(DEPTH):
        last = l == DEPTH - 1
        p = _layer_params(l, w_in, b_in, conv_a_w, conv_a_b, w_rg_r, b_rg_r, w_rg_i, b_rg_i, rg_lambda, w_pool,
                          b_pool, pool_scale, w_four, b_four, conv_d_w, conv_d_b, ln_d_g, ln_d_b, w_pw, b_pw,
                          norm1_g)
        mod3 = mod[l]
        w_out_bf = w_out[l].astype(BF16)
        wr_t = w_router[l].T
        wr_hi = wr_t.astype(BF16)
        wr2 = jnp.concatenate([wr_hi, (wr_t - wr_hi.astype(F32)).astype(BF16)], axis=0)
        x_pos = pos if l == 0 else None

        mix_c, h_ctx = _token_mixers(ctx, None, mod3, h_zero, p, consts, True, not last)
        mix_x, _ = _token_mixers(x, x_pos, mod3, h_ctx, p, consts, False, True)
        n_ctx, final_g = (0, final_norm_g) if last else (bn * Lc, None)
        T = n_ctx + bn * L
        moe_in = None
        if not last:
            ctx, *moe_in = _out_projection(ctx, None, mix_c, mod3, norm2_g[l], w_out_bf, b_out[l], wr2,
                                           b_router[l], True, T, 0, None)
        x, hp, lg = _out_projection(x, x_pos, mix_x, mod3, norm2_g[l], w_out_bf, b_out[l], wr2, b_router[l],
                                    False, T, n_ctx, moe_in)
        half = T // 2
        moe_a = _moe(hp, lg, (0, half), l, w_gu, b_gu, w_down, b_down)
        moe_b = _moe(hp, lg, (half, T), l, w_gu, b_gu, w_down, b_down)
        if n_ctx:
            ctx = _combine(ctx, (0, n_ctx), moe_a, 0, mod3, None, True)
        split = half - n_ctx
        xa = _combine(x, (0, split), moe_a, n_ctx, mod3, final_g, False)
        x = _combine(x, (split, x.shape[0]), moe_b, 0, mod3, final_g, False, prev=xa)
    return x
```

```python
import functools
import math

import jax
import jax.numpy as jnp
from jax import lax
from jax.experimental import pallas as pl
from jax.experimental.pallas import tpu as pltpu

F32 = jnp.float32
BF16 = jnp.bfloat16

D_MODEL = 1024
DEPTH = 2
GRID_W = 64
W_GROUP = 256
N_SUB = 4
D_SUB = 64
D_IN = 6 * W_GROUP
RG_CONV = 4
RG_C = 8.0
CONF_KERNEL = 31
N_EXPERTS = 32
TOP_K = 4
D_FF = D_MODEL
SWIGLU_LIMIT = 7.0
SWIGLU_ALPHA = 1.702
EPS = 1e-6

SUBLANES = 8
VMEM_LIMIT_BYTES = 56 * 1024 * 1024
MOD_ROWS = 16
RG_HALO = 8 * SUBLANES
POOL_HALO = 8 * SUBLANES
CONF_HALO = 16 * SUBLANES
MOE_TILE = 512
MOE_SUBTILE = 128
TOK_TILE = 256
SC_WINDOW = 64
ROUTE_TILE = 1024


def _cparams(sem):
    return pltpu.CompilerParams(dimension_semantics=sem, vmem_limit_bytes=VMEM_LIMIT_BYTES)


def _full(shape):
    nd = len(shape)
    return pl.BlockSpec(shape, lambda *_: (0,) * nd)


def _dot(a, b):
    return jnp.dot(a, b, preferred_element_type=F32)


def _split_bf16(v):
    hi = v.astype(BF16)
    lo = (v - hi.astype(F32)).astype(BF16)
    return hi, lo


def _mod_body(c_ref, w_ref, b_ref, o_ref):
    c = c_ref[...]
    s = c * jax.nn.sigmoid(c)
    o_ref[...] = jnp.dot(s, w_ref[...], precision=lax.Precision.HIGHEST,
                         preferred_element_type=F32) + b_ref[...]


def _modulation(c_rows, w_mod, b_mod):
    tn = 1536
    n6 = 6 * D_MODEL
    return pl.pallas_call(
        _mod_body,
        grid=(DEPTH, n6 // tn),
        in_specs=[_full((MOD_ROWS, D_MODEL)),
                  pl.BlockSpec((None, D_MODEL, tn), lambda l, j: (l, 0, j)),
                  pl.BlockSpec((None, 1, tn), lambda l, j: (l, 0, j))],
        out_specs=pl.BlockSpec((None, MOD_ROWS, tn), lambda l, j: (l, 0, j)),
        out_shape=jax.ShapeDtypeStruct((DEPTH, MOD_ROWS, n6), F32),
        compiler_params=_cparams(("parallel", "parallel")),
        name="modulation",
    )(c_rows, w_mod, b_mod.reshape(DEPTH, 1, n6))


def _mod_spec(chunk, ctx):
    return pl.BlockSpec((SUBLANES, D_MODEL), lambda i: (1 if ctx else 0, chunk))


def _scale_rows(v, m):
    r, d = v.shape
    return (v.reshape(r // SUBLANES, SUBLANES, d) * m[None]).reshape(r, d)


def _rms_mod(x, g, shift, scale):
    r, d = x.shape
    y = x * lax.rsqrt(jnp.mean(x * x, axis=-1, keepdims=True) + EPS) * g
    y3 = y.reshape(r // SUBLANES, SUBLANES, d)
    return (y3 * (1.0 + scale)[None] + shift[None]).reshape(r, d)


def _load_plus_pos(x_ref, prow_ref, pcol_ref):
    row = prow_ref[...]
    return jnp.concatenate([x_ref[:, t, :] + jnp.concatenate([row, pcol_ref[t:t + 1, :]], axis=1)
                            for t in range(GRID_W)], axis=0)


def _x_spec(x, tr):
    if x.ndim == 3:
        return pl.BlockSpec((SUBLANES, tr // SUBLANES, x.shape[2]), lambda i: (0, i, 0))
    return pl.BlockSpec((tr, x.shape[1]), lambda i: (i, 0))


def _inproj_body(*refs, add_pos):
    if add_pos:
        (x_ref, prow_ref, pcol_ref, sh_ref, sc_ref, g_ref, w_ref, b_ref,
         xa_ref, ga_ref, xb_ref, xc_ref, xd_ref) = refs
        x = _load_plus_pos(x_ref, prow_ref, pcol_ref)
    else:
        x_ref, sh_ref, sc_ref, g_ref, w_ref, b_ref, xa_ref, ga_ref, xb_ref, xc_ref, xd_ref = refs
        x = x_ref[...]
    u = _rms_mod(x, g_ref[...], sh_ref[...], sc_ref[...])
    p = _dot(u.astype(BF16), w_ref[...]) + b_ref[...]
    xa_ref[...] = p[:, 0:256]
    ga_ref[...] = p[:, 256:512]
    xb_ref[...] = p[:, 512:768]
    xc_ref[...] = p[:, 768:1024].astype(BF16)
    xd_ref[...] = p[:, 1024:1536]


def _in_projection(x, pos, mod, norm_g, w_in_bf, b_in, ctx):
    add_pos = pos is not None
    assert add_pos == (x.ndim == 3)
    R, D = x.size // x.shape[-1], x.shape[-1]
    tr = min(R, 512)
    row = lambda w: pl.BlockSpec((tr, w), lambda i: (i, 0))
    in_specs = [_x_spec(x, tr)]
    args = [x]
    if add_pos:
        assert tr == GRID_W * SUBLANES
        in_specs += [pl.BlockSpec((None, 1, D // 2), lambda i: (i, 0, 0)), _full((GRID_W, D // 2))]
        args += list(pos)
    in_specs += [_mod_spec(0, ctx), _mod_spec(1, ctx), _full((1, D)), _full((D, D_IN)), _full((1, D_IN))]
    args += [mod, mod, norm_g.reshape(1, D), w_in_bf, b_in.reshape(1, D_IN)]
    out_shape = [jax.ShapeDtypeStruct((R, 256), F32)] * 3 + [
        jax.ShapeDtypeStruct((R, 256), BF16), jax.ShapeDtypeStruct((R, 512), F32)]
    return pl.pallas_call(
        functools.partial(_inproj_body, add_pos=add_pos),
        grid=(R // tr,),
        in_specs=in_specs,
        out_specs=[row(256), row(256), row(256), row(256), row(512)],
        out_shape=out_shape,
        compiler_params=_cparams(("parallel",)),
        name="in_projection",
    )(*args)


def _rg_gates(xc, wg, bg, lam):
    g = _dot(xc.astype(BF16), wg) + bg
    r = jax.nn.sigmoid(g[:, :W_GROUP])
    gi = jax.nn.sigmoid(g[:, W_GROUP:])
    z = -lam
    softplus = jnp.maximum(z, 0.0) + jnp.log1p(jnp.exp(-jnp.abs(z)))
    log_a = (-RG_C) * r * softplus
    a = jnp.exp(log_a)
    b = jnp.sqrt(-jnp.tanh(log_a) * (a * a + 1.0)) * (gi * xc)
    return a, b


def _rg_body(xf_ref, xfh_ref, xr_ref, xrh_ref, cw_ref, cb_ref, wg_ref, bg_ref, lam_ref, h0_ref,
             yf_ref, yb_ref, hfin_ref, af_ref, ab_ref, hc_ref, *, n, tt):
    i = pl.program_id(0)
    tr = tt * SUBLANES
    keep = RG_HALO - (RG_CONV - 1) * SUBLANES

    @pl.when(i == 0)
    def _():
        hc_ref[...] = h0_ref[...]

    halo = jnp.where(i > 0, xfh_ref[...], 0.0)
    ext = jnp.concatenate([halo[keep:], xf_ref[...]], axis=0)
    xc = cb_ref[0]
    for k in range(RG_CONV):
        xc = xc + cw_ref[0, k:k + 1, :] * ext[k * SUBLANES:k * SUBLANES + tr]
    a, b = _rg_gates(xc, wg_ref[0], bg_ref[0], lam_ref[0])
    af_ref[...] = a
    yf_ref[...] = b

    halo = jnp.where(i > 0, xrh_ref[...], 0.0)
    ext = jnp.concatenate([xr_ref[...], halo[:(RG_CONV - 1) * SUBLANES]], axis=0)
    xc = cb_ref[1]
    for k in range(RG_CONV):
        o = (RG_CONV - 1 - k) * SUBLANES
        xc = xc + cw_ref[1, k:k + 1, :] * ext[o:o + tr]
    a, b = _rg_gates(xc, wg_ref[1], bg_ref[1], lam_ref[1])
    ab_ref[...] = a
    yb_ref[...] = b

    def step(t, carry):
        hf, hb = carry
        rf = pl.multiple_of(t * SUBLANES, SUBLANES)
        hf = af_ref[pl.ds(rf, SUBLANES), :] * hf + yf_ref[pl.ds(rf, SUBLANES), :]
        yf_ref[pl.ds(rf, SUBLANES), :] = hf
        rb = pl.multiple_of((tt - 1 - t) * SUBLANES, SUBLANES)
        hb = ab_ref[pl.ds(rb, SUBLANES), :] * hb + yb_ref[pl.ds(rb, SUBLANES), :]
        yb_ref[pl.ds(rb, SUBLANES), :] = hb
        return hf, hb

    hf, hb = lax.fori_loop(0, tt, step, (hc_ref[0], hc_ref[1]), unroll=8)
    hc_ref[0] = hf
    hc_ref[1] = hb
    hfin_ref[0] = hf
    hfin_ref[1] = hb


def _rglru(xa2, conv_w, conv_b, wg_bf, bg, lam, h0, L):
    tt = min(L, 256)
    n = L // tt
    tr = tt * SUBLANES
    per = tr // RG_HALO
    last_halo = L * SUBLANES // RG_HALO - 1
    row = lambda i: (i, 0)
    rev = lambda i: (n - 1 - i, 0)
    in_specs = [
        pl.BlockSpec((tr, W_GROUP), row),
        pl.BlockSpec((RG_HALO, W_GROUP), lambda i: (jnp.maximum(i * per - 1, 0), 0)),
        pl.BlockSpec((tr, W_GROUP), rev),
        pl.BlockSpec((RG_HALO, W_GROUP), lambda i: (jnp.minimum((n - i) * per, last_halo), 0)),
        _full((2, RG_CONV, W_GROUP)), _full((2, 1, W_GROUP)), _full((2, W_GROUP, 2 * W_GROUP)),
        _full((2, 1, 2 * W_GROUP)), _full((2, 1, W_GROUP)), _full((2, SUBLANES, W_GROUP)),
    ]
    return pl.pallas_call(
        functools.partial(_rg_body, n=n, tt=tt),
        grid=(n,),
        in_specs=in_specs,
        out_specs=[pl.BlockSpec((tr, W_GROUP), row), pl.BlockSpec((tr, W_GROUP), rev),
                   _full((2, SUBLANES, W_GROUP))],
        out_shape=[jax.ShapeDtypeStruct((L * SUBLANES, W_GROUP), F32)] * 2
        + [jax.ShapeDtypeStruct((2, SUBLANES, W_GROUP), F32)],
        scratch_shapes=[pltpu.VMEM((tr, W_GROUP), F32), pltpu.VMEM((tr, W_GROUP), F32),
                        pltpu.VMEM((2, SUBLANES, W_GROUP), F32)],
        compiler_params=_cparams(("arbitrary",)),
        name="rglru",
    )(xa2, xa2, xa2, xa2, conv_w, conv_b.reshape(2, 1, W_GROUP), wg_bf, bg, lam.reshape(2, 1, W_GROUP), h0)


def _pool_body(xm_ref, xp_ref, xn_ref, w_ref, b_ref, s_ref, o_ref, *, n, tt, L):
    i = pl.program_id(0)
    tr = tt * SUBLANES
    S = SUBLANES
    xm = xm_ref[...]
    prev = jnp.where(i > 0, xp_ref[...], 0.0)
    nxt = jnp.where(i < n - 1, xn_ref[...], 0.0)
    xe = jnp.concatenate([prev, xm, nxt], axis=0)
    e = xe.shape[0]
    p2 = xe[S:e] + xe[0:e - S]
    n4 = (tt + 13) * S
    p4 = p2[0:n4] + p2[2 * S:2 * S + n4]
    n8 = (tt + 9) * S
    p8 = p4[0:n8] + p4[4 * S:4 * S + n8]
    s16 = p8[0:tr] + p8[8 * S:8 * S + tr]
    s2 = p2[7 * S:7 * S + tr]
    s4 = p4[6 * S:6 * S + tr]
    s8 = p8[4 * S:4 * S + tr]
    grp = lax.broadcasted_iota(jnp.int32, (1, W_GROUP), 1) // D_SUB
    half = jnp.left_shift(1, grp)
    t = i * tt + lax.broadcasted_iota(jnp.int32, (tr, 1), 0) // S
    cnt = (jnp.minimum(t + half, L) - jnp.maximum(t - half, 0)).astype(F32)
    s = jnp.where(grp == 0, s2, jnp.where(grp == 1, s4, jnp.where(grp == 2, s8, s16)))
    pooled = s / cnt - xm
    y = _dot(pooled.astype(BF16), w_ref[...]) + b_ref[...]
    o_ref[...] = (y * s_ref[...]).astype(BF16)


def _pool_mixer(xb2, w_bd_bf, b, scale, L):
    tt = min(L, 256)
    n = L // tt
    tr = tt * SUBLANES
    per = tr // POOL_HALO
    last_halo = L * SUBLANES // POOL_HALO - 1
    return pl.pallas_call(
        functools.partial(_pool_body, n=n, tt=tt, L=L),
        grid=(n,),
        in_specs=[pl.BlockSpec((tr, W_GROUP), lambda i: (i, 0)),
                  pl.BlockSpec((POOL_HALO, W_GROUP), lambda i: (jnp.maximum(i * per - 1, 0), 0)),
                  pl.BlockSpec((POOL_HALO, W_GROUP), lambda i: (jnp.minimum((i + 1) * per, last_halo), 0)),
                  _full((W_GROUP, W_GROUP)), _full((1, W_GROUP)), _full((1, W_GROUP))],
        out_specs=pl.BlockSpec((tr, W_GROUP), lambda i: (i, 0)),
        out_shape=jax.ShapeDtypeStruct((L * SUBLANES, W_GROUP), BF16),
        compiler_params=_cparams(("parallel",)),
        name="pool_mixer",
    )(xb2, xb2, xb2, w_bd_bf, b.reshape(1, W_GROUP), scale.reshape(1, W_GROUP))


def _fourier_body(c_ref, s_ref, xe_ref, xo_ref, cw_ref, sw_ref, cc_ref, sc_ref, w_ref, b_ref, o_ref):
    c, s = c_ref[...], s_ref[...]
    xe, xo = xe_ref[...], xo_ref[...]
    ec, es = _dot(c, xe), _dot(s, xe)
    oc, os_ = _dot(c, xo), _dot(s, xo)
    cw, sw = cw_ref[...], sw_ref[...]
    tc = cw * oc - sw * os_
    ts = cw * os_ + sw * oc
    for h, (z1, z2) in enumerate(((ec + tc, es + ts), (ec - tc, es - ts))):
        for j in range(xe.shape[1] // W_GROUP):
            sl = slice(j * W_GROUP, (j + 1) * W_GROUP)
            a_hi, a_lo = _split_bf16(z1[:, sl])
            b_hi, b_lo = _split_bf16(z2[:, sl])
            f = ((_dot(a_hi, cc_ref[...]) + _dot(a_lo, cc_ref[...]))
                 - (_dot(b_hi, sc_ref[...]) + _dot(b_lo, sc_ref[...])))
            o_ref[h, :, sl] = (_dot(f.astype(BF16), w_ref[...]) + b_ref[...]).astype(BF16)


def _fourier_mixer(xc, tables, cc, sc, w_bd_bf, b, L):
    ch, sh, cw, sw = tables
    M = L // 2
    ncol = SUBLANES * W_GROUP
    x2 = xc.reshape(M, 2 * ncol)
    nb = 1024
    tk = min(M, 256)
    out = pl.pallas_call(
        _fourier_body,
        grid=(ncol // nb, M // tk),
        in_specs=[pl.BlockSpec((tk, M), lambda j, k: (k, 0)),
                  pl.BlockSpec((tk, M), lambda j, k: (k, 0)),
                  pl.BlockSpec((M, nb), lambda j, k: (0, j)),
                  pl.BlockSpec((M, nb), lambda j, k: (0, ncol // nb + j)),
                  pl.BlockSpec((tk, 1), lambda j, k: (k, 0)),
                  pl.BlockSpec((tk, 1), lambda j, k: (k, 0)),
                  _full((W_GROUP, W_GROUP)), _full((W_GROUP, W_GROUP)), _full((W_GROUP, W_GROUP)),
                  _full((1, W_GROUP))],
        out_specs=pl.BlockSpec((2, tk, nb), lambda j, k: (0, k, j)),
        out_shape=jax.ShapeDtypeStruct((2, M, ncol), BF16),
        compiler_params=_cparams(("parallel", "parallel")),
        name="fourier_mixer",
    )(ch, sh, x2, x2, cw, sw, cc, sc, w_bd_bf, b.reshape(1, W_GROUP))
    return out.reshape(L * SUBLANES, W_GROUP)


def _time_dft_tables(L):
    M = L // 2
    ch, sh = _dft_matrices(M, 1.0 / math.sqrt(L))
    ang = jnp.arange(M, dtype=F32) * (2.0 * math.pi / L)
    return ch.astype(BF16), sh.astype(BF16), jnp.cos(ang).reshape(M, 1), jnp.sin(ang).reshape(M, 1)


def _dft_matrices(L, scale):
    f = 1 << (max(L.bit_length() - 1, 0) // 2)
    n = jnp.arange(L, dtype=jnp.int32)[None, :]

    def table(rows):
        ang = ((rows[:, None] * n) % L).astype(F32) * (2.0 * math.pi / L)
        return jnp.cos(ang), jnp.sin(ang)

    ac, as_ = table(jnp.arange(L // f, dtype=jnp.int32) * f)
    bc, bs = table(jnp.arange(f, dtype=jnp.int32))
    cos = (ac[:, None, :] * bc[None, :, :] - as_[:, None, :] * bs[None, :, :]).reshape(L, L) * scale
    sin = (as_[:, None, :] * bc[None, :, :] + ac[:, None, :] * bs[None, :, :]).reshape(L, L) * scale
    return cos, sin


def _block_diag(w):
    g, a, b = w.shape
    eye = jnp.eye(g, dtype=w.dtype)
    return (eye[:, None, :, None] * w[:, :, None, :]).reshape(g * a, g * b)


CONF_CHUNK = 64


def _conformer_body(xm_ref, xp_ref, xn_ref, cw_ref, cb_ref, lg_ref, lb_ref, avg_ref, w_ref, b_ref,
                    o_ref, v_ref, c_ref, *, n, tt):
    i = pl.program_id(0)
    tr = tt * SUBLANES
    H = CONF_HALO

    def glu(v):
        return v[:, :W_GROUP] * jax.nn.sigmoid(v[:, W_GROUP:])

    v_ref[0:H] = jnp.where(i > 0, glu(xp_ref[...]), 0.0)
    v_ref[H:H + tr] = glu(xm_ref[...])
    v_ref[H + tr:H + tr + H] = jnp.where(i < n - 1, glu(xn_ref[...]), 0.0)

    def chunk(c, carry):
        r0 = pl.multiple_of(c * CONF_CHUNK, CONF_CHUNK)
        acc = jnp.broadcast_to(cb_ref[...], (CONF_CHUNK, W_GROUP))
        for k in range(CONF_KERNEL):
            acc = acc + cw_ref[k:k + 1, :] * v_ref[pl.ds(r0 + (k + 1) * SUBLANES, CONF_CHUNK), :]
        c_ref[pl.ds(r0, CONF_CHUNK), :] = acc
        return carry

    lax.fori_loop(0, tr // CONF_CHUNK, chunk, 0)

    v = c_ref[...]
    avg = avg_ref[...]
    v_hi, v_lo = _split_bf16(v)
    mu = _dot(v_hi, avg) + _dot(v_lo, avg)
    d = v - mu
    q_hi, q_lo = _split_bf16(d * d)
    var = _dot(q_hi, avg) + _dot(q_lo, avg)
    vn = d * lax.rsqrt(var + EPS) * lg_ref[...] + lb_ref[...]
    act = vn * jax.nn.sigmoid(vn)
    o_ref[...] = (_dot(act.astype(BF16), w_ref[...]) + b_ref[...]).astype(BF16)


def _conformer(xd2, conv_w, conv_b, ln_g, ln_b, avg_bf, w_pw_bf, b_pw, L):
    tt = min(L, 256)
    n = L // tt
    tr = tt * SUBLANES
    per = tr // CONF_HALO
    last_halo = L * SUBLANES // CONF_HALO - 1
    vec = lambda a: a.reshape(1, W_GROUP)
    return pl.pallas_call(
        functools.partial(_conformer_body, n=n, tt=tt),
        grid=(n,),
        in_specs=[pl.BlockSpec((tr, 2 * W_GROUP), lambda i: (i, 0)),
                  pl.BlockSpec((CONF_HALO, 2 * W_GROUP), lambda i: (jnp.maximum(i * per - 1, 0), 0)),
                  pl.BlockSpec((CONF_HALO, 2 * W_GROUP), lambda i: (jnp.minimum((i + 1) * per, last_halo), 0)),
                  _full((CONF_KERNEL, W_GROUP)), _full((1, W_GROUP)), _full((1, W_GROUP)), _full((1, W_GROUP)),
                  _full((W_GROUP, W_GROUP)), _full((W_GROUP, W_GROUP)), _full((1, W_GROUP))],
        out_specs=pl.BlockSpec((tr, W_GROUP), lambda i: (i, 0)),
        out_shape=jax.ShapeDtypeStruct((L * SUBLANES, W_GROUP), BF16),
        scratch_shapes=[pltpu.VMEM((tr + 2 * CONF_HALO, W_GROUP), F32), pltpu.VMEM((tr, W_GROUP), F32)],
        compiler_params=_cparams(("parallel",)),
        name="conformer",
    )(xd2, xd2, xd2, conv_w, vec(conv_b), vec(ln_g), vec(ln_b), avg_bf, w_pw_bf, vec(b_pw))


def _gelu_tanh(x):
    return 0.5 * x * (1.0 + jnp.tanh(math.sqrt(2.0 / math.pi) * (x + 0.044715 * (x * x * x))))


def _pack_bf16_pairs(h_bf):
    u = pltpu.bitcast(h_bf.astype(F32), jnp.uint32)
    half = h_bf.shape[1] // 2
    return (u[:, :half] & jnp.uint32(0xFFFF0000)) | (u[:, half:] >> 16)


def _unpack_bf16_pairs(p):
    hi = pltpu.bitcast(p & jnp.uint32(0xFFFF0000), F32).astype(BF16)
    lo = pltpu.bitcast(p << 16, F32).astype(BF16)
    return hi, lo


def _outproj_body(*refs, add_pos, chained):
    refs = list(refs)
    xo_ref, hp_ref, lg_ref = refs[-3:]
    del refs[-5 if chained else -3:]
    if add_pos:
        (x_ref, prow_ref, pcol_ref, yf_ref, yb_ref, ga_ref, yp_ref, yc_ref, yd_ref, g1_ref, sh_ref, sc_ref,
         ng_ref, wo_ref, bo_ref, wr_ref, br_ref) = refs
        x = _load_plus_pos(x_ref, prow_ref, pcol_ref)
    else:
        (x_ref, yf_ref, yb_ref, ga_ref, yp_ref, yc_ref, yd_ref, g1_ref, sh_ref, sc_ref, ng_ref,
         wo_ref, bo_ref, wr_ref, br_ref) = refs
        x = x_ref[...]
    ya = (yf_ref[...] + yb_ref[...]) * _gelu_tanh(ga_ref[...])
    ycat = jnp.concatenate([ya.astype(BF16), yp_ref[...], yc_ref[...], yd_ref[...]], axis=1)
    y = _dot(ycat, wo_ref[...]) + bo_ref[...]
    xn = x + _scale_rows(y, g1_ref[...])
    xo_ref[...] = xn
    h = _rms_mod(xn, ng_ref[...], sh_ref[...], sc_ref[...])
    h_hi, h_lo = _split_bf16(h)
    nt = (((1,), (1,)), ((), ()))
    wr = wr_ref[...]
    acc = lax.dot_general(wr, h_hi, nt, preferred_element_type=F32)
    acc = acc + lax.dot_general(wr, h_lo, nt, preferred_element_type=F32)
    lg_ref[...] = acc[:N_EXPERTS] + acc[N_EXPERTS:] + br_ref[...]
    hp_ref[...] = _pack_bf16_pairs(h_hi)


def _out_projection(x, pos, mix, mod, norm_g, w_out_bf, b_out, wr2_bf, b_router, ctx, n_tok, tok0, prev):
    add_pos = pos is not None
    assert add_pos == (x.ndim == 3)
    R, D = x.size // x.shape[-1], x.shape[-1]
    tr = min(R, 512)
    row = lambda w: pl.BlockSpec((tr, w), lambda i: (i, 0))
    in_specs = [_x_spec(x, tr)]
    args = [x]
    if add_pos:
        assert tr == GRID_W * SUBLANES
        in_specs += [pl.BlockSpec((None, 1, D // 2), lambda i: (i, 0, 0)), _full((GRID_W, D // 2))]
        args += list(pos)
    in_specs += [row(W_GROUP)] * 6
    args += list(mix)
    in_specs += [_mod_spec(2, ctx), _mod_spec(3, ctx), _mod_spec(4, ctx), _full((1, D)), _full((D, D)),
                 _full((1, D)), _full((2 * N_EXPERTS, D)), _full((N_EXPERTS, 1))]
    args += [mod, mod, mod, norm_g.reshape(1, D), w_out_bf, b_out.reshape(1, D), wr2_bf,
             b_router.reshape(N_EXPERTS, 1)]
    aliases = {}
    if prev is not None:
        aliases = {len(args): 1, len(args) + 1: 2}
        in_specs += [pl.BlockSpec(memory_space=pl.ANY)] * 2
        args += list(prev)
    t0 = tok0 // tr
    assert t0 * tr == tok0
    return pl.pallas_call(
        functools.partial(_outproj_body, add_pos=add_pos, chained=prev is not None),
        grid=(R // tr,),
        in_specs=in_specs,
        out_specs=[row(D), pl.BlockSpec((tr, D // 2), lambda i: (t0 + i, 0)),
                   pl.BlockSpec((N_EXPERTS, tr), lambda i: (0, t0 + i))],
        out_shape=[jax.ShapeDtypeStruct((R, D), F32),
                   jax.ShapeDtypeStruct((n_tok, D // 2), jnp.uint32),
                   jax.ShapeDtypeStruct((N_EXPERTS, n_tok), F32)],
        input_output_aliases=aliases,
        compiler_params=_cparams(("parallel",)),
        name="out_projection",
    )(*args)


def _top4(v):
    eid = lax.broadcasted_iota(jnp.int32, v.shape, 0)
    out = []
    work = v
    for _ in range(TOP_K):
        m = jnp.max(work, axis=0, keepdims=True)
        idx = jnp.min(jnp.where(work == m, eid, N_EXPERTS), axis=0, keepdims=True)
        oh = eid == idx
        out.append((m, oh))
        work = jnp.where(oh, -jnp.inf, work)
    return out


def _count_body(lg_ref, cnt_ref):
    @pl.when(pl.program_id(0) == 0)
    def _():
        cnt_ref[...] = jnp.zeros_like(cnt_ref)

    sel = jnp.zeros(lg_ref.shape, F32)
    for _, oh in _top4(lg_ref[...]):
        sel = sel + oh.astype(F32)
    cnt_ref[...] += jnp.sum(sel, axis=1, keepdims=True)


def _route_body(lg_ref, ps_ref, tri_ref, dest_ref, gate_ref, carry_ref):
    @pl.when(pl.program_id(0) == 0)
    def _():
        carry_ref[...] = jnp.zeros_like(carry_ref)

    top = _top4(lg_ref[...])
    sel = jnp.zeros(lg_ref.shape, F32)
    for _, oh in top:
        sel = sel + oh.astype(F32)
    before = _dot(sel.astype(BF16), tri_ref[...]) + carry_ref[...] + ps_ref[...]
    m0 = top[0][0]
    es = [jnp.exp(m - m0) for m, _ in top]
    den = es[0] + es[1] + es[2] + es[3]
    for k, (_, oh) in enumerate(top):
        dest_ref[k:k + 1, :] = jnp.sum(jnp.where(oh, before, 0.0), axis=0, keepdims=True).astype(jnp.int32)
        gate_ref[k:k + 1, :] = es[k] / den
    carry_ref[...] += jnp.sum(sel, axis=1, keepdims=True)


def _routing(logits_t):
    E, T = logits_t.shape
    tt = ROUTE_TILE
    nt = T // tt
    counts = pl.pallas_call(
        _count_body,
        grid=(nt,),
        in_specs=[pl.BlockSpec((E, tt), lambda i: (0, i))],
        out_specs=_full((E, 1)),
        out_shape=jax.ShapeDtypeStruct((E, 1), F32),
        compiler_params=_cparams(("arbitrary",)),
        name="route_count",
    )(logits_t)
    cnt = counts[:, 0].astype(jnp.int32)
    padded = ((cnt + MOE_TILE - 1) // MOE_TILE) * MOE_TILE
    pend = jnp.cumsum(padded)
    pstart = pend - padded
    n_tiles = -(-(T * TOP_K) // MOE_TILE) + N_EXPERTS
    tile_start = jnp.arange(n_tiles, dtype=jnp.int32) * MOE_TILE
    tile_e = jnp.minimum(jnp.sum((pend[None, :] <= tile_start[:, None]).astype(jnp.int32), axis=1), N_EXPERTS - 1)
    n_used = (pend[-1] // MOE_TILE).astype(jnp.int32).reshape(1)
    n_valid = jnp.clip((pstart + cnt)[tile_e] - tile_start, 0, MOE_TILE).astype(jnp.int32)
    tri = (jnp.arange(tt)[:, None] < jnp.arange(tt)[None, :]).astype(BF16)
    dest, gates = pl.pallas_call(
        _route_body,
        grid=(nt,),
        in_specs=[pl.BlockSpec((E, tt), lambda i: (0, i)), _full((E, 1)), _full((tt, tt))],
        out_specs=[pl.BlockSpec((TOP_K, tt), lambda i: (0, i)), pl.BlockSpec((TOP_K, tt), lambda i: (0, i))],
        out_shape=[jax.ShapeDtypeStruct((TOP_K, T), jnp.int32), jax.ShapeDtypeStruct((TOP_K, T), F32)],
        scratch_shapes=[pltpu.VMEM((E, 1), F32)],
        compiler_params=_cparams(("arbitrary",)),
        name="route_assign",
    )(logits_t, pstart.astype(F32).reshape(E, 1), tri)
    return dest, gates, cnt, tile_e, n_valid, n_used, n_tiles


def _sc_workers():
    from jax.experimental.pallas import tpu_sc as plsc
    mesh = plsc.VectorSubcoreMesh(core_axis_name="c", subcore_axis_name="s")
    n_workers = mesh.num_cores * mesh.num_subcores
    worker = lambda: lax.axis_index("s") * mesh.num_cores + lax.axis_index("c")
    return mesh, n_workers, worker


def _dispatch(hp, tok0, dest_flat, n_rows):
    W = hp.shape[1]
    T = dest_flat.shape[0] // TOP_K
    mesh, n_workers, worker = _sc_workers()
    per = dest_flat.shape[0] // n_workers
    steps = per // SC_WINDOW
    assert per * n_workers == dest_flat.shape[0] and steps * SC_WINDOW == per and steps % 2 == 0
    assert T % SC_WINDOW == 0 and tok0 % SC_WINDOW == 0

    @functools.partial(
        pl.kernel, mesh=mesh, out_type=jax.ShapeDtypeStruct((n_rows, W), hp.dtype),
        scratch_types=[pltpu.VMEM((SC_WINDOW,), jnp.int32), pltpu.VMEM((SC_WINDOW,), jnp.int32),
                       pltpu.VMEM((SC_WINDOW, W), hp.dtype), pltpu.VMEM((SC_WINDOW, W), hp.dtype),
                       pltpu.SemaphoreType.DMA, pltpu.SemaphoreType.DMA],
        name="moe_dispatch")
    def scatter(hp_hbm, dest_hbm, xs_hbm, idx0, idx1, rows0, rows1, sem0, sem1):
        base = worker() * per
        bufs = ((idx0, rows0, sem0), (idx1, rows1, sem1))

        def window(j, b, first):
            idx_v, rows_v, sem = bufs[b]

            @pl.when(jnp.logical_not(first))
            def _():
                pltpu.make_async_copy(rows_v, xs_hbm.at[idx_v], sem).wait()

            off = pl.multiple_of(base + j * SC_WINDOW, SC_WINDOW)
            tok = pl.multiple_of(tok0 + lax.rem(off, T), SC_WINDOW)
            pltpu.sync_copy(dest_hbm.at[pl.ds(off, SC_WINDOW)], idx_v)
            pltpu.sync_copy(hp_hbm.at[pl.ds(tok, SC_WINDOW)], rows_v)
            pltpu.async_copy(rows_v, xs_hbm.at[idx_v], sem)

        @pl.loop(0, steps, step=2)
        def _(j):
            window(j, 0, j == 0)
            window(j + 1, 1, j == 0)

        for idx_v, rows_v, sem in bufs:
            pltpu.make_async_copy(rows_v, xs_hbm.at[idx_v], sem).wait()

    return scatter(hp, dest_flat)


def _gather_rows(table, idx_flat):
    n = idx_flat.shape[0]
    W = table.shape[1]
    mesh, n_workers, worker = _sc_workers()
    per = n // n_workers
    steps = per // SC_WINDOW
    assert per * n_workers == n and steps * SC_WINDOW == per and steps % 2 == 0

    @functools.partial(
        pl.kernel, mesh=mesh, out_type=jax.ShapeDtypeStruct((n, W), table.dtype),
        scratch_types=[pltpu.VMEM((SC_WINDOW,), jnp.int32), pltpu.VMEM((SC_WINDOW,), jnp.int32),
                       pltpu.VMEM((SC_WINDOW, W), table.dtype), pltpu.VMEM((SC_WINDOW, W), table.dtype),
                       pltpu.SemaphoreType.DMA, pltpu.SemaphoreType.DMA, pltpu.SemaphoreType.DMA],
        name="moe_gather")
    def gather(table_hbm, idx_hbm, out_hbm, idx0, idx1, rows0, rows1, sem0, sem1, gsem):
        base = worker() * per
        bufs = ((idx0, rows0, sem0), (idx1, rows1, sem1))

        def window(j, b, first):
            idx_v, rows_v, sem = bufs[b]
            off = pl.multiple_of(base + j * SC_WINDOW, SC_WINDOW)

            @pl.when(jnp.logical_not(first))
            def _():
                pltpu.make_async_copy(rows_v, out_hbm.at[pl.ds(off, SC_WINDOW)], sem).wait()

            pltpu.sync_copy(idx_hbm.at[pl.ds(off, SC_WINDOW)], idx_v)
            pltpu.async_copy(table_hbm.at[idx_v], rows_v, gsem).wait()
            pltpu.async_copy(rows_v, out_hbm.at[pl.ds(off, SC_WINDOW)], sem)

        @pl.loop(0, steps, step=2)
        def _(j):
            window(j, 0, j == 0)
            window(j + 1, 1, j == 0)

        for _, rows_v, sem in bufs:
            pltpu.make_async_copy(rows_v, out_hbm.at[pl.ds(base, SC_WINDOW)], sem).wait()

    return gather(table, idx_flat)


def _expert_body(te_ref, nv_ref, nu_ref, nx_ref, sl_ref, xs_ref, wgu_hbm, bgu_ref, wd_hbm, bd_ref, ys_ref,
                 wgu_f32, wd_f32, wgu_bf_ref, wd_bf_ref, sems, *, layer):
    i = pl.program_id(0)
    used = i < nu_ref[0]
    e = te_ref[i]
    s = sl_ref[i]
    new_expert = jnp.logical_or(i == 0, e != te_ref[jnp.maximum(i - 1, 0)])

    def weight_copies(expert, slot):
        return (pltpu.make_async_copy(wgu_hbm.at[layer, expert], wgu_f32.at[slot], sems.at[0, slot]),
                pltpu.make_async_copy(wd_hbm.at[layer, expert], wd_f32.at[slot], sems.at[1, slot]))

    @pl.when(jnp.logical_and(used, i == 0))
    def _():
        for c in weight_copies(e, s):
            c.start()

    @pl.when(jnp.logical_and(used, new_expert))
    def _():
        for c in weight_copies(e, s):
            c.wait()

        @pl.when(nx_ref[i] >= 0)
        def _():
            for c in weight_copies(nx_ref[i], 1 - s):
                c.start()

        wgu_bf_ref[...] = wgu_f32[s].astype(BF16)
        wd_bf_ref[...] = wd_f32[s].astype(BF16)

    def experts_on(n_rows):
        live = lax.broadcasted_iota(jnp.int32, (n_rows, 1), 0) < nv_ref[i]
        x = jnp.concatenate(_unpack_bf16_pairs(jnp.where(live, xs_ref[:n_rows, :], jnp.uint32(0))), axis=1)
        gu = _dot(x, wgu_bf_ref[...]) + bgu_ref[...]
        gt = jnp.minimum(gu[:, :D_FF], SWIGLU_LIMIT)
        up = jnp.clip(gu[:, D_FF:], -SWIGLU_LIMIT, SWIGLU_LIMIT)
        act = (up + 1.0) * (gt * jax.nn.sigmoid(SWIGLU_ALPHA * gt))
        y = _dot(act.astype(BF16), wd_bf_ref[...]) + bd_ref[...]
        ys_ref[:n_rows, :] = _pack_bf16_pairs(y.astype(BF16))

    for n_rows in range(MOE_SUBTILE, MOE_TILE + 1, MOE_SUBTILE):
        fits = jnp.logical_and(nv_ref[i] > n_rows - MOE_SUBTILE, nv_ref[i] <= n_rows)

        @pl.when(jnp.logical_and(used, fits))
        def _():
            experts_on(n_rows)
            if n_rows < MOE_TILE:
                ys_ref[n_rows:, :] = jnp.zeros((MOE_TILE - n_rows, ys_ref.shape[1]), ys_ref.dtype)

    @pl.when(jnp.logical_not(used))
    def _():
        ys_ref[...] = jnp.zeros_like(ys_ref)


def _experts(xs, counts, tile_e, n_valid, n_used, layer, w_gu, b_gu, w_down, b_down):
    n_rows, W = xs.shape
    n_tiles = n_rows // MOE_TILE
    e_ids = jnp.arange(N_EXPERTS, dtype=jnp.int32)
    nonempty = counts > 0
    rank = jnp.cumsum(nonempty.astype(jnp.int32)) - 1
    later = jnp.logical_and(e_ids[None, :] > e_ids[:, None], nonempty[None, :])
    nxt_of = jnp.min(jnp.where(later, e_ids[None, :], N_EXPERTS), axis=1)
    nxt_of = jnp.where(nxt_of < N_EXPERTS, nxt_of, -1)
    slot = (rank % 2)[tile_e].astype(jnp.int32)
    nxt = nxt_of[tile_e].astype(jnp.int32)
    bias = lambda w: pl.BlockSpec((None, None, 1, w), lambda i, te, nv, nu, nx, sl: (layer, te[i], 0, 0))
    tile = pl.BlockSpec((MOE_TILE, W), lambda i, te, nv, nu, nx, sl: (i, 0))
    grid_spec = pltpu.PrefetchScalarGridSpec(
        num_scalar_prefetch=5,
        grid=(n_tiles,),
        in_specs=[tile, pl.BlockSpec(memory_space=pl.ANY), bias(2 * D_FF),
                  pl.BlockSpec(memory_space=pl.ANY), bias(D_MODEL)],
        out_specs=tile,
        scratch_shapes=[pltpu.VMEM((2, D_MODEL, 2 * D_FF), F32), pltpu.VMEM((2, D_FF, D_MODEL), F32),
                        pltpu.VMEM((D_MODEL, 2 * D_FF), BF16), pltpu.VMEM((D_FF, D_MODEL), BF16),
                        pltpu.SemaphoreType.DMA((2, 2))],
    )
    return pl.pallas_call(
        functools.partial(_expert_body, layer=layer),
        grid_spec=grid_spec,
        out_shape=jax.ShapeDtypeStruct((n_rows, W), jnp.uint32),
        compiler_params=_cparams(("arbitrary",)),
        name="moe_experts",
    )(tile_e, n_valid, n_used, nxt, slot, xs, w_gu, b_gu.reshape(DEPTH, N_EXPERTS, 1, 2 * D_FF), w_down,
      b_down.reshape(DEPTH, N_EXPERTS, 1, D_MODEL))


def _combine_body(x_ref, y0_ref, y1_ref, y2_ref, y3_ref, gate_ref, g2_ref, *rest, final, chained):
    rest = list(rest)
    o_ref = rest.pop()
    if chained:
        rest.pop()
    g = gate_ref[...]
    acc_hi = acc_lo = None
    for k, y_ref in enumerate((y0_ref, y1_ref, y2_ref, y3_ref)):
        p = y_ref[...]
        gk = g[:, k:k + 1]
        hi = gk * pltpu.bitcast(p & jnp.uint32(0xFFFF0000), F32)
        lo = gk * pltpu.bitcast(p << 16, F32)
        acc_hi = hi if acc_hi is None else acc_hi + hi
        acc_lo = lo if acc_lo is None else acc_lo + lo
    acc = jnp.concatenate([acc_hi, acc_lo], axis=1)
    xn = x_ref[...] + _scale_rows(acc, g2_ref[...])
    if final:
        xn = xn * lax.rsqrt(jnp.mean(xn * xn, axis=-1, keepdims=True) + EPS) * rest[0][...]
        for t in range(o_ref.shape[1]):
            o_ref[:, t, :] = xn[t * SUBLANES:(t + 1) * SUBLANES, :]
    else:
        o_ref[...] = xn


def _combine(x, rows, moe_out, tok_off, mod, final_g, ctx, prev=None):
    ysg, gates_t = moe_out
    R, D = x.shape
    tl = TOK_TILE
    r0 = rows[0] // tl
    t0 = tok_off // tl
    nt = gates_t.shape[0] // tl
    final = final_g is not None
    y_specs = [pl.BlockSpec((tl, D // 2), lambda i, k=k: (k * nt + t0 + i, 0)) for k in range(TOP_K)]
    in_specs = ([pl.BlockSpec((tl, D), lambda i: (r0 + i, 0))] + y_specs
                + [pl.BlockSpec((tl, TOP_K), lambda i: (t0 + i, 0)), _mod_spec(5, ctx)])
    args = [x, ysg, ysg, ysg, ysg, gates_t, mod]
    if final:
        in_specs.append(_full((1, D)))
        args.append(final_g.reshape(1, D))
        out_spec = pl.BlockSpec((SUBLANES, tl // SUBLANES, D), lambda i: (0, r0 + i, 0))
        out_shape = jax.ShapeDtypeStruct((SUBLANES, R // SUBLANES, D), F32)
    else:
        out_spec = pl.BlockSpec((tl, D), lambda i: (r0 + i, 0))
        out_shape = jax.ShapeDtypeStruct((R, D), F32)
    aliases = {}
    if prev is not None:
        aliases = {len(args): 0}
        in_specs.append(pl.BlockSpec(memory_space=pl.ANY))
        args.append(prev)
    return pl.pallas_call(
        functools.partial(_combine_body, final=final, chained=prev is not None),
        grid=((rows[1] - rows[0]) // tl,),
        in_specs=in_specs,
        out_specs=out_spec,
        out_shape=out_shape,
        input_output_aliases=aliases,
        compiler_params=_cparams(("parallel",)),
        name="moe_combine",
    )(*args)


def _moe(hp, logits_t, toks, layer, w_gu, b_gu, w_down, b_down):
    dest, gates, counts, tile_e, n_valid, n_used, n_tiles = _routing(logits_t[:, toks[0]:toks[1]])
    dest_flat = dest.reshape(-1)
    xs = _dispatch(hp, toks[0], dest_flat, n_tiles * MOE_TILE)
    ys = _experts(xs, counts, tile_e, n_valid, n_used, layer, w_gu, b_gu, w_down, b_down)
    return _gather_rows(ys, dest_flat), gates.T


def _token_mixers(x, pos, mod, h0, p, consts, ctx, need_out):
    R = x.size // x.shape[-1]
    L = R // SUBLANES
    xa, ga, xb, xc, xd = _in_projection(x, pos, mod, p["norm1_g"], p["w_in"], p["b_in"], ctx)
    yf, yb, hfin = _rglru(xa, p["conv_a_w"], p["conv_a_b"], p["wg"], p["bg"], p["rg_lambda"], h0, L)
    if not need_out:
        return None, hfin
    yp = _pool_mixer(xb, p["w_pool"], p["b_pool"], p["pool_scale"], L)
    yc = _fourier_mixer(xc, consts["dft"][L], consts["cc"], consts["sc"], p["w_four"], p["b_four"], L)
    yd = _conformer(xd, p["conv_d_w"], p["conv_d_b"], p["ln_d_g"], p["ln_d_b"], consts["avg"], p["w_pw"],
                    p["b_pw"], L)
    return (yf, yb, ga, yp, yc, yd), hfin


def _pos_embed(n_tokens):
    rows_n = n_tokens // GRID_W
    q = D_MODEL // 4
    omega = 1.0 / (10000.0 ** (jnp.arange(q, dtype=F32) / q))

    def emb(n):
        ang = jnp.arange(n, dtype=F32)[:, None] * omega[None, :]
        return jnp.concatenate([jnp.sin(ang), jnp.cos(ang)], axis=-1)

    return emb(rows_n).reshape(rows_n, 1, D_MODEL // 2), emb(GRID_W)


def _layer_params(l, w_in, b_in, conv_a_w, conv_a_b, w_rg_r, b_rg_r, w_rg_i, b_rg_i, rg_lambda, w_pool, b_pool,
                  pool_scale, w_four, b_four, conv_d_w, conv_d_b, ln_d_g, ln_d_b, w_pw, b_pw, norm1_g):
    wg = jnp.stack([jnp.concatenate([_block_diag(w_rg_r[l, d]), _block_diag(w_rg_i[l, d])], axis=1)
                    for d in range(2)]).astype(BF16)
    bg = jnp.concatenate([b_rg_r[l].reshape(2, 1, W_GROUP), b_rg_i[l].reshape(2, 1, W_GROUP)], axis=-1)
    return dict(
        norm1_g=norm1_g[l], w_in=w_in[l].astype(BF16), b_in=b_in[l],
        conv_a_w=conv_a_w[l], conv_a_b=conv_a_b[l], wg=wg, bg=bg, rg_lambda=rg_lambda[l],
        w_pool=_block_diag(w_pool[l]).astype(BF16), b_pool=b_pool[l], pool_scale=pool_scale[l],
        w_four=_block_diag(w_four[l]).astype(BF16), b_four=b_four[l],
        conv_d_w=conv_d_w[l], conv_d_b=conv_d_b[l], ln_d_g=ln_d_g[l], ln_d_b=ln_d_b[l],
        w_pw=w_pw[l].astype(BF16), b_pw=b_pw[l])


def kernel(x, c, ctx, c_ctx, w_mod, b_mod, norm1_g, norm2_g, w_in, b_in, conv_a_w, conv_a_b, w_rg_r, b_rg_r,
           w_rg_i, b_rg_i, rg_lambda, w_pool, b_pool, pool_scale, w_four, b_four, conv_d_w, conv_d_b, ln_d_g,
           ln_d_b, w_pw, b_pw, w_out, b_out, w_router, b_router, w_gu, b_gu, w_down, b_down, final_norm_g):
    bn, L, D = x.shape
    Lc = ctx.shape[1]
    assert bn == SUBLANES and D == D_MODEL

    pos = _pos_embed(L)
    c_rows = jnp.concatenate([c, jnp.broadcast_to(c_ctx[None], (MOD_ROWS - bn, D))], axis=0)
    mod = _modulation(c_rows, w_mod, b_mod)
    ctx = jnp.transpose(ctx, (1, 0, 2)).reshape(Lc * bn, D)

    cc1, sc1 = _dft_matrices(D_SUB, 1.0 / math.sqrt(D_SUB))
    eye = jnp.eye(N_SUB, dtype=F32)
    consts = dict(
        dft={n: _time_dft_tables(n) for n in sorted({L, Lc})},
        cc=jnp.kron(eye, cc1).astype(BF16), sc=jnp.kron(eye, sc1).astype(BF16),
        avg=jnp.kron(eye, jnp.full((D_SUB, D_SUB), 1.0 / D_SUB, F32)).astype(BF16))
    h_zero = jnp.zeros((2, SUBLANES, W_GROUP), F32)

    for l in range(DEPTH):
        last = l == DEPTH - 1
        p = _layer_params(l, w_in, b_in, conv_a_w, conv_a_b, w_rg_r, b_rg_r, w_rg_i, b_rg_i, rg_lambda, w_pool,
                          b_pool, pool_scale, w_four, b_four, conv_d_w, conv_d_b, ln_d_g, ln_d_b, w_pw, b_pw,
                          norm1_g)
        mod3 = mod[l]
        w_out_bf = w_out[l].astype(BF16)
        wr_t = w_router[l].T
        wr_hi = wr_t.astype(BF16)
        wr2 = jnp.concatenate([wr_hi, (wr_t - wr_hi.astype(F32)).astype(BF16)], axis=0)
        x_pos = pos if l == 0 else None

        mix_c, h_ctx = _token_mixers(ctx, None, mod3, h_zero, p, consts, True, not last)
        mix_x, _ = _token_mixers(x, x_pos, mod3, h_ctx, p, consts, False, True)
        n_ctx, final_g = (0, final_norm_g) if last else (bn * Lc, None)
        T = n_ctx + bn * L
        moe_in = None
        if not last:
            ctx, *moe_in = _out_projection(ctx, None, mix_c, mod3, norm2_g[l], w_out_bf, b_out[l], wr2,
                                           b_router[l], True, T, 0, None)
        x, hp, lg = _out_projection(x, x_pos, mix_x, mod3, norm2_g[l], w_out_bf, b_out[l], wr2, b_router[l],
                                    False, T, n_ctx, moe_in)
        half = T // 2
        moe_a = _moe(hp, lg, (0, half), l, w_gu, b_gu, w_down, b_down)
        moe_b = _moe(hp, lg, (half, T), l, w_gu, b_gu, w_down, b_down)
        if n_ctx:
            ctx = _combine(ctx, (0, n_ctx), moe_a, 0, mod3, None, True)
        split = half - n_ctx
        xa = _combine(x, (0, split), moe_a, n_ctx, mod3, final_g, False)
        x = _combine(x, (split, x.shape[0]), moe_b, 0, mod3, final_g, False, prev=xa)
    return x
```

```python
import functools
import math

import jax
import jax.numpy as jnp
from jax import lax
from jax.experimental import pallas as pl
from jax.experimental.pallas import tpu as pltpu

F32 = jnp.float32
BF16 = jnp.bfloat16

D_MODEL = 1024
DEPTH = 2
GRID_W = 64
W_GROUP = 256
N_SUB = 4
D_SUB = 64
D_IN = 6 * W_GROUP
RG_CONV = 4
RG_C = 8.0
CONF_KERNEL = 31
N_EXPERTS = 32
TOP_K = 4
D_FF = D_MODEL
SWIGLU_LIMIT = 7.0
SWIGLU_ALPHA = 1.702
EPS = 1e-6

SUBLANES = 8
VMEM_LIMIT_BYTES = 56 * 1024 * 1024
MOD_ROWS = 16
RG_HALO = 8 * SUBLANES
POOL_HALO = 8 * SUBLANES
CONF_HALO = 16 * SUBLANES
MOE_TILE = 512
MOE_SUBTILE = 128
TOK_TILE = 256
SC_WINDOW = 64
ROUTE_TILE = 1024


def _cparams(sem):
    return pltpu.CompilerParams(dimension_semantics=sem, vmem_limit_bytes=VMEM_LIMIT_BYTES)


def _full(shape):
    nd = len(shape)
    return pl.BlockSpec(shape, lambda *_: (0,) * nd)


def _dot(a, b):
    return jnp.dot(a, b, preferred_element_type=F32)


def _split_bf16(v):
    hi = v.astype(BF16)
    lo = (v - hi.astype(F32)).astype(BF16)
    return hi, lo


def _mod_body(c_ref, w_ref, b_ref, o_ref):
    c = c_ref[...]
    s = c * jax.nn.sigmoid(c)
    o_ref[...] = jnp.dot(s, w_ref[...], precision=lax.Precision.HIGHEST,
                         preferred_element_type=F32) + b_ref[...]


def _modulation(c_rows, w_mod, b_mod):
    tn = 1536
    n6 = 6 * D_MODEL
    return pl.pallas_call(
        _mod_body,
        grid=(DEPTH, n6 // tn),
        in_specs=[_full((MOD_ROWS, D_MODEL)),
                  pl.BlockSpec((None, D_MODEL, tn), lambda l, j: (l, 0, j)),
                  pl.BlockSpec((None, 1, tn), lambda l, j: (l, 0, j))],
        out_specs=pl.BlockSpec((None, MOD_ROWS, tn), lambda l, j: (l, 0, j)),
        out_shape=jax.ShapeDtypeStruct((DEPTH, MOD_ROWS, n6), F32),
        compiler_params=_cparams(("parallel", "parallel")),
        name="modulation",
    )(c_rows, w_mod, b_mod.reshape(DEPTH, 1, n6))


def _mod_spec(chunk, ctx):
    return pl.BlockSpec((SUBLANES, D_MODEL), lambda i: (1 if ctx else 0, chunk))


def _scale_rows(v, m):
    r, d = v.shape
    return (v.reshape(r // SUBLANES, SUBLANES, d) * m[None]).reshape(r, d)


def _rms_mod(x, g, shift, scale):
    r, d = x.shape
    y = x * lax.rsqrt(jnp.mean(x * x, axis=-1, keepdims=True) + EPS) * g
    y3 = y.reshape(r // SUBLANES, SUBLANES, d)
    return (y3 * (1.0 + scale)[None] + shift[None]).reshape(r, d)


def _load_plus_pos(x_ref, prow_ref, pcol_ref):
    row = prow_ref[...]
    return jnp.concatenate([x_ref[:, t, :] + jnp.concatenate([row, pcol_ref[t:t + 1, :]], axis=1)
                            for t in range(GRID_W)], axis=0)


def _x_spec(x, tr):
    if x.ndim == 3:
        return pl.BlockSpec((SUBLANES, tr // SUBLANES, x.shape[2]), lambda i: (0, i, 0))
    return pl.BlockSpec((tr, x.shape[1]), lambda i: (i, 0))


def _inproj_body(*refs, add_pos):
    if add_pos:
        (x_ref, prow_ref, pcol_ref, sh_ref, sc_ref, g_ref, w_ref, b_ref,
         xa_ref, ga_ref, xb_ref, xc_ref, xd_ref) = refs
        x = _load_plus_pos(x_ref, prow_ref, pcol_ref)
    else:
        x_ref, sh_ref, sc_ref, g_ref, w_ref, b_ref, xa_ref, ga_ref, xb_ref, xc_ref, xd_ref = refs
        x = x_ref[...]
    u = _rms_mod(x, g_ref[...], sh_ref[...], sc_ref[...])
    p = _dot(u.astype(BF16), w_ref[...]) + b_ref[...]
    xa_ref[...] = p[:, 0:256]
    ga_ref[...] = p[:, 256:512]
    xb_ref[...] = p[:, 512:768]
    xc_ref[...] = p[:, 768:1024].astype(BF16)
    xd_ref[...] = p[:, 1024:1536]


def _in_projection(x, pos, mod, norm_g, w_in_bf, b_in, ctx):
    add_pos = pos is not None
    assert add_pos == (x.ndim == 3)
    R, D = x.size // x.shape[-1], x.shape[-1]
    tr = min(R, 512)
    row = lambda w: pl.BlockSpec((tr, w), lambda i: (i, 0))
    in_specs = [_x_spec(x, tr)]
    args = [x]
    if add_pos:
        assert tr == GRID_W * SUBLANES
        in_specs += [pl.BlockSpec((None, 1, D // 2), lambda i: (i, 0, 0)), _full((GRID_W, D // 2))]
        args += list(pos)
    in_specs += [_mod_spec(0, ctx), _mod_spec(1, ctx), _full((1, D)), _full((D, D_IN)), _full((1, D_IN))]
    args += [mod, mod, norm_g.reshape(1, D), w_in_bf, b_in.reshape(1, D_IN)]
    out_shape = [jax.ShapeDtypeStruct((R, 256), F32)] * 3 + [
        jax.ShapeDtypeStruct((R, 256), BF16), jax.ShapeDtypeStruct((R, 512), F32)]
    return pl.pallas_call(
        functools.partial(_inproj_body, add_pos=add_pos),
        grid=(R // tr,),
        in_specs=in_specs,
        out_specs=[row(256), row(256), row(256), row(256), row(512)],
        out_shape=out_shape,
        compiler_params=_cparams(("parallel",)),
        name="in_projection",
    )(*args)


def _rg_gates(xc, wg, bg, lam):
    g = _dot(xc.astype(BF16), wg) + bg
    r = jax.nn.sigmoid(g[:, :W_GROUP])
    gi = jax.nn.sigmoid(g[:, W_GROUP:])
    z = -lam
    softplus = jnp.maximum(z, 0.0) + jnp.log1p(jnp.exp(-jnp.abs(z)))
    log_a = (-RG_C) * r * softplus
    a = jnp.exp(log_a)
    b = jnp.sqrt(-jnp.tanh(log_a) * (a * a + 1.0)) * (gi * xc)
    return a, b


def _rg_body(xf_ref, xfh_ref, xr_ref, xrh_ref, cw_ref, cb_ref, wg_ref, bg_ref, lam_ref, h0_ref,
             yf_ref, yb_ref, hfin_ref, af_ref, ab_ref, hc_ref, *, n, tt):
    i = pl.program_id(0)
    tr = tt * SUBLANES
    keep = RG_HALO - (RG_CONV - 1) * SUBLANES

    @pl.when(i == 0)
    def _():
        hc_ref[...] = h0_ref[...]

    halo = jnp.where(i > 0, xfh_ref[...], 0.0)
    ext = jnp.concatenate([halo[keep:], xf_ref[...]], axis=0)
    xc = cb_ref[0]
    for k in range(RG_CONV):
        xc = xc + cw_ref[0, k:k + 1, :] * ext[k * SUBLANES:k * SUBLANES + tr]
    a, b = _rg_gates(xc, wg_ref[0], bg_ref[0], lam_ref[0])
    af_ref[...] = a
    yf_ref[...] = b

    halo = jnp.where(i > 0, xrh_ref[...], 0.0)
    ext = jnp.concatenate([xr_ref[...], halo[:(RG_CONV - 1) * SUBLANES]], axis=0)
    xc = cb_ref[1]
    for k in range(RG_CONV):
        o = (RG_CONV - 1 - k) * SUBLANES
        xc = xc + cw_ref[1, k:k + 1, :] * ext[o:o + tr]
    a, b = _rg_gates(xc, wg_ref[1], bg_ref[1], lam_ref[1])
    ab_ref[...] = a
    yb_ref[...] = b

    def step(t, carry):
        hf, hb = carry
        rf = pl.multiple_of(t * SUBLANES, SUBLANES)
        hf = af_ref[pl.ds(rf, SUBLANES), :] * hf + yf_ref[pl.ds(rf, SUBLANES), :]
        yf_ref[pl.ds(rf, SUBLANES), :] = hf
        rb = pl.multiple_of((tt - 1 - t) * SUBLANES, SUBLANES)
        hb = ab_ref[pl.ds(rb, SUBLANES), :] * hb + yb_ref[pl.ds(rb, SUBLANES), :]
        yb_ref[pl.ds(rb, SUBLANES), :] = hb
        return hf, hb

    hf, hb = lax.fori_loop(0, tt, step, (hc_ref[0], hc_ref[1]), unroll=8)
    hc_ref[0] = hf
    hc_ref[1] = hb
    hfin_ref[0] = hf
    hfin_ref[1] = hb


def _rglru(xa2, conv_w, conv_b, wg_bf, bg, lam, h0, L):
    tt = min(L, 256)
    n = L // tt
    tr = tt * SUBLANES
    per = tr // RG_HALO
    last_halo = L * SUBLANES // RG_HALO - 1
    row = lambda i: (i, 0)
    rev = lambda i: (n - 1 - i, 0)
    in_specs = [
        pl.BlockSpec((tr, W_GROUP), row),
        pl.BlockSpec((RG_HALO, W_GROUP), lambda i: (jnp.maximum(i * per - 1, 0), 0)),
        pl.BlockSpec((tr, W_GROUP), rev),
        pl.BlockSpec((RG_HALO, W_GROUP), lambda i: (jnp.minimum((n - i) * per, last_halo), 0)),
        _full((2, RG_CONV, W_GROUP)), _full((2, 1, W_GROUP)), _full((2, W_GROUP, 2 * W_GROUP)),
        _full((2, 1, 2 * W_GROUP)), _full((2, 1, W_GROUP)), _full((2, SUBLANES, W_GROUP)),
    ]
    return pl.pallas_call(
        functools.partial(_rg_body, n=n, tt=tt),
        grid=(n,),
        in_specs=in_specs,
        out_specs=[pl.BlockSpec((tr, W_GROUP), row), pl.BlockSpec((tr, W_GROUP), rev),
                   _full((2, SUBLANES, W_GROUP))],
        out_shape=[jax.ShapeDtypeStruct((L * SUBLANES, W_GROUP), F32)] * 2
        + [jax.ShapeDtypeStruct((2, SUBLANES, W_GROUP), F32)],
        scratch_shapes=[pltpu.VMEM((tr, W_GROUP), F32), pltpu.VMEM((tr, W_GROUP), F32),
                        pltpu.VMEM((2, SUBLANES, W_GROUP), F32)],
        compiler_params=_cparams(("arbitrary",)),
        name="rglru",
    )(xa2, xa2, xa2, xa2, conv_w, conv_b.reshape(2, 1, W_GROUP), wg_bf, bg, lam.reshape(2, 1, W_GROUP), h0)


def _pool_body(xm_ref, xp_ref, xn_ref, w_ref, b_ref, s_ref, o_ref, *, n, tt, L):
    i = pl.program_id(0)
    tr = tt * SUBLANES
    S = SUBLANES
    xm = xm_ref[...]
    prev = jnp.where(i > 0, xp_ref[...], 0.0)
    nxt = jnp.where(i < n - 1, xn_ref[...], 0.0)
    xe = jnp.concatenate([prev, xm, nxt], axis=0)
    e = xe.shape[0]
    p2 = xe[S:e] + xe[0:e - S]
    n4 = (tt + 13) * S
    p4 = p2[0:n4] + p2[2 * S:2 * S + n4]
    n8 = (tt + 9) * S
    p8 = p4[0:n8] + p4[4 * S:4 * S + n8]
    s16 = p8[0:tr] + p8[8 * S:8 * S + tr]
    s2 = p2[7 * S:7 * S + tr]
    s4 = p4[6 * S:6 * S + tr]
    s8 = p8[4 * S:4 * S + tr]
    grp = lax.broadcasted_iota(jnp.int32, (1, W_GROUP), 1) // D_SUB
    half = jnp.left_shift(1, grp)
    t = i * tt + lax.broadcasted_iota(jnp.int32, (tr, 1), 0) // S
    cnt = (jnp.minimum(t + half, L) - jnp.maximum(t - half, 0)).astype(F32)
    s = jnp.where(grp == 0, s2, jnp.where(grp == 1, s4, jnp.where(grp == 2, s8, s16)))
    pooled = s / cnt - xm
    y = _dot(pooled.astype(BF16), w_ref[...]) + b_ref[...]
    o_ref[...] = (y * s_ref[...]).astype(BF16)


def _pool_mixer(xb2, w_bd_bf, b, scale, L):
    tt = min(L, 256)
    n = L // tt
    tr = tt * SUBLANES
    per = tr // POOL_HALO
    last_halo = L * SUBLANES // POOL_HALO - 1
    return pl.pallas_call(
        functools.partial(_pool_body, n=n, tt=tt, L=L),
        grid=(n,),
        in_specs=[pl.BlockSpec((tr, W_GROUP), lambda i: (i, 0)),
                  pl.BlockSpec((POOL_HALO, W_GROUP), lambda i: (jnp.maximum(i * per - 1, 0), 0)),
                  pl.BlockSpec((POOL_HALO, W_GROUP), lambda i: (jnp.minimum((i + 1) * per, last_halo), 0)),
                  _full((W_GROUP, W_GROUP)), _full((1, W_GROUP)), _full((1, W_GROUP))],
        out_specs=pl.BlockSpec((tr, W_GROUP), lambda i: (i, 0)),
        out_shape=jax.ShapeDtypeStruct((L * SUBLANES, W_GROUP), BF16),
        compiler_params=_cparams(("parallel",)),
        name="pool_mixer",
    )(xb2, xb2, xb2, w_bd_bf, b.reshape(1, W_GROUP), scale.reshape(1, W_GROUP))


def _fourier_body(c_ref, s_ref, xe_ref, xo_ref, cw_ref, sw_ref, cc_ref, sc_ref, w_ref, b_ref, o_ref):
    c, s = c_ref[...], s_ref[...]
    xe, xo = xe_ref[...], xo_ref[...]
    ec, es = _dot(c, xe), _dot(s, xe)
    oc, os_ = _dot(c, xo), _dot(s, xo)
    cw, sw = cw_ref[...], sw_ref[...]
    tc = cw * oc - sw * os_
    ts = cw * os_ + sw * oc
    for h, (z1, z2) in enumerate(((ec + tc, es + ts), (ec - tc, es - ts))):
        for j in range(xe.shape[1] // W_GROUP):
            sl = slice(j * W_GROUP, (j + 1) * W_GROUP)
            a_hi, a_lo = _split_bf16(z1[:, sl])
            b_hi, b_lo = _split_bf16(z2[:, sl])
            f = ((_dot(a_hi, cc_ref[...]) + _dot(a_lo, cc_ref[...]))
                 - (_dot(b_hi, sc_ref[...]) + _dot(b_lo, sc_ref[...])))
            o_ref[h, :, sl] = (_dot(f.astype(BF16), w_ref[...]) + b_ref[...]).astype(BF16)


def _fourier_mixer(xc, tables, cc, sc, w_bd_bf, b, L):
    ch, sh, cw, sw = tables
    M = L // 2
    ncol = SUBLANES * W_GROUP
    x2 = xc.reshape(M, 2 * ncol)
    nb = 1024
    tk = min(M, 256)
    out = pl.pallas_call(
        _fourier_body,
        grid=(ncol // nb, M // tk),
        in_specs=[pl.BlockSpec((tk, M), lambda j, k: (k, 0)),
                  pl.BlockSpec((tk, M), lambda j, k: (k, 0)),
                  pl.BlockSpec((M, nb), lambda j, k: (0, j)),
                  pl.BlockSpec((M, nb), lambda j, k: (0, ncol // nb + j)),
                  pl.BlockSpec((tk, 1), lambda j, k: (k, 0)),
                  pl.BlockSpec((tk, 1), lambda j, k: (k, 0)),
                  _full((W_GROUP, W_GROUP)), _full((W_GROUP, W_GROUP)), _full((W_GROUP, W_GROUP)),
                  _full((1, W_GROUP))],
        out_specs=pl.BlockSpec((2, tk, nb), lambda j, k: (0, k, j)),
        out_shape=jax.ShapeDtypeStruct((2, M, ncol), BF16),
        compiler_params=_cparams(("parallel", "parallel")),
        name="fourier_mixer",
    )(ch, sh, x2, x2, cw, sw, cc, sc, w_bd_bf, b.reshape(1, W_GROUP))
    return out.reshape(L * SUBLANES, W_GROUP)


def _time_dft_tables(L):
    M = L // 2
    ch, sh = _dft_matrices(M, 1.0 / math.sqrt(L))
    ang = jnp.arange(M, dtype=F32) * (2.0 * math.pi / L)
    return ch.astype(BF16), sh.astype(BF16), jnp.cos(ang).reshape(M, 1), jnp.sin(ang).reshape(M, 1)


def _dft_matrices(L, scale):
    f = 1 << (max(L.bit_length() - 1, 0) // 2)
    n = jnp.arange(L, dtype=jnp.int32)[None, :]

    def table(rows):
        ang = ((rows[:, None] * n) % L).astype(F32) * (2.0 * math.pi / L)
        return jnp.cos(ang), jnp.sin(ang)

    ac, as_ = table(jnp.arange(L // f, dtype=jnp.int32) * f)
    bc, bs = table(jnp.arange(f, dtype=jnp.int32))
    cos = (ac[:, None, :] * bc[None, :, :] - as_[:, None, :] * bs[None, :, :]).reshape(L, L) * scale
    sin = (as_[:, None, :] * bc[None, :, :] + ac[:, None, :] * bs[None, :, :]).reshape(L, L) * scale
    return cos, sin


def _block_diag(w):
    g, a, b = w.shape
    eye = jnp.eye(g, dtype=w.dtype)
    return (eye[:, None, :, None] * w[:, :, None, :]).reshape(g * a, g * b)


CONF_CHUNK = 64


def _conformer_body(xm_ref, xp_ref, xn_ref, cw_ref, cb_ref, lg_ref, lb_ref, avg_ref, w_ref, b_ref,
                    o_ref, v_ref, c_ref, *, n, tt):
    i = pl.program_id(0)
    tr = tt * SUBLANES
    H = CONF_HALO

    def glu(v):
        return v[:, :W_GROUP] * jax.nn.sigmoid(v[:, W_GROUP:])

    v_ref[0:H] = jnp.where(i > 0, glu(xp_ref[...]), 0.0)
    v_ref[H:H + tr] = glu(xm_ref[...])
    v_ref[H + tr:H + tr + H] = jnp.where(i < n - 1, glu(xn_ref[...]), 0.0)

    def chunk(c, carry):
        r0 = pl.multiple_of(c * CONF_CHUNK, CONF_CHUNK)
        acc = jnp.broadcast_to(cb_ref[...], (CONF_CHUNK, W_GROUP))
        for k in range(CONF_KERNEL):
            acc = acc + cw_ref[k:k + 1, :] * v_ref[pl.ds(r0 + (k + 1) * SUBLANES, CONF_CHUNK), :]
        c_ref[pl.ds(r0, CONF_CHUNK), :] = acc
        return carry

    lax.fori_loop(0, tr // CONF_CHUNK, chunk, 0)

    v = c_ref[...]
    avg = avg_ref[...]
    v_hi, v_lo = _split_bf16(v)
    mu = _dot(v_hi, avg) + _dot(v_lo, avg)
    d = v - mu
    q_hi, q_lo = _split_bf16(d * d)
    var = _dot(q_hi, avg) + _dot(q_lo, avg)
    vn = d * lax.rsqrt(var + EPS) * lg_ref[...] + lb_ref[...]
    act = vn * jax.nn.sigmoid(vn)
    o_ref[...] = (_dot(act.astype(BF16), w_ref[...]) + b_ref[...]).astype(BF16)


def _conformer(xd2, conv_w, conv_b, ln_g, ln_b, avg_bf, w_pw_bf, b_pw, L):
    tt = min(L, 256)
    n = L // tt
    tr = tt * SUBLANES
    per = tr // CONF_HALO
    last_halo = L * SUBLANES // CONF_HALO - 1
    vec = lambda a: a.reshape(1, W_GROUP)
    return pl.pallas_call(
        functools.partial(_conformer_body, n=n, tt=tt),
        grid=(n,),
        in_specs=[pl.BlockSpec((tr, 2 * W_GROUP), lambda i: (i, 0)),
                  pl.BlockSpec((CONF_HALO, 2 * W_GROUP), lambda i: (jnp.maximum(i * per - 1, 0), 0)),
                  pl.BlockSpec((CONF_HALO, 2 * W_GROUP), lambda i: (jnp.minimum((i + 1) * per, last_halo), 0)),
                  _full((CONF_KERNEL, W_GROUP)), _full((1, W_GROUP)), _full((1, W_GROUP)), _full((1, W_GROUP)),
                  _full((W_GROUP, W_GROUP)), _full((W_GROUP, W_GROUP)), _full((1, W_GROUP))],
        out_specs=pl.BlockSpec((tr, W_GROUP), lambda i: (i, 0)),
        out_shape=jax.ShapeDtypeStruct((L * SUBLANES, W_GROUP), BF16),
        scratch_shapes=[pltpu.VMEM((tr + 2 * CONF_HALO, W_GROUP), F32), pltpu.VMEM((tr, W_GROUP), F32)],
        compiler_params=_cparams(("parallel",)),
        name="conformer",
    )(xd2, xd2, xd2, conv_w, vec(conv_b), vec(ln_g), vec(ln_b), avg_bf, w_pw_bf, vec(b_pw))


def _gelu_tanh(x):
    return 0.5 * x * (1.0 + jnp.tanh(math.sqrt(2.0 / math.pi) * (x + 0.044715 * (x * x * x))))


def _pack_bf16_pairs(h_bf):
    u = pltpu.bitcast(h_bf.astype(F32), jnp.uint32)
    half = h_bf.shape[1] // 2
    return (u[:, :half] & jnp.uint32(0xFFFF0000)) | (u[:, half:] >> 16)


def _unpack_bf16_pairs(p):
    hi = pltpu.bitcast(p & jnp.uint32(0xFFFF0000), F32).astype(BF16)
    lo = pltpu.bitcast(p << 16, F32).astype(BF16)
    return hi, lo


def _outproj_body(*refs, add_pos, chained):
    refs = list(refs)
    xo_ref, hp_ref, lg_ref = refs[-3:]
    del refs[-5 if chained else -3:]
    if add_pos:
        (x_ref, prow_ref, pcol_ref, yf_ref, yb_ref, ga_ref, yp_ref, yc_ref, yd_ref, g1_ref, sh_ref, sc_ref,
         ng_ref, wo_ref, bo_ref, wr_ref, br_ref) = refs
        x = _load_plus_pos(x_ref, prow_ref, pcol_ref)
    else:
        (x_ref, yf_ref, yb_ref, ga_ref, yp_ref, yc_ref, yd_ref, g1_ref, sh_ref, sc_ref, ng_ref,
         wo_ref, bo_ref, wr_ref, br_ref) = refs
        x = x_ref[...]
    ya = (yf_ref[...] + yb_ref[...]) * _gelu_tanh(ga_ref[...])
    ycat = jnp.concatenate([ya.astype(BF16), yp_ref[...], yc_ref[...], yd_ref[...]], axis=1)
    y = _dot(ycat, wo_ref[...]) + bo_ref[...]
    xn = x + _scale_rows(y, g1_ref[...])
    xo_ref[...] = xn
    h = _rms_mod(xn, ng_ref[...], sh_ref[...], sc_ref[...])
    h_hi, h_lo = _split_bf16(h)
    nt = (((1,), (1,)), ((), ()))
    wr = wr_ref[...]
    acc = lax.dot_general(wr, h_hi, nt, preferred_element_type=F32)
    acc = acc + lax.dot_general(wr, h_lo, nt, preferred_element_type=F32)
    lg_ref[...] = acc[:N_EXPERTS] + acc[N_EXPERTS:] + br_ref[...]
    hp_ref[...] = _pack_bf16_pairs(h_hi)


def _out_projection(x, pos, mix, mod, norm_g, w_out_bf, b_out, wr2_bf, b_router, ctx, n_tok, tok0, prev):
    add_pos = pos is not None
    assert add_pos == (x.ndim == 3)
    R, D = x.size // x.shape[-1], x.shape[-1]
    tr = min(R, 512)
    row = lambda w: pl.BlockSpec((tr, w), lambda i: (i, 0))
    in_specs = [_x_spec(x, tr)]
    args = [x]
    if add_pos:
        assert tr == GRID_W * SUBLANES
        in_specs += [pl.BlockSpec((None, 1, D // 2), lambda i: (i, 0, 0)), _full((GRID_W, D // 2))]
        args += list(pos)
    in_specs += [row(W_GROUP)] * 6
    args += list(mix)
    in_specs += [_mod_spec(2, ctx), _mod_spec(3, ctx), _mod_spec(4, ctx), _full((1, D)), _full((D, D)),
                 _full((1, D)), _full((2 * N_EXPERTS, D)), _full((N_EXPERTS, 1))]
    args += [mod, mod, mod, norm_g.reshape(1, D), w_out_bf, b_out.reshape(1, D), wr2_bf,
             b_router.reshape(N_EXPERTS, 1)]
    aliases = {}
    if prev is not None:
        aliases = {len(args): 1, len(args) + 1: 2}
        in_specs += [pl.BlockSpec(memory_space=pl.ANY)] * 2
        args += list(prev)
    t0 = tok0 // tr
    assert t0 * tr == tok0
    return pl.pallas_call(
        functools.partial(_outproj_body, add_pos=add_pos, chained=prev is not None),
        grid=(R // tr,),
        in_specs=in_specs,
        out_specs=[row(D), pl.BlockSpec((tr, D // 2), lambda i: (t0 + i, 0)),
                   pl.BlockSpec((N_EXPERTS, tr), lambda i: (0, t0 + i))],
        out_shape=[jax.ShapeDtypeStruct((R, D), F32),
                   jax.ShapeDtypeStruct((n_tok, D // 2), jnp.uint32),
                   jax.ShapeDtypeStruct((N_EXPERTS, n_tok), F32)],
        input_output_aliases=aliases,
        compiler_params=_cparams(("parallel",)),
        name="out_projection",
    )(*args)


def _top4(v):
    eid = lax.broadcasted_iota(jnp.int32, v.shape, 0)
    out = []
    work = v
    for _ in range(TOP_K):
        m = jnp.max(work, axis=0, keepdims=True)
        idx = jnp.min(jnp.where(work == m, eid, N_EXPERTS), axis=0, keepdims=True)
        oh = eid == idx
        out.append((m, oh))
        work = jnp.where(oh, -jnp.inf, work)
    return out


def _count_body(lg_ref, cnt_ref):
    @pl.when(pl.program_id(0) == 0)
    def _():
        cnt_ref[...] = jnp.zeros_like(cnt_ref)

    sel = jnp.zeros(lg_ref.shape, F32)
    for _, oh in _top4(lg_ref[...]):
        sel = sel + oh.astype(F32)
    cnt_ref[...] += jnp.sum(sel, axis=1, keepdims=True)


def _route_body(lg_ref, ps_ref, tri_ref, dest_ref, gate_ref, carry_ref):
    @pl.when(pl.program_id(0) == 0)
    def _():
        carry_ref[...] = jnp.zeros_like(carry_ref)

    top = _top4(lg_ref[...])
    sel = jnp.zeros(lg_ref.shape, F32)
    for _, oh in top:
        sel = sel + oh.astype(F32)
    before = _dot(sel.astype(BF16), tri_ref[...]) + carry_ref[...] + ps_ref[...]
    m0 = top[0][0]
    es = [jnp.exp(m - m0) for m, _ in top]
    den = es[0] + es[1] + es[2] + es[3]
    for k, (_, oh) in enumerate(top):
        dest_ref[k:k + 1, :] = jnp.sum(jnp.where(oh, before, 0.0), axis=0, keepdims=True).astype(jnp.int32)
        gate_ref[k:k + 1, :] = es[k] / den
    carry_ref[...] += jnp.sum(sel, axis=1, keepdims=True)


def _per_expert(table, experts):
    hit = experts[:, None] == jnp.arange(N_EXPERTS, dtype=jnp.int32)[None, :]
    return jnp.sum(jnp.where(hit, table[None, :].astype(jnp.int32), 0), axis=1).astype(jnp.int32)


def _routing(logits_t):
    E, T = logits_t.shape
    tt = ROUTE_TILE
    nt = T // tt
    counts = pl.pallas_call(
        _count_body,
        grid=(nt,),
        in_specs=[pl.BlockSpec((E, tt), lambda i: (0, i))],
        out_specs=_full((E, 1)),
        out_shape=jax.ShapeDtypeStruct((E, 1), F32),
        compiler_params=_cparams(("arbitrary",)),
        name="route_count",
    )(logits_t)
    cnt = counts[:, 0].astype(jnp.int32)
    padded = ((cnt + MOE_TILE - 1) // MOE_TILE) * MOE_TILE
    pend = jnp.cumsum(padded)
    pstart = pend - padded
    n_tiles = -(-(T * TOP_K) // MOE_TILE) + N_EXPERTS
    tile_start = jnp.arange(n_tiles, dtype=jnp.int32) * MOE_TILE
    tile_e = jnp.minimum(jnp.sum((pend[None, :] <= tile_start[:, None]).astype(jnp.int32), axis=1), N_EXPERTS - 1)
    n_used = (pend[-1] // MOE_TILE).astype(jnp.int32).reshape(1)
    n_valid = jnp.clip(_per_expert(pstart + cnt, tile_e) - tile_start, 0, MOE_TILE).astype(jnp.int32)
    tri = (jnp.arange(tt)[:, None] < jnp.arange(tt)[None, :]).astype(BF16)
    dest, gates = pl.pallas_call(
        _route_body,
        grid=(nt,),
        in_specs=[pl.BlockSpec((E, tt), lambda i: (0, i)), _full((E, 1)), _full((tt, tt))],
        out_specs=[pl.BlockSpec((TOP_K, tt), lambda i: (0, i)), pl.BlockSpec((TOP_K, tt), lambda i: (0, i))],
        out_shape=[jax.ShapeDtypeStruct((TOP_K, T), jnp.int32), jax.ShapeDtypeStruct((TOP_K, T), F32)],
        scratch_shapes=[pltpu.VMEM((E, 1), F32)],
        compiler_params=_cparams(("arbitrary",)),
        name="route_assign",
    )(logits_t, pstart.astype(F32).reshape(E, 1), tri)
    return dest, gates, cnt, tile_e, n_valid, n_used, n_tiles


def _sc_workers():
    from jax.experimental.pallas import tpu_sc as plsc
    mesh = plsc.VectorSubcoreMesh(core_axis_name="c", subcore_axis_name="s")
    n_workers = mesh.num_cores * mesh.num_subcores
    worker = lambda: lax.axis_index("s") * mesh.num_cores + lax.axis_index("c")
    return mesh, n_workers, worker


def _dispatch(hp, tok0, dest_flat, n_rows):
    W = hp.shape[1]
    T = dest_flat.shape[0] // TOP_K
    mesh, n_workers, worker = _sc_workers()
    per = dest_flat.shape[0] // n_workers
    steps = per // SC_WINDOW
    assert per * n_workers == dest_flat.shape[0] and steps * SC_WINDOW == per and steps % 2 == 0
    assert T % SC_WINDOW == 0 and tok0 % SC_WINDOW == 0

    @functools.partial(
        pl.kernel, mesh=mesh, out_type=jax.ShapeDtypeStruct((n_rows, W), hp.dtype),
        scratch_types=[pltpu.VMEM((SC_WINDOW,), jnp.int32), pltpu.VMEM((SC_WINDOW,), jnp.int32),
                       pltpu.VMEM((SC_WINDOW, W), hp.dtype), pltpu.VMEM((SC_WINDOW, W), hp.dtype),
                       pltpu.SemaphoreType.DMA, pltpu.SemaphoreType.DMA],
        name="moe_dispatch")
    def scatter(hp_hbm, dest_hbm, xs_hbm, idx0, idx1, rows0, rows1, sem0, sem1):
        base = worker() * per
        bufs = ((idx0, rows0, sem0), (idx1, rows1, sem1))

        def window(j, b, first):
            idx_v, rows_v, sem = bufs[b]

            @pl.when(jnp.logical_not(first))
            def _():
                pltpu.make_async_copy(rows_v, xs_hbm.at[idx_v], sem).wait()

            off = pl.multiple_of(base + j * SC_WINDOW, SC_WINDOW)
            tok = pl.multiple_of(tok0 + lax.rem(off, T), SC_WINDOW)
            pltpu.sync_copy(dest_hbm.at[pl.ds(off, SC_WINDOW)], idx_v)
            pltpu.sync_copy(hp_hbm.at[pl.ds(tok, SC_WINDOW)], rows_v)
            pltpu.async_copy(rows_v, xs_hbm.at[idx_v], sem)

        @pl.loop(0, steps, step=2)
        def _(j):
            window(j, 0, j == 0)
            window(j + 1, 1, j == 0)

        for idx_v, rows_v, sem in bufs:
            pltpu.make_async_copy(rows_v, xs_hbm.at[idx_v], sem).wait()

    return scatter(hp, dest_flat)


def _gather_rows(table, idx_flat):
    n = idx_flat.shape[0]
    W = table.shape[1]
    mesh, n_workers, worker = _sc_workers()
    per = n // n_workers
    steps = per // SC_WINDOW
    assert per * n_workers == n and steps * SC_WINDOW == per and steps % 2 == 0

    @functools.partial(
        pl.kernel, mesh=mesh, out_type=jax.ShapeDtypeStruct((n, W), table.dtype),
        scratch_types=[pltpu.VMEM((SC_WINDOW,), jnp.int32), pltpu.VMEM((SC_WINDOW,), jnp.int32),
                       pltpu.VMEM((SC_WINDOW, W), table.dtype), pltpu.VMEM((SC_WINDOW, W), table.dtype),
                       pltpu.SemaphoreType.DMA, pltpu.SemaphoreType.DMA, pltpu.SemaphoreType.DMA],
        name="moe_gather")
    def gather(table_hbm, idx_hbm, out_hbm, idx0, idx1, rows0, rows1, sem0, sem1, gsem):
        base = worker() * per
        bufs = ((idx0, rows0, sem0), (idx1, rows1, sem1))

        def window(j, b, first):
            idx_v, rows_v, sem = bufs[b]
            off = pl.multiple_of(base + j * SC_WINDOW, SC_WINDOW)

            @pl.when(jnp.logical_not(first))
            def _():
                pltpu.make_async_copy(rows_v, out_hbm.at[pl.ds(off, SC_WINDOW)], sem).wait()

            pltpu.sync_copy(idx_hbm.at[pl.ds(off, SC_WINDOW)], idx_v)
            pltpu.async_copy(table_hbm.at[idx_v], rows_v, gsem).wait()
            pltpu.async_copy(rows_v, out_hbm.at[pl.ds(off, SC_WINDOW)], sem)

        @pl.loop(0, steps, step=2)
        def _(j):
            window(j, 0, j == 0)
            window(j + 1, 1, j == 0)

        for _, rows_v, sem in bufs:
            pltpu.make_async_copy(rows_v, out_hbm.at[pl.ds(base, SC_WINDOW)], sem).wait()

    return gather(table, idx_flat)


def _expert_body(te_ref, nv_ref, nu_ref, nx_ref, sl_ref, xs_ref, wgu_hbm, bgu_ref, wd_hbm, bd_ref, ys_ref,
                 wgu_f32, wd_f32, wgu_bf_ref, wd_bf_ref, sems, *, layer):
    i = pl.program_id(0)
    used = i < nu_ref[0]
    e = te_ref[i]
    s = sl_ref[i]
    new_expert = jnp.logical_or(i == 0, e != te_ref[jnp.maximum(i - 1, 0)])

    def weight_copies(expert, slot):
        return (pltpu.make_async_copy(wgu_hbm.at[layer, expert], wgu_f32.at[slot], sems.at[0, slot]),
                pltpu.make_async_copy(wd_hbm.at[layer, expert], wd_f32.at[slot], sems.at[1, slot]))

    @pl.when(jnp.logical_and(used, i == 0))
    def _():
        for c in weight_copies(e, s):
            c.start()

    @pl.when(jnp.logical_and(used, new_expert))
    def _():
        for c in weight_copies(e, s):
            c.wait()

        @pl.when(nx_ref[i] >= 0)
        def _():
            for c in weight_copies(nx_ref[i], 1 - s):
                c.start()

        wgu_bf_ref[...] = wgu_f32[s].astype(BF16)
        wd_bf_ref[...] = wd_f32[s].astype(BF16)

    def experts_on(n_rows):
        live = lax.broadcasted_iota(jnp.int32, (n_rows, 1), 0) < nv_ref[i]
        x = jnp.concatenate(_unpack_bf16_pairs(jnp.where(live, xs_ref[:n_rows, :], jnp.uint32(0))), axis=1)
        gu = _dot(x, wgu_bf_ref[...]) + bgu_ref[...]
        gt = jnp.minimum(gu[:, :D_FF], SWIGLU_LIMIT)
        up = jnp.clip(gu[:, D_FF:], -SWIGLU_LIMIT, SWIGLU_LIMIT)
        act = (up + 1.0) * (gt * jax.nn.sigmoid(SWIGLU_ALPHA * gt))
        y = _dot(act.astype(BF16), wd_bf_ref[...]) + bd_ref[...]
        ys_ref[:n_rows, :] = _pack_bf16_pairs(y.astype(BF16))

    for n_rows in range(MOE_SUBTILE, MOE_TILE + 1, MOE_SUBTILE):
        fits = jnp.logical_and(nv_ref[i] > n_rows - MOE_SUBTILE, nv_ref[i] <= n_rows)

        @pl.when(jnp.logical_and(used, fits))
        def _():
            experts_on(n_rows)
            if n_rows < MOE_TILE:
                ys_ref[n_rows:, :] = jnp.zeros((MOE_TILE - n_rows, ys_ref.shape[1]), ys_ref.dtype)

    @pl.when(jnp.logical_not(used))
    def _():
        ys_ref[...] = jnp.zeros_like(ys_ref)


def _experts(xs, counts, tile_e, n_valid, n_used, layer, w_gu, b_gu, w_down, b_down):
    n_rows, W = xs.shape
    n_tiles = n_rows // MOE_TILE
    e_ids = jnp.arange(N_EXPERTS, dtype=jnp.int32)
    nonempty = counts > 0
    rank = jnp.cumsum(nonempty.astype(jnp.int32)) - 1
    later = jnp.logical_and(e_ids[None, :] > e_ids[:, None], nonempty[None, :])
    nxt_of = jnp.min(jnp.where(later, e_ids[None, :], N_EXPERTS), axis=1)
    nxt_of = jnp.where(nxt_of < N_EXPERTS, nxt_of, -1)
    slot = _per_expert(rank % 2, tile_e)
    nxt = _per_expert(nxt_of, tile_e)
    bias = lambda w: pl.BlockSpec((None, None, 1, w), lambda i, te, nv, nu, nx, sl: (layer, te[i], 0, 0))
    tile = pl.BlockSpec((MOE_TILE, W), lambda i, te, nv, nu, nx, sl: (i, 0))
    grid_spec = pltpu.PrefetchScalarGridSpec(
        num_scalar_prefetch=5,
        grid=(n_tiles,),
        in_specs=[tile, pl.BlockSpec(memory_space=pl.ANY), bias(2 * D_FF),
                  pl.BlockSpec(memory_space=pl.ANY), bias(D_MODEL)],
        out_specs=tile,
        scratch_shapes=[pltpu.VMEM((2, D_MODEL, 2 * D_FF), F32), pltpu.VMEM((2, D_FF, D_MODEL), F32),
                        pltpu.VMEM((D_MODEL, 2 * D_FF), BF16), pltpu.VMEM((D_FF, D_MODEL), BF16),
                        pltpu.SemaphoreType.DMA((2, 2))],
    )
    return pl.pallas_call(
        functools.partial(_expert_body, layer=layer),
        grid_spec=grid_spec,
        out_shape=jax.ShapeDtypeStruct((n_rows, W), jnp.uint32),
        compiler_params=_cparams(("arbitrary",)),
        name="moe_experts",
    )(tile_e, n_valid, n_used, nxt, slot, xs, w_gu, b_gu.reshape(DEPTH, N_EXPERTS, 1, 2 * D_FF), w_down,
      b_down.reshape(DEPTH, N_EXPERTS, 1, D_MODEL))


def _combine_body(x_ref, y0_ref, y1_ref, y2_ref, y3_ref, gate_ref, g2_ref, *rest, final, chained):
    rest = list(rest)
    o_ref = rest.pop()
    if chained:
        rest.pop()
    g = gate_ref[...]
    acc_hi = acc_lo = None
    for k, y_ref in enumerate((y0_ref, y1_ref, y2_ref, y3_ref)):
        p = y_ref[...]
        gk = g[:, k:k + 1]
        hi = gk * pltpu.bitcast(p & jnp.uint32(0xFFFF0000), F32)
        lo = gk * pltpu.bitcast(p << 16, F32)
        acc_hi = hi if acc_hi is None else acc_hi + hi
        acc_lo = lo if acc_lo is None else acc_lo + lo
    acc = jnp.concatenate([acc_hi, acc_lo], axis=1)
    xn = x_ref[...] + _scale_rows(acc, g2_ref[...])
    if final:
        xn = xn * lax.rsqrt(jnp.mean(xn * xn, axis=-1, keepdims=True) + EPS) * rest[0][...]
        for t in range(o_ref.shape[1]):
            o_ref[:, t, :] = xn[t * SUBLANES:(t + 1) * SUBLANES, :]
    else:
        o_ref[...] = xn


def _combine(x, rows, moe_out, tok_off, mod, final_g, ctx, prev=None):
    ysg, gates_t = moe_out
    R, D = x.shape
    tl = TOK_TILE
    r0 = rows[0] // tl
    t0 = tok_off // tl
    nt = gates_t.shape[0] // tl
    final = final_g is not None
    y_specs = [pl.BlockSpec((tl, D // 2), lambda i, k=k: (k * nt + t0 + i, 0)) for k in range(TOP_K)]
    in_specs = ([pl.BlockSpec((tl, D), lambda i: (r0 + i, 0))] + y_specs
                + [pl.BlockSpec((tl, TOP_K), lambda i: (t0 + i, 0)), _mod_spec(5, ctx)])
    args = [x, ysg, ysg, ysg, ysg, gates_t, mod]
    if final:
        in_specs.append(_full((1, D)))
        args.append(final_g.reshape(1, D))
        out_spec = pl.BlockSpec((SUBLANES, tl // SUBLANES, D), lambda i: (0, r0 + i, 0))
        out_shape = jax.ShapeDtypeStruct((SUBLANES, R // SUBLANES, D), F32)
    else:
        out_spec = pl.BlockSpec((tl, D), lambda i: (r0 + i, 0))
        out_shape = jax.ShapeDtypeStruct((R, D), F32)
    aliases = {}
    if prev is not None:
        aliases = {len(args): 0}
        in_specs.append(pl.BlockSpec(memory_space=pl.ANY))
        args.append(prev)
    return pl.pallas_call(
        functools.partial(_combine_body, final=final, chained=prev is not None),
        grid=((rows[1] - rows[0]) // tl,),
        in_specs=in_specs,
        out_specs=out_spec,
        out_shape=out_shape,
        input_output_aliases=aliases,
        compiler_params=_cparams(("parallel",)),
        name="moe_combine",
    )(*args)


def _moe(hp, logits_t, toks, layer, w_gu, b_gu, w_down, b_down):
    dest, gates, counts, tile_e, n_valid, n_used, n_tiles = _routing(logits_t[:, toks[0]:toks[1]])
    dest_flat = dest.reshape(-1)
    xs = _dispatch(hp, toks[0], dest_flat, n_tiles * MOE_TILE)
    ys = _experts(xs, counts, tile_e, n_valid, n_used, layer, w_gu, b_gu, w_down, b_down)
    return _gather_rows(ys, dest_flat), gates.T


def _token_mixers(x, pos, mod, h0, p, consts, ctx, need_out):
    R = x.size // x.shape[-1]
    L = R // SUBLANES
    xa, ga, xb, xc, xd = _in_projection(x, pos, mod, p["norm1_g"], p["w_in"], p["b_in"], ctx)
    yf, yb, hfin = _rglru(xa, p["conv_a_w"], p["conv_a_b"], p["wg"], p["bg"], p["rg_lambda"], h0, L)
    if not need_out:
        return None, hfin
    yp = _pool_mixer(xb, p["w_pool"], p["b_pool"], p["pool_scale"], L)
    yc = _fourier_mixer(xc, consts["dft"][L], consts["cc"], consts["sc"], p["w_four"], p["b_four"], L)
    yd = _conformer(xd, p["conv_d_w"], p["conv_d_b"], p["ln_d_g"], p["ln_d_b"], consts["avg"], p["w_pw"],
                    p["b_pw"], L)
    return (yf, yb, ga, yp, yc, yd), hfin


def _pos_embed(n_tokens):
    rows_n = n_tokens // GRID_W
    q = D_MODEL // 4
    omega = 1.0 / (10000.0 ** (jnp.arange(q, dtype=F32) / q))

    def emb(n):
        ang = jnp.arange(n, dtype=F32)[:, None] * omega[None, :]
        return jnp.concatenate([jnp.sin(ang), jnp.cos(ang)], axis=-1)

    return emb(rows_n).reshape(rows_n, 1, D_MODEL // 2), emb(GRID_W)


def _layer_params(l, w_in, b_in, conv_a_w, conv_a_b, w_rg_r, b_rg_r, w_rg_i, b_rg_i, rg_lambda, w_pool, b_pool,
                  pool_scale, w_four, b_four, conv_d_w, conv_d_b, ln_d_g, ln_d_b, w_pw, b_pw, norm1_g):
    wg = jnp.stack([jnp.concatenate([_block_diag(w_rg_r[l, d]), _block_diag(w_rg_i[l, d])], axis=1)
                    for d in range(2)]).astype(BF16)
    bg = jnp.concatenate([b_rg_r[l].reshape(2, 1, W_GROUP), b_rg_i[l].reshape(2, 1, W_GROUP)], axis=-1)
    return dict(
        norm1_g=norm1_g[l], w_in=w_in[l].astype(BF16), b_in=b_in[l],
        conv_a_w=conv_a_w[l], conv_a_b=conv_a_b[l], wg=wg, bg=bg, rg_lambda=rg_lambda[l],
        w_pool=_block_diag(w_pool[l]).astype(BF16), b_pool=b_pool[l], pool_scale=pool_scale[l],
        w_four=_block_diag(w_four[l]).astype(BF16), b_four=b_four[l],
        conv_d_w=conv_d_w[l], conv_d_b=conv_d_b[l], ln_d_g=ln_d_g[l], ln_d_b=ln_d_b[l],
        w_pw=w_pw[l].astype(BF16), b_pw=b_pw[l])


def kernel(x, c, ctx, c_ctx, w_mod, b_mod, norm1_g, norm2_g, w_in, b_in, conv_a_w, conv_a_b, w_rg_r, b_rg_r,
           w_rg_i, b_rg_i, rg_lambda, w_pool, b_pool, pool_scale, w_four, b_four, conv_d_w, conv_d_b, ln_d_g,
           ln_d_b, w_pw, b_pw, w_out, b_out, w_router, b_router, w_gu, b_gu, w_down, b_down, final_norm_g):
    bn, L, D = x.shape
    Lc = ctx.shape[1]
    assert bn == SUBLANES and D == D_MODEL

    pos = _pos_embed(L)
    c_rows = jnp.concatenate([c, jnp.broadcast_to(c_ctx[None], (MOD_ROWS - bn, D))], axis=0)
    mod = _modulation(c_rows, w_mod, b_mod)
    ctx = jnp.transpose(ctx, (1, 0, 2)).reshape(Lc * bn, D)

    cc1, sc1 = _dft_matrices(D_SUB, 1.0 / math.sqrt(D_SUB))
    eye = jnp.eye(N_SUB, dtype=F32)
    consts = dict(
        dft={n: _time_dft_tables(n) for n in sorted({L, Lc})},
        cc=jnp.kron(eye, cc1).astype(BF16), sc=jnp.kron(eye, sc1).astype(BF16),
        avg=jnp.kron(eye, jnp.full((D_SUB, D_SUB), 1.0 / D_SUB, F32)).astype(BF16))
    h_zero = jnp.zeros((2, SUBLANES, W_GROUP), F32)

    for l in range(DEPTH):
        last = l == DEPTH - 1
        p = _layer_params(l, w_in, b_in, conv_a_w, conv_a_b, w_rg_r, b_rg_r, w_rg_i, b_rg_i, rg_lambda, w_pool,
                          b_pool, pool_scale, w_four, b_four, conv_d_w, conv_d_b, ln_d_g, ln_d_b, w_pw, b_pw,
                          norm1_g)
        mod3 = mod[l]
        w_out_bf = w_out[l].astype(BF16)
        wr_t = w_router[l].T
        wr_hi = wr_t.astype(BF16)
        wr2 = jnp.concatenate([wr_hi, (wr_t - wr_hi.astype(F32)).astype(BF16)], axis=0)
        x_pos = pos if l == 0 else None

        mix_c, h_ctx = _token_mixers(ctx, None, mod3, h_zero, p, consts, True, not last)
        mix_x, _ = _token_mixers(x, x_pos, mod3, h_ctx, p, consts, False, True)
        n_ctx, final_g = (0, final_norm_g) if last else (bn * Lc, None)
        T = n_ctx + bn * L
        moe_in = None
        if not last:
            ctx, *moe_in = _out_projection(ctx, None, mix_c, mod3, norm2_g[l], w_out_bf, b_out[l], wr2,
                                           b_router[l], True, T, 0, None)
        x, hp, lg = _out_projection(x, x_pos, mix_x, mod3, norm2_g[l], w_out_bf, b_out[l], wr2, b_router[l],
                                    False, T, n_ctx, moe_in)
        half = T // 2
        moe_a = _moe(hp, lg, (0, half), l, w_gu, b_gu, w_down, b_down)
        moe_b = _moe(hp, lg, (half, T), l, w_gu, b_gu, w_down, b_down)
        if n_ctx:
            ctx = _combine(ctx, (0, n_ctx), moe_a, 0, mod3, None, True)
        split = half - n_ctx
        xa = _combine(x, (0, split), moe_a, n_ctx, mod3, final_g, False)
        x = _combine(x, (split, x.shape[0]), moe_b, 0, mod3, final_g, False, prev=xa)
    return x
```

```python
import functools
import math

import jax
import jax.numpy as jnp
from jax import lax
from jax.experimental import pallas as pl
from jax.experimental.pallas import tpu as pltpu

F32 = jnp.float32
BF16 = jnp.bfloat16

D_MODEL = 1024
DEPTH = 2
GRID_W = 64
W_GROUP = 256
N_SUB = 4
D_SUB = 64
D_IN = 6 * W_GROUP
RG_CONV = 4
RG_C = 8.0
CONF_KERNEL = 31
N_EXPERTS = 32
TOP_K = 4
D_FF = D_MODEL
SWIGLU_LIMIT = 7.0
SWIGLU_ALPHA = 1.702
EPS = 1e-6

SUBLANES = 8
VMEM_LIMIT_BYTES = 56 * 1024 * 1024
MOD_ROWS = 16
RG_HALO = 8 * SUBLANES
POOL_HALO = 8 * SUBLANES
CONF_HALO = 16 * SUBLANES
MOE_TILE = 512
MOE_SUBTILE = 128
TOK_TILE = 512
SC_WINDOW = 64
ROUTE_TILE = 1024


def _cparams(sem):
    return pltpu.CompilerParams(dimension_semantics=sem, vmem_limit_bytes=VMEM_LIMIT_BYTES)


def _full(shape):
    nd = len(shape)
    return pl.BlockSpec(shape, lambda *_: (0,) * nd)


def _dot(a, b):
    return jnp.dot(a, b, preferred_element_type=F32)


def _split_bf16(v):
    hi = v.astype(BF16)
    lo = (v - hi.astype(F32)).astype(BF16)
    return hi, lo


def _mod_body(c_ref, w_ref, b_ref, o_ref):
    c = c_ref[...]
    s = c * jax.nn.sigmoid(c)
    o_ref[...] = jnp.dot(s, w_ref[...], precision=lax.Precision.HIGHEST,
                         preferred_element_type=F32) + b_ref[...]


def _modulation(c_rows, w_mod, b_mod):
    tn = 1536
    n6 = 6 * D_MODEL
    return pl.pallas_call(
        _mod_body,
        grid=(DEPTH, n6 // tn),
        in_specs=[_full((MOD_ROWS, D_MODEL)),
                  pl.BlockSpec((None, D_MODEL, tn), lambda l, j: (l, 0, j)),
                  pl.BlockSpec((None, 1, tn), lambda l, j: (l, 0, j))],
        out_specs=pl.BlockSpec((None, MOD_ROWS, tn), lambda l, j: (l, 0, j)),
        out_shape=jax.ShapeDtypeStruct((DEPTH, MOD_ROWS, n6), F32),
        compiler_params=_cparams(("parallel", "parallel")),
        name="modulation",
    )(c_rows, w_mod, b_mod.reshape(DEPTH, 1, n6))


def _mod_spec(chunk, ctx):
    return pl.BlockSpec((SUBLANES, D_MODEL), lambda i: (1 if ctx else 0, chunk))


def _scale_rows(v, m):
    r, d = v.shape
    return (v.reshape(r // SUBLANES, SUBLANES, d) * m[None]).reshape(r, d)


def _rms_mod(x, g, shift, scale):
    r, d = x.shape
    y = x * lax.rsqrt(jnp.mean(x * x, axis=-1, keepdims=True) + EPS) * g
    y3 = y.reshape(r // SUBLANES, SUBLANES, d)
    return (y3 * (1.0 + scale)[None] + shift[None]).reshape(r, d)


def _load_plus_pos(x_ref, prow_ref, pcol_ref):
    row = prow_ref[...]
    return jnp.concatenate([x_ref[:, t, :] + jnp.concatenate([row, pcol_ref[t:t + 1, :]], axis=1)
                            for t in range(GRID_W)], axis=0)


def _x_spec(x, tr):
    if x.ndim == 3:
        return pl.BlockSpec((SUBLANES, tr // SUBLANES, x.shape[2]), lambda i: (0, i, 0))
    return pl.BlockSpec((tr, x.shape[1]), lambda i: (i, 0))


def _inproj_body(*refs, add_pos):
    if add_pos:
        (x_ref, prow_ref, pcol_ref, sh_ref, sc_ref, g_ref, w_ref, b_ref,
         xa_ref, ga_ref, xb_ref, xc_ref, xd_ref) = refs
        x = _load_plus_pos(x_ref, prow_ref, pcol_ref)
    else:
        x_ref, sh_ref, sc_ref, g_ref, w_ref, b_ref, xa_ref, ga_ref, xb_ref, xc_ref, xd_ref = refs
        x = x_ref[...]
    u = _rms_mod(x, g_ref[...], sh_ref[...], sc_ref[...])
    p = _dot(u.astype(BF16), w_ref[...]) + b_ref[...]
    xa_ref[...] = p[:, 0:256]
    ga_ref[...] = p[:, 256:512]
    xb_ref[...] = p[:, 512:768]
    xc_ref[...] = p[:, 768:1024].astype(BF16)
    xd_ref[...] = p[:, 1024:1536]


def _in_projection(x, pos, mod, norm_g, w_in_bf, b_in, ctx):
    add_pos = pos is not None
    assert add_pos == (x.ndim == 3)
    R, D = x.size // x.shape[-1], x.shape[-1]
    tr = min(R, 512)
    row = lambda w: pl.BlockSpec((tr, w), lambda i: (i, 0))
    in_specs = [_x_spec(x, tr)]
    args = [x]
    if add_pos:
        assert tr == GRID_W * SUBLANES
        in_specs += [pl.BlockSpec((None, 1, D // 2), lambda i: (i, 0, 0)), _full((GRID_W, D // 2))]
        args += list(pos)
    in_specs += [_mod_spec(0, ctx), _mod_spec(1, ctx), _full((1, D)), _full((D, D_IN)), _full((1, D_IN))]
    args += [mod, mod, norm_g.reshape(1, D), w_in_bf, b_in.reshape(1, D_IN)]
    out_shape = [jax.ShapeDtypeStruct((R, 256), F32)] * 3 + [
        jax.ShapeDtypeStruct((R, 256), BF16), jax.ShapeDtypeStruct((R, 512), F32)]
    return pl.pallas_call(
        functools.partial(_inproj_body, add_pos=add_pos),
        grid=(R // tr,),
        in_specs=in_specs,
        out_specs=[row(256), row(256), row(256), row(256), row(512)],
        out_shape=out_shape,
        compiler_params=_cparams(("parallel",)),
        name="in_projection",
    )(*args)


def _rg_gates(xc, wg, bg, lam):
    g = _dot(xc.astype(BF16), wg) + bg
    r = jax.nn.sigmoid(g[:, :W_GROUP])
    gi = jax.nn.sigmoid(g[:, W_GROUP:])
    z = -lam
    softplus = jnp.maximum(z, 0.0) + jnp.log1p(jnp.exp(-jnp.abs(z)))
    log_a = (-RG_C) * r * softplus
    a = jnp.exp(log_a)
    b = jnp.sqrt(-jnp.tanh(log_a) * (a * a + 1.0)) * (gi * xc)
    return a, b


def _rg_body(xf_ref, xfh_ref, xr_ref, xrh_ref, cw_ref, cb_ref, wg_ref, bg_ref, lam_ref, h0_ref,
             yf_ref, yb_ref, hfin_ref, af_ref, ab_ref, hc_ref, *, n, tt):
    i = pl.program_id(0)
    tr = tt * SUBLANES
    keep = RG_HALO - (RG_CONV - 1) * SUBLANES

    @pl.when(i == 0)
    def _():
        hc_ref[...] = h0_ref[...]

    halo = jnp.where(i > 0, xfh_ref[...], 0.0)
    ext = jnp.concatenate([halo[keep:], xf_ref[...]], axis=0)
    xc = cb_ref[0]
    for k in range(RG_CONV):
        xc = xc + cw_ref[0, k:k + 1, :] * ext[k * SUBLANES:k * SUBLANES + tr]
    a, b = _rg_gates(xc, wg_ref[0], bg_ref[0], lam_ref[0])
    af_ref[...] = a
    yf_ref[...] = b

    halo = jnp.where(i > 0, xrh_ref[...], 0.0)
    ext = jnp.concatenate([xr_ref[...], halo[:(RG_CONV - 1) * SUBLANES]], axis=0)
    xc = cb_ref[1]
    for k in range(RG_CONV):
        o = (RG_CONV - 1 - k) * SUBLANES
        xc = xc + cw_ref[1, k:k + 1, :] * ext[o:o + tr]
    a, b = _rg_gates(xc, wg_ref[1], bg_ref[1], lam_ref[1])
    ab_ref[...] = a
    yb_ref[...] = b

    def step(t, carry):
        hf, hb = carry
        rf = pl.multiple_of(t * SUBLANES, SUBLANES)
        hf = af_ref[pl.ds(rf, SUBLANES), :] * hf + yf_ref[pl.ds(rf, SUBLANES), :]
        yf_ref[pl.ds(rf, SUBLANES), :] = hf
        rb = pl.multiple_of((tt - 1 - t) * SUBLANES, SUBLANES)
        hb = ab_ref[pl.ds(rb, SUBLANES), :] * hb + yb_ref[pl.ds(rb, SUBLANES), :]
        yb_ref[pl.ds(rb, SUBLANES), :] = hb
        return hf, hb

    hf, hb = lax.fori_loop(0, tt, step, (hc_ref[0], hc_ref[1]), unroll=8)
    hc_ref[0] = hf
    hc_ref[1] = hb
    hfin_ref[0] = hf
    hfin_ref[1] = hb


def _rglru(xa2, conv_w, conv_b, wg_bf, bg, lam, h0, L):
    tt = min(L, 256)
    n = L // tt
    tr = tt * SUBLANES
    per = tr // RG_HALO
    last_halo = L * SUBLANES // RG_HALO - 1
    row = lambda i: (i, 0)
    rev = lambda i: (n - 1 - i, 0)
    in_specs = [
        pl.BlockSpec((tr, W_GROUP), row),
        pl.BlockSpec((RG_HALO, W_GROUP), lambda i: (jnp.maximum(i * per - 1, 0), 0)),
        pl.BlockSpec((tr, W_GROUP), rev),
        pl.BlockSpec((RG_HALO, W_GROUP), lambda i: (jnp.minimum((n - i) * per, last_halo), 0)),
        _full((2, RG_CONV, W_GROUP)), _full((2, 1, W_GROUP)), _full((2, W_GROUP, 2 * W_GROUP)),
        _full((2, 1, 2 * W_GROUP)), _full((2, 1, W_GROUP)), _full((2, SUBLANES, W_GROUP)),
    ]
    return pl.pallas_call(
        functools.partial(_rg_body, n=n, tt=tt),
        grid=(n,),
        in_specs=in_specs,
        out_specs=[pl.BlockSpec((tr, W_GROUP), row), pl.BlockSpec((tr, W_GROUP), rev),
                   _full((2, SUBLANES, W_GROUP))],
        out_shape=[jax.ShapeDtypeStruct((L * SUBLANES, W_GROUP), F32)] * 2
        + [jax.ShapeDtypeStruct((2, SUBLANES, W_GROUP), F32)],
        scratch_shapes=[pltpu.VMEM((tr, W_GROUP), F32), pltpu.VMEM((tr, W_GROUP), F32),
                        pltpu.VMEM((2, SUBLANES, W_GROUP), F32)],
        compiler_params=_cparams(("arbitrary",)),
        name="rglru",
    )(xa2, xa2, xa2, xa2, conv_w, conv_b.reshape(2, 1, W_GROUP), wg_bf, bg, lam.reshape(2, 1, W_GROUP), h0)


def _pool_body(xm_ref, xp_ref, xn_ref, w_ref, b_ref, s_ref, o_ref, *, n, tt, L):
    i = pl.program_id(0)
    tr = tt * SUBLANES
    S = SUBLANES
    xm = xm_ref[...]
    prev = jnp.where(i > 0, xp_ref[...], 0.0)
    nxt = jnp.where(i < n - 1, xn_ref[...], 0.0)
    xe = jnp.concatenate([prev, xm, nxt], axis=0)
    e = xe.shape[0]
    p2 = xe[S:e] + xe[0:e - S]
    n4 = (tt + 13) * S
    p4 = p2[0:n4] + p2[2 * S:2 * S + n4]
    n8 = (tt + 9) * S
    p8 = p4[0:n8] + p4[4 * S:4 * S + n8]
    s16 = p8[0:tr] + p8[8 * S:8 * S + tr]
    s2 = p2[7 * S:7 * S + tr]
    s4 = p4[6 * S:6 * S + tr]
    s8 = p8[4 * S:4 * S + tr]
    grp = lax.broadcasted_iota(jnp.int32, (1, W_GROUP), 1) // D_SUB
    half = jnp.left_shift(1, grp)
    t = i * tt + lax.broadcasted_iota(jnp.int32, (tr, 1), 0) // S
    cnt = (jnp.minimum(t + half, L) - jnp.maximum(t - half, 0)).astype(F32)
    s = jnp.where(grp == 0, s2, jnp.where(grp == 1, s4, jnp.where(grp == 2, s8, s16)))
    pooled = s / cnt - xm
    y = _dot(pooled.astype(BF16), w_ref[...]) + b_ref[...]
    o_ref[...] = (y * s_ref[...]).astype(BF16)


def _pool_mixer(xb2, w_bd_bf, b, scale, L):
    tt = min(L, 256)
    n = L // tt
    tr = tt * SUBLANES
    per = tr // POOL_HALO
    last_halo = L * SUBLANES // POOL_HALO - 1
    return pl.pallas_call(
        functools.partial(_pool_body, n=n, tt=tt, L=L),
        grid=(n,),
        in_specs=[pl.BlockSpec((tr, W_GROUP), lambda i: (i, 0)),
                  pl.BlockSpec((POOL_HALO, W_GROUP), lambda i: (jnp.maximum(i * per - 1, 0), 0)),
                  pl.BlockSpec((POOL_HALO, W_GROUP), lambda i: (jnp.minimum((i + 1) * per, last_halo), 0)),
                  _full((W_GROUP, W_GROUP)), _full((1, W_GROUP)), _full((1, W_GROUP))],
        out_specs=pl.BlockSpec((tr, W_GROUP), lambda i: (i, 0)),
        out_shape=jax.ShapeDtypeStruct((L * SUBLANES, W_GROUP), BF16),
        compiler_params=_cparams(("parallel",)),
        name="pool_mixer",
    )(xb2, xb2, xb2, w_bd_bf, b.reshape(1, W_GROUP), scale.reshape(1, W_GROUP))


def _fourier_body(c_ref, s_ref, xe_ref, xo_ref, cw_ref, sw_ref, cc_ref, sc_ref, w_ref, b_ref, o_ref):
    c, s = c_ref[...], s_ref[...]
    xe, xo = xe_ref[...], xo_ref[...]
    ec, es = _dot(c, xe), _dot(s, xe)
    oc, os_ = _dot(c, xo), _dot(s, xo)
    cw, sw = cw_ref[...], sw_ref[...]
    tc = cw * oc - sw * os_
    ts = cw * os_ + sw * oc
    for h, (z1, z2) in enumerate(((ec + tc, es + ts), (ec - tc, es - ts))):
        for j in range(xe.shape[1] // W_GROUP):
            sl = slice(j * W_GROUP, (j + 1) * W_GROUP)
            a_hi, a_lo = _split_bf16(z1[:, sl])
            b_hi, b_lo = _split_bf16(z2[:, sl])
            f = ((_dot(a_hi, cc_ref[...]) + _dot(a_lo, cc_ref[...]))
                 - (_dot(b_hi, sc_ref[...]) + _dot(b_lo, sc_ref[...])))
            o_ref[h, :, sl] = (_dot(f.astype(BF16), w_ref[...]) + b_ref[...]).astype(BF16)


def _fourier_mixer(xc, tables, cc, sc, w_bd_bf, b, L):
    ch, sh, cw, sw = tables
    M = L // 2
    ncol = SUBLANES * W_GROUP
    x2 = xc.reshape(M, 2 * ncol)
    nb = 1024
    tk = min(M, 256)
    out = pl.pallas_call(
        _fourier_body,
        grid=(ncol // nb, M // tk),
        in_specs=[pl.BlockSpec((tk, M), lambda j, k: (k, 0)),
                  pl.BlockSpec((tk, M), lambda j, k: (k, 0)),
                  pl.BlockSpec((M, nb), lambda j, k: (0, j)),
                  pl.BlockSpec((M, nb), lambda j, k: (0, ncol // nb + j)),
                  pl.BlockSpec((tk, 1), lambda j, k: (k, 0)),
                  pl.BlockSpec((tk, 1), lambda j, k: (k, 0)),
                  _full((W_GROUP, W_GROUP)), _full((W_GROUP, W_GROUP)), _full((W_GROUP, W_GROUP)),
                  _full((1, W_GROUP))],
        out_specs=pl.BlockSpec((2, tk, nb), lambda j, k: (0, k, j)),
        out_shape=jax.ShapeDtypeStruct((2, M, ncol), BF16),
        compiler_params=_cparams(("parallel", "parallel")),
        name="fourier_mixer",
    )(ch, sh, x2, x2, cw, sw, cc, sc, w_bd_bf, b.reshape(1, W_GROUP))
    return out.reshape(L * SUBLANES, W_GROUP)


def _time_dft_tables(L):
    M = L // 2
    ch, sh = _dft_matrices(M, 1.0 / math.sqrt(L))
    ang = jnp.arange(M, dtype=F32) * (2.0 * math.pi / L)
    return ch.astype(BF16), sh.astype(BF16), jnp.cos(ang).reshape(M, 1), jnp.sin(ang).reshape(M, 1)


def _dft_matrices(L, scale):
    f = 1 << (max(L.bit_length() - 1, 0) // 2)
    n = jnp.arange(L, dtype=jnp.int32)[None, :]

    def table(rows):
        ang = ((rows[:, None] * n) % L).astype(F32) * (2.0 * math.pi / L)
        return jnp.cos(ang), jnp.sin(ang)

    ac, as_ = table(jnp.arange(L // f, dtype=jnp.int32) * f)
    bc, bs = table(jnp.arange(f, dtype=jnp.int32))
    cos = (ac[:, None, :] * bc[None, :, :] - as_[:, None, :] * bs[None, :, :]).reshape(L, L) * scale
    sin = (as_[:, None, :] * bc[None, :, :] + ac[:, None, :] * bs[None, :, :]).reshape(L, L) * scale
    return cos, sin


def _block_diag(w):
    g, a, b = w.shape
    eye = jnp.eye(g, dtype=w.dtype)
    return (eye[:, None, :, None] * w[:, :, None, :]).reshape(g * a, g * b)


CONF_CHUNK = 64


def _conformer_body(xm_ref, xp_ref, xn_ref, cw_ref, cb_ref, lg_ref, lb_ref, avg_ref, w_ref, b_ref,
                    o_ref, v_ref, c_ref, *, n, tt):
    i = pl.program_id(0)
    tr = tt * SUBLANES
    H = CONF_HALO

    def glu(v):
        return v[:, :W_GROUP] * jax.nn.sigmoid(v[:, W_GROUP:])

    v_ref[0:H] = jnp.where(i > 0, glu(xp_ref[...]), 0.0)
    v_ref[H:H + tr] = glu(xm_ref[...])
    v_ref[H + tr:H + tr + H] = jnp.where(i < n - 1, glu(xn_ref[...]), 0.0)

    def chunk(c, carry):
        r0 = pl.multiple_of(c * CONF_CHUNK, CONF_CHUNK)
        acc = jnp.broadcast_to(cb_ref[...], (CONF_CHUNK, W_GROUP))
        for k in range(CONF_KERNEL):
            acc = acc + cw_ref[k:k + 1, :] * v_ref[pl.ds(r0 + (k + 1) * SUBLANES, CONF_CHUNK), :]
        c_ref[pl.ds(r0, CONF_CHUNK), :] = acc
        return carry

    lax.fori_loop(0, tr // CONF_CHUNK, chunk, 0)

    v = c_ref[...]
    avg = avg_ref[...]
    v_hi, v_lo = _split_bf16(v)
    mu = _dot(v_hi, avg) + _dot(v_lo, avg)
    d = v - mu
    q_hi, q_lo = _split_bf16(d * d)
    var = _dot(q_hi, avg) + _dot(q_lo, avg)
    vn = d * lax.rsqrt(var + EPS) * lg_ref[...] + lb_ref[...]
    act = vn * jax.nn.sigmoid(vn)
    o_ref[...] = (_dot(act.astype(BF16), w_ref[...]) + b_ref[...]).astype(BF16)


def _conformer(xd2, conv_w, conv_b, ln_g, ln_b, avg_bf, w_pw_bf, b_pw, L):
    tt = min(L, 256)
    n = L // tt
    tr = tt * SUBLANES
    per = tr // CONF_HALO
    last_halo = L * SUBLANES // CONF_HALO - 1
    vec = lambda a: a.reshape(1, W_GROUP)
    return pl.pallas_call(
        functools.partial(_conformer_body, n=n, tt=tt),
        grid=(n,),
        in_specs=[pl.BlockSpec((tr, 2 * W_GROUP), lambda i: (i, 0)),
                  pl.BlockSpec((CONF_HALO, 2 * W_GROUP), lambda i: (jnp.maximum(i * per - 1, 0), 0)),
                  pl.BlockSpec((CONF_HALO, 2 * W_GROUP), lambda i: (jnp.minimum((i + 1) * per, last_halo), 0)),
                  _full((CONF_KERNEL, W_GROUP)), _full((1, W_GROUP)), _full((1, W_GROUP)), _full((1, W_GROUP)),
                  _full((W_GROUP, W_GROUP)), _full((W_GROUP, W_GROUP)), _full((1, W_GROUP))],
        out_specs=pl.BlockSpec((tr, W_GROUP), lambda i: (i, 0)),
        out_shape=jax.ShapeDtypeStruct((L * SUBLANES, W_GROUP), BF16),
        scratch_shapes=[pltpu.VMEM((tr + 2 * CONF_HALO, W_GROUP), F32), pltpu.VMEM((tr, W_GROUP), F32)],
        compiler_params=_cparams(("parallel",)),
        name="conformer",
    )(xd2, xd2, xd2, conv_w, vec(conv_b), vec(ln_g), vec(ln_b), avg_bf, w_pw_bf, vec(b_pw))


def _gelu_tanh(x):
    return 0.5 * x * (1.0 + jnp.tanh(math.sqrt(2.0 / math.pi) * (x + 0.044715 * (x * x * x))))


def _pack_bf16_pairs(h_bf):
    u = pltpu.bitcast(h_bf.astype(F32), jnp.uint32)
    half = h_bf.shape[1] // 2
    return (u[:, :half] & jnp.uint32(0xFFFF0000)) | (u[:, half:] >> 16)


def _unpack_bf16_pairs(p):
    hi = pltpu.bitcast(p & jnp.uint32(0xFFFF0000), F32).astype(BF16)
    lo = pltpu.bitcast(p << 16, F32).astype(BF16)
    return hi, lo


def _outproj_body(*refs, add_pos, chained):
    refs = list(refs)
    xo_ref, hp_ref, lg_ref = refs[-3:]
    del refs[-5 if chained else -3:]
    if add_pos:
        (x_ref, prow_ref, pcol_ref, yf_ref, yb_ref, ga_ref, yp_ref, yc_ref, yd_ref, g1_ref, sh_ref, sc_ref,
         ng_ref, wo_ref, bo_ref, wr_ref, br_ref) = refs
        x = _load_plus_pos(x_ref, prow_ref, pcol_ref)
    else:
        (x_ref, yf_ref, yb_ref, ga_ref, yp_ref, yc_ref, yd_ref, g1_ref, sh_ref, sc_ref, ng_ref,
         wo_ref, bo_ref, wr_ref, br_ref) = refs
        x = x_ref[...]
    ya = (yf_ref[...] + yb_ref[...]) * _gelu_tanh(ga_ref[...])
    ycat = jnp.concatenate([ya.astype(BF16), yp_ref[...], yc_ref[...], yd_ref[...]], axis=1)
    y = _dot(ycat, wo_ref[...]) + bo_ref[...]
    xn = x + _scale_rows(y, g1_ref[...])
    xo_ref[...] = xn
    h = _rms_mod(xn, ng_ref[...], sh_ref[...], sc_ref[...])
    h_hi, h_lo = _split_bf16(h)
    nt = (((1,), (1,)), ((), ()))
    wr = wr_ref[...]
    acc = lax.dot_general(wr, h_hi, nt, preferred_element_type=F32)
    acc = acc + lax.dot_general(wr, h_lo, nt, preferred_element_type=F32)
    lg_ref[...] = acc[:N_EXPERTS] + acc[N_EXPERTS:] + br_ref[...]
    hp_ref[...] = _pack_bf16_pairs(h_hi)


def _out_projection(x, pos, mix, mod, norm_g, w_out_bf, b_out, wr2_bf, b_router, ctx, n_tok, tok0, prev):
    add_pos = pos is not None
    assert add_pos == (x.ndim == 3)
    R, D = x.size // x.shape[-1], x.shape[-1]
    tr = min(R, 512)
    row = lambda w: pl.BlockSpec((tr, w), lambda i: (i, 0))
    in_specs = [_x_spec(x, tr)]
    args = [x]
    if add_pos:
        assert tr == GRID_W * SUBLANES
        in_specs += [pl.BlockSpec((None, 1, D // 2), lambda i: (i, 0, 0)), _full((GRID_W, D // 2))]
        args += list(pos)
    in_specs += [row(W_GROUP)] * 6
    args += list(mix)
    in_specs += [_mod_spec(2, ctx), _mod_spec(3, ctx), _mod_spec(4, ctx), _full((1, D)), _full((D, D)),
                 _full((1, D)), _full((2 * N_EXPERTS, D)), _full((N_EXPERTS, 1))]
    args += [mod, mod, mod, norm_g.reshape(1, D), w_out_bf, b_out.reshape(1, D), wr2_bf,
             b_router.reshape(N_EXPERTS, 1)]
    aliases = {}
    if prev is not None:
        aliases = {len(args): 1, len(args) + 1: 2}
        in_specs += [pl.BlockSpec(memory_space=pl.ANY)] * 2
        args += list(prev)
    t0 = tok0 // tr
    assert t0 * tr == tok0
    return pl.pallas_call(
        functools.partial(_outproj_body, add_pos=add_pos, chained=prev is not None),
        grid=(R // tr,),
        in_specs=in_specs,
        out_specs=[row(D), pl.BlockSpec((tr, D // 2), lambda i: (t0 + i, 0)),
                   pl.BlockSpec((N_EXPERTS, tr), lambda i: (0, t0 + i))],
        out_shape=[jax.ShapeDtypeStruct((R, D), F32),
                   jax.ShapeDtypeStruct((n_tok, D // 2), jnp.uint32),
                   jax.ShapeDtypeStruct((N_EXPERTS, n_tok), F32)],
        input_output_aliases=aliases,
        compiler_params=_cparams(("parallel",)),
        name="out_projection",
    )(*args)


def _top4(v):
    eid = lax.broadcasted_iota(jnp.int32, v.shape, 0)
    out = []
    work = v
    for _ in range(TOP_K):
        m = jnp.max(work, axis=0, keepdims=True)
        idx = jnp.min(jnp.where(work == m, eid, N_EXPERTS), axis=0, keepdims=True)
        oh = eid == idx
        out.append((m, oh))
        work = jnp.where(oh, -jnp.inf, work)
    return out


def _count_body(lg_ref, cnt_ref):
    @pl.when(pl.program_id(0) == 0)
    def _():
        cnt_ref[...] = jnp.zeros_like(cnt_ref)

    sel = jnp.zeros(lg_ref.shape, F32)
    for _, oh in _top4(lg_ref[...]):
        sel = sel + oh.astype(F32)
    cnt_ref[...] += jnp.sum(sel, axis=1, keepdims=True)


def _route_body(lg_ref, ps_ref, tri_ref, dest_ref, gate_ref, carry_ref):
    @pl.when(pl.program_id(0) == 0)
    def _():
        carry_ref[...] = jnp.zeros_like(carry_ref)

    top = _top4(lg_ref[...])
    sel = jnp.zeros(lg_ref.shape, F32)
    for _, oh in top:
        sel = sel + oh.astype(F32)
    before = _dot(sel.astype(BF16), tri_ref[...]) + carry_ref[...] + ps_ref[...]
    m0 = top[0][0]
    es = [jnp.exp(m - m0) for m, _ in top]
    den = es[0] + es[1] + es[2] + es[3]
    for k, (_, oh) in enumerate(top):
        dest_ref[k:k + 1, :] = jnp.sum(jnp.where(oh, before, 0.0), axis=0, keepdims=True).astype(jnp.int32)
        gate_ref[k:k + 1, :] = es[k] / den
    carry_ref[...] += jnp.sum(sel, axis=1, keepdims=True)


def _per_expert(table, experts):
    hit = experts[:, None] == jnp.arange(N_EXPERTS, dtype=jnp.int32)[None, :]
    return jnp.sum(jnp.where(hit, table[None, :].astype(jnp.int32), 0), axis=1).astype(jnp.int32)


def _routing(logits_t):
    E, T = logits_t.shape
    tt = ROUTE_TILE
    nt = T // tt
    counts = pl.pallas_call(
        _count_body,
        grid=(nt,),
        in_specs=[pl.BlockSpec((E, tt), lambda i: (0, i))],
        out_specs=_full((E, 1)),
        out_shape=jax.ShapeDtypeStruct((E, 1), F32),
        compiler_params=_cparams(("arbitrary",)),
        name="route_count",
    )(logits_t)
    cnt = counts[:, 0].astype(jnp.int32)
    padded = ((cnt + MOE_TILE - 1) // MOE_TILE) * MOE_TILE
    pend = jnp.cumsum(padded)
    pstart = pend - padded
    n_tiles = -(-(T * TOP_K) // MOE_TILE) + N_EXPERTS
    tile_start = jnp.arange(n_tiles, dtype=jnp.int32) * MOE_TILE
    tile_e = jnp.minimum(jnp.sum((pend[None, :] <= tile_start[:, None]).astype(jnp.int32), axis=1), N_EXPERTS - 1)
    n_used = (pend[-1] // MOE_TILE).astype(jnp.int32).reshape(1)
    n_valid = jnp.clip(_per_expert(pstart + cnt, tile_e) - tile_start, 0, MOE_TILE).astype(jnp.int32)
    tri = (jnp.arange(tt)[:, None] < jnp.arange(tt)[None, :]).astype(BF16)
    dest, gates = pl.pallas_call(
        _route_body,
        grid=(nt,),
        in_specs=[pl.BlockSpec((E, tt), lambda i: (0, i)), _full((E, 1)), _full((tt, tt))],
        out_specs=[pl.BlockSpec((TOP_K, tt), lambda i: (0, i)), pl.BlockSpec((TOP_K, tt), lambda i: (0, i))],
        out_shape=[jax.ShapeDtypeStruct((TOP_K, T), jnp.int32), jax.ShapeDtypeStruct((TOP_K, T), F32)],
        scratch_shapes=[pltpu.VMEM((E, 1), F32)],
        compiler_params=_cparams(("arbitrary",)),
        name="route_assign",
    )(logits_t, pstart.astype(F32).reshape(E, 1), tri)
    return dest, gates, cnt, tile_e, n_valid, n_used, n_tiles


def _sc_workers():
    from jax.experimental.pallas import tpu_sc as plsc
    mesh = plsc.VectorSubcoreMesh(core_axis_name="c", subcore_axis_name="s")
    n_workers = mesh.num_cores * mesh.num_subcores
    worker = lambda: lax.axis_index("s") * mesh.num_cores + lax.axis_index("c")
    return mesh, n_workers, worker


def _dispatch(hp, tok0, dest_flat, n_rows):
    W = hp.shape[1]
    T = dest_flat.shape[0] // TOP_K
    mesh, n_workers, worker = _sc_workers()
    per = dest_flat.shape[0] // n_workers
    steps = per // SC_WINDOW
    assert per * n_workers == dest_flat.shape[0] and steps * SC_WINDOW == per and steps % 2 == 0
    assert T % SC_WINDOW == 0 and tok0 % SC_WINDOW == 0

    @functools.partial(
        pl.kernel, mesh=mesh, out_type=jax.ShapeDtypeStruct((n_rows, W), hp.dtype),
        scratch_types=[pltpu.VMEM((SC_WINDOW,), jnp.int32), pltpu.VMEM((SC_WINDOW,), jnp.int32),
                       pltpu.VMEM((SC_WINDOW, W), hp.dtype), pltpu.VMEM((SC_WINDOW, W), hp.dtype),
                       pltpu.SemaphoreType.DMA, pltpu.SemaphoreType.DMA],
        name="moe_dispatch")
    def scatter(hp_hbm, dest_hbm, xs_hbm, idx0, idx1, rows0, rows1, sem0, sem1):
        base = worker() * per
        bufs = ((idx0, rows0, sem0), (idx1, rows1, sem1))

        def window(j, b, first):
            idx_v, rows_v, sem = bufs[b]

            @pl.when(jnp.logical_not(first))
            def _():
                pltpu.make_async_copy(rows_v, xs_hbm.at[idx_v], sem).wait()

            off = pl.multiple_of(base + j * SC_WINDOW, SC_WINDOW)
            tok = pl.multiple_of(tok0 + lax.rem(off, T), SC_WINDOW)
            pltpu.sync_copy(dest_hbm.at[pl.ds(off, SC_WINDOW)], idx_v)
            pltpu.sync_copy(hp_hbm.at[pl.ds(tok, SC_WINDOW)], rows_v)
            pltpu.async_copy(rows_v, xs_hbm.at[idx_v], sem)

        @pl.loop(0, steps, step=2)
        def _(j):
            window(j, 0, j == 0)
            window(j + 1, 1, j == 0)

        for idx_v, rows_v, sem in bufs:
            pltpu.make_async_copy(rows_v, xs_hbm.at[idx_v], sem).wait()

    return scatter(hp, dest_flat)


def _gather_rows(table, idx_flat):
    n = idx_flat.shape[0]
    W = table.shape[1]
    mesh, n_workers, worker = _sc_workers()
    per = n // n_workers
    steps = per // SC_WINDOW
    assert per * n_workers == n and steps * SC_WINDOW == per and steps % 2 == 0

    @functools.partial(
        pl.kernel, mesh=mesh, out_type=jax.ShapeDtypeStruct((n, W), table.dtype),
        scratch_types=[pltpu.VMEM((SC_WINDOW,), jnp.int32), pltpu.VMEM((SC_WINDOW,), jnp.int32),
                       pltpu.VMEM((SC_WINDOW, W), table.dtype), pltpu.VMEM((SC_WINDOW, W), table.dtype),
                       pltpu.SemaphoreType.DMA, pltpu.SemaphoreType.DMA, pltpu.SemaphoreType.DMA],
        name="moe_gather")
    def gather(table_hbm, idx_hbm, out_hbm, idx0, idx1, rows0, rows1, sem0, sem1, gsem):
        base = worker() * per
        bufs = ((idx0, rows0, sem0), (idx1, rows1, sem1))

        def window(j, b, first):
            idx_v, rows_v, sem = bufs[b]
            off = pl.multiple_of(base + j * SC_WINDOW, SC_WINDOW)

            @pl.when(jnp.logical_not(first))
            def _():
                pltpu.make_async_copy(rows_v, out_hbm.at[pl.ds(off, SC_WINDOW)], sem).wait()

            pltpu.sync_copy(idx_hbm.at[pl.ds(off, SC_WINDOW)], idx_v)
            pltpu.async_copy(table_hbm.at[idx_v], rows_v, gsem).wait()
            pltpu.async_copy(rows_v, out_hbm.at[pl.ds(off, SC_WINDOW)], sem)

        @pl.loop(0, steps, step=2)
        def _(j):
            window(j, 0, j == 0)
            window(j + 1, 1, j == 0)

        for _, rows_v, sem in bufs:
            pltpu.make_async_copy(rows_v, out_hbm.at[pl.ds(base, SC_WINDOW)], sem).wait()

    return gather(table, idx_flat)


def _expert_body(te_ref, nv_ref, nu_ref, nx_ref, sl_ref, xs_ref, wgu_hbm, bgu_ref, wd_hbm, bd_ref, ys_ref,
                 wgu_f32, wd_f32, wgu_bf_ref, wd_bf_ref, sems, *, layer):
    i = pl.program_id(0)
    used = i < nu_ref[0]
    e = te_ref[i]
    s = sl_ref[i]
    new_expert = jnp.logical_or(i == 0, e != te_ref[jnp.maximum(i - 1, 0)])

    def weight_copies(expert, slot):
        return (pltpu.make_async_copy(wgu_hbm.at[layer, expert], wgu_f32.at[slot], sems.at[0, slot]),
                pltpu.make_async_copy(wd_hbm.at[layer, expert], wd_f32.at[slot], sems.at[1, slot]))

    @pl.when(jnp.logical_and(used, i == 0))
    def _():
        for c in weight_copies(e, s):
            c.start()

    @pl.when(jnp.logical_and(used, new_expert))
    def _():
        for c in weight_copies(e, s):
            c.wait()

        @pl.when(nx_ref[i] >= 0)
        def _():
            for c in weight_copies(nx_ref[i], 1 - s):
                c.start()

        wgu_bf_ref[...] = wgu_f32[s].astype(BF16)
        wd_bf_ref[...] = wd_f32[s].astype(BF16)

    def experts_on(n_rows):
        live = lax.broadcasted_iota(jnp.int32, (n_rows, 1), 0) < nv_ref[i]
        x = jnp.concatenate(_unpack_bf16_pairs(jnp.where(live, xs_ref[:n_rows, :], jnp.uint32(0))), axis=1)
        gu = _dot(x, wgu_bf_ref[...]) + bgu_ref[...]
        gt = jnp.minimum(gu[:, :D_FF], SWIGLU_LIMIT)
        up = jnp.clip(gu[:, D_FF:], -SWIGLU_LIMIT, SWIGLU_LIMIT)
        act = (up + 1.0) * (gt * jax.nn.sigmoid(SWIGLU_ALPHA * gt))
        y = _dot(act.astype(BF16), wd_bf_ref[...]) + bd_ref[...]
        ys_ref[:n_rows, :] = _pack_bf16_pairs(y.astype(BF16))

    for n_rows in range(MOE_SUBTILE, MOE_TILE + 1, MOE_SUBTILE):
        fits = jnp.logical_and(nv_ref[i] > n_rows - MOE_SUBTILE, nv_ref[i] <= n_rows)

        @pl.when(jnp.logical_and(used, fits))
        def _():
            experts_on(n_rows)
            if n_rows < MOE_TILE:
                ys_ref[n_rows:, :] = jnp.zeros((MOE_TILE - n_rows, ys_ref.shape[1]), ys_ref.dtype)

    @pl.when(jnp.logical_not(used))
    def _():
        ys_ref[...] = jnp.zeros_like(ys_ref)


def _experts(xs, counts, tile_e, n_valid, n_used, layer, w_gu, b_gu, w_down, b_down):
    n_rows, W = xs.shape
    n_tiles = n_rows // MOE_TILE
    e_ids = jnp.arange(N_EXPERTS, dtype=jnp.int32)
    nonempty = counts > 0
    rank = jnp.cumsum(nonempty.astype(jnp.int32)) - 1
    later = jnp.logical_and(e_ids[None, :] > e_ids[:, None], nonempty[None, :])
    nxt_of = jnp.min(jnp.where(later, e_ids[None, :], N_EXPERTS), axis=1)
    nxt_of = jnp.where(nxt_of < N_EXPERTS, nxt_of, -1)
    slot = _per_expert(rank % 2, tile_e)
    nxt = _per_expert(nxt_of, tile_e)
    bias = lambda w: pl.BlockSpec((None, None, 1, w), lambda i, te, nv, nu, nx, sl: (layer, te[i], 0, 0))
    tile = pl.BlockSpec((MOE_TILE, W), lambda i, te, nv, nu, nx, sl: (i, 0))
    grid_spec = pltpu.PrefetchScalarGridSpec(
        num_scalar_prefetch=5,
        grid=(n_tiles,),
        in_specs=[tile, pl.BlockSpec(memory_space=pl.ANY), bias(2 * D_FF),
                  pl.BlockSpec(memory_space=pl.ANY), bias(D_MODEL)],
        out_specs=tile,
        scratch_shapes=[pltpu.VMEM((2, D_MODEL, 2 * D_FF), F32), pltpu.VMEM((2, D_FF, D_MODEL), F32),
                        pltpu.VMEM((D_MODEL, 2 * D_FF), BF16), pltpu.VMEM((D_FF, D_MODEL), BF16),
                        pltpu.SemaphoreType.DMA((2, 2))],
    )
    return pl.pallas_call(
        functools.partial(_expert_body, layer=layer),
        grid_spec=grid_spec,
        out_shape=jax.ShapeDtypeStruct((n_rows, W), jnp.uint32),
        compiler_params=_cparams(("arbitrary",)),
        name="moe_experts",
    )(tile_e, n_valid, n_used, nxt, slot, xs, w_gu, b_gu.reshape(DEPTH, N_EXPERTS, 1, 2 * D_FF), w_down,
      b_down.reshape(DEPTH, N_EXPERTS, 1, D_MODEL))


def _combine_body(x_ref, y0_ref, y1_ref, y2_ref, y3_ref, gate_ref, g2_ref, *rest, final, chained):
    rest = list(rest)
    o_ref = rest.pop()
    if chained:
        rest.pop()
    g = gate_ref[...]
    acc_hi = acc_lo = None
    for k, y_ref in enumerate((y0_ref, y1_ref, y2_ref, y3_ref)):
        p = y_ref[...]
        gk = g[:, k:k + 1]
        hi = gk * pltpu.bitcast(p & jnp.uint32(0xFFFF0000), F32)
        lo = gk * pltpu.bitcast(p << 16, F32)
        acc_hi = hi if acc_hi is None else acc_hi + hi
        acc_lo = lo if acc_lo is None else acc_lo + lo
    acc = jnp.concatenate([acc_hi, acc_lo], axis=1)
    xn = x_ref[...] + _scale_rows(acc, g2_ref[...])
    if final:
        xn = xn * lax.rsqrt(jnp.mean(xn * xn, axis=-1, keepdims=True) + EPS) * rest[0][...]
        for t in range(o_ref.shape[1]):
            o_ref[:, t, :] = xn[t * SUBLANES:(t + 1) * SUBLANES, :]
    else:
        o_ref[...] = xn


def _combine(x, rows, moe_out, tok_off, mod, final_g, ctx, prev=None):
    ysg, gates_t = moe_out
    R, D = x.shape
    tl = TOK_TILE
    r0 = rows[0] // tl
    t0 = tok_off // tl
    nt = gates_t.shape[0] // tl
    final = final_g is not None
    y_specs = [pl.BlockSpec((tl, D // 2), lambda i, k=k: (k * nt + t0 + i, 0)) for k in range(TOP_K)]
    in_specs = ([pl.BlockSpec((tl, D), lambda i: (r0 + i, 0))] + y_specs
                + [pl.BlockSpec((tl, TOP_K), lambda i: (t0 + i, 0)), _mod_spec(5, ctx)])
    args = [x, ysg, ysg, ysg, ysg, gates_t, mod]
    if final:
        in_specs.append(_full((1, D)))
        args.append(final_g.reshape(1, D))
        out_spec = pl.BlockSpec((SUBLANES, tl // SUBLANES, D), lambda i: (0, r0 + i, 0))
        out_shape = jax.ShapeDtypeStruct((SUBLANES, R // SUBLANES, D), F32)
    else:
        out_spec = pl.BlockSpec((tl, D), lambda i: (r0 + i, 0))
        out_shape = jax.ShapeDtypeStruct((R, D), F32)
    aliases = {}
    if prev is not None:
        aliases = {len(args): 0}
        in_specs.append(pl.BlockSpec(memory_space=pl.ANY))
        args.append(prev)
    return pl.pallas_call(
        functools.partial(_combine_body, final=final, chained=prev is not None),
        grid=((rows[1] - rows[0]) // tl,),
        in_specs=in_specs,
        out_specs=out_spec,
        out_shape=out_shape,
        input_output_aliases=aliases,
        compiler_params=_cparams(("parallel",)),
        name="moe_combine",
    )(*args)


def _moe(hp, logits_t, toks, layer, w_gu, b_gu, w_down, b_down):
    dest, gates, counts, tile_e, n_valid, n_used, n_tiles = _routing(logits_t[:, toks[0]:toks[1]])
    dest_flat = dest.reshape(-1)
    xs = _dispatch(hp, toks[0], dest_flat, n_tiles * MOE_TILE)
    ys = _experts(xs, counts, tile_e, n_valid, n_used, layer, w_gu, b_gu, w_down, b_down)
    return _gather_rows(ys, dest_flat), gates.T


def _token_mixers(x, pos, mod, h0, p, consts, ctx, need_out):
    R = x.size // x.shape[-1]
    L = R // SUBLANES
    xa, ga, xb, xc, xd = _in_projection(x, pos, mod, p["norm1_g"], p["w_in"], p["b_in"], ctx)
    yf, yb, hfin = _rglru(xa, p["conv_a_w"], p["conv_a_b"], p["wg"], p["bg"], p["rg_lambda"], h0, L)
    if not need_out:
        return None, hfin
    yp = _pool_mixer(xb, p["w_pool"], p["b_pool"], p["pool_scale"], L)
    yc = _fourier_mixer(xc, consts["dft"][L], consts["cc"], consts["sc"], p["w_four"], p["b_four"], L)
    yd = _conformer(xd, p["conv_d_w"], p["conv_d_b"], p["ln_d_g"], p["ln_d_b"], consts["avg"], p["w_pw"],
                    p["b_pw"], L)
    return (yf, yb, ga, yp, yc, yd), hfin


def _pos_embed(n_tokens):
    rows_n = n_tokens // GRID_W
    q = D_MODEL // 4
    omega = 1.0 / (10000.0 ** (jnp.arange(q, dtype=F32) / q))

    def emb(n):
        ang = jnp.arange(n, dtype=F32)[:, None] * omega[None, :]
        return jnp.concatenate([jnp.sin(ang), jnp.cos(ang)], axis=-1)

    return emb(rows_n).reshape(rows_n, 1, D_MODEL // 2), emb(GRID_W)


def _layer_params(l, w_in, b_in, conv_a_w, conv_a_b, w_rg_r, b_rg_r, w_rg_i, b_rg_i, rg_lambda, w_pool, b_pool,
                  pool_scale, w_four, b_four, conv_d_w, conv_d_b, ln_d_g, ln_d_b, w_pw, b_pw, norm1_g):
    wg = jnp.stack([jnp.concatenate([_block_diag(w_rg_r[l, d]), _block_diag(w_rg_i[l, d])], axis=1)
                    for d in range(2)]).astype(BF16)
    bg = jnp.concatenate([b_rg_r[l].reshape(2, 1, W_GROUP), b_rg_i[l].reshape(2, 1, W_GROUP)], axis=-1)
    return dict(
        norm1_g=norm1_g[l], w_in=w_in[l].astype(BF16), b_in=b_in[l],
        conv_a_w=conv_a_w[l], conv_a_b=conv_a_b[l], wg=wg, bg=bg, rg_lambda=rg_lambda[l],
        w_pool=_block_diag(w_pool[l]).astype(BF16), b_pool=b_pool[l], pool_scale=pool_scale[l],
        w_four=_block_diag(w_four[l]).astype(BF16), b_four=b_four[l],
        conv_d_w=conv_d_w[l], conv_d_b=conv_d_b[l], ln_d_g=ln_d_g[l], ln_d_b=ln_d_b[l],
        w_pw=w_pw[l].astype(BF16), b_pw=b_pw[l])


def kernel(x, c, ctx, c_ctx, w_mod, b_mod, norm1_g, norm2_g, w_in, b_in, conv_a_w, conv_a_b, w_rg_r, b_rg_r,
           w_rg_i, b_rg_i, rg_lambda, w_pool, b_pool, pool_scale, w_four, b_four, conv_d_w, conv_d_b, ln_d_g,
           ln_d_b, w_pw, b_pw, w_out, b_out, w_router, b_router, w_gu, b_gu, w_down, b_down, final_norm_g):
    bn, L, D = x.shape
    Lc = ctx.shape[1]
    assert bn == SUBLANES and D == D_MODEL

    pos = _pos_embed(L)
    c_rows = jnp.concatenate([c, jnp.broadcast_to(c_ctx[None], (MOD_ROWS - bn, D))], axis=0)
    mod = _modulation(c_rows, w_mod, b_mod)
    ctx = jnp.transpose(ctx, (1, 0, 2)).reshape(Lc * bn, D)

    cc1, sc1 = _dft_matrices(D_SUB, 1.0 / math.sqrt(D_SUB))
    eye = jnp.eye(N_SUB, dtype=F32)
    consts = dict(
        dft={n: _time_dft_tables(n) for n in sorted({L, Lc})},
        cc=jnp.kron(eye, cc1).astype(BF16), sc=jnp.kron(eye, sc1).astype(BF16),
        avg=jnp.kron(eye, jnp.full((D_SUB, D_SUB), 1.0 / D_SUB, F32)).astype(BF16))
    h_zero = jnp.zeros((2, SUBLANES, W_GROUP), F32)

    for l in range(DEPTH):
        last = l == DEPTH - 1
        p = _layer_params(l, w_in, b_in, conv_a_w, conv_a_b, w_rg_r, b_rg_r, w_rg_i, b_rg_i, rg_lambda, w_pool,
                          b_pool, pool_scale, w_four, b_four, conv_d_w, conv_d_b, ln_d_g, ln_d_b, w_pw, b_pw,
                          norm1_g)
        mod3 = mod[l]
        w_out_bf = w_out[l].astype(BF16)
        wr_t = w_router[l].T
        wr_hi = wr_t.astype(BF16)
        wr2 = jnp.concatenate([wr_hi, (wr_t - wr_hi.astype(F32)).astype(BF16)], axis=0)
        x_pos = pos if l == 0 else None

        mix_c, h_ctx = _token_mixers(ctx, None, mod3, h_zero, p, consts, True, not last)
        mix_x, _ = _token_mixers(x, x_pos, mod3, h_ctx, p, consts, False, True)
        n_ctx, final_g = (0, final_norm_g) if last else (bn * Lc, None)
        T = n_ctx + bn * L
        moe_in = None
        if not last:
            ctx, *moe_in = _out_projection(ctx, None, mix_c, mod3, norm2_g[l], w_out_bf, b_out[l], wr2,
                                           b_router[l], True, T, 0, None)
        x, hp, lg = _out_projection(x, x_pos, mix_x, mod3, norm2_g[l], w_out_bf, b_out[l], wr2, b_router[l],
                                    False, T, n_ctx, moe_in)
        half = T // 2
        moe_a = _moe(hp, lg, (0, half), l, w_gu, b_gu, w_down, b_down)
        moe_b = _moe(hp, lg, (half, T), l, w_gu, b_gu, w_down, b_down)
        if n_ctx:
            ctx = _combine(ctx, (0, n_ctx), moe_a, 0, mod3, None, True)
        split = half - n_ctx
        xa = _combine(x, (0, split), moe_a, n_ctx, mod3, final_g, False)
        x = _combine(x, (split, x.shape[0]), moe_b, 0, mod3, final_g, False, prev=xa)
    return x
```

```python
import functools
import math

import jax
import jax.numpy as jnp
from jax import lax
from jax.experimental import pallas as pl
from jax.experimental.pallas import tpu as pltpu

F32 = jnp.float32
BF16 = jnp.bfloat16

D_MODEL = 1024
DEPTH = 2
GRID_W = 64
W_GROUP = 256
N_SUB = 4
D_SUB = 64
D_IN = 6 * W_GROUP
RG_CONV = 4
RG_C = 8.0
CONF_KERNEL = 31
N_EXPERTS = 32
TOP_K = 4
D_FF = D_MODEL
SWIGLU_LIMIT = 7.0
SWIGLU_ALPHA = 1.702
EPS = 1e-6

SUBLANES = 8
VMEM_LIMIT_BYTES = 56 * 1024 * 1024
MOD_ROWS = 16
RG_HALO = 8 * SUBLANES
POOL_HALO = 8 * SUBLANES
CONF_HALO = 16 * SUBLANES
MOE_TILE = 512
MOE_SUBTILE = 128
TOK_TILE = 512
SC_WINDOW = 64
ROUTE_TILE = 1024


def _cparams(sem):
    return pltpu.CompilerParams(dimension_semantics=sem, vmem_limit_bytes=VMEM_LIMIT_BYTES)


def _full(shape):
    nd = len(shape)
    return pl.BlockSpec(shape, lambda *_: (0,) * nd)


def _dot(a, b):
    return jnp.dot(a, b, preferred_element_type=F32)


def _split_bf16(v):
    hi = v.astype(BF16)
    lo = (v - hi.astype(F32)).astype(BF16)
    return hi, lo


def _mod_body(c_ref, w_ref, b_ref, o_ref):
    c = c_ref[...]
    s = c * jax.nn.sigmoid(c)
    o_ref[...] = jnp.dot(s, w_ref[...], precision=lax.Precision.HIGHEST,
                         preferred_element_type=F32) + b_ref[...]


def _modulation(c_rows, w_mod, b_mod):
    tn = 1536
    n6 = 6 * D_MODEL
    return pl.pallas_call(
        _mod_body,
        grid=(DEPTH, n6 // tn),
        in_specs=[_full((MOD_ROWS, D_MODEL)),
                  pl.BlockSpec((None, D_MODEL, tn), lambda l, j: (l, 0, j)),
                  pl.BlockSpec((None, 1, tn), lambda l, j: (l, 0, j))],
        out_specs=pl.BlockSpec((None, MOD_ROWS, tn), lambda l, j: (l, 0, j)),
        out_shape=jax.ShapeDtypeStruct((DEPTH, MOD_ROWS, n6), F32),
        compiler_params=_cparams(("parallel", "parallel")),
        name="modulation",
    )(c_rows, w_mod, b_mod.reshape(DEPTH, 1, n6))


def _mod_spec(chunk, ctx):
    return pl.BlockSpec((SUBLANES, D_MODEL), lambda i: (1 if ctx else 0, chunk))


def _scale_rows(v, m):
    r, d = v.shape
    return (v.reshape(r // SUBLANES, SUBLANES, d) * m[None]).reshape(r, d)


def _rms_mod(x, g, shift, scale):
    r, d = x.shape
    y = x * lax.rsqrt(jnp.mean(x * x, axis=-1, keepdims=True) + EPS) * g
    y3 = y.reshape(r // SUBLANES, SUBLANES, d)
    return (y3 * (1.0 + scale)[None] + shift[None]).reshape(r, d)


def _load_plus_pos(x_ref, prow_ref, pcol_ref):
    row = prow_ref[...]
    return jnp.concatenate([x_ref[:, t, :] + jnp.concatenate([row, pcol_ref[t:t + 1, :]], axis=1)
                            for t in range(GRID_W)], axis=0)


def _x_spec(x, tr):
    if x.ndim == 3:
        return pl.BlockSpec((SUBLANES, tr // SUBLANES, x.shape[2]), lambda i: (0, i, 0))
    return pl.BlockSpec((tr, x.shape[1]), lambda i: (i, 0))


def _inproj_body(*refs, add_pos):
    if add_pos:
        (x_ref, prow_ref, pcol_ref, sh_ref, sc_ref, g_ref, w_ref, b_ref,
         xa_ref, ga_ref, xb_ref, xc_ref, xd_ref) = refs
        x = _load_plus_pos(x_ref, prow_ref, pcol_ref)
    else:
        x_ref, sh_ref, sc_ref, g_ref, w_ref, b_ref, xa_ref, ga_ref, xb_ref, xc_ref, xd_ref = refs
        x = x_ref[...]
    u = _rms_mod(x, g_ref[...], sh_ref[...], sc_ref[...])
    p = _dot(u.astype(BF16), w_ref[...]) + b_ref[...]
    xa_ref[...] = p[:, 0:256]
    ga_ref[...] = p[:, 256:512]
    xb_ref[...] = p[:, 512:768]
    xc_ref[...] = p[:, 768:1024].astype(BF16)
    xd_ref[...] = p[:, 1024:1536]


def _in_projection(x, pos, mod, norm_g, w_in_bf, b_in, ctx):
    add_pos = pos is not None
    assert add_pos == (x.ndim == 3)
    R, D = x.size // x.shape[-1], x.shape[-1]
    tr = min(R, 512)
    row = lambda w: pl.BlockSpec((tr, w), lambda i: (i, 0))
    in_specs = [_x_spec(x, tr)]
    args = [x]
    if add_pos:
        assert tr == GRID_W * SUBLANES
        in_specs += [pl.BlockSpec((None, 1, D // 2), lambda i: (i, 0, 0)), _full((GRID_W, D // 2))]
        args += list(pos)
    in_specs += [_mod_spec(0, ctx), _mod_spec(1, ctx), _full((1, D)), _full((D, D_IN)), _full((1, D_IN))]
    args += [mod, mod, norm_g.reshape(1, D), w_in_bf, b_in.reshape(1, D_IN)]
    out_shape = [jax.ShapeDtypeStruct((R, 256), F32)] * 3 + [
        jax.ShapeDtypeStruct((R, 256), BF16), jax.ShapeDtypeStruct((R, 512), F32)]
    return pl.pallas_call(
        functools.partial(_inproj_body, add_pos=add_pos),
        grid=(R // tr,),
        in_specs=in_specs,
        out_specs=[row(256), row(256), row(256), row(256), row(512)],
        out_shape=out_shape,
        compiler_params=_cparams(("parallel",)),
        name="in_projection",
    )(*args)


def _rg_gates(xc, wg, bg, lam):
    g = _dot(xc.astype(BF16), wg) + bg
    r = jax.nn.sigmoid(g[:, :W_GROUP])
    gi = jax.nn.sigmoid(g[:, W_GROUP:])
    z = -lam
    softplus = jnp.maximum(z, 0.0) + jnp.log1p(jnp.exp(-jnp.abs(z)))
    log_a = (-RG_C) * r * softplus
    a = jnp.exp(log_a)
    b = jnp.sqrt(-jnp.tanh(log_a) * (a * a + 1.0)) * (gi * xc)
    return a, b


def _rg_body(xf_ref, xfh_ref, xr_ref, xrh_ref, cw_ref, cb_ref, wg_ref, bg_ref, lam_ref, h0_ref,
             yf_ref, yb_ref, hfin_ref, af_ref, ab_ref, hc_ref, *, n, tt):
    i = pl.program_id(0)
    tr = tt * SUBLANES
    keep = RG_HALO - (RG_CONV - 1) * SUBLANES

    @pl.when(i == 0)
    def _():
        hc_ref[...] = h0_ref[...]

    halo = jnp.where(i > 0, xfh_ref[...], 0.0)
    ext = jnp.concatenate([halo[keep:], xf_ref[...]], axis=0)
    xc = cb_ref[0]
    for k in range(RG_CONV):
        xc = xc + cw_ref[0, k:k + 1, :] * ext[k * SUBLANES:k * SUBLANES + tr]
    a, b = _rg_gates(xc, wg_ref[0], bg_ref[0], lam_ref[0])
    af_ref[...] = a
    yf_ref[...] = b

    halo = jnp.where(i > 0, xrh_ref[...], 0.0)
    ext = jnp.concatenate([xr_ref[...], halo[:(RG_CONV - 1) * SUBLANES]], axis=0)
    xc = cb_ref[1]
    for k in range(RG_CONV):
        o = (RG_CONV - 1 - k) * SUBLANES
        xc = xc + cw_ref[1, k:k + 1, :] * ext[o:o + tr]
    a, b = _rg_gates(xc, wg_ref[1], bg_ref[1], lam_ref[1])
    ab_ref[...] = a
    yb_ref[...] = b

    def step(t, carry):
        hf, hb = carry
        rf = pl.multiple_of(t * SUBLANES, SUBLANES)
        hf = af_ref[pl.ds(rf, SUBLANES), :] * hf + yf_ref[pl.ds(rf, SUBLANES), :]
        yf_ref[pl.ds(rf, SUBLANES), :] = hf
        rb = pl.multiple_of((tt - 1 - t) * SUBLANES, SUBLANES)
        hb = ab_ref[pl.ds(rb, SUBLANES), :] * hb + yb_ref[pl.ds(rb, SUBLANES), :]
        yb_ref[pl.ds(rb, SUBLANES), :] = hb
        return hf, hb

    hf, hb = lax.fori_loop(0, tt, step, (hc_ref[0], hc_ref[1]), unroll=8)
    hc_ref[0] = hf
    hc_ref[1] = hb
    hfin_ref[0] = hf
    hfin_ref[1] = hb


def _rglru(xa2, conv_w, conv_b, wg_bf, bg, lam, h0, L):
    tt = min(L, 256)
    n = L // tt
    tr = tt * SUBLANES
    per = tr // RG_HALO
    last_halo = L * SUBLANES // RG_HALO - 1
    row = lambda i: (i, 0)
    rev = lambda i: (n - 1 - i, 0)
    in_specs = [
        pl.BlockSpec((tr, W_GROUP), row),
        pl.BlockSpec((RG_HALO, W_GROUP), lambda i: (jnp.maximum(i * per - 1, 0), 0)),
        pl.BlockSpec((tr, W_GROUP), rev),
        pl.BlockSpec((RG_HALO, W_GROUP), lambda i: (jnp.minimum((n - i) * per, last_halo), 0)),
        _full((2, RG_CONV, W_GROUP)), _full((2, 1, W_GROUP)), _full((2, W_GROUP, 2 * W_GROUP)),
        _full((2, 1, 2 * W_GROUP)), _full((2, 1, W_GROUP)), _full((2, SUBLANES, W_GROUP)),
    ]
    return pl.pallas_call(
        functools.partial(_rg_body, n=n, tt=tt),
        grid=(n,),
        in_specs=in_specs,
        out_specs=[pl.BlockSpec((tr, W_GROUP), row), pl.BlockSpec((tr, W_GROUP), rev),
                   _full((2, SUBLANES, W_GROUP))],
        out_shape=[jax.ShapeDtypeStruct((L * SUBLANES, W_GROUP), F32)] * 2
        + [jax.ShapeDtypeStruct((2, SUBLANES, W_GROUP), F32)],
        scratch_shapes=[pltpu.VMEM((tr, W_GROUP), F32), pltpu.VMEM((tr, W_GROUP), F32),
                        pltpu.VMEM((2, SUBLANES, W_GROUP), F32)],
        compiler_params=_cparams(("arbitrary",)),
        name="rglru",
    )(xa2, xa2, xa2, xa2, conv_w, conv_b.reshape(2, 1, W_GROUP), wg_bf, bg, lam.reshape(2, 1, W_GROUP), h0)


def _pool_body(xm_ref, xp_ref, xn_ref, w_ref, b_ref, s_ref, o_ref, *, n, tt, L):
    i = pl.program_id(0)
    tr = tt * SUBLANES
    S = SUBLANES
    xm = xm_ref[...]
    prev = jnp.where(i > 0, xp_ref[...], 0.0)
    nxt = jnp.where(i < n - 1, xn_ref[...], 0.0)
    xe = jnp.concatenate([prev, xm, nxt], axis=0)
    e = xe.shape[0]
    p2 = xe[S:e] + xe[0:e - S]
    n4 = (tt + 13) * S
    p4 = p2[0:n4] + p2[2 * S:2 * S + n4]
    n8 = (tt + 9) * S
    p8 = p4[0:n8] + p4[4 * S:4 * S + n8]
    s16 = p8[0:tr] + p8[8 * S:8 * S + tr]
    s2 = p2[7 * S:7 * S + tr]
    s4 = p4[6 * S:6 * S + tr]
    s8 = p8[4 * S:4 * S + tr]
    grp = lax.broadcasted_iota(jnp.int32, (1, W_GROUP), 1) // D_SUB
    half = jnp.left_shift(1, grp)
    t = i * tt + lax.broadcasted_iota(jnp.int32, (tr, 1), 0) // S
    cnt = (jnp.minimum(t + half, L) - jnp.maximum(t - half, 0)).astype(F32)
    s = jnp.where(grp == 0, s2, jnp.where(grp == 1, s4, jnp.where(grp == 2, s8, s16)))
    pooled = s / cnt - xm
    y = _dot(pooled.astype(BF16), w_ref[...]) + b_ref[...]
    o_ref[...] = (y * s_ref[...]).astype(BF16)


def _pool_mixer(xb2, w_bd_bf, b, scale, L):
    tt = min(L, 256)
    n = L // tt
    tr = tt * SUBLANES
    per = tr // POOL_HALO
    last_halo = L * SUBLANES // POOL_HALO - 1
    return pl.pallas_call(
        functools.partial(_pool_body, n=n, tt=tt, L=L),
        grid=(n,),
        in_specs=[pl.BlockSpec((tr, W_GROUP), lambda i: (i, 0)),
                  pl.BlockSpec((POOL_HALO, W_GROUP), lambda i: (jnp.maximum(i * per - 1, 0), 0)),
                  pl.BlockSpec((POOL_HALO, W_GROUP), lambda i: (jnp.minimum((i + 1) * per, last_halo), 0)),
                  _full((W_GROUP, W_GROUP)), _full((1, W_GROUP)), _full((1, W_GROUP))],
        out_specs=pl.BlockSpec((tr, W_GROUP), lambda i: (i, 0)),
        out_shape=jax.ShapeDtypeStruct((L * SUBLANES, W_GROUP), BF16),
        compiler_params=_cparams(("parallel",)),
        name="pool_mixer",
    )(xb2, xb2, xb2, w_bd_bf, b.reshape(1, W_GROUP), scale.reshape(1, W_GROUP))


def _fourier_body(c_ref, s_ref, xe_ref, xo_ref, cw_ref, sw_ref, cc_ref, sc_ref, w_ref, b_ref, o_ref):
    c, s = c_ref[...], s_ref[...]
    xe, xo = xe_ref[...], xo_ref[...]
    ec, es = _dot(c, xe), _dot(s, xe)
    oc, os_ = _dot(c, xo), _dot(s, xo)
    cw, sw = cw_ref[...], sw_ref[...]
    tc = cw * oc - sw * os_
    ts = cw * os_ + sw * oc
    for h, (z1, z2) in enumerate(((ec + tc, es + ts), (ec - tc, es - ts))):
        for j in range(xe.shape[1] // W_GROUP):
            sl = slice(j * W_GROUP, (j + 1) * W_GROUP)
            a_hi, a_lo = _split_bf16(z1[:, sl])
            b_hi, b_lo = _split_bf16(z2[:, sl])
            f = ((_dot(a_hi, cc_ref[...]) + _dot(a_lo, cc_ref[...]))
                 - (_dot(b_hi, sc_ref[...]) + _dot(b_lo, sc_ref[...])))
            o_ref[h, :, sl] = (_dot(f.astype(BF16), w_ref[...]) + b_ref[...]).astype(BF16)


def _fourier_mixer(xc, tables, cc, sc, w_bd_bf, b, L):
    ch, sh, cw, sw = tables
    M = L // 2
    ncol = SUBLANES * W_GROUP
    x2 = xc.reshape(M, 2 * ncol)
    nb = 1024
    tk = min(M, 256)
    out = pl.pallas_call(
        _fourier_body,
        grid=(ncol // nb, M // tk),
        in_specs=[pl.BlockSpec((tk, M), lambda j, k: (k, 0)),
                  pl.BlockSpec((tk, M), lambda j, k: (k, 0)),
                  pl.BlockSpec((M, nb), lambda j, k: (0, j)),
                  pl.BlockSpec((M, nb), lambda j, k: (0, ncol // nb + j)),
                  pl.BlockSpec((tk, 1), lambda j, k: (k, 0)),
                  pl.BlockSpec((tk, 1), lambda j, k: (k, 0)),
                  _full((W_GROUP, W_GROUP)), _full((W_GROUP, W_GROUP)), _full((W_GROUP, W_GROUP)),
                  _full((1, W_GROUP))],
        out_specs=pl.BlockSpec((2, tk, nb), lambda j, k: (0, k, j)),
        out_shape=jax.ShapeDtypeStruct((2, M, ncol), BF16),
        compiler_params=_cparams(("parallel", "parallel")),
        name="fourier_mixer",
    )(ch, sh, x2, x2, cw, sw, cc, sc, w_bd_bf, b.reshape(1, W_GROUP))
    return out.reshape(L * SUBLANES, W_GROUP)


def _time_dft_tables(L):
    M = L // 2
    ch, sh = _dft_matrices(M, 1.0 / math.sqrt(L))
    ang = jnp.arange(M, dtype=F32) * (2.0 * math.pi / L)
    return ch.astype(BF16), sh.astype(BF16), jnp.cos(ang).reshape(M, 1), jnp.sin(ang).reshape(M, 1)


def _dft_matrices(L, scale):
    f = 1 << (max(L.bit_length() - 1, 0) // 2)
    n = jnp.arange(L, dtype=jnp.int32)[None, :]

    def table(rows):
        ang = ((rows[:, None] * n) % L).astype(F32) * (2.0 * math.pi / L)
        return jnp.cos(ang), jnp.sin(ang)

    ac, as_ = table(jnp.arange(L // f, dtype=jnp.int32) * f)
    bc, bs = table(jnp.arange(f, dtype=jnp.int32))
    cos = (ac[:, None, :] * bc[None, :, :] - as_[:, None, :] * bs[None, :, :]).reshape(L, L) * scale
    sin = (as_[:, None, :] * bc[None, :, :] + ac[:, None, :] * bs[None, :, :]).reshape(L, L) * scale
    return cos, sin


def _block_diag(w):
    g, a, b = w.shape
    eye = jnp.eye(g, dtype=w.dtype)
    return (eye[:, None, :, None] * w[:, :, None, :]).reshape(g * a, g * b)


CONF_CHUNK = 64


def _conformer_body(xm_ref, xp_ref, xn_ref, cw_ref, cb_ref, lg_ref, lb_ref, avg_ref, w_ref, b_ref,
                    o_ref, v_ref, c_ref, *, n, tt):
    i = pl.program_id(0)
    tr = tt * SUBLANES
    H = CONF_HALO

    def glu(v):
        return v[:, :W_GROUP] * jax.nn.sigmoid(v[:, W_GROUP:])

    v_ref[0:H] = jnp.where(i > 0, glu(xp_ref[...]), 0.0)
    v_ref[H:H + tr] = glu(xm_ref[...])
    v_ref[H + tr:H + tr + H] = jnp.where(i < n - 1, glu(xn_ref[...]), 0.0)

    def chunk(c, carry):
        r0 = pl.multiple_of(c * CONF_CHUNK, CONF_CHUNK)
        acc = jnp.broadcast_to(cb_ref[...], (CONF_CHUNK, W_GROUP))
        for k in range(CONF_KERNEL):
            acc = acc + cw_ref[k:k + 1, :] * v_ref[pl.ds(r0 + (k + 1) * SUBLANES, CONF_CHUNK), :]
        c_ref[pl.ds(r0, CONF_CHUNK), :] = acc
        return carry

    lax.fori_loop(0, tr // CONF_CHUNK, chunk, 0)

    v = c_ref[...]
    avg = avg_ref[...]
    v_hi, v_lo = _split_bf16(v)
    mu = _dot(v_hi, avg) + _dot(v_lo, avg)
    d = v - mu
    q_hi, q_lo = _split_bf16(d * d)
    var = _dot(q_hi, avg) + _dot(q_lo, avg)
    vn = d * lax.rsqrt(var + EPS) * lg_ref[...] + lb_ref[...]
    act = vn * jax.nn.sigmoid(vn)
    o_ref[...] = (_dot(act.astype(BF16), w_ref[...]) + b_ref[...]).astype(BF16)


def _conformer(xd2, conv_w, conv_b, ln_g, ln_b, avg_bf, w_pw_bf, b_pw, L):
    tt = min(L, 256)
    n = L // tt
    tr = tt * SUBLANES
    per = tr // CONF_HALO
    last_halo = L * SUBLANES // CONF_HALO - 1
    vec = lambda a: a.reshape(1, W_GROUP)
    return pl.pallas_call(
        functools.partial(_conformer_body, n=n, tt=tt),
        grid=(n,),
        in_specs=[pl.BlockSpec((tr, 2 * W_GROUP), lambda i: (i, 0)),
                  pl.BlockSpec((CONF_HALO, 2 * W_GROUP), lambda i: (jnp.maximum(i * per - 1, 0), 0)),
                  pl.BlockSpec((CONF_HALO, 2 * W_GROUP), lambda i: (jnp.minimum((i + 1) * per, last_halo), 0)),
                  _full((CONF_KERNEL, W_GROUP)), _full((1, W_GROUP)), _full((1, W_GROUP)), _full((1, W_GROUP)),
                  _full((W_GROUP, W_GROUP)), _full((W_GROUP, W_GROUP)), _full((1, W_GROUP))],
        out_specs=pl.BlockSpec((tr, W_GROUP), lambda i: (i, 0)),
        out_shape=jax.ShapeDtypeStruct((L * SUBLANES, W_GROUP), BF16),
        scratch_shapes=[pltpu.VMEM((tr + 2 * CONF_HALO, W_GROUP), F32), pltpu.VMEM((tr, W_GROUP), F32)],
        compiler_params=_cparams(("parallel",)),
        name="conformer",
    )(xd2, xd2, xd2, conv_w, vec(conv_b), vec(ln_g), vec(ln_b), avg_bf, w_pw_bf, vec(b_pw))


def _gelu_tanh(x):
    return 0.5 * x * (1.0 + jnp.tanh(math.sqrt(2.0 / math.pi) * (x + 0.044715 * (x * x * x))))


def _pack_bf16_pairs(h_bf):
    u = pltpu.bitcast(h_bf.astype(F32), jnp.uint32)
    half = h_bf.shape[1] // 2
    return (u[:, :half] & jnp.uint32(0xFFFF0000)) | (u[:, half:] >> 16)


def _unpack_bf16_pairs(p):
    hi = pltpu.bitcast(p & jnp.uint32(0xFFFF0000), F32).astype(BF16)
    lo = pltpu.bitcast(p << 16, F32).astype(BF16)
    return hi, lo


def _outproj_body(*refs, add_pos, chained):
    refs = list(refs)
    xo_ref, hp_ref, lg_ref = refs[-3:]
    del refs[-5 if chained else -3:]
    if add_pos:
        (x_ref, prow_ref, pcol_ref, yf_ref, yb_ref, ga_ref, yp_ref, yc_ref, yd_ref, g1_ref, sh_ref, sc_ref,
         ng_ref, wo_ref, bo_ref, wr_ref, br_ref) = refs
        x = _load_plus_pos(x_ref, prow_ref, pcol_ref)
    else:
        (x_ref, yf_ref, yb_ref, ga_ref, yp_ref, yc_ref, yd_ref, g1_ref, sh_ref, sc_ref, ng_ref,
         wo_ref, bo_ref, wr_ref, br_ref) = refs
        x = x_ref[...]
    ya = (yf_ref[...] + yb_ref[...]) * _gelu_tanh(ga_ref[...])
    ycat = jnp.concatenate([ya.astype(BF16), yp_ref[...], yc_ref[...], yd_ref[...]], axis=1)
    y = _dot(ycat, wo_ref[...]) + bo_ref[...]
    xn = x + _scale_rows(y, g1_ref[...])
    xo_ref[...] = xn
    h = _rms_mod(xn, ng_ref[...], sh_ref[...], sc_ref[...])
    h_hi, h_lo = _split_bf16(h)
    nt = (((1,), (1,)), ((), ()))
    wr = wr_ref[...]
    acc = lax.dot_general(wr, h_hi, nt, preferred_element_type=F32)
    acc = acc + lax.dot_general(wr, h_lo, nt, preferred_element_type=F32)
    lg_ref[...] = acc[:N_EXPERTS] + acc[N_EXPERTS:] + br_ref[...]
    hp_ref[...] = _pack_bf16_pairs(h_hi)


def _out_projection(x, pos, mix, mod, norm_g, w_out_bf, b_out, wr2_bf, b_router, ctx, n_tok, tok0, prev):
    add_pos = pos is not None
    assert add_pos == (x.ndim == 3)
    R, D = x.size // x.shape[-1], x.shape[-1]
    tr = min(R, 512)
    row = lambda w: pl.BlockSpec((tr, w), lambda i: (i, 0))
    in_specs = [_x_spec(x, tr)]
    args = [x]
    if add_pos:
        assert tr == GRID_W * SUBLANES
        in_specs += [pl.BlockSpec((None, 1, D // 2), lambda i: (i, 0, 0)), _full((GRID_W, D // 2))]
        args += list(pos)
    in_specs += [row(W_GROUP)] * 6
    args += list(mix)
    in_specs += [_mod_spec(2, ctx), _mod_spec(3, ctx), _mod_spec(4, ctx), _full((1, D)), _full((D, D)),
                 _full((1, D)), _full((2 * N_EXPERTS, D)), _full((N_EXPERTS, 1))]
    args += [mod, mod, mod, norm_g.reshape(1, D), w_out_bf, b_out.reshape(1, D), wr2_bf,
             b_router.reshape(N_EXPERTS, 1)]
    aliases = {}
    if prev is not None:
        aliases = {len(args): 1, len(args) + 1: 2}
        in_specs += [pl.BlockSpec(memory_space=pl.ANY)] * 2
        args += list(prev)
    t0 = tok0 // tr
    assert t0 * tr == tok0
    return pl.pallas_call(
        functools.partial(_outproj_body, add_pos=add_pos, chained=prev is not None),
        grid=(R // tr,),
        in_specs=in_specs,
        out_specs=[row(D), pl.BlockSpec((tr, D // 2), lambda i: (t0 + i, 0)),
                   pl.BlockSpec((N_EXPERTS, tr), lambda i: (0, t0 + i))],
        out_shape=[jax.ShapeDtypeStruct((R, D), F32),
                   jax.ShapeDtypeStruct((n_tok, D // 2), jnp.uint32),
                   jax.ShapeDtypeStruct((N_EXPERTS, n_tok), F32)],
        input_output_aliases=aliases,
        compiler_params=_cparams(("parallel",)),
        name="out_projection",
    )(*args)


def _top4(v):
    eid = lax.broadcasted_iota(jnp.int32, v.shape, 0)
    out = []
    work = v
    for _ in range(TOP_K):
        m = jnp.max(work, axis=0, keepdims=True)
        idx = jnp.min(jnp.where(work == m, eid, N_EXPERTS), axis=0, keepdims=True)
        oh = eid == idx
        out.append((m, oh))
        work = jnp.where(oh, -jnp.inf, work)
    return out


def _count_body(lg_ref, cnt_ref):
    @pl.when(pl.program_id(0) == 0)
    def _():
        cnt_ref[...] = jnp.zeros_like(cnt_ref)

    sel = jnp.zeros(lg_ref.shape, F32)
    for _, oh in _top4(lg_ref[...]):
        sel = sel + oh.astype(F32)
    cnt_ref[...] += jnp.sum(sel, axis=1, keepdims=True)


def _route_body(lg_ref, ps_ref, tri_ref, dest_ref, gate_ref, carry_ref):
    @pl.when(pl.program_id(0) == 0)
    def _():
        carry_ref[...] = jnp.zeros_like(carry_ref)

    top = _top4(lg_ref[...])
    sel = jnp.zeros(lg_ref.shape, F32)
    for _, oh in top:
        sel = sel + oh.astype(F32)
    before = _dot(sel.astype(BF16), tri_ref[...]) + carry_ref[...] + ps_ref[...]
    m0 = top[0][0]
    es = [jnp.exp(m - m0) for m, _ in top]
    den = es[0] + es[1] + es[2] + es[3]
    for k, (_, oh) in enumerate(top):
        dest_ref[k:k + 1, :] = jnp.sum(jnp.where(oh, before, 0.0), axis=0, keepdims=True).astype(jnp.int32)
        gate_ref[k:k + 1, :] = es[k] / den
    carry_ref[...] += jnp.sum(sel, axis=1, keepdims=True)


def _per_expert(table, experts):
    hit = experts[:, None] == jnp.arange(N_EXPERTS, dtype=jnp.int32)[None, :]
    return jnp.sum(jnp.where(hit, table[None, :].astype(jnp.int32), 0), axis=1).astype(jnp.int32)


def _routing(logits_t):
    E, T = logits_t.shape
    tt = ROUTE_TILE
    nt = T // tt
    counts = pl.pallas_call(
        _count_body,
        grid=(nt,),
        in_specs=[pl.BlockSpec((E, tt), lambda i: (0, i))],
        out_specs=_full((E, 1)),
        out_shape=jax.ShapeDtypeStruct((E, 1), F32),
        compiler_params=_cparams(("arbitrary",)),
        name="route_count",
    )(logits_t)
    cnt = counts[:, 0].astype(jnp.int32)
    padded = ((cnt + MOE_TILE - 1) // MOE_TILE) * MOE_TILE
    pend = jnp.cumsum(padded)
    pstart = pend - padded
    n_tiles = -(-(T * TOP_K) // MOE_TILE) + N_EXPERTS
    tile_start = jnp.arange(n_tiles, dtype=jnp.int32) * MOE_TILE
    tile_e = jnp.minimum(jnp.sum((pend[None, :] <= tile_start[:, None]).astype(jnp.int32), axis=1), N_EXPERTS - 1)
    n_used = (pend[-1] // MOE_TILE).astype(jnp.int32).reshape(1)
    n_valid = jnp.clip(_per_expert(pstart + cnt, tile_e) - tile_start, 0, MOE_TILE).astype(jnp.int32)
    tri = (jnp.arange(tt)[:, None] < jnp.arange(tt)[None, :]).astype(BF16)
    dest, gates = pl.pallas_call(
        _route_body,
        grid=(nt,),
        in_specs=[pl.BlockSpec((E, tt), lambda i: (0, i)), _full((E, 1)), _full((tt, tt))],
        out_specs=[pl.BlockSpec((TOP_K, tt), lambda i: (0, i)), pl.BlockSpec((TOP_K, tt), lambda i: (0, i))],
        out_shape=[jax.ShapeDtypeStruct((TOP_K, T), jnp.int32), jax.ShapeDtypeStruct((TOP_K, T), F32)],
        scratch_shapes=[pltpu.VMEM((E, 1), F32)],
        compiler_params=_cparams(("arbitrary",)),
        name="route_assign",
    )(logits_t, pstart.astype(F32).reshape(E, 1), tri)
    return dest, gates, cnt, tile_e, n_valid, n_used, n_tiles


def _sc_workers():
    from jax.experimental.pallas import tpu_sc as plsc
    mesh = plsc.VectorSubcoreMesh(core_axis_name="c", subcore_axis_name="s")
    n_workers = mesh.num_cores * mesh.num_subcores
    worker = lambda: lax.axis_index("s") * mesh.num_cores + lax.axis_index("c")
    return mesh, n_workers, worker


def _dispatch(hp, tok0, dest_flat, n_rows):
    W = hp.shape[1]
    T = dest_flat.shape[0] // TOP_K
    mesh, n_workers, worker = _sc_workers()
    per = dest_flat.shape[0] // n_workers
    steps = per // SC_WINDOW
    assert per * n_workers == dest_flat.shape[0] and steps * SC_WINDOW == per and steps % 2 == 0
    assert T % SC_WINDOW == 0 and tok0 % SC_WINDOW == 0

    @functools.partial(
        pl.kernel, mesh=mesh, out_type=jax.ShapeDtypeStruct((n_rows, W), hp.dtype),
        scratch_types=[pltpu.VMEM((SC_WINDOW,), jnp.int32), pltpu.VMEM((SC_WINDOW,), jnp.int32),
                       pltpu.VMEM((SC_WINDOW, W), hp.dtype), pltpu.VMEM((SC_WINDOW, W), hp.dtype),
                       pltpu.SemaphoreType.DMA, pltpu.SemaphoreType.DMA],
        name="moe_dispatch")
    def scatter(hp_hbm, dest_hbm, xs_hbm, idx0, idx1, rows0, rows1, sem0, sem1):
        base = worker() * per
        bufs = ((idx0, rows0, sem0), (idx1, rows1, sem1))

        def window(j, b, first):
            idx_v, rows_v, sem = bufs[b]

            @pl.when(jnp.logical_not(first))
            def _():
                pltpu.make_async_copy(rows_v, xs_hbm.at[idx_v], sem).wait()

            off = pl.multiple_of(base + j * SC_WINDOW, SC_WINDOW)
            tok = pl.multiple_of(tok0 + lax.rem(off, T), SC_WINDOW)
            pltpu.sync_copy(dest_hbm.at[pl.ds(off, SC_WINDOW)], idx_v)
            pltpu.sync_copy(hp_hbm.at[pl.ds(tok, SC_WINDOW)], rows_v)
            pltpu.async_copy(rows_v, xs_hbm.at[idx_v], sem)

        @pl.loop(0, steps, step=2)
        def _(j):
            window(j, 0, j == 0)
            window(j + 1, 1, j == 0)

        for idx_v, rows_v, sem in bufs:
            pltpu.make_async_copy(rows_v, xs_hbm.at[idx_v], sem).wait()

    return scatter(hp, dest_flat)


def _gather_rows(table, idx_flat):
    n = idx_flat.shape[0]
    W = table.shape[1]
    mesh, n_workers, worker = _sc_workers()
    per = n // n_workers
    steps = per // SC_WINDOW
    assert per * n_workers == n and steps * SC_WINDOW == per and steps % 2 == 0

    @functools.partial(
        pl.kernel, mesh=mesh, out_type=jax.ShapeDtypeStruct((n, W), table.dtype),
        scratch_types=[pltpu.VMEM((SC_WINDOW,), jnp.int32), pltpu.VMEM((SC_WINDOW,), jnp.int32),
                       pltpu.VMEM((SC_WINDOW, W), table.dtype), pltpu.VMEM((SC_WINDOW, W), table.dtype),
                       pltpu.SemaphoreType.DMA, pltpu.SemaphoreType.DMA, pltpu.SemaphoreType.DMA],
        name="moe_gather")
    def gather(table_hbm, idx_hbm, out_hbm, idx0, idx1, rows0, rows1, sem0, sem1, gsem):
        base = worker() * per
        bufs = ((idx0, rows0, sem0), (idx1, rows1, sem1))

        def window(j, b, first):
            idx_v, rows_v, sem = bufs[b]
            off = pl.multiple_of(base + j * SC_WINDOW, SC_WINDOW)

            @pl.when(jnp.logical_not(first))
            def _():
                pltpu.make_async_copy(rows_v, out_hbm.at[pl.ds(off, SC_WINDOW)], sem).wait()

            pltpu.sync_copy(idx_hbm.at[pl.ds(off, SC_WINDOW)], idx_v)
            pltpu.async_copy(table_hbm.at[idx_v], rows_v, gsem).wait()
            pltpu.async_copy(rows_v, out_hbm.at[pl.ds(off, SC_WINDOW)], sem)

        @pl.loop(0, steps, step=2)
        def _(j):
            window(j, 0, j == 0)
            window(j + 1, 1, j == 0)

        for _, rows_v, sem in bufs:
            pltpu.make_async_copy(rows_v, out_hbm.at[pl.ds(base, SC_WINDOW)], sem).wait()

    return gather(table, idx_flat)


def _expert_body(te_ref, nv_ref, nu_ref, nx_ref, sl_ref, xs_ref, wgu_hbm, bgu_ref, wd_hbm, bd_ref, ys_ref,
                 wgu_f32, wd_f32, wgu_bf_ref, wd_bf_ref, sems, *, layer):
    i = pl.program_id(0)
    used = i < nu_ref[0]
    e = te_ref[i]
    s = sl_ref[i]
    new_expert = jnp.logical_or(i == 0, e != te_ref[jnp.maximum(i - 1, 0)])

    def weight_copies(expert, slot):
        return (pltpu.make_async_copy(wgu_hbm.at[layer, expert], wgu_f32.at[slot], sems.at[0, slot]),
                pltpu.make_async_copy(wd_hbm.at[layer, expert], wd_f32.at[slot], sems.at[1, slot]))

    @pl.when(jnp.logical_and(used, i == 0))
    def _():
        for c in weight_copies(e, s):
            c.start()

    @pl.when(jnp.logical_and(used, new_expert))
    def _():
        for c in weight_copies(e, s):
            c.wait()

        @pl.when(nx_ref[i] >= 0)
        def _():
            for c in weight_copies(nx_ref[i], 1 - s):
                c.start()

        wgu_bf_ref[...] = wgu_f32[s].astype(BF16)
        wd_bf_ref[...] = wd_f32[s].astype(BF16)

    def experts_on(n_rows):
        live = lax.broadcasted_iota(jnp.int32, (n_rows, 1), 0) < nv_ref[i]
        x = jnp.concatenate(_unpack_bf16_pairs(jnp.where(live, xs_ref[:n_rows, :], jnp.uint32(0))), axis=1)
        gu = _dot(x, wgu_bf_ref[...]) + bgu_ref[...]
        gt = jnp.minimum(gu[:, :D_FF], SWIGLU_LIMIT)
        up = jnp.clip(gu[:, D_FF:], -SWIGLU_LIMIT, SWIGLU_LIMIT)
        act = (up + 1.0) * (gt * jax.nn.sigmoid(SWIGLU_ALPHA * gt))
        y = _dot(act.astype(BF16), wd_bf_ref[...]) + bd_ref[...]
        ys_ref[:n_rows, :] = _pack_bf16_pairs(y.astype(BF16))

    for n_rows in range(MOE_SUBTILE, MOE_TILE + 1, MOE_SUBTILE):
        fits = jnp.logical_and(nv_ref[i] > n_rows - MOE_SUBTILE, nv_ref[i] <= n_rows)

        @pl.when(jnp.logical_and(used, fits))
        def _():
            experts_on(n_rows)
            if n_rows < MOE_TILE:
                ys_ref[n_rows:, :] = jnp.zeros((MOE_TILE - n_rows, ys_ref.shape[1]), ys_ref.dtype)

    @pl.when(jnp.logical_not(used))
    def _():
        ys_ref[...] = jnp.zeros_like(ys_ref)


def _experts(xs, counts, tile_e, n_valid, n_used, layer, w_gu, b_gu, w_down, b_down):
    n_rows, W = xs.shape
    n_tiles = n_rows // MOE_TILE
    e_ids = jnp.arange(N_EXPERTS, dtype=jnp.int32)
    nonempty = counts > 0
    rank = jnp.cumsum(nonempty.astype(jnp.int32)) - 1
    later = jnp.logical_and(e_ids[None, :] > e_ids[:, None], nonempty[None, :])
    nxt_of = jnp.min(jnp.where(later, e_ids[None, :], N_EXPERTS), axis=1)
    nxt_of = jnp.where(nxt_of < N_EXPERTS, nxt_of, -1)
    slot = _per_expert(rank % 2, tile_e)
    nxt = _per_expert(nxt_of, tile_e)
    bias = lambda w: pl.BlockSpec((None, None, 1, w), lambda i, te, nv, nu, nx, sl: (layer, te[i], 0, 0))
    tile = pl.BlockSpec((MOE_TILE, W), lambda i, te, nv, nu, nx, sl: (i, 0))
    grid_spec = pltpu.PrefetchScalarGridSpec(
        num_scalar_prefetch=5,
        grid=(n_tiles,),
        in_specs=[tile, pl.BlockSpec(memory_space=pl.ANY), bias(2 * D_FF),
                  pl.BlockSpec(memory_space=pl.ANY), bias(D_MODEL)],
        out_specs=tile,
        scratch_shapes=[pltpu.VMEM((2, D_MODEL, 2 * D_FF), F32), pltpu.VMEM((2, D_FF, D_MODEL), F32),
                        pltpu.VMEM((D_MODEL, 2 * D_FF), BF16), pltpu.VMEM((D_FF, D_MODEL), BF16),
                        pltpu.SemaphoreType.DMA((2, 2))],
    )
    return pl.pallas_call(
        functools.partial(_expert_body, layer=layer),
        grid_spec=grid_spec,
        out_shape=jax.ShapeDtypeStruct((n_rows, W), jnp.uint32),
        compiler_params=_cparams(("arbitrary",)),
        name="moe_experts",
    )(tile_e, n_valid, n_used, nxt, slot, xs, w_gu, b_gu.reshape(DEPTH, N_EXPERTS, 1, 2 * D_FF), w_down,
      b_down.reshape(DEPTH, N_EXPERTS, 1, D_MODEL))


def _combine_body(x_ref, y0_ref, y1_ref, y2_ref, y3_ref, gate_ref, g2_ref, *rest, final, chained):
    rest = list(rest)
    o_ref = rest.pop()
    if chained:
        rest.pop()
    g = gate_ref[...]
    acc_hi = acc_lo = None
    for k, y_ref in enumerate((y0_ref, y1_ref, y2_ref, y3_ref)):
        p = y_ref[...]
        gk = g[:, k:k + 1]
        hi = gk * pltpu.bitcast(p & jnp.uint32(0xFFFF0000), F32)
        lo = gk * pltpu.bitcast(p << 16, F32)
        acc_hi = hi if acc_hi is None else acc_hi + hi
        acc_lo = lo if acc_lo is None else acc_lo + lo
    acc = jnp.concatenate([acc_hi, acc_lo], axis=1)
    xn = x_ref[...] + _scale_rows(acc, g2_ref[...])
    if final:
        xn = xn * lax.rsqrt(jnp.mean(xn * xn, axis=-1, keepdims=True) + EPS) * rest[0][...]
        for t in range(o_ref.shape[1]):
            o_ref[:, t, :] = xn[t * SUBLANES:(t + 1) * SUBLANES, :]
    else:
        o_ref[...] = xn


def _combine(x, rows, moe_out, tok_off, mod, final_g, ctx, prev=None):
    ysg, gates_t = moe_out
    R, D = x.shape
    tl = TOK_TILE
    r0 = rows[0] // tl
    t0 = tok_off // tl
    nt = gates_t.shape[0] // tl
    final = final_g is not None
    y_specs = [pl.BlockSpec((tl, D // 2), lambda i, k=k: (k * nt + t0 + i, 0)) for k in range(TOP_K)]
    in_specs = ([pl.BlockSpec((tl, D), lambda i: (r0 + i, 0))] + y_specs
                + [pl.BlockSpec((tl, TOP_K), lambda i: (t0 + i, 0)), _mod_spec(5, ctx)])
    args = [x, ysg, ysg, ysg, ysg, gates_t, mod]
    if final:
        in_specs.append(_full((1, D)))
        args.append(final_g.reshape(1, D))
        out_spec = pl.BlockSpec((SUBLANES, tl // SUBLANES, D), lambda i: (0, r0 + i, 0))
        out_shape = jax.ShapeDtypeStruct((SUBLANES, R // SUBLANES, D), F32)
    else:
        out_spec = pl.BlockSpec((tl, D), lambda i: (r0 + i, 0))
        out_shape = jax.ShapeDtypeStruct((R, D), F32)
    aliases = {}
    if prev is not None:
        aliases = {len(args): 0}
        in_specs.append(pl.BlockSpec(memory_space=pl.ANY))
        args.append(prev)
    return pl.pallas_call(
        functools.partial(_combine_body, final=final, chained=prev is not None),
        grid=((rows[1] - rows[0]) // tl,),
        in_specs=in_specs,
        out_specs=out_spec,
        out_shape=out_shape,
        input_output_aliases=aliases,
        compiler_params=_cparams(("parallel",)),
        name="moe_combine",
    )(*args)


def _moe(hp, logits_t, toks, layer, w_gu, b_gu, w_down, b_down):
    dest, gates, counts, tile_e, n_valid, n_used, n_tiles = _routing(logits_t[:, toks[0]:toks[1]])
    dest_flat = dest.reshape(-1)
    xs = _dispatch(hp, toks[0], dest_flat, n_tiles * MOE_TILE)
    ys = _experts(xs, counts, tile_e, n_valid, n_used, layer, w_gu, b_gu, w_down, b_down)
    return _gather_rows(ys, dest_flat), gates.T


def _token_mixers(x, pos, mod, h0, p, consts, ctx, need_out):
    R = x.size // x.shape[-1]
    L = R // SUBLANES
    xa, ga, xb, xc, xd = _in_projection(x, pos, mod, p["norm1_g"], p["w_in"], p["b_in"], ctx)
    yf, yb, hfin = _rglru(xa, p["conv_a_w"], p["conv_a_b"], p["wg"], p["bg"], p["rg_lambda"], h0, L)
    if not need_out:
        return None, hfin
    yp = _pool_mixer(xb, p["w_pool"], p["b_pool"], p["pool_scale"], L)
    yc = _fourier_mixer(xc, consts["dft"][L], consts["cc"], consts["sc"], p["w_four"], p["b_four"], L)
    yd = _conformer(xd, p["conv_d_w"], p["conv_d_b"], p["ln_d_g"], p["ln_d_b"], consts["avg"], p["w_pw"],
                    p["b_pw"], L)
    return (yf, yb, ga, yp, yc, yd), hfin


def _pos_embed(n_tokens):
    rows_n = n_tokens // GRID_W
    q = D_MODEL // 4
    omega = 1.0 / (10000.0 ** (jnp.arange(q, dtype=F32) / q))

    def emb(n):
        ang = jnp.arange(n, dtype=F32)[:, None] * omega[None, :]
        return jnp.concatenate([jnp.sin(ang), jnp.cos(ang)], axis=-1)

    return emb(rows_n).reshape(rows_n, 1, D_MODEL // 2), emb(GRID_W)


def _layer_params(l, w_in, b_in, conv_a_w, conv_a_b, w_rg_r, b_rg_r, w_rg_i, b_rg_i, rg_lambda, w_pool, b_pool,
                  pool_scale, w_four, b_four, conv_d_w, conv_d_b, ln_d_g, ln_d_b, w_pw, b_pw, norm1_g):
    wg = jnp.stack([jnp.concatenate([_block_diag(w_rg_r[l, d]), _block_diag(w_rg_i[l, d])], axis=1)
                    for d in range(2)]).astype(BF16)
    bg = jnp.concatenate([b_rg_r[l].reshape(2, 1, W_GROUP), b_rg_i[l].reshape(2, 1, W_GROUP)], axis=-1)
    return dict(
        norm1_g=norm1_g[l], w_in=w_in[l].astype(BF16), b_in=b_in[l],
        conv_a_w=conv_a_w[l], conv_a_b=conv_a_b[l], wg=wg, bg=bg, rg_lambda=rg_lambda[l],
        w_pool=_block_diag(w_pool[l]).astype(BF16), b_pool=b_pool[l], pool_scale=pool_scale[l],
        w_four=_block_diag(w_four[l]).astype(BF16), b_four=b_four[l],
        conv_d_w=conv_d_w[l], conv_d_b=conv_d_b[l], ln_d_g=ln_d_g[l], ln_d_b=ln_d_b[l],
        w_pw=w_pw[l].astype(BF16), b_pw=b_pw[l])


def kernel(x, c, ctx, c_ctx, w_mod, b_mod, norm1_g, norm2_g, w_in, b_in, conv_a_w, conv_a_b, w_rg_r, b_rg_r,
           w_rg_i, b_rg_i, rg_lambda, w_pool, b_pool, pool_scale, w_four, b_four, conv_d_w, conv_d_b, ln_d_g,
           ln_d_b, w_pw, b_pw, w_out, b_out, w_router, b_router, w_gu, b_gu, w_down, b_down, final_norm_g):
    bn, L, D = x.shape
    Lc = ctx.shape[1]
    assert bn == SUBLANES and D == D_MODEL

    pos = _pos_embed(L)
    c_rows = jnp.concatenate([c, jnp.broadcast_to(c_ctx[None], (MOD_ROWS - bn, D))], axis=0)
    mod = _modulation(c_rows, w_mod, b_mod)
    ctx = jnp.transpose(ctx, (1, 0, 2)).reshape(Lc * bn, D)

    cc1, sc1 = _dft_matrices(D_SUB, 1.0 / math.sqrt(D_SUB))
    eye = jnp.eye(N_SUB, dtype=F32)
    consts = dict(
        dft={n: _time_dft_tables(n) for n in sorted({L, Lc})},
        cc=jnp.kron(eye, cc1).astype(BF16), sc=jnp.kron(eye, sc1).astype(BF16),
        avg=jnp.kron(eye, jnp.full((D_SUB, D_SUB), 1.0 / D_SUB, F32)).astype(BF16))
    h_zero = jnp.zeros((2, SUBLANES, W_GROUP), F32)

    for l in range(DEPTH):
        last = l == DEPTH - 1
        p = _layer_params(l, w_in, b_in, conv_a_w, conv_a_b, w_rg_r, b_rg_r, w_rg_i, b_rg_i, rg_lambda, w_pool,
                          b_pool, pool_scale, w_four, b_four, conv_d_w, conv_d_b, ln_d_g, ln_d_b, w_pw, b_pw,
                          norm1_g)
        mod_l = mod[l]
        w_out_bf = w_out[l].astype(BF16)
        wr_t = w_router[l].T
        wr_hi = wr_t.astype(BF16)
        wr2 = jnp.concatenate([wr_hi, (wr_t - wr_hi.astype(F32)).astype(BF16)], axis=0)
        x_pos = pos if l == 0 else None

        mix_c, h_ctx = _token_mixers(ctx, None, mod_l, h_zero, p, consts, True, not last)
        mix_x, _ = _token_mixers(x, x_pos, mod_l, h_ctx, p, consts, False, True)
        n_ctx, final_g = (0, final_norm_g) if last else (bn * Lc, None)
        T = n_ctx + bn * L
        moe_in = None
        if not last:
            ctx, *moe_in = _out_projection(ctx, None, mix_c, mod_l, norm2_g[l], w_out_bf, b_out[l], wr2,
                                           b_router[l], True, T, 0, None)
        x, hp, lg = _out_projection(x, x_pos, mix_x, mod_l, norm2_g[l], w_out_bf, b_out[l], wr2, b_router[l],
                                    False, T, n_ctx, moe_in)
        half = T // 2
        moe_a = _moe(hp, lg, (0, half), l, w_gu, b_gu, w_down, b_down)
        moe_b = _moe(hp, lg, (half, T), l, w_gu, b_gu, w_down, b_down)
        if n_ctx:
            ctx = _combine(ctx, (0, n_ctx), moe_a, 0, mod_l, None, True)
        split = half - n_ctx
        xa = _combine(x, (0, split), moe_a, n_ctx, mod_l, final_g, False)
        x = _combine(x, (split, x.shape[0]), moe_b, 0, mod_l, final_g, False, prev=xa)
    return x
```

```python
import functools
import math

import jax
import jax.numpy as jnp
from jax import lax
from jax.experimental import pallas as pl
from jax.experimental.pallas import tpu as pltpu

F32 = jnp.float32
BF16 = jnp.bfloat16

D_MODEL = 1024
DEPTH = 2
GRID_W = 64
W_GROUP = 256
N_SUB = 4
D_SUB = 64
D_IN = 6 * W_GROUP
RG_CONV = 4
RG_C = 8.0
CONF_KERNEL = 31
N_EXPERTS = 32
TOP_K = 4
D_FF = D_MODEL
SWIGLU_LIMIT = 7.0
SWIGLU_ALPHA = 1.702
EPS = 1e-6

SUBLANES = 8
VMEM_LIMIT_BYTES = 56 * 1024 * 1024
MOD_ROWS = 16
RG_HALO = 8 * SUBLANES
POOL_HALO = 8 * SUBLANES
CONF_HALO = 16 * SUBLANES
MOE_TILE = 512
MOE_SUBTILE = 128
TOK_TILE = 512
SC_WINDOW = 64
ROUTE_TILE = 1024


def _cparams(sem):
    return pltpu.CompilerParams(dimension_semantics=sem, vmem_limit_bytes=VMEM_LIMIT_BYTES)


def _full(shape):
    nd = len(shape)
    return pl.BlockSpec(shape, lambda *_: (0,) * nd)


def _dot(a, b):
    return jnp.dot(a, b, preferred_element_type=F32)


def _split_bf16(v):
    hi = v.astype(BF16)
    lo = (v - hi.astype(F32)).astype(BF16)
    return hi, lo


def _mod_body(c_ref, w_ref, b_ref, o_ref):
    c = c_ref[...]
    s = c * jax.nn.sigmoid(c)
    o_ref[...] = jnp.dot(s, w_ref[...], precision=lax.Precision.HIGHEST,
                         preferred_element_type=F32) + b_ref[...]


def _modulation(c_rows, w_mod, b_mod):
    tn = 1536
    n6 = 6 * D_MODEL
    return pl.pallas_call(
        _mod_body,
        grid=(DEPTH, n6 // tn),
        in_specs=[_full((MOD_ROWS, D_MODEL)),
                  pl.BlockSpec((None, D_MODEL, tn), lambda l, j: (l, 0, j)),
                  pl.BlockSpec((None, 1, tn), lambda l, j: (l, 0, j))],
        out_specs=pl.BlockSpec((None, MOD_ROWS, tn), lambda l, j: (l, 0, j)),
        out_shape=jax.ShapeDtypeStruct((DEPTH, MOD_ROWS, n6), F32),
        compiler_params=_cparams(("parallel", "parallel")),
        name="modulation",
    )(c_rows, w_mod, b_mod.reshape(DEPTH, 1, n6))


def _mod_spec(chunk, ctx):
    return pl.BlockSpec((SUBLANES, D_MODEL), lambda i: (1 if ctx else 0, chunk))


def _scale_rows(v, m):
    r, d = v.shape
    return (v.reshape(r // SUBLANES, SUBLANES, d) * m[None]).reshape(r, d)


def _rms_mod(x, g, shift, scale):
    r, d = x.shape
    y = x * lax.rsqrt(jnp.mean(x * x, axis=-1, keepdims=True) + EPS) * g
    y3 = y.reshape(r // SUBLANES, SUBLANES, d)
    return (y3 * (1.0 + scale)[None] + shift[None]).reshape(r, d)


def _load_plus_pos(x_ref, prow_ref, pcol_ref):
    row = prow_ref[...]
    return jnp.concatenate([x_ref[:, t, :] + jnp.concatenate([row, pcol_ref[t:t + 1, :]], axis=1)
                            for t in range(GRID_W)], axis=0)


def _x_spec(x, tr, r0=0):
    if x.ndim == 3:
        return pl.BlockSpec((SUBLANES, tr // SUBLANES, x.shape[2]), lambda i: (0, r0 + i, 0))
    return pl.BlockSpec((tr, x.shape[1]), lambda i: (r0 + i, 0))


def _inproj_body(*refs, add_pos):
    if add_pos:
        (x_ref, prow_ref, pcol_ref, sh_ref, sc_ref, g_ref, w_ref, b_ref,
         xa_ref, ga_ref, xb_ref, xc_ref, xd_ref) = refs
        x = _load_plus_pos(x_ref, prow_ref, pcol_ref)
    else:
        x_ref, sh_ref, sc_ref, g_ref, w_ref, b_ref, xa_ref, ga_ref, xb_ref, xc_ref, xd_ref = refs
        x = x_ref[...]
    u = _rms_mod(x, g_ref[...], sh_ref[...], sc_ref[...])
    p = _dot(u.astype(BF16), w_ref[...]) + b_ref[...]
    xa_ref[...] = p[:, 0:256]
    ga_ref[...] = p[:, 256:512]
    xb_ref[...] = p[:, 512:768]
    xc_ref[...] = p[:, 768:1024].astype(BF16)
    xd_ref[...] = p[:, 1024:1536]


def _in_projection(x, pos, mod, norm_g, w_in_bf, b_in, ctx):
    add_pos = pos is not None
    assert add_pos == (x.ndim == 3)
    R, D = x.size // x.shape[-1], x.shape[-1]
    tr = min(R, 512)
    row = lambda w: pl.BlockSpec((tr, w), lambda i: (i, 0))
    in_specs = [_x_spec(x, tr)]
    args = [x]
    if add_pos:
        assert tr == GRID_W * SUBLANES
        in_specs += [pl.BlockSpec((None, 1, D // 2), lambda i: (i, 0, 0)), _full((GRID_W, D // 2))]
        args += list(pos)
    in_specs += [_mod_spec(0, ctx), _mod_spec(1, ctx), _full((1, D)), _full((D, D_IN)), _full((1, D_IN))]
    args += [mod, mod, norm_g.reshape(1, D), w_in_bf, b_in.reshape(1, D_IN)]
    out_shape = [jax.ShapeDtypeStruct((R, 256), F32)] * 3 + [
        jax.ShapeDtypeStruct((R, 256), BF16), jax.ShapeDtypeStruct((R, 512), F32)]
    return pl.pallas_call(
        functools.partial(_inproj_body, add_pos=add_pos),
        grid=(R // tr,),
        in_specs=in_specs,
        out_specs=[row(256), row(256), row(256), row(256), row(512)],
        out_shape=out_shape,
        compiler_params=_cparams(("parallel",)),
        name="in_projection",
    )(*args)


def _rg_gates(xc, wg, bg, lam):
    g = _dot(xc.astype(BF16), wg) + bg
    r = jax.nn.sigmoid(g[:, :W_GROUP])
    gi = jax.nn.sigmoid(g[:, W_GROUP:])
    z = -lam
    softplus = jnp.maximum(z, 0.0) + jnp.log1p(jnp.exp(-jnp.abs(z)))
    log_a = (-RG_C) * r * softplus
    a = jnp.exp(log_a)
    b = jnp.sqrt(-jnp.tanh(log_a) * (a * a + 1.0)) * (gi * xc)
    return a, b


def _rg_body(xf_ref, xfh_ref, xr_ref, xrh_ref, cw_ref, cb_ref, wg_ref, bg_ref, lam_ref, h0_ref,
             yf_ref, yb_ref, hfin_ref, af_ref, ab_ref, hc_ref, *, n, tt):
    i = pl.program_id(0)
    tr = tt * SUBLANES
    keep = RG_HALO - (RG_CONV - 1) * SUBLANES

    @pl.when(i == 0)
    def _():
        hc_ref[...] = h0_ref[...]

    halo = jnp.where(i > 0, xfh_ref[...], 0.0)
    ext = jnp.concatenate([halo[keep:], xf_ref[...]], axis=0)
    xc = cb_ref[0]
    for k in range(RG_CONV):
        xc = xc + cw_ref[0, k:k + 1, :] * ext[k * SUBLANES:k * SUBLANES + tr]
    a, b = _rg_gates(xc, wg_ref[0], bg_ref[0], lam_ref[0])
    af_ref[...] = a
    yf_ref[...] = b

    halo = jnp.where(i > 0, xrh_ref[...], 0.0)
    ext = jnp.concatenate([xr_ref[...], halo[:(RG_CONV - 1) * SUBLANES]], axis=0)
    xc = cb_ref[1]
    for k in range(RG_CONV):
        o = (RG_CONV - 1 - k) * SUBLANES
        xc = xc + cw_ref[1, k:k + 1, :] * ext[o:o + tr]
    a, b = _rg_gates(xc, wg_ref[1], bg_ref[1], lam_ref[1])
    ab_ref[...] = a
    yb_ref[...] = b

    def step(t, carry):
        hf, hb = carry
        rf = pl.multiple_of(t * SUBLANES, SUBLANES)
        hf = af_ref[pl.ds(rf, SUBLANES), :] * hf + yf_ref[pl.ds(rf, SUBLANES), :]
        yf_ref[pl.ds(rf, SUBLANES), :] = hf
        rb = pl.multiple_of((tt - 1 - t) * SUBLANES, SUBLANES)
        hb = ab_ref[pl.ds(rb, SUBLANES), :] * hb + yb_ref[pl.ds(rb, SUBLANES), :]
        yb_ref[pl.ds(rb, SUBLANES), :] = hb
        return hf, hb

    hf, hb = lax.fori_loop(0, tt, step, (hc_ref[0], hc_ref[1]), unroll=8)
    hc_ref[0] = hf
    hc_ref[1] = hb
    hfin_ref[0] = hf
    hfin_ref[1] = hb


def _rglru(xa2, conv_w, conv_b, wg_bf, bg, lam, h0, L):
    tt = min(L, 256)
    n = L // tt
    tr = tt * SUBLANES
    per = tr // RG_HALO
    last_halo = L * SUBLANES // RG_HALO - 1
    row = lambda i: (i, 0)
    rev = lambda i: (n - 1 - i, 0)
    in_specs = [
        pl.BlockSpec((tr, W_GROUP), row),
        pl.BlockSpec((RG_HALO, W_GROUP), lambda i: (jnp.maximum(i * per - 1, 0), 0)),
        pl.BlockSpec((tr, W_GROUP), rev),
        pl.BlockSpec((RG_HALO, W_GROUP), lambda i: (jnp.minimum((n - i) * per, last_halo), 0)),
        _full((2, RG_CONV, W_GROUP)), _full((2, 1, W_GROUP)), _full((2, W_GROUP, 2 * W_GROUP)),
        _full((2, 1, 2 * W_GROUP)), _full((2, 1, W_GROUP)), _full((2, SUBLANES, W_GROUP)),
    ]
    return pl.pallas_call(
        functools.partial(_rg_body, n=n, tt=tt),
        grid=(n,),
        in_specs=in_specs,
        out_specs=[pl.BlockSpec((tr, W_GROUP), row), pl.BlockSpec((tr, W_GROUP), rev),
                   _full((2, SUBLANES, W_GROUP))],
        out_shape=[jax.ShapeDtypeStruct((L * SUBLANES, W_GROUP), F32)] * 2
        + [jax.ShapeDtypeStruct((2, SUBLANES, W_GROUP), F32)],
        scratch_shapes=[pltpu.VMEM((tr, W_GROUP), F32), pltpu.VMEM((tr, W_GROUP), F32),
                        pltpu.VMEM((2, SUBLANES, W_GROUP), F32)],
        compiler_params=_cparams(("arbitrary",)),
        name="rglru",
    )(xa2, xa2, xa2, xa2, conv_w, conv_b.reshape(2, 1, W_GROUP), wg_bf, bg, lam.reshape(2, 1, W_GROUP), h0)


def _pool_body(xm_ref, xp_ref, xn_ref, w_ref, b_ref, s_ref, o_ref, *, n, tt, L):
    i = pl.program_id(0)
    tr = tt * SUBLANES
    S = SUBLANES
    xm = xm_ref[...]
    prev = jnp.where(i > 0, xp_ref[...], 0.0)
    nxt = jnp.where(i < n - 1, xn_ref[...], 0.0)
    xe = jnp.concatenate([prev, xm, nxt], axis=0)
    e = xe.shape[0]
    p2 = xe[S:e] + xe[0:e - S]
    n4 = (tt + 13) * S
    p4 = p2[0:n4] + p2[2 * S:2 * S + n4]
    n8 = (tt + 9) * S
    p8 = p4[0:n8] + p4[4 * S:4 * S + n8]
    s16 = p8[0:tr] + p8[8 * S:8 * S + tr]
    s2 = p2[7 * S:7 * S + tr]
    s4 = p4[6 * S:6 * S + tr]
    s8 = p8[4 * S:4 * S + tr]
    grp = lax.broadcasted_iota(jnp.int32, (1, W_GROUP), 1) // D_SUB
    half = jnp.left_shift(1, grp)
    t = i * tt + lax.broadcasted_iota(jnp.int32, (tr, 1), 0) // S
    cnt = (jnp.minimum(t + half, L) - jnp.maximum(t - half, 0)).astype(F32)
    s = jnp.where(grp == 0, s2, jnp.where(grp == 1, s4, jnp.where(grp == 2, s8, s16)))
    pooled = s / cnt - xm
    y = _dot(pooled.astype(BF16), w_ref[...]) + b_ref[...]
    o_ref[...] = (y * s_ref[...]).astype(BF16)


def _pool_mixer(xb2, w_bd_bf, b, scale, L):
    tt = min(L, 256)
    n = L // tt
    tr = tt * SUBLANES
    per = tr // POOL_HALO
    last_halo = L * SUBLANES // POOL_HALO - 1
    return pl.pallas_call(
        functools.partial(_pool_body, n=n, tt=tt, L=L),
        grid=(n,),
        in_specs=[pl.BlockSpec((tr, W_GROUP), lambda i: (i, 0)),
                  pl.BlockSpec((POOL_HALO, W_GROUP), lambda i: (jnp.maximum(i * per - 1, 0), 0)),
                  pl.BlockSpec((POOL_HALO, W_GROUP), lambda i: (jnp.minimum((i + 1) * per, last_halo), 0)),
                  _full((W_GROUP, W_GROUP)), _full((1, W_GROUP)), _full((1, W_GROUP))],
        out_specs=pl.BlockSpec((tr, W_GROUP), lambda i: (i, 0)),
        out_shape=jax.ShapeDtypeStruct((L * SUBLANES, W_GROUP), BF16),
        compiler_params=_cparams(("parallel",)),
        name="pool_mixer",
    )(xb2, xb2, xb2, w_bd_bf, b.reshape(1, W_GROUP), scale.reshape(1, W_GROUP))


def _fourier_body(c_ref, s_ref, xe_ref, xo_ref, cw_ref, sw_ref, cc_ref, sc_ref, w_ref, b_ref, o_ref):
    c, s = c_ref[...], s_ref[...]
    xe, xo = xe_ref[...], xo_ref[...]
    ec, es = _dot(c, xe), _dot(s, xe)
    oc, os_ = _dot(c, xo), _dot(s, xo)
    cw, sw = cw_ref[...], sw_ref[...]
    tc = cw * oc - sw * os_
    ts = cw * os_ + sw * oc
    for h, (z1, z2) in enumerate(((ec + tc, es + ts), (ec - tc, es - ts))):
        for j in range(xe.shape[1] // W_GROUP):
            sl = slice(j * W_GROUP, (j + 1) * W_GROUP)
            a_hi, a_lo = _split_bf16(z1[:, sl])
            b_hi, b_lo = _split_bf16(z2[:, sl])
            f = ((_dot(a_hi, cc_ref[...]) + _dot(a_lo, cc_ref[...]))
                 - (_dot(b_hi, sc_ref[...]) + _dot(b_lo, sc_ref[...])))
            o_ref[h, :, sl] = (_dot(f.astype(BF16), w_ref[...]) + b_ref[...]).astype(BF16)


def _fourier_mixer(xc, tables, cc, sc, w_bd_bf, b, L):
    ch, sh, cw, sw = tables
    M = L // 2
    ncol = SUBLANES * W_GROUP
    x2 = xc.reshape(M, 2 * ncol)
    nb = 1024
    tk = min(M, 256)
    out = pl.pallas_call(
        _fourier_body,
        grid=(ncol // nb, M // tk),
        in_specs=[pl.BlockSpec((tk, M), lambda j, k: (k, 0)),
                  pl.BlockSpec((tk, M), lambda j, k: (k, 0)),
                  pl.BlockSpec((M, nb), lambda j, k: (0, j)),
                  pl.BlockSpec((M, nb), lambda j, k: (0, ncol // nb + j)),
                  pl.BlockSpec((tk, 1), lambda j, k: (k, 0)),
                  pl.BlockSpec((tk, 1), lambda j, k: (k, 0)),
                  _full((W_GROUP, W_GROUP)), _full((W_GROUP, W_GROUP)), _full((W_GROUP, W_GROUP)),
                  _full((1, W_GROUP))],
        out_specs=pl.BlockSpec((2, tk, nb), lambda j, k: (0, k, j)),
        out_shape=jax.ShapeDtypeStruct((2, M, ncol), BF16),
        compiler_params=_cparams(("parallel", "parallel")),
        name="fourier_mixer",
    )(ch, sh, x2, x2, cw, sw, cc, sc, w_bd_bf, b.reshape(1, W_GROUP))
    return out.reshape(L * SUBLANES, W_GROUP)


def _time_dft_tables(L):
    M = L // 2
    ch, sh = _dft_matrices(M, 1.0 / math.sqrt(L))
    ang = jnp.arange(M, dtype=F32) * (2.0 * math.pi / L)
    return ch.astype(BF16), sh.astype(BF16), jnp.cos(ang).reshape(M, 1), jnp.sin(ang).reshape(M, 1)


def _dft_matrices(L, scale):
    f = 1 << (max(L.bit_length() - 1, 0) // 2)
    n = jnp.arange(L, dtype=jnp.int32)[None, :]

    def table(rows):
        ang = ((rows[:, None] * n) % L).astype(F32) * (2.0 * math.pi / L)
        return jnp.cos(ang), jnp.sin(ang)

    ac, as_ = table(jnp.arange(L // f, dtype=jnp.int32) * f)
    bc, bs = table(jnp.arange(f, dtype=jnp.int32))
    cos = (ac[:, None, :] * bc[None, :, :] - as_[:, None, :] * bs[None, :, :]).reshape(L, L) * scale
    sin = (as_[:, None, :] * bc[None, :, :] + ac[:, None, :] * bs[None, :, :]).reshape(L, L) * scale
    return cos, sin


def _block_diag(w):
    g, a, b = w.shape
    eye = jnp.eye(g, dtype=w.dtype)
    return (eye[:, None, :, None] * w[:, :, None, :]).reshape(g * a, g * b)


CONF_CHUNK = 64


def _conformer_body(xm_ref, xp_ref, xn_ref, cw_ref, cb_ref, lg_ref, lb_ref, avg_ref, w_ref, b_ref,
                    o_ref, v_ref, c_ref, *, n, tt):
    i = pl.program_id(0)
    tr = tt * SUBLANES
    H = CONF_HALO

    def glu(v):
        return v[:, :W_GROUP] * jax.nn.sigmoid(v[:, W_GROUP:])

    v_ref[0:H] = jnp.where(i > 0, glu(xp_ref[...]), 0.0)
    v_ref[H:H + tr] = glu(xm_ref[...])
    v_ref[H + tr:H + tr + H] = jnp.where(i < n - 1, glu(xn_ref[...]), 0.0)

    def chunk(c, carry):
        r0 = pl.multiple_of(c * CONF_CHUNK, CONF_CHUNK)
        acc = jnp.broadcast_to(cb_ref[...], (CONF_CHUNK, W_GROUP))
        for k in range(CONF_KERNEL):
            acc = acc + cw_ref[k:k + 1, :] * v_ref[pl.ds(r0 + (k + 1) * SUBLANES, CONF_CHUNK), :]
        c_ref[pl.ds(r0, CONF_CHUNK), :] = acc
        return carry

    lax.fori_loop(0, tr // CONF_CHUNK, chunk, 0)

    v = c_ref[...]
    avg = avg_ref[...]
    v_hi, v_lo = _split_bf16(v)
    mu = _dot(v_hi, avg) + _dot(v_lo, avg)
    d = v - mu
    q_hi, q_lo = _split_bf16(d * d)
    var = _dot(q_hi, avg) + _dot(q_lo, avg)
    vn = d * lax.rsqrt(var + EPS) * lg_ref[...] + lb_ref[...]
    act = vn * jax.nn.sigmoid(vn)
    o_ref[...] = (_dot(act.astype(BF16), w_ref[...]) + b_ref[...]).astype(BF16)


def _conformer(xd2, conv_w, conv_b, ln_g, ln_b, avg_bf, w_pw_bf, b_pw, L):
    tt = min(L, 256)
    n = L // tt
    tr = tt * SUBLANES
    per = tr // CONF_HALO
    last_halo = L * SUBLANES // CONF_HALO - 1
    vec = lambda a: a.reshape(1, W_GROUP)
    return pl.pallas_call(
        functools.partial(_conformer_body, n=n, tt=tt),
        grid=(n,),
        in_specs=[pl.BlockSpec((tr, 2 * W_GROUP), lambda i: (i, 0)),
                  pl.BlockSpec((CONF_HALO, 2 * W_GROUP), lambda i: (jnp.maximum(i * per - 1, 0), 0)),
                  pl.BlockSpec((CONF_HALO, 2 * W_GROUP), lambda i: (jnp.minimum((i + 1) * per, last_halo), 0)),
                  _full((CONF_KERNEL, W_GROUP)), _full((1, W_GROUP)), _full((1, W_GROUP)), _full((1, W_GROUP)),
                  _full((W_GROUP, W_GROUP)), _full((W_GROUP, W_GROUP)), _full((1, W_GROUP))],
        out_specs=pl.BlockSpec((tr, W_GROUP), lambda i: (i, 0)),
        out_shape=jax.ShapeDtypeStruct((L * SUBLANES, W_GROUP), BF16),
        scratch_shapes=[pltpu.VMEM((tr + 2 * CONF_HALO, W_GROUP), F32), pltpu.VMEM((tr, W_GROUP), F32)],
        compiler_params=_cparams(("parallel",)),
        name="conformer",
    )(xd2, xd2, xd2, conv_w, vec(conv_b), vec(ln_g), vec(ln_b), avg_bf, w_pw_bf, vec(b_pw))


def _gelu_tanh(x):
    return 0.5 * x * (1.0 + jnp.tanh(math.sqrt(2.0 / math.pi) * (x + 0.044715 * (x * x * x))))


def _pack_bf16_pairs(h_bf):
    u = pltpu.bitcast(h_bf.astype(F32), jnp.uint32)
    half = h_bf.shape[1] // 2
    return (u[:, :half] & jnp.uint32(0xFFFF0000)) | (u[:, half:] >> 16)


def _unpack_bf16_pairs(p):
    hi = pltpu.bitcast(p & jnp.uint32(0xFFFF0000), F32).astype(BF16)
    lo = pltpu.bitcast(p << 16, F32).astype(BF16)
    return hi, lo


def _outproj_body(*refs, add_pos, n_aliased):
    refs = list(refs)
    xo_ref, hp_ref, lg_ref = refs[-3:]
    del refs[-3 - n_aliased:]
    if add_pos:
        (x_ref, prow_ref, pcol_ref, yf_ref, yb_ref, ga_ref, yp_ref, yc_ref, yd_ref, g1_ref, sh_ref, sc_ref,
         ng_ref, wo_ref, bo_ref, wr_ref, br_ref) = refs
        x = _load_plus_pos(x_ref, prow_ref, pcol_ref)
    else:
        (x_ref, yf_ref, yb_ref, ga_ref, yp_ref, yc_ref, yd_ref, g1_ref, sh_ref, sc_ref, ng_ref,
         wo_ref, bo_ref, wr_ref, br_ref) = refs
        x = x_ref[...]
    ya = (yf_ref[...] + yb_ref[...]) * _gelu_tanh(ga_ref[...])
    ycat = jnp.concatenate([ya.astype(BF16), yp_ref[...], yc_ref[...], yd_ref[...]], axis=1)
    y = _dot(ycat, wo_ref[...]) + bo_ref[...]
    xn = x + _scale_rows(y, g1_ref[...])
    xo_ref[...] = xn
    h = _rms_mod(xn, ng_ref[...], sh_ref[...], sc_ref[...])
    h_hi, h_lo = _split_bf16(h)
    nt = (((1,), (1,)), ((), ()))
    wr = wr_ref[...]
    acc = lax.dot_general(wr, h_hi, nt, preferred_element_type=F32)
    acc = acc + lax.dot_general(wr, h_lo, nt, preferred_element_type=F32)
    lg_ref[...] = acc[:N_EXPERTS] + acc[N_EXPERTS:] + br_ref[...]
    hp_ref[...] = _pack_bf16_pairs(h_hi)


def _out_projection(x, rows, pos, mix, mod, norm_g, w_out_bf, b_out, wr2_bf, b_router, ctx, n_tok, tok0, prev,
                    prev_x):
    add_pos = pos is not None
    assert add_pos == (x.ndim == 3)
    R, D = x.size // x.shape[-1], x.shape[-1]
    tr = min(R, 512)
    r0 = rows[0] // tr
    assert r0 * tr == rows[0] and (rows[1] - rows[0]) % tr == 0
    row = lambda w: pl.BlockSpec((tr, w), lambda i: (r0 + i, 0))
    in_specs = [_x_spec(x, tr, r0)]
    args = [x]
    if add_pos:
        assert tr == GRID_W * SUBLANES
        in_specs += [pl.BlockSpec((None, 1, D // 2), lambda i: (r0 + i, 0, 0)), _full((GRID_W, D // 2))]
        args += list(pos)
    in_specs += [row(W_GROUP)] * 6
    args += list(mix)
    in_specs += [_mod_spec(2, ctx), _mod_spec(3, ctx), _mod_spec(4, ctx), _full((1, D)), _full((D, D)),
                 _full((1, D)), _full((2 * N_EXPERTS, D)), _full((N_EXPERTS, 1))]
    args += [mod, mod, mod, norm_g.reshape(1, D), w_out_bf, b_out.reshape(1, D), wr2_bf,
             b_router.reshape(N_EXPERTS, 1)]
    aliases = {}
    n_plain = len(args)
    if prev_x is not None:
        aliases[len(args)] = 0
        args.append(prev_x)
    if prev is not None:
        aliases.update({len(args): 1, len(args) + 1: 2})
        args += list(prev)
    in_specs += [pl.BlockSpec(memory_space=pl.ANY)] * (len(args) - n_plain)
    t0 = tok0 // tr
    assert t0 * tr == tok0
    return pl.pallas_call(
        functools.partial(_outproj_body, add_pos=add_pos, n_aliased=len(args) - n_plain),
        grid=((rows[1] - rows[0]) // tr,),
        in_specs=in_specs,
        out_specs=[row(D), pl.BlockSpec((tr, D // 2), lambda i: (t0 + i, 0)),
                   pl.BlockSpec((N_EXPERTS, tr), lambda i: (0, t0 + i))],
        out_shape=[jax.ShapeDtypeStruct((R, D), F32),
                   jax.ShapeDtypeStruct((n_tok, D // 2), jnp.uint32),
                   jax.ShapeDtypeStruct((N_EXPERTS, n_tok), F32)],
        input_output_aliases=aliases,
        compiler_params=_cparams(("parallel",)),
        name="out_projection",
    )(*args)


def _top4(v):
    eid = lax.broadcasted_iota(jnp.int32, v.shape, 0)
    out = []
    work = v
    for _ in range(TOP_K):
        m = jnp.max(work, axis=0, keepdims=True)
        idx = jnp.min(jnp.where(work == m, eid, N_EXPERTS), axis=0, keepdims=True)
        oh = eid == idx
        out.append((m, oh))
        work = jnp.where(oh, -jnp.inf, work)
    return out


def _count_body(lg_ref, cnt_ref):
    @pl.when(pl.program_id(0) == 0)
    def _():
        cnt_ref[...] = jnp.zeros_like(cnt_ref)

    sel = jnp.zeros(lg_ref.shape, F32)
    for _, oh in _top4(lg_ref[...]):
        sel = sel + oh.astype(F32)
    cnt_ref[...] += jnp.sum(sel, axis=1, keepdims=True)


def _route_body(lg_ref, ps_ref, tri_ref, dest_ref, gate_ref, carry_ref):
    @pl.when(pl.program_id(0) == 0)
    def _():
        carry_ref[...] = jnp.zeros_like(carry_ref)

    top = _top4(lg_ref[...])
    sel = jnp.zeros(lg_ref.shape, F32)
    for _, oh in top:
        sel = sel + oh.astype(F32)
    before = _dot(sel.astype(BF16), tri_ref[...]) + carry_ref[...] + ps_ref[...]
    m0 = top[0][0]
    es = [jnp.exp(m - m0) for m, _ in top]
    den = es[0] + es[1] + es[2] + es[3]
    for k, (_, oh) in enumerate(top):
        dest_ref[k:k + 1, :] = jnp.sum(jnp.where(oh, before, 0.0), axis=0, keepdims=True).astype(jnp.int32)
        gate_ref[k:k + 1, :] = es[k] / den
    carry_ref[...] += jnp.sum(sel, axis=1, keepdims=True)


def _per_expert(table, experts):
    hit = experts[:, None] == jnp.arange(N_EXPERTS, dtype=jnp.int32)[None, :]
    return jnp.sum(jnp.where(hit, table[None, :].astype(jnp.int32), 0), axis=1).astype(jnp.int32)


def _routing(logits_t):
    E, T = logits_t.shape
    tt = ROUTE_TILE
    nt = T // tt
    counts = pl.pallas_call(
        _count_body,
        grid=(nt,),
        in_specs=[pl.BlockSpec((E, tt), lambda i: (0, i))],
        out_specs=_full((E, 1)),
        out_shape=jax.ShapeDtypeStruct((E, 1), F32),
        compiler_params=_cparams(("arbitrary",)),
        name="route_count",
    )(logits_t)
    cnt = counts[:, 0].astype(jnp.int32)
    padded = ((cnt + MOE_TILE - 1) // MOE_TILE) * MOE_TILE
    pend = jnp.cumsum(padded)
    pstart = pend - padded
    n_tiles = -(-(T * TOP_K) // MOE_TILE) + N_EXPERTS
    tile_start = jnp.arange(n_tiles, dtype=jnp.int32) * MOE_TILE
    tile_e = jnp.minimum(jnp.sum((pend[None, :] <= tile_start[:, None]).astype(jnp.int32), axis=1), N_EXPERTS - 1)
    n_used = (pend[-1] // MOE_TILE).astype(jnp.int32).reshape(1)
    n_valid = jnp.clip(_per_expert(pstart + cnt, tile_e) - tile_start, 0, MOE_TILE).astype(jnp.int32)
    tri = (jnp.arange(tt)[:, None] < jnp.arange(tt)[None, :]).astype(BF16)
    dest, gates = pl.pallas_call(
        _route_body,
        grid=(nt,),
        in_specs=[pl.BlockSpec((E, tt), lambda i: (0, i)), _full((E, 1)), _full((tt, tt))],
        out_specs=[pl.BlockSpec((TOP_K, tt), lambda i: (0, i)), pl.BlockSpec((TOP_K, tt), lambda i: (0, i))],
        out_shape=[jax.ShapeDtypeStruct((TOP_K, T), jnp.int32), jax.ShapeDtypeStruct((TOP_K, T), F32)],
        scratch_shapes=[pltpu.VMEM((E, 1), F32)],
        compiler_params=_cparams(("arbitrary",)),
        name="route_assign",
    )(logits_t, pstart.astype(F32).reshape(E, 1), tri)
    return dest, gates, cnt, tile_e, n_valid, n_used, n_tiles


def _sc_workers():
    from jax.experimental.pallas import tpu_sc as plsc
    mesh = plsc.VectorSubcoreMesh(core_axis_name="c", subcore_axis_name="s")
    n_workers = mesh.num_cores * mesh.num_subcores
    worker = lambda: lax.axis_index("s") * mesh.num_cores + lax.axis_index("c")
    return mesh, n_workers, worker


def _dispatch(hp, tok0, dest_flat, n_rows):
    W = hp.shape[1]
    T = dest_flat.shape[0] // TOP_K
    mesh, n_workers, worker = _sc_workers()
    per = dest_flat.shape[0] // n_workers
    steps = per // SC_WINDOW
    assert per * n_workers == dest_flat.shape[0] and steps * SC_WINDOW == per and steps % 2 == 0
    assert T % SC_WINDOW == 0 and tok0 % SC_WINDOW == 0

    @functools.partial(
        pl.kernel, mesh=mesh, out_type=jax.ShapeDtypeStruct((n_rows, W), hp.dtype),
        scratch_types=[pltpu.VMEM((SC_WINDOW,), jnp.int32), pltpu.VMEM((SC_WINDOW,), jnp.int32),
                       pltpu.VMEM((SC_WINDOW, W), hp.dtype), pltpu.VMEM((SC_WINDOW, W), hp.dtype),
                       pltpu.SemaphoreType.DMA, pltpu.SemaphoreType.DMA],
        name="moe_dispatch")
    def scatter(hp_hbm, dest_hbm, xs_hbm, idx0, idx1, rows0, rows1, sem0, sem1):
        base = worker() * per
        bufs = ((idx0, rows0, sem0), (idx1, rows1, sem1))

        def window(j, b, first):
            idx_v, rows_v, sem = bufs[b]

            @pl.when(jnp.logical_not(first))
            def _():
                pltpu.make_async_copy(rows_v, xs_hbm.at[idx_v], sem).wait()

            off = pl.multiple_of(base + j * SC_WINDOW, SC_WINDOW)
            tok = pl.multiple_of(tok0 + lax.rem(off, T), SC_WINDOW)
            pltpu.sync_copy(dest_hbm.at[pl.ds(off, SC_WINDOW)], idx_v)
            pltpu.sync_copy(hp_hbm.at[pl.ds(tok, SC_WINDOW)], rows_v)
            pltpu.async_copy(rows_v, xs_hbm.at[idx_v], sem)

        @pl.loop(0, steps, step=2)
        def _(j):
            window(j, 0, j == 0)
            window(j + 1, 1, j == 0)

        for idx_v, rows_v, sem in bufs:
            pltpu.make_async_copy(rows_v, xs_hbm.at[idx_v], sem).wait()

    return scatter(hp, dest_flat)


def _gather_rows(table, idx_flat):
    n = idx_flat.shape[0]
    W = table.shape[1]
    mesh, n_workers, worker = _sc_workers()
    per = n // n_workers
    steps = per // SC_WINDOW
    assert per * n_workers == n and steps * SC_WINDOW == per and steps % 2 == 0

    @functools.partial(
        pl.kernel, mesh=mesh, out_type=jax.ShapeDtypeStruct((n, W), table.dtype),
        scratch_types=[pltpu.VMEM((SC_WINDOW,), jnp.int32), pltpu.VMEM((SC_WINDOW,), jnp.int32),
                       pltpu.VMEM((SC_WINDOW, W), table.dtype), pltpu.VMEM((SC_WINDOW, W), table.dtype),
                       pltpu.SemaphoreType.DMA, pltpu.SemaphoreType.DMA, pltpu.SemaphoreType.DMA],
        name="moe_gather")
    def gather(table_hbm, idx_hbm, out_hbm, idx0, idx1, rows0, rows1, sem0, sem1, gsem):
        base = worker() * per
        bufs = ((idx0, rows0, sem0), (idx1, rows1, sem1))

        def window(j, b, first):
            idx_v, rows_v, sem = bufs[b]
            off = pl.multiple_of(base + j * SC_WINDOW, SC_WINDOW)

            @pl.when(jnp.logical_not(first))
            def _():
                pltpu.make_async_copy(rows_v, out_hbm.at[pl.ds(off, SC_WINDOW)], sem).wait()

            pltpu.sync_copy(idx_hbm.at[pl.ds(off, SC_WINDOW)], idx_v)
            pltpu.async_copy(table_hbm.at[idx_v], rows_v, gsem).wait()
            pltpu.async_copy(rows_v, out_hbm.at[pl.ds(off, SC_WINDOW)], sem)

        @pl.loop(0, steps, step=2)
        def _(j):
            window(j, 0, j == 0)
            window(j + 1, 1, j == 0)

        for _, rows_v, sem in bufs:
            pltpu.make_async_copy(rows_v, out_hbm.at[pl.ds(base, SC_WINDOW)], sem).wait()

    return gather(table, idx_flat)


def _expert_body(te_ref, nv_ref, nu_ref, nx_ref, sl_ref, xs_ref, wgu_hbm, bgu_ref, wd_hbm, bd_ref, ys_ref,
                 wgu_f32, wd_f32, wgu_bf_ref, wd_bf_ref, sems, *, layer):
    i = pl.program_id(0)
    used = i < nu_ref[0]
    e = te_ref[i]
    s = sl_ref[i]
    new_expert = jnp.logical_or(i == 0, e != te_ref[jnp.maximum(i - 1, 0)])

    def weight_copies(expert, slot):
        return (pltpu.make_async_copy(wgu_hbm.at[layer, expert], wgu_f32.at[slot], sems.at[0, slot]),
                pltpu.make_async_copy(wd_hbm.at[layer, expert], wd_f32.at[slot], sems.at[1, slot]))

    @pl.when(jnp.logical_and(used, i == 0))
    def _():
        for c in weight_copies(e, s):
            c.start()

    @pl.when(jnp.logical_and(used, new_expert))
    def _():
        for c in weight_copies(e, s):
            c.wait()

        @pl.when(nx_ref[i] >= 0)
        def _():
            for c in weight_copies(nx_ref[i], 1 - s):
                c.start()

        wgu_bf_ref[...] = wgu_f32[s].astype(BF16)
        wd_bf_ref[...] = wd_f32[s].astype(BF16)

    def experts_on(n_rows):
        live = lax.broadcasted_iota(jnp.int32, (n_rows, 1), 0) < nv_ref[i]
        x = jnp.concatenate(_unpack_bf16_pairs(jnp.where(live, xs_ref[:n_rows, :], jnp.uint32(0))), axis=1)
        gu = _dot(x, wgu_bf_ref[...]) + bgu_ref[...]
        gt = jnp.minimum(gu[:, :D_FF], SWIGLU_LIMIT)
        up = jnp.clip(gu[:, D_FF:], -SWIGLU_LIMIT, SWIGLU_LIMIT)
        act = (up + 1.0) * (gt * jax.nn.sigmoid(SWIGLU_ALPHA * gt))
        y = _dot(act.astype(BF16), wd_bf_ref[...]) + bd_ref[...]
        ys_ref[:n_rows, :] = _pack_bf16_pairs(y.astype(BF16))

    for n_rows in range(MOE_SUBTILE, MOE_TILE + 1, MOE_SUBTILE):
        fits = jnp.logical_and(nv_ref[i] > n_rows - MOE_SUBTILE, nv_ref[i] <= n_rows)

        @pl.when(jnp.logical_and(used, fits))
        def _():
            experts_on(n_rows)
            if n_rows < MOE_TILE:
                ys_ref[n_rows:, :] = jnp.zeros((MOE_TILE - n_rows, ys_ref.shape[1]), ys_ref.dtype)

    @pl.when(jnp.logical_not(used))
    def _():
        ys_ref[...] = jnp.zeros_like(ys_ref)


def _experts(xs, counts, tile_e, n_valid, n_used, layer, w_gu, b_gu, w_down, b_down):
    n_rows, W = xs.shape
    n_tiles = n_rows // MOE_TILE
    e_ids = jnp.arange(N_EXPERTS, dtype=jnp.int32)
    nonempty = counts > 0
    rank = jnp.cumsum(nonempty.astype(jnp.int32)) - 1
    later = jnp.logical_and(e_ids[None, :] > e_ids[:, None], nonempty[None, :])
    nxt_of = jnp.min(jnp.where(later, e_ids[None, :], N_EXPERTS), axis=1)
    nxt_of = jnp.where(nxt_of < N_EXPERTS, nxt_of, -1)
    slot = _per_expert(rank % 2, tile_e)
    nxt = _per_expert(nxt_of, tile_e)
    bias = lambda w: pl.BlockSpec((None, None, 1, w), lambda i, te, nv, nu, nx, sl: (layer, te[i], 0, 0))
    tile = pl.BlockSpec((MOE_TILE, W), lambda i, te, nv, nu, nx, sl: (i, 0))
    grid_spec = pltpu.PrefetchScalarGridSpec(
        num_scalar_prefetch=5,
        grid=(n_tiles,),
        in_specs=[tile, pl.BlockSpec(memory_space=pl.ANY), bias(2 * D_FF),
                  pl.BlockSpec(memory_space=pl.ANY), bias(D_MODEL)],
        out_specs=tile,
        scratch_shapes=[pltpu.VMEM((2, D_MODEL, 2 * D_FF), F32), pltpu.VMEM((2, D_FF, D_MODEL), F32),
                        pltpu.VMEM((D_MODEL, 2 * D_FF), BF16), pltpu.VMEM((D_FF, D_MODEL), BF16),
                        pltpu.SemaphoreType.DMA((2, 2))],
    )
    return pl.pallas_call(
        functools.partial(_expert_body, layer=layer),
        grid_spec=grid_spec,
        out_shape=jax.ShapeDtypeStruct((n_rows, W), jnp.uint32),
        compiler_params=_cparams(("arbitrary",)),
        name="moe_experts",
    )(tile_e, n_valid, n_used, nxt, slot, xs, w_gu, b_gu.reshape(DEPTH, N_EXPERTS, 1, 2 * D_FF), w_down,
      b_down.reshape(DEPTH, N_EXPERTS, 1, D_MODEL))


def _combine_body(x_ref, y0_ref, y1_ref, y2_ref, y3_ref, gate_ref, g2_ref, *rest, final, chained):
    rest = list(rest)
    o_ref = rest.pop()
    if chained:
        rest.pop()
    g = gate_ref[...]
    acc_hi = acc_lo = None
    for k, y_ref in enumerate((y0_ref, y1_ref, y2_ref, y3_ref)):
        p = y_ref[...]
        gk = g[:, k:k + 1]
        hi = gk * pltpu.bitcast(p & jnp.uint32(0xFFFF0000), F32)
        lo = gk * pltpu.bitcast(p << 16, F32)
        acc_hi = hi if acc_hi is None else acc_hi + hi
        acc_lo = lo if acc_lo is None else acc_lo + lo
    acc = jnp.concatenate([acc_hi, acc_lo], axis=1)
    xn = x_ref[...] + _scale_rows(acc, g2_ref[...])
    if final:
        xn = xn * lax.rsqrt(jnp.mean(xn * xn, axis=-1, keepdims=True) + EPS) * rest[0][...]
        for t in range(o_ref.shape[1]):
            o_ref[:, t, :] = xn[t * SUBLANES:(t + 1) * SUBLANES, :]
    else:
        o_ref[...] = xn


def _combine(x, rows, moe_out, tok_off, mod, final_g, ctx, prev=None):
    ysg, gates_t = moe_out
    R, D = x.shape
    tl = TOK_TILE
    r0 = rows[0] // tl
    t0 = tok_off // tl
    nt = gates_t.shape[0] // tl
    final = final_g is not None
    y_specs = [pl.BlockSpec((tl, D // 2), lambda i, k=k: (k * nt + t0 + i, 0)) for k in range(TOP_K)]
    in_specs = ([pl.BlockSpec((tl, D), lambda i: (r0 + i, 0))] + y_specs
                + [pl.BlockSpec((tl, TOP_K), lambda i: (t0 + i, 0)), _mod_spec(5, ctx)])
    args = [x, ysg, ysg, ysg, ysg, gates_t, mod]
    if final:
        in_specs.append(_full((1, D)))
        args.append(final_g.reshape(1, D))
        out_spec = pl.BlockSpec((SUBLANES, tl // SUBLANES, D), lambda i: (0, r0 + i, 0))
        out_shape = jax.ShapeDtypeStruct((SUBLANES, R // SUBLANES, D), F32)
    else:
        out_spec = pl.BlockSpec((tl, D), lambda i: (r0 + i, 0))
        out_shape = jax.ShapeDtypeStruct((R, D), F32)
    aliases = {}
    if prev is not None:
        aliases = {len(args): 0}
        in_specs.append(pl.BlockSpec(memory_space=pl.ANY))
        args.append(prev)
    return pl.pallas_call(
        functools.partial(_combine_body, final=final, chained=prev is not None),
        grid=((rows[1] - rows[0]) // tl,),
        in_specs=in_specs,
        out_specs=out_spec,
        out_shape=out_shape,
        input_output_aliases=aliases,
        compiler_params=_cparams(("parallel",)),
        name="moe_combine",
    )(*args)


def _moe(hp, logits_t, layer, w_gu, b_gu, w_down, b_down):
    dest, gates, counts, tile_e, n_valid, n_used, n_tiles = _routing(logits_t)
    dest_flat = dest.reshape(-1)
    xs = _dispatch(hp, 0, dest_flat, n_tiles * MOE_TILE)
    ys = _experts(xs, counts, tile_e, n_valid, n_used, layer, w_gu, b_gu, w_down, b_down)
    return _gather_rows(ys, dest_flat), gates.T


def _token_mixers(x, pos, mod, h0, p, consts, ctx, need_out):
    R = x.size // x.shape[-1]
    L = R // SUBLANES
    xa, ga, xb, xc, xd = _in_projection(x, pos, mod, p["norm1_g"], p["w_in"], p["b_in"], ctx)
    yf, yb, hfin = _rglru(xa, p["conv_a_w"], p["conv_a_b"], p["wg"], p["bg"], p["rg_lambda"], h0, L)
    if not need_out:
        return None, hfin
    yp = _pool_mixer(xb, p["w_pool"], p["b_pool"], p["pool_scale"], L)
    yc = _fourier_mixer(xc, consts["dft"][L], consts["cc"], consts["sc"], p["w_four"], p["b_four"], L)
    yd = _conformer(xd, p["conv_d_w"], p["conv_d_b"], p["ln_d_g"], p["ln_d_b"], consts["avg"], p["w_pw"],
                    p["b_pw"], L)
    return (yf, yb, ga, yp, yc, yd), hfin


def _pos_embed(n_tokens):
    rows_n = n_tokens // GRID_W
    q = D_MODEL // 4
    omega = 1.0 / (10000.0 ** (jnp.arange(q, dtype=F32) / q))

    def emb(n):
        ang = jnp.arange(n, dtype=F32)[:, None] * omega[None, :]
        return jnp.concatenate([jnp.sin(ang), jnp.cos(ang)], axis=-1)

    return emb(rows_n).reshape(rows_n, 1, D_MODEL // 2), emb(GRID_W)


def _layer_params(l, w_in, b_in, conv_a_w, conv_a_b, w_rg_r, b_rg_r, w_rg_i, b_rg_i, rg_lambda, w_pool, b_pool,
                  pool_scale, w_four, b_four, conv_d_w, conv_d_b, ln_d_g, ln_d_b, w_pw, b_pw, norm1_g):
    wg = jnp.stack([jnp.concatenate([_block_diag(w_rg_r[l, d]), _block_diag(w_rg_i[l, d])], axis=1)
                    for d in range(2)]).astype(BF16)
    bg = jnp.concatenate([b_rg_r[l].reshape(2, 1, W_GROUP), b_rg_i[l].reshape(2, 1, W_GROUP)], axis=-1)
    return dict(
        norm1_g=norm1_g[l], w_in=w_in[l].astype(BF16), b_in=b_in[l],
        conv_a_w=conv_a_w[l], conv_a_b=conv_a_b[l], wg=wg, bg=bg, rg_lambda=rg_lambda[l],
        w_pool=_block_diag(w_pool[l]).astype(BF16), b_pool=b_pool[l], pool_scale=pool_scale[l],
        w_four=_block_diag(w_four[l]).astype(BF16), b_four=b_four[l],
        conv_d_w=conv_d_w[l], conv_d_b=conv_d_b[l], ln_d_g=ln_d_g[l], ln_d_b=ln_d_b[l],
        w_pw=w_pw[l].astype(BF16), b_pw=b_pw[l])


def kernel(x, c, ctx, c_ctx, w_mod, b_mod, norm1_g, norm2_g, w_in, b_in, conv_a_w, conv_a_b, w_rg_r, b_rg_r,
           w_rg_i, b_rg_i, rg_lambda, w_pool, b_pool, pool_scale, w_four, b_four, conv_d_w, conv_d_b, ln_d_g,
           ln_d_b, w_pw, b_pw, w_out, b_out, w_router, b_router, w_gu, b_gu, w_down, b_down, final_norm_g):
    bn, L, D = x.shape
    Lc = ctx.shape[1]
    assert bn == SUBLANES and D == D_MODEL

    pos = _pos_embed(L)
    c_rows = jnp.concatenate([c, jnp.broadcast_to(c_ctx[None], (MOD_ROWS - bn, D))], axis=0)
    mod = _modulation(c_rows, w_mod, b_mod)
    ctx = jnp.transpose(ctx, (1, 0, 2)).reshape(Lc * bn, D)

    cc1, sc1 = _dft_matrices(D_SUB, 1.0 / math.sqrt(D_SUB))
    eye = jnp.eye(N_SUB, dtype=F32)
    consts = dict(
        dft={n: _time_dft_tables(n) for n in sorted({L, Lc})},
        cc=jnp.kron(eye, cc1).astype(BF16), sc=jnp.kron(eye, sc1).astype(BF16),
        avg=jnp.kron(eye, jnp.full((D_SUB, D_SUB), 1.0 / D_SUB, F32)).astype(BF16))
    h_zero = jnp.zeros((2, SUBLANES, W_GROUP), F32)

    for l in range(DEPTH):
        last = l == DEPTH - 1
        p = _layer_params(l, w_in, b_in, conv_a_w, conv_a_b, w_rg_r, b_rg_r, w_rg_i, b_rg_i, rg_lambda, w_pool,
                          b_pool, pool_scale, w_four, b_four, conv_d_w, conv_d_b, ln_d_g, ln_d_b, w_pw, b_pw,
                          norm1_g)
        mod_l = mod[l]
        w_out_bf = w_out[l].astype(BF16)
        wr_t = w_router[l].T
        wr_hi = wr_t.astype(BF16)
        wr2 = jnp.concatenate([wr_hi, (wr_t - wr_hi.astype(F32)).astype(BF16)], axis=0)
        x_pos = pos if l == 0 else None

        mix_c, h_ctx = _token_mixers(ctx, None, mod_l, h_zero, p, consts, True, not last)
        mix_x, _ = _token_mixers(x, x_pos, mod_l, h_ctx, p, consts, False, True)
        n_ctx, final_g = (0, final_norm_g) if last else (bn * Lc, None)
        T = n_ctx + bn * L
        half = T // 2
        split = half - n_ctx
        proj = functools.partial(_out_projection, mod=mod_l, norm_g=norm2_g[l], w_out_bf=w_out_bf, b_out=b_out[l],
                                 wr2_bf=wr2, b_router=b_router[l])
        moe_in = None
        if not last:
            ctx, *moe_in = proj(ctx, (0, n_ctx), None, mix_c, ctx=True, n_tok=half, tok0=0, prev=None, prev_x=None)
        xa, hp_a, lg_a = proj(x, (0, split), x_pos, mix_x, ctx=False, n_tok=half, tok0=n_ctx, prev=moe_in,
                              prev_x=None)
        moe_a = _moe(hp_a, lg_a, l, w_gu, b_gu, w_down, b_down)
        x, hp_b, lg_b = proj(x, (split, bn * L), x_pos, mix_x, ctx=False, n_tok=T - half, tok0=0, prev=None,
                             prev_x=xa)
        moe_b = _moe(hp_b, lg_b, l, w_gu, b_gu, w_down, b_down)
        if n_ctx:
            ctx = _combine(ctx, (0, n_ctx), moe_a, 0, mod_l, None, True)
        xa = _combine(x, (0, split), moe_a, n_ctx, mod_l, final_g, False)
        x = _combine(x, (split, x.shape[0]), moe_b, 0, mod_l, final_g, False, prev=xa)
    return x
```

```python
import functools
import math

import jax
import jax.numpy as jnp
from jax import lax
from jax.experimental import pallas as pl
from jax.experimental.pallas import tpu as pltpu

F32 = jnp.float32
BF16 = jnp.bfloat16

D_MODEL = 1024
DEPTH = 2
GRID_W = 64
W_GROUP = 256
N_SUB = 4
D_SUB = 64
D_IN = 6 * W_GROUP
RG_CONV = 4
RG_C = 8.0
CONF_KERNEL = 31
N_EXPERTS = 32
TOP_K = 4
D_FF = D_MODEL
SWIGLU_LIMIT = 7.0
SWIGLU_ALPHA = 1.702
EPS = 1e-6

SUBLANES = 8
VMEM_LIMIT_BYTES = 56 * 1024 * 1024
MOD_ROWS = 16
RG_HALO = 8 * SUBLANES
POOL_HALO = 8 * SUBLANES
CONF_HALO = 16 * SUBLANES
MOE_TILE = 512
MOE_SUBTILE = 128
TOK_TILE = 512
SC_WINDOW = 64
ROUTE_TILE = 1024
STREAM_BUFFERS = 3


def _cparams(sem):
    return pltpu.CompilerParams(dimension_semantics=sem, vmem_limit_bytes=VMEM_LIMIT_BYTES)


def _full(shape):
    nd = len(shape)
    return pl.BlockSpec(shape, lambda *_: (0,) * nd)


def _dot(a, b):
    return jnp.dot(a, b, preferred_element_type=F32)


def _split_bf16(v):
    hi = v.astype(BF16)
    lo = (v - hi.astype(F32)).astype(BF16)
    return hi, lo


def _mod_body(c_ref, w_ref, b_ref, o_ref):
    c = c_ref[...]
    s = c * jax.nn.sigmoid(c)
    o_ref[...] = jnp.dot(s, w_ref[...], precision=lax.Precision.HIGHEST,
                         preferred_element_type=F32) + b_ref[...]


def _modulation(c_rows, w_mod, b_mod):
    tn = 1536
    n6 = 6 * D_MODEL
    return pl.pallas_call(
        _mod_body,
        grid=(DEPTH, n6 // tn),
        in_specs=[_full((MOD_ROWS, D_MODEL)),
                  pl.BlockSpec((None, D_MODEL, tn), lambda l, j: (l, 0, j)),
                  pl.BlockSpec((None, 1, tn), lambda l, j: (l, 0, j))],
        out_specs=pl.BlockSpec((None, MOD_ROWS, tn), lambda l, j: (l, 0, j)),
        out_shape=jax.ShapeDtypeStruct((DEPTH, MOD_ROWS, n6), F32),
        compiler_params=_cparams(("parallel", "parallel")),
        name="modulation",
    )(c_rows, w_mod, b_mod.reshape(DEPTH, 1, n6))


def _mod_spec(chunk, ctx):
    return pl.BlockSpec((SUBLANES, D_MODEL), lambda i: (1 if ctx else 0, chunk))


def _scale_rows(v, m):
    r, d = v.shape
    return (v.reshape(r // SUBLANES, SUBLANES, d) * m[None]).reshape(r, d)


def _rms_mod(x, g, shift, scale):
    r, d = x.shape
    y = x * lax.rsqrt(jnp.mean(x * x, axis=-1, keepdims=True) + EPS) * g
    y3 = y.reshape(r // SUBLANES, SUBLANES, d)
    return (y3 * (1.0 + scale)[None] + shift[None]).reshape(r, d)


def _load_plus_pos(x_ref, prow_ref, pcol_ref):
    row = prow_ref[...]
    return jnp.concatenate([x_ref[:, t, :] + jnp.concatenate([row, pcol_ref[t:t + 1, :]], axis=1)
                            for t in range(GRID_W)], axis=0)


def _x_spec(x, tr):
    if x.ndim == 3:
        return pl.BlockSpec((SUBLANES, tr // SUBLANES, x.shape[2]), lambda i: (0, i, 0))
    return pl.BlockSpec((tr, x.shape[1]), lambda i: (i, 0))


def _inproj_body(*refs, add_pos):
    if add_pos:
        (x_ref, prow_ref, pcol_ref, sh_ref, sc_ref, g_ref, w_ref, b_ref,
         xa_ref, ga_ref, xb_ref, xc_ref, xd_ref) = refs
        x = _load_plus_pos(x_ref, prow_ref, pcol_ref)
    else:
        x_ref, sh_ref, sc_ref, g_ref, w_ref, b_ref, xa_ref, ga_ref, xb_ref, xc_ref, xd_ref = refs
        x = x_ref[...]
    u = _rms_mod(x, g_ref[...], sh_ref[...], sc_ref[...])
    p = _dot(u.astype(BF16), w_ref[...]) + b_ref[...]
    xa_ref[...] = p[:, 0:256]
    ga_ref[...] = p[:, 256:512]
    xb_ref[...] = p[:, 512:768]
    xc_ref[...] = p[:, 768:1024].astype(BF16)
    xd_ref[...] = p[:, 1024:1536]


def _in_projection(x, pos, mod, norm_g, w_in_bf, b_in, ctx):
    add_pos = pos is not None
    assert add_pos == (x.ndim == 3)
    R, D = x.size // x.shape[-1], x.shape[-1]
    tr = min(R, 512)
    row = lambda w: pl.BlockSpec((tr, w), lambda i: (i, 0))
    in_specs = [_x_spec(x, tr)]
    args = [x]
    if add_pos:
        assert tr == GRID_W * SUBLANES
        in_specs += [pl.BlockSpec((None, 1, D // 2), lambda i: (i, 0, 0)), _full((GRID_W, D // 2))]
        args += list(pos)
    in_specs += [_mod_spec(0, ctx), _mod_spec(1, ctx), _full((1, D)), _full((D, D_IN)), _full((1, D_IN))]
    args += [mod, mod, norm_g.reshape(1, D), w_in_bf, b_in.reshape(1, D_IN)]
    out_shape = [jax.ShapeDtypeStruct((R, 256), F32)] * 3 + [
        jax.ShapeDtypeStruct((R, 256), BF16), jax.ShapeDtypeStruct((R, 512), F32)]
    return pl.pallas_call(
        functools.partial(_inproj_body, add_pos=add_pos),
        grid=(R // tr,),
        in_specs=in_specs,
        out_specs=[row(256), row(256), row(256), row(256), row(512)],
        out_shape=out_shape,
        compiler_params=_cparams(("parallel",)),
        name="in_projection",
    )(*args)


def _rg_gates(xc, wg, bg, lam):
    g = _dot(xc.astype(BF16), wg) + bg
    r = jax.nn.sigmoid(g[:, :W_GROUP])
    gi = jax.nn.sigmoid(g[:, W_GROUP:])
    z = -lam
    softplus = jnp.maximum(z, 0.0) + jnp.log1p(jnp.exp(-jnp.abs(z)))
    log_a = (-RG_C) * r * softplus
    a = jnp.exp(log_a)
    b = jnp.sqrt(-jnp.tanh(log_a) * (a * a + 1.0)) * (gi * xc)
    return a, b


def _rg_body(xf_ref, xfh_ref, xr_ref, xrh_ref, cw_ref, cb_ref, wg_ref, bg_ref, lam_ref, h0_ref,
             yf_ref, yb_ref, hfin_ref, af_ref, ab_ref, hc_ref, *, n, tt):
    i = pl.program_id(0)
    tr = tt * SUBLANES
    keep = RG_HALO - (RG_CONV - 1) * SUBLANES

    @pl.when(i == 0)
    def _():
        hc_ref[...] = h0_ref[...]

    halo = jnp.where(i > 0, xfh_ref[...], 0.0)
    ext = jnp.concatenate([halo[keep:], xf_ref[...]], axis=0)
    xc = cb_ref[0]
    for k in range(RG_CONV):
        xc = xc + cw_ref[0, k:k + 1, :] * ext[k * SUBLANES:k * SUBLANES + tr]
    a, b = _rg_gates(xc, wg_ref[0], bg_ref[0], lam_ref[0])
    af_ref[...] = a
    yf_ref[...] = b

    halo = jnp.where(i > 0, xrh_ref[...], 0.0)
    ext = jnp.concatenate([xr_ref[...], halo[:(RG_CONV - 1) * SUBLANES]], axis=0)
    xc = cb_ref[1]
    for k in range(RG_CONV):
        o = (RG_CONV - 1 - k) * SUBLANES
        xc = xc + cw_ref[1, k:k + 1, :] * ext[o:o + tr]
    a, b = _rg_gates(xc, wg_ref[1], bg_ref[1], lam_ref[1])
    ab_ref[...] = a
    yb_ref[...] = b

    def step(t, carry):
        hf, hb = carry
        rf = pl.multiple_of(t * SUBLANES, SUBLANES)
        hf = af_ref[pl.ds(rf, SUBLANES), :] * hf + yf_ref[pl.ds(rf, SUBLANES), :]
        yf_ref[pl.ds(rf, SUBLANES), :] = hf
        rb = pl.multiple_of((tt - 1 - t) * SUBLANES, SUBLANES)
        hb = ab_ref[pl.ds(rb, SUBLANES), :] * hb + yb_ref[pl.ds(rb, SUBLANES), :]
        yb_ref[pl.ds(rb, SUBLANES), :] = hb
        return hf, hb

    hf, hb = lax.fori_loop(0, tt, step, (hc_ref[0], hc_ref[1]), unroll=8)
    hc_ref[0] = hf
    hc_ref[1] = hb
    hfin_ref[0] = hf
    hfin_ref[1] = hb


def _rglru(xa2, conv_w, conv_b, wg_bf, bg, lam, h0, L):
    tt = min(L, 256)
    n = L // tt
    tr = tt * SUBLANES
    per = tr // RG_HALO
    last_halo = L * SUBLANES // RG_HALO - 1
    row = lambda i: (i, 0)
    rev = lambda i: (n - 1 - i, 0)
    in_specs = [
        pl.BlockSpec((tr, W_GROUP), row),
        pl.BlockSpec((RG_HALO, W_GROUP), lambda i: (jnp.maximum(i * per - 1, 0), 0)),
        pl.BlockSpec((tr, W_GROUP), rev),
        pl.BlockSpec((RG_HALO, W_GROUP), lambda i: (jnp.minimum((n - i) * per, last_halo), 0)),
        _full((2, RG_CONV, W_GROUP)), _full((2, 1, W_GROUP)), _full((2, W_GROUP, 2 * W_GROUP)),
        _full((2, 1, 2 * W_GROUP)), _full((2, 1, W_GROUP)), _full((2, SUBLANES, W_GROUP)),
    ]
    return pl.pallas_call(
        functools.partial(_rg_body, n=n, tt=tt),
        grid=(n,),
        in_specs=in_specs,
        out_specs=[pl.BlockSpec((tr, W_GROUP), row), pl.BlockSpec((tr, W_GROUP), rev),
                   _full((2, SUBLANES, W_GROUP))],
        out_shape=[jax.ShapeDtypeStruct((L * SUBLANES, W_GROUP), F32)] * 2
        + [jax.ShapeDtypeStruct((2, SUBLANES, W_GROUP), F32)],
        scratch_shapes=[pltpu.VMEM((tr, W_GROUP), F32), pltpu.VMEM((tr, W_GROUP), F32),
                        pltpu.VMEM((2, SUBLANES, W_GROUP), F32)],
        compiler_params=_cparams(("arbitrary",)),
        name="rglru",
    )(xa2, xa2, xa2, xa2, conv_w, conv_b.reshape(2, 1, W_GROUP), wg_bf, bg, lam.reshape(2, 1, W_GROUP), h0)


def _pool_body(xm_ref, xp_ref, xn_ref, w_ref, b_ref, s_ref, o_ref, *, n, tt, L):
    i = pl.program_id(0)
    tr = tt * SUBLANES
    S = SUBLANES
    xm = xm_ref[...]
    prev = jnp.where(i > 0, xp_ref[...], 0.0)
    nxt = jnp.where(i < n - 1, xn_ref[...], 0.0)
    xe = jnp.concatenate([prev, xm, nxt], axis=0)
    e = xe.shape[0]
    p2 = xe[S:e] + xe[0:e - S]
    n4 = (tt + 13) * S
    p4 = p2[0:n4] + p2[2 * S:2 * S + n4]
    n8 = (tt + 9) * S
    p8 = p4[0:n8] + p4[4 * S:4 * S + n8]
    s16 = p8[0:tr] + p8[8 * S:8 * S + tr]
    s2 = p2[7 * S:7 * S + tr]
    s4 = p4[6 * S:6 * S + tr]
    s8 = p8[4 * S:4 * S + tr]
    grp = lax.broadcasted_iota(jnp.int32, (1, W_GROUP), 1) // D_SUB
    half = jnp.left_shift(1, grp)
    t = i * tt + lax.broadcasted_iota(jnp.int32, (tr, 1), 0) // S
    cnt = (jnp.minimum(t + half, L) - jnp.maximum(t - half, 0)).astype(F32)
    s = jnp.where(grp == 0, s2, jnp.where(grp == 1, s4, jnp.where(grp == 2, s8, s16)))
    pooled = s / cnt - xm
    y = _dot(pooled.astype(BF16), w_ref[...]) + b_ref[...]
    o_ref[...] = (y * s_ref[...]).astype(BF16)


def _pool_mixer(xb2, w_bd_bf, b, scale, L):
    tt = min(L, 256)
    n = L // tt
    tr = tt * SUBLANES
    per = tr // POOL_HALO
    last_halo = L * SUBLANES // POOL_HALO - 1
    return pl.pallas_call(
        functools.partial(_pool_body, n=n, tt=tt, L=L),
        grid=(n,),
        in_specs=[pl.BlockSpec((tr, W_GROUP), lambda i: (i, 0)),
                  pl.BlockSpec((POOL_HALO, W_GROUP), lambda i: (jnp.maximum(i * per - 1, 0), 0)),
                  pl.BlockSpec((POOL_HALO, W_GROUP), lambda i: (jnp.minimum((i + 1) * per, last_halo), 0)),
                  _full((W_GROUP, W_GROUP)), _full((1, W_GROUP)), _full((1, W_GROUP))],
        out_specs=pl.BlockSpec((tr, W_GROUP), lambda i: (i, 0)),
        out_shape=jax.ShapeDtypeStruct((L * SUBLANES, W_GROUP), BF16),
        compiler_params=_cparams(("parallel",)),
        name="pool_mixer",
    )(xb2, xb2, xb2, w_bd_bf, b.reshape(1, W_GROUP), scale.reshape(1, W_GROUP))


def _fourier_body(c_ref, s_ref, xe_ref, xo_ref, cw_ref, sw_ref, cc_ref, sc_ref, w_ref, b_ref, o_ref):
    c, s = c_ref[...], s_ref[...]
    xe, xo = xe_ref[...], xo_ref[...]
    ec, es = _dot(c, xe), _dot(s, xe)
    oc, os_ = _dot(c, xo), _dot(s, xo)
    cw, sw = cw_ref[...], sw_ref[...]
    tc = cw * oc - sw * os_
    ts = cw * os_ + sw * oc
    for h, (z1, z2) in enumerate(((ec + tc, es + ts), (ec - tc, es - ts))):
        for j in range(xe.shape[1] // W_GROUP):
            sl = slice(j * W_GROUP, (j + 1) * W_GROUP)
            a_hi, a_lo = _split_bf16(z1[:, sl])
            b_hi, b_lo = _split_bf16(z2[:, sl])
            f = ((_dot(a_hi, cc_ref[...]) + _dot(a_lo, cc_ref[...]))
                 - (_dot(b_hi, sc_ref[...]) + _dot(b_lo, sc_ref[...])))
            o_ref[h, :, sl] = (_dot(f.astype(BF16), w_ref[...]) + b_ref[...]).astype(BF16)


def _fourier_mixer(xc, tables, cc, sc, w_bd_bf, b, L):
    ch, sh, cw, sw = tables
    M = L // 2
    ncol = SUBLANES * W_GROUP
    x2 = xc.reshape(M, 2 * ncol)
    nb = 1024
    tk = min(M, 256)
    out = pl.pallas_call(
        _fourier_body,
        grid=(ncol // nb, M // tk),
        in_specs=[pl.BlockSpec((tk, M), lambda j, k: (k, 0)),
                  pl.BlockSpec((tk, M), lambda j, k: (k, 0)),
                  pl.BlockSpec((M, nb), lambda j, k: (0, j)),
                  pl.BlockSpec((M, nb), lambda j, k: (0, ncol // nb + j)),
                  pl.BlockSpec((tk, 1), lambda j, k: (k, 0)),
                  pl.BlockSpec((tk, 1), lambda j, k: (k, 0)),
                  _full((W_GROUP, W_GROUP)), _full((W_GROUP, W_GROUP)), _full((W_GROUP, W_GROUP)),
                  _full((1, W_GROUP))],
        out_specs=pl.BlockSpec((2, tk, nb), lambda j, k: (0, k, j)),
        out_shape=jax.ShapeDtypeStruct((2, M, ncol), BF16),
        compiler_params=_cparams(("parallel", "parallel")),
        name="fourier_mixer",
    )(ch, sh, x2, x2, cw, sw, cc, sc, w_bd_bf, b.reshape(1, W_GROUP))
    return out.reshape(L * SUBLANES, W_GROUP)


def _time_dft_tables(L):
    M = L // 2
    ch, sh = _dft_matrices(M, 1.0 / math.sqrt(L))
    ang = jnp.arange(M, dtype=F32) * (2.0 * math.pi / L)
    return ch.astype(BF16), sh.astype(BF16), jnp.cos(ang).reshape(M, 1), jnp.sin(ang).reshape(M, 1)


def _dft_matrices(L, scale):
    f = 1 << (max(L.bit_length() - 1, 0) // 2)
    n = jnp.arange(L, dtype=jnp.int32)[None, :]

    def table(rows):
        ang = ((rows[:, None] * n) % L).astype(F32) * (2.0 * math.pi / L)
        return jnp.cos(ang), jnp.sin(ang)

    ac, as_ = table(jnp.arange(L // f, dtype=jnp.int32) * f)
    bc, bs = table(jnp.arange(f, dtype=jnp.int32))
    cos = (ac[:, None, :] * bc[None, :, :] - as_[:, None, :] * bs[None, :, :]).reshape(L, L) * scale
    sin = (as_[:, None, :] * bc[None, :, :] + ac[:, None, :] * bs[None, :, :]).reshape(L, L) * scale
    return cos, sin


def _block_diag(w):
    g, a, b = w.shape
    eye = jnp.eye(g, dtype=w.dtype)
    return (eye[:, None, :, None] * w[:, :, None, :]).reshape(g * a, g * b)


CONF_CHUNK = 64


def _conformer_body(xm_ref, xp_ref, xn_ref, cw_ref, cb_ref, lg_ref, lb_ref, avg_ref, w_ref, b_ref,
                    o_ref, v_ref, c_ref, *, n, tt):
    i = pl.program_id(0)
    tr = tt * SUBLANES
    H = CONF_HALO

    def glu(v):
        return v[:, :W_GROUP] * jax.nn.sigmoid(v[:, W_GROUP:])

    v_ref[0:H] = jnp.where(i > 0, glu(xp_ref[...]), 0.0)
    v_ref[H:H + tr] = glu(xm_ref[...])
    v_ref[H + tr:H + tr + H] = jnp.where(i < n - 1, glu(xn_ref[...]), 0.0)

    def chunk(c, carry):
        r0 = pl.multiple_of(c * CONF_CHUNK, CONF_CHUNK)
        acc = jnp.broadcast_to(cb_ref[...], (CONF_CHUNK, W_GROUP))
        for k in range(CONF_KERNEL):
            acc = acc + cw_ref[k:k + 1, :] * v_ref[pl.ds(r0 + (k + 1) * SUBLANES, CONF_CHUNK), :]
        c_ref[pl.ds(r0, CONF_CHUNK), :] = acc
        return carry

    lax.fori_loop(0, tr // CONF_CHUNK, chunk, 0)

    v = c_ref[...]
    avg = avg_ref[...]
    v_hi, v_lo = _split_bf16(v)
    mu = _dot(v_hi, avg) + _dot(v_lo, avg)
    d = v - mu
    q_hi, q_lo = _split_bf16(d * d)
    var = _dot(q_hi, avg) + _dot(q_lo, avg)
    vn = d * lax.rsqrt(var + EPS) * lg_ref[...] + lb_ref[...]
    act = vn * jax.nn.sigmoid(vn)
    o_ref[...] = (_dot(act.astype(BF16), w_ref[...]) + b_ref[...]).astype(BF16)


def _conformer(xd2, conv_w, conv_b, ln_g, ln_b, avg_bf, w_pw_bf, b_pw, L):
    tt = min(L, 256)
    n = L // tt
    tr = tt * SUBLANES
    per = tr // CONF_HALO
    last_halo = L * SUBLANES // CONF_HALO - 1
    vec = lambda a: a.reshape(1, W_GROUP)
    return pl.pallas_call(
        functools.partial(_conformer_body, n=n, tt=tt),
        grid=(n,),
        in_specs=[pl.BlockSpec((tr, 2 * W_GROUP), lambda i: (i, 0)),
                  pl.BlockSpec((CONF_HALO, 2 * W_GROUP), lambda i: (jnp.maximum(i * per - 1, 0), 0)),
                  pl.BlockSpec((CONF_HALO, 2 * W_GROUP), lambda i: (jnp.minimum((i + 1) * per, last_halo), 0)),
                  _full((CONF_KERNEL, W_GROUP)), _full((1, W_GROUP)), _full((1, W_GROUP)), _full((1, W_GROUP)),
                  _full((W_GROUP, W_GROUP)), _full((W_GROUP, W_GROUP)), _full((1, W_GROUP))],
        out_specs=pl.BlockSpec((tr, W_GROUP), lambda i: (i, 0)),
        out_shape=jax.ShapeDtypeStruct((L * SUBLANES, W_GROUP), BF16),
        scratch_shapes=[pltpu.VMEM((tr + 2 * CONF_HALO, W_GROUP), F32), pltpu.VMEM((tr, W_GROUP), F32)],
        compiler_params=_cparams(("parallel",)),
        name="conformer",
    )(xd2, xd2, xd2, conv_w, vec(conv_b), vec(ln_g), vec(ln_b), avg_bf, w_pw_bf, vec(b_pw))


def _gelu_tanh(x):
    return 0.5 * x * (1.0 + jnp.tanh(math.sqrt(2.0 / math.pi) * (x + 0.044715 * (x * x * x))))


def _pack_bf16_pairs(h_bf):
    u = pltpu.bitcast(h_bf.astype(F32), jnp.uint32)
    half = h_bf.shape[1] // 2
    return (u[:, :half] & jnp.uint32(0xFFFF0000)) | (u[:, half:] >> 16)


def _unpack_bf16_pairs(p):
    hi = pltpu.bitcast(p & jnp.uint32(0xFFFF0000), F32).astype(BF16)
    lo = pltpu.bitcast(p << 16, F32).astype(BF16)
    return hi, lo


def _outproj_body(*refs, add_pos, chained):
    refs = list(refs)
    xo_ref, hp_ref, lg_ref = refs[-3:]
    del refs[-5 if chained else -3:]
    if add_pos:
        (x_ref, prow_ref, pcol_ref, yf_ref, yb_ref, ga_ref, yp_ref, yc_ref, yd_ref, g1_ref, sh_ref, sc_ref,
         ng_ref, wo_ref, bo_ref, wr_ref, br_ref) = refs
        x = _load_plus_pos(x_ref, prow_ref, pcol_ref)
    else:
        (x_ref, yf_ref, yb_ref, ga_ref, yp_ref, yc_ref, yd_ref, g1_ref, sh_ref, sc_ref, ng_ref,
         wo_ref, bo_ref, wr_ref, br_ref) = refs
        x = x_ref[...]
    ya = (yf_ref[...] + yb_ref[...]) * _gelu_tanh(ga_ref[...])
    ycat = jnp.concatenate([ya.astype(BF16), yp_ref[...], yc_ref[...], yd_ref[...]], axis=1)
    y = _dot(ycat, wo_ref[...]) + bo_ref[...]
    xn = x + _scale_rows(y, g1_ref[...])
    xo_ref[...] = xn
    h = _rms_mod(xn, ng_ref[...], sh_ref[...], sc_ref[...])
    h_hi, h_lo = _split_bf16(h)
    nt = (((1,), (1,)), ((), ()))
    wr = wr_ref[...]
    acc = lax.dot_general(wr, h_hi, nt, preferred_element_type=F32)
    acc = acc + lax.dot_general(wr, h_lo, nt, preferred_element_type=F32)
    lg_ref[...] = acc[:N_EXPERTS] + acc[N_EXPERTS:] + br_ref[...]
    hp_ref[...] = _pack_bf16_pairs(h_hi)


def _out_projection(x, pos, mix, mod, norm_g, w_out_bf, b_out, wr2_bf, b_router, ctx, n_tok, tok0, prev):
    add_pos = pos is not None
    assert add_pos == (x.ndim == 3)
    R, D = x.size // x.shape[-1], x.shape[-1]
    tr = min(R, 512)
    row = lambda w: pl.BlockSpec((tr, w), lambda i: (i, 0))
    in_specs = [_x_spec(x, tr)]
    args = [x]
    if add_pos:
        assert tr == GRID_W * SUBLANES
        in_specs += [pl.BlockSpec((None, 1, D // 2), lambda i: (i, 0, 0)), _full((GRID_W, D // 2))]
        args += list(pos)
    in_specs += [row(W_GROUP)] * 6
    args += list(mix)
    in_specs += [_mod_spec(2, ctx), _mod_spec(3, ctx), _mod_spec(4, ctx), _full((1, D)), _full((D, D)),
                 _full((1, D)), _full((2 * N_EXPERTS, D)), _full((N_EXPERTS, 1))]
    args += [mod, mod, mod, norm_g.reshape(1, D), w_out_bf, b_out.reshape(1, D), wr2_bf,
             b_router.reshape(N_EXPERTS, 1)]
    aliases = {}
    if prev is not None:
        aliases = {len(args): 1, len(args) + 1: 2}
        in_specs += [pl.BlockSpec(memory_space=pl.ANY)] * 2
        args += list(prev)
    t0 = tok0 // tr
    assert t0 * tr == tok0
    return pl.pallas_call(
        functools.partial(_outproj_body, add_pos=add_pos, chained=prev is not None),
        grid=(R // tr,),
        in_specs=in_specs,
        out_specs=[row(D), pl.BlockSpec((tr, D // 2), lambda i: (t0 + i, 0)),
                   pl.BlockSpec((N_EXPERTS, tr), lambda i: (0, t0 + i))],
        out_shape=[jax.ShapeDtypeStruct((R, D), F32),
                   jax.ShapeDtypeStruct((n_tok, D // 2), jnp.uint32),
                   jax.ShapeDtypeStruct((N_EXPERTS, n_tok), F32)],
        input_output_aliases=aliases,
        compiler_params=_cparams(("parallel",)),
        name="out_projection",
    )(*args)


def _top4(v):
    eid = lax.broadcasted_iota(jnp.int32, v.shape, 0)
    out = []
    work = v
    for _ in range(TOP_K):
        m = jnp.max(work, axis=0, keepdims=True)
        idx = jnp.min(jnp.where(work == m, eid, N_EXPERTS), axis=0, keepdims=True)
        oh = eid == idx
        out.append((m, oh))
        work = jnp.where(oh, -jnp.inf, work)
    return out


def _count_body(lg_ref, cnt_ref):
    @pl.when(pl.program_id(0) == 0)
    def _():
        cnt_ref[...] = jnp.zeros_like(cnt_ref)

    sel = jnp.zeros(lg_ref.shape, F32)
    for _, oh in _top4(lg_ref[...]):
        sel = sel + oh.astype(F32)
    cnt_ref[...] += jnp.sum(sel, axis=1, keepdims=True)


def _route_body(lg_ref, ps_ref, tri_ref, dest_ref, gate_ref, carry_ref):
    @pl.when(pl.program_id(0) == 0)
    def _():
        carry_ref[...] = jnp.zeros_like(carry_ref)

    top = _top4(lg_ref[...])
    sel = jnp.zeros(lg_ref.shape, F32)
    for _, oh in top:
        sel = sel + oh.astype(F32)
    before = _dot(sel.astype(BF16), tri_ref[...]) + carry_ref[...] + ps_ref[...]
    m0 = top[0][0]
    es = [jnp.exp(m - m0) for m, _ in top]
    den = es[0] + es[1] + es[2] + es[3]
    for k, (_, oh) in enumerate(top):
        dest_ref[k:k + 1, :] = jnp.sum(jnp.where(oh, before, 0.0), axis=0, keepdims=True).astype(jnp.int32)
        gate_ref[k:k + 1, :] = es[k] / den
    carry_ref[...] += jnp.sum(sel, axis=1, keepdims=True)


def _per_expert(table, experts):
    hit = experts[:, None] == jnp.arange(N_EXPERTS, dtype=jnp.int32)[None, :]
    return jnp.sum(jnp.where(hit, table[None, :].astype(jnp.int32), 0), axis=1).astype(jnp.int32)


def _routing(logits_t):
    E, T = logits_t.shape
    tt = ROUTE_TILE
    nt = T // tt
    counts = pl.pallas_call(
        _count_body,
        grid=(nt,),
        in_specs=[pl.BlockSpec((E, tt), lambda i: (0, i))],
        out_specs=_full((E, 1)),
        out_shape=jax.ShapeDtypeStruct((E, 1), F32),
        compiler_params=_cparams(("arbitrary",)),
        name="route_count",
    )(logits_t)
    cnt = counts[:, 0].astype(jnp.int32)
    padded = ((cnt + MOE_TILE - 1) // MOE_TILE) * MOE_TILE
    pend = jnp.cumsum(padded)
    pstart = pend - padded
    n_tiles = -(-(T * TOP_K) // MOE_TILE) + N_EXPERTS
    tile_start = jnp.arange(n_tiles, dtype=jnp.int32) * MOE_TILE
    tile_e = jnp.minimum(jnp.sum((pend[None, :] <= tile_start[:, None]).astype(jnp.int32), axis=1), N_EXPERTS - 1)
    n_used = (pend[-1] // MOE_TILE).astype(jnp.int32).reshape(1)
    n_valid = jnp.clip(_per_expert(pstart + cnt, tile_e) - tile_start, 0, MOE_TILE).astype(jnp.int32)
    tri = (jnp.arange(tt)[:, None] < jnp.arange(tt)[None, :]).astype(BF16)
    dest, gates = pl.pallas_call(
        _route_body,
        grid=(nt,),
        in_specs=[pl.BlockSpec((E, tt), lambda i: (0, i)), _full((E, 1)), _full((tt, tt))],
        out_specs=[pl.BlockSpec((TOP_K, tt), lambda i: (0, i)), pl.BlockSpec((TOP_K, tt), lambda i: (0, i))],
        out_shape=[jax.ShapeDtypeStruct((TOP_K, T), jnp.int32), jax.ShapeDtypeStruct((TOP_K, T), F32)],
        scratch_shapes=[pltpu.VMEM((E, 1), F32)],
        compiler_params=_cparams(("arbitrary",)),
        name="route_assign",
    )(logits_t, pstart.astype(F32).reshape(E, 1), tri)
    return dest, gates, cnt, tile_e, n_valid, n_used, n_tiles


def _sc_workers():
    from jax.experimental.pallas import tpu_sc as plsc
    mesh = plsc.VectorSubcoreMesh(core_axis_name="c", subcore_axis_name="s")
    n_workers = mesh.num_cores * mesh.num_subcores
    worker = lambda: lax.axis_index("s") * mesh.num_cores + lax.axis_index("c")
    return mesh, n_workers, worker


def _dispatch(hp, tok0, dest_flat, n_rows):
    W = hp.shape[1]
    T = dest_flat.shape[0] // TOP_K
    mesh, n_workers, worker = _sc_workers()
    per = dest_flat.shape[0] // n_workers
    steps = per // SC_WINDOW
    assert per * n_workers == dest_flat.shape[0] and steps * SC_WINDOW == per and steps % 2 == 0
    assert T % SC_WINDOW == 0 and tok0 % SC_WINDOW == 0

    @functools.partial(
        pl.kernel, mesh=mesh, out_type=jax.ShapeDtypeStruct((n_rows, W), hp.dtype),
        scratch_types=[pltpu.VMEM((SC_WINDOW,), jnp.int32), pltpu.VMEM((SC_WINDOW,), jnp.int32),
                       pltpu.VMEM((SC_WINDOW, W), hp.dtype), pltpu.VMEM((SC_WINDOW, W), hp.dtype),
                       pltpu.SemaphoreType.DMA, pltpu.SemaphoreType.DMA],
        name="moe_dispatch")
    def scatter(hp_hbm, dest_hbm, xs_hbm, idx0, idx1, rows0, rows1, sem0, sem1):
        base = worker() * per
        bufs = ((idx0, rows0, sem0), (idx1, rows1, sem1))

        def window(j, b, first):
            idx_v, rows_v, sem = bufs[b]

            @pl.when(jnp.logical_not(first))
            def _():
                pltpu.make_async_copy(rows_v, xs_hbm.at[idx_v], sem).wait()

            off = pl.multiple_of(base + j * SC_WINDOW, SC_WINDOW)
            tok = pl.multiple_of(tok0 + lax.rem(off, T), SC_WINDOW)
            pltpu.sync_copy(dest_hbm.at[pl.ds(off, SC_WINDOW)], idx_v)
            pltpu.sync_copy(hp_hbm.at[pl.ds(tok, SC_WINDOW)], rows_v)
            pltpu.async_copy(rows_v, xs_hbm.at[idx_v], sem)

        @pl.loop(0, steps, step=2)
        def _(j):
            window(j, 0, j == 0)
            window(j + 1, 1, j == 0)

        for idx_v, rows_v, sem in bufs:
            pltpu.make_async_copy(rows_v, xs_hbm.at[idx_v], sem).wait()

    return scatter(hp, dest_flat)


def _gather_rows(table, idx_flat):
    n = idx_flat.shape[0]
    W = table.shape[1]
    mesh, n_workers, worker = _sc_workers()
    per = n // n_workers
    steps = per // SC_WINDOW
    assert per * n_workers == n and steps * SC_WINDOW == per and steps % 2 == 0

    @functools.partial(
        pl.kernel, mesh=mesh, out_type=jax.ShapeDtypeStruct((n, W), table.dtype),
        scratch_types=[pltpu.VMEM((SC_WINDOW,), jnp.int32), pltpu.VMEM((SC_WINDOW,), jnp.int32),
                       pltpu.VMEM((SC_WINDOW, W), table.dtype), pltpu.VMEM((SC_WINDOW, W), table.dtype),
                       pltpu.SemaphoreType.DMA, pltpu.SemaphoreType.DMA, pltpu.SemaphoreType.DMA],
        name="moe_gather")
    def gather(table_hbm, idx_hbm, out_hbm, idx0, idx1, rows0, rows1, sem0, sem1, gsem):
        base = worker() * per
        bufs = ((idx0, rows0, sem0), (idx1, rows1, sem1))

        def window(j, b, first):
            idx_v, rows_v, sem = bufs[b]
            off = pl.multiple_of(base + j * SC_WINDOW, SC_WINDOW)

            @pl.when(jnp.logical_not(first))
            def _():
                pltpu.make_async_copy(rows_v, out_hbm.at[pl.ds(off, SC_WINDOW)], sem).wait()

            pltpu.sync_copy(idx_hbm.at[pl.ds(off, SC_WINDOW)], idx_v)
            pltpu.async_copy(table_hbm.at[idx_v], rows_v, gsem).wait()
            pltpu.async_copy(rows_v, out_hbm.at[pl.ds(off, SC_WINDOW)], sem)

        @pl.loop(0, steps, step=2)
        def _(j):
            window(j, 0, j == 0)
            window(j + 1, 1, j == 0)

        for _, rows_v, sem in bufs:
            pltpu.make_async_copy(rows_v, out_hbm.at[pl.ds(base, SC_WINDOW)], sem).wait()

    return gather(table, idx_flat)


def _expert_body(te_ref, nv_ref, nu_ref, nx_ref, sl_ref, xs_ref, wgu_hbm, bgu_ref, wd_hbm, bd_ref, ys_ref,
                 wgu_f32, wd_f32, wgu_bf_ref, wd_bf_ref, sems, *, layer):
    i = pl.program_id(0)
    used = i < nu_ref[0]
    e = te_ref[i]
    s = sl_ref[i]
    new_expert = jnp.logical_or(i == 0, e != te_ref[jnp.maximum(i - 1, 0)])

    def weight_copies(expert, slot):
        return (pltpu.make_async_copy(wgu_hbm.at[layer, expert], wgu_f32.at[slot], sems.at[0, slot]),
                pltpu.make_async_copy(wd_hbm.at[layer, expert], wd_f32.at[slot], sems.at[1, slot]))

    @pl.when(jnp.logical_and(used, i == 0))
    def _():
        for c in weight_copies(e, s):
            c.start()

    @pl.when(jnp.logical_and(used, new_expert))
    def _():
        for c in weight_copies(e, s):
            c.wait()

        @pl.when(nx_ref[i] >= 0)
        def _():
            for c in weight_copies(nx_ref[i], 1 - s):
                c.start()

        wgu_bf_ref[...] = wgu_f32[s].astype(BF16)
        wd_bf_ref[...] = wd_f32[s].astype(BF16)

    def experts_on(n_rows):
        live = lax.broadcasted_iota(jnp.int32, (n_rows, 1), 0) < nv_ref[i]
        x = jnp.concatenate(_unpack_bf16_pairs(jnp.where(live, xs_ref[:n_rows, :], jnp.uint32(0))), axis=1)
        gu = _dot(x, wgu_bf_ref[...]) + bgu_ref[...]
        gt = jnp.minimum(gu[:, :D_FF], SWIGLU_LIMIT)
        up = jnp.clip(gu[:, D_FF:], -SWIGLU_LIMIT, SWIGLU_LIMIT)
        act = (up + 1.0) * (gt * jax.nn.sigmoid(SWIGLU_ALPHA * gt))
        y = _dot(act.astype(BF16), wd_bf_ref[...]) + bd_ref[...]
        ys_ref[:n_rows, :] = _pack_bf16_pairs(y.astype(BF16))

    for n_rows in range(MOE_SUBTILE, MOE_TILE + 1, MOE_SUBTILE):
        fits = jnp.logical_and(nv_ref[i] > n_rows - MOE_SUBTILE, nv_ref[i] <= n_rows)

        @pl.when(jnp.logical_and(used, fits))
        def _():
            experts_on(n_rows)
            if n_rows < MOE_TILE:
                ys_ref[n_rows:, :] = jnp.zeros((MOE_TILE - n_rows, ys_ref.shape[1]), ys_ref.dtype)

    @pl.when(jnp.logical_not(used))
    def _():
        ys_ref[...] = jnp.zeros_like(ys_ref)


def _experts(xs, counts, tile_e, n_valid, n_used, layer, w_gu, b_gu, w_down, b_down):
    n_rows, W = xs.shape
    n_tiles = n_rows // MOE_TILE
    e_ids = jnp.arange(N_EXPERTS, dtype=jnp.int32)
    nonempty = counts > 0
    rank = jnp.cumsum(nonempty.astype(jnp.int32)) - 1
    later = jnp.logical_and(e_ids[None, :] > e_ids[:, None], nonempty[None, :])
    nxt_of = jnp.min(jnp.where(later, e_ids[None, :], N_EXPERTS), axis=1)
    nxt_of = jnp.where(nxt_of < N_EXPERTS, nxt_of, -1)
    slot = _per_expert(rank % 2, tile_e)
    nxt = _per_expert(nxt_of, tile_e)
    bias = lambda w: pl.BlockSpec((None, None, 1, w), lambda i, te, nv, nu, nx, sl: (layer, te[i], 0, 0))
    tile = pl.BlockSpec((MOE_TILE, W), lambda i, te, nv, nu, nx, sl: (i, 0))
    grid_spec = pltpu.PrefetchScalarGridSpec(
        num_scalar_prefetch=5,
        grid=(n_tiles,),
        in_specs=[tile, pl.BlockSpec(memory_space=pl.ANY), bias(2 * D_FF),
                  pl.BlockSpec(memory_space=pl.ANY), bias(D_MODEL)],
        out_specs=tile,
        scratch_shapes=[pltpu.VMEM((2, D_MODEL, 2 * D_FF), F32), pltpu.VMEM((2, D_FF, D_MODEL), F32),
                        pltpu.VMEM((D_MODEL, 2 * D_FF), BF16), pltpu.VMEM((D_FF, D_MODEL), BF16),
                        pltpu.SemaphoreType.DMA((2, 2))],
    )
    return pl.pallas_call(
        functools.partial(_expert_body, layer=layer),
        grid_spec=grid_spec,
        out_shape=jax.ShapeDtypeStruct((n_rows, W), jnp.uint32),
        compiler_params=_cparams(("arbitrary",)),
        name="moe_experts",
    )(tile_e, n_valid, n_used, nxt, slot, xs, w_gu, b_gu.reshape(DEPTH, N_EXPERTS, 1, 2 * D_FF), w_down,
      b_down.reshape(DEPTH, N_EXPERTS, 1, D_MODEL))


def _combine_body(x_ref, y0_ref, y1_ref, y2_ref, y3_ref, gate_ref, g2_ref, *rest, final, chained):
    rest = list(rest)
    o_ref = rest.pop()
    if chained:
        rest.pop()
    g = gate_ref[...]
    acc_hi = acc_lo = None
    for k, y_ref in enumerate((y0_ref, y1_ref, y2_ref, y3_ref)):
        p = y_ref[...]
        gk = g[:, k:k + 1]
        hi = gk * pltpu.bitcast(p & jnp.uint32(0xFFFF0000), F32)
        lo = gk * pltpu.bitcast(p << 16, F32)
        acc_hi = hi if acc_hi is None else acc_hi + hi
        acc_lo = lo if acc_lo is None else acc_lo + lo
    acc = jnp.concatenate([acc_hi, acc_lo], axis=1)
    xn = x_ref[...] + _scale_rows(acc, g2_ref[...])
    if final:
        xn = xn * lax.rsqrt(jnp.mean(xn * xn, axis=-1, keepdims=True) + EPS) * rest[0][...]
        for t in range(o_ref.shape[1]):
            o_ref[:, t, :] = xn[t * SUBLANES:(t + 1) * SUBLANES, :]
    else:
        o_ref[...] = xn


def _combine(x, rows, moe_out, tok_off, mod, final_g, ctx, prev=None):
    ysg, gates_t = moe_out
    R, D = x.shape
    tl = TOK_TILE
    r0 = rows[0] // tl
    t0 = tok_off // tl
    nt = gates_t.shape[0] // tl
    final = final_g is not None
    deep = dict(pipeline_mode=pl.Buffered(STREAM_BUFFERS))
    y_specs = [pl.BlockSpec((tl, D // 2), lambda i, k=k: (k * nt + t0 + i, 0), **deep) for k in range(TOP_K)]
    in_specs = ([pl.BlockSpec((tl, D), lambda i: (r0 + i, 0), **deep)] + y_specs
                + [pl.BlockSpec((tl, TOP_K), lambda i: (t0 + i, 0)),
                   pl.BlockSpec((SUBLANES, D), lambda i: (1 if ctx else 0, 5))])
    args = [x, ysg, ysg, ysg, ysg, gates_t, mod]
    if final:
        in_specs.append(pl.BlockSpec((1, D), lambda i: (0, 0)))
        args.append(final_g.reshape(1, D))
        out_spec = pl.BlockSpec((SUBLANES, tl // SUBLANES, D), lambda i: (0, r0 + i, 0))
        out_shape = jax.ShapeDtypeStruct((SUBLANES, R // SUBLANES, D), F32)
    else:
        out_spec = pl.BlockSpec((tl, D), lambda i: (r0 + i, 0))
        out_shape = jax.ShapeDtypeStruct((R, D), F32)
    n_in = len(args)
    aliases = {}
    if prev is not None:
        aliases = {len(args): 0}
        args.append(prev)
    steps = (rows[1] - rows[0]) // tl

    def outer(*refs):
        pltpu.emit_pipeline(
            functools.partial(_combine_body, final=final, chained=False),
            grid=(steps,), in_specs=in_specs, out_specs=[out_spec],
        )(*refs[:n_in], refs[-1])

    return pl.pallas_call(
        outer,
        in_specs=[pl.BlockSpec(memory_space=pl.ANY)] * len(args),
        out_specs=pl.BlockSpec(memory_space=pl.ANY),
        out_shape=out_shape,
        input_output_aliases=aliases,
        compiler_params=pltpu.CompilerParams(vmem_limit_bytes=VMEM_LIMIT_BYTES),
        name="moe_combine",
    )(*args)


def _moe(hp, logits_t, toks, layer, w_gu, b_gu, w_down, b_down):
    dest, gates, counts, tile_e, n_valid, n_used, n_tiles = _routing(logits_t[:, toks[0]:toks[1]])
    dest_flat = dest.reshape(-1)
    xs = _dispatch(hp, toks[0], dest_flat, n_tiles * MOE_TILE)
    ys = _experts(xs, counts, tile_e, n_valid, n_used, layer, w_gu, b_gu, w_down, b_down)
    return _gather_rows(ys, dest_flat), gates.T


def _token_mixers(x, pos, mod, h0, p, consts, ctx, need_out):
    R = x.size // x.shape[-1]
    L = R // SUBLANES
    xa, ga, xb, xc, xd = _in_projection(x, pos, mod, p["norm1_g"], p["w_in"], p["b_in"], ctx)
    yf, yb, hfin = _rglru(xa, p["conv_a_w"], p["conv_a_b"], p["wg"], p["bg"], p["rg_lambda"], h0, L)
    if not need_out:
        return None, hfin
    yp = _pool_mixer(xb, p["w_pool"], p["b_pool"], p["pool_scale"], L)
    yc = _fourier_mixer(xc, consts["dft"][L], consts["cc"], consts["sc"], p["w_four"], p["b_four"], L)
    yd = _conformer(xd, p["conv_d_w"], p["conv_d_b"], p["ln_d_g"], p["ln_d_b"], consts["avg"], p["w_pw"],
                    p["b_pw"], L)
    return (yf, yb, ga, yp, yc, yd), hfin


def _pos_embed(n_tokens):
    rows_n = n_tokens // GRID_W
    q = D_MODEL // 4
    omega = 1.0 / (10000.0 ** (jnp.arange(q, dtype=F32) / q))

    def emb(n):
        ang = jnp.arange(n, dtype=F32)[:, None] * omega[None, :]
        return jnp.concatenate([jnp.sin(ang), jnp.cos(ang)], axis=-1)

    return emb(rows_n).reshape(rows_n, 1, D_MODEL // 2), emb(GRID_W)


def _layer_params(l, w_in, b_in, conv_a_w, conv_a_b, w_rg_r, b_rg_r, w_rg_i, b_rg_i, rg_lambda, w_pool, b_pool,
                  pool_scale, w_four, b_four, conv_d_w, conv_d_b, ln_d_g, ln_d_b, w_pw, b_pw, norm1_g):
    wg = jnp.stack([jnp.concatenate([_block_diag(w_rg_r[l, d]), _block_diag(w_rg_i[l, d])], axis=1)
                    for d in range(2)]).astype(BF16)
    bg = jnp.concatenate([b_rg_r[l].reshape(2, 1, W_GROUP), b_rg_i[l].reshape(2, 1, W_GROUP)], axis=-1)
    return dict(
        norm1_g=norm1_g[l], w_in=w_in[l].astype(BF16), b_in=b_in[l],
        conv_a_w=conv_a_w[l], conv_a_b=conv_a_b[l], wg=wg, bg=bg, rg_lambda=rg_lambda[l],
        w_pool=_block_diag(w_pool[l]).astype(BF16), b_pool=b_pool[l], pool_scale=pool_scale[l],
        w_four=_block_diag(w_four[l]).astype(BF16), b_four=b_four[l],
        conv_d_w=conv_d_w[l], conv_d_b=conv_d_b[l], ln_d_g=ln_d_g[l], ln_d_b=ln_d_b[l],
        w_pw=w_pw[l].astype(BF16), b_pw=b_pw[l])


def kernel(x, c, ctx, c_ctx, w_mod, b_mod, norm1_g, norm2_g, w_in, b_in, conv_a_w, conv_a_b, w_rg_r, b_rg_r,
           w_rg_i, b_rg_i, rg_lambda, w_pool, b_pool, pool_scale, w_four, b_four, conv_d_w, conv_d_b, ln_d_g,
           ln_d_b, w_pw, b_pw, w_out, b_out, w_router, b_router, w_gu, b_gu, w_down, b_down, final_norm_g):
    bn, L, D = x.shape
    Lc = ctx.shape[1]
    assert bn == SUBLANES and D == D_MODEL

    pos = _pos_embed(L)
    c_rows = jnp.concatenate([c, jnp.broadcast_to(c_ctx[None], (MOD_ROWS - bn, D))], axis=0)
    mod = _modulation(c_rows, w_mod, b_mod)
    ctx = jnp.transpose(ctx, (1, 0, 2)).reshape(Lc * bn, D)

    cc1, sc1 = _dft_matrices(D_SUB, 1.0 / math.sqrt(D_SUB))
    eye = jnp.eye(N_SUB, dtype=F32)
    consts = dict(
        dft={n: _time_dft_tables(n) for n in sorted({L, Lc})},
        cc=jnp.kron(eye, cc1).astype(BF16), sc=jnp.kron(eye, sc1).astype(BF16),
        avg=jnp.kron(eye, jnp.full((D_SUB, D_SUB), 1.0 / D_SUB, F32)).astype(BF16))
    h_zero = jnp.zeros((2, SUBLANES, W_GROUP), F32)

    for l in range(DEPTH):
        last = l == DEPTH - 1
        p = _layer_params(l, w_in, b_in, conv_a_w, conv_a_b, w_rg_r, b_rg_r, w_rg_i, b_rg_i, rg_lambda, w_pool,
                          b_pool, pool_scale, w_four, b_four, conv_d_w, conv_d_b, ln_d_g, ln_d_b, w_pw, b_pw,
                          norm1_g)
        mod_l = mod[l]
        w_out_bf = w_out[l].astype(BF16)
        wr_t = w_router[l].T
        wr_hi = wr_t.astype(BF16)
        wr2 = jnp.concatenate([wr_hi, (wr_t - wr_hi.astype(F32)).astype(BF16)], axis=0)
        x_pos = pos if l == 0 else None

        mix_c, h_ctx = _token_mixers(ctx, None, mod_l, h_zero, p, consts, True, not last)
        mix_x, _ = _token_mixers(x, x_pos, mod_l, h_ctx, p, consts, False, True)
        n_ctx, final_g = (0, final_norm_g) if last else (bn * Lc, None)
        T = n_ctx + bn * L
        moe_in = None
        if not last:
            ctx, *moe_in = _out_projection(ctx, None, mix_c, mod_l, norm2_g[l], w_out_bf, b_out[l], wr2,
                                           b_router[l], True, T, 0, None)
        x, hp, lg = _out_projection(x, x_pos, mix_x, mod_l, norm2_g[l], w_out_bf, b_out[l], wr2, b_router[l],
                                    False, T, n_ctx, moe_in)
        half = T // 2
        moe_a = _moe(hp, lg, (0, half), l, w_gu, b_gu, w_down, b_down)
        moe_b = _moe(hp, lg, (half, T), l, w_gu, b_gu, w_down, b_down)
        if n_ctx:
            ctx = _combine(ctx, (0, n_ctx), moe_a, 0, mod_l, None, True)
        split = half - n_ctx
        xa = _combine(x, (0, split), moe_a, n_ctx, mod_l, final_g, False)
        x = _combine(x, (split, x.shape[0]), moe_b, 0, mod_l, final_g, False, prev=xa)
    return x
```
